```python
import math
import jax, jax.numpy as jnp
from jax import lax
import numpy as np

D_MODEL = 2048
BATCH = 8
SEQ = 4096
DEPTH = 2
DEC_BATCH = 8
DEC_SEQ = 64
PAST_LEN = 4096

CHUNK = 64
Q_BLOCK = 128
HEAD_DIM = 128
H_A = D_MODEL // (2 * HEAD_DIM)
H_B = D_MODEL // (2 * HEAD_DIM)
H_C = D_MODEL // HEAD_DIM
DA_HALF = HEAD_DIM // 2
WA = H_A * HEAD_DIM
WB = H_B * HEAD_DIM
AB_IN = 3 * WA + 3 * WB + H_B
AB_OUT = WA + WB
C_W = H_C * HEAD_DIM
N_T5_BUCKETS = 32
T5_MAX_DIST = 128
C_PREV_CHUNKS = 8
C_BAND = C_PREV_CHUNKS * CHUNK
REL_CLIP = 128
D_FF = ((8 * D_MODEL // 3 + 127) // 128) * 128
N_AB_LAYERS = (DEPTH + 1) // 2
N_C_LAYERS = DEPTH // 2
FORGET_BIAS_INIT = 2.0
EPS = 1e-6
NEG_INF = -1e30

kernel_name = 'hybrid_streaming_encoder_step'


def rmsnorm(x, g):
    xf = x.astype(jnp.float32)
    y = xf * lax.rsqrt(jnp.mean(xf * xf, axis=-1, keepdims=True) + EPS)
    return (y * g.astype(jnp.float32)).astype(x.dtype)


def modulate(x, g, shift, scale):
    return rmsnorm(x, g) * (1.0 + scale[:, None, :]) + shift[:, None, :]


def ada_mods(c, w, b):
    m = jax.nn.silu(c) @ w + b
    s = jnp.split(m, 9, axis=-1)
    return s[0:3], s[3:6], s[6:9]


def swiglu(h, w_in, w_out):
    g, u = jnp.split(h @ w_in, 2, axis=-1)
    return (jax.nn.silu(g) * u) @ w_out


def ffn_residual(x, mod, g, w_in, w_out):
    shift, scale, gate = mod
    return x + 0.5 * gate[:, None, :] * swiglu(modulate(x, g, shift, scale), w_in, w_out)


def to_blocks(a, size):
    b, t = a.shape[0], a.shape[1]
    return jnp.moveaxis(a.reshape((b, t // size, size) + a.shape[2:]), 1, 0)


def from_blocks(a):
    a = jnp.moveaxis(a, 0, 1)
    return a.reshape((a.shape[0], a.shape[1] * a.shape[2]) + a.shape[3:])


def chunk_id(pos):
    return jnp.floor_divide(pos, CHUNK)


def t5_bucket(rel):
    nb = N_T5_BUCKETS // 2
    max_exact = nb // 2
    n = jnp.abs(rel)
    nf = jnp.maximum(n, 1).astype(jnp.float32)
    large = max_exact + (jnp.log(nf / max_exact) / math.log(T5_MAX_DIST / max_exact) * (nb - max_exact)).astype(jnp.int32)
    large = jnp.minimum(large, nb - 1)
    return jnp.where(rel > 0, nb, 0) + jnp.where(n < max_exact, n, large)


def diff_attn(q, k, v, q_pos, k_pos, t5_table, lam, lam_init, subln_g):
    b, tq = q.shape[0], q.shape[1]
    tk = k.shape[1]
    q2 = q.reshape(b, tq, H_A, 2, DA_HALF)
    k2 = k.reshape(b, tk, H_A, 2, DA_HALF)
    s = jnp.einsum('bqhmd,bkhmd->bmhqk', q2, k2).astype(jnp.float32) * (DA_HALF ** -0.5)
    bias = jnp.transpose(t5_table[t5_bucket(k_pos[None, :] - q_pos[:, None])], (2, 0, 1)).astype(jnp.float32)
    mask = chunk_id(k_pos)[None, :] <= chunk_id(q_pos)[:, None]
    p = jax.nn.softmax(jnp.where(mask, s + bias, NEG_INF), axis=-1)
    attn = p[:, 0] - lam * p[:, 1]
    o = jnp.einsum('bhqk,bkhd->bqhd', attn.astype(v.dtype), v)
    o = rmsnorm(o, subln_g) * (1.0 - lam_init)
    return o.reshape(b, tq, WA)


def fox_attn(q, k, v, f_q, f_k, q_pos, k_pos):
    b, tq = q.shape[0], q.shape[1]
    s = jnp.einsum('bqhd,bkhd->bhqk', q, k).astype(jnp.float32) * (HEAD_DIM ** -0.5)
    decay = jnp.transpose(f_q, (0, 2, 1))[:, :, :, None] - jnp.transpose(f_k, (0, 2, 1))[:, :, None, :]
    mask = k_pos[None, :] <= q_pos[:, None]
    p = jax.nn.softmax(jnp.where(mask, s + decay, NEG_INF), axis=-1)
    o = jnp.einsum('bhqk,bkhd->bqhd', p.astype(v.dtype), v)
    return o.reshape(b, tq, WB)


def band_attn(q, k, v, q_pos, k_pos, rel_table):
    b, tq = q.shape[0], q.shape[1]
    s = jnp.einsum('bqhd,bkhd->bhqk', q, k).astype(jnp.float32) * (HEAD_DIM ** -0.5)
    rel = jnp.clip(k_pos[None, :] - q_pos[:, None], -REL_CLIP, REL_CLIP) + REL_CLIP
    bias = jnp.transpose(rel_table[rel], (2, 0, 1)).astype(jnp.float32)
    qc = chunk_id(q_pos)[:, None]
    kc = chunk_id(k_pos)[None, :]
    mask = (kc <= qc) & (qc - kc <= C_PREV_CHUNKS) & (k_pos[None, :] >= 0)
    p = jax.nn.softmax(jnp.where(mask, s + bias, NEG_INF), axis=-1)
    o = jnp.einsum('bhqk,bkhd->bqhd', p.astype(v.dtype), v)
    return o.reshape(b, tq, C_W)


def ab_project(h, w_in, b_f):
    b, t, _ = h.shape
    cuts = [WA, 2 * WA, 3 * WA, 3 * WA + WB, 3 * WA + 2 * WB, 3 * WA + 3 * WB]
    qa, ka, va, qb, kb, vb, f = jnp.split(h @ w_in, cuts, axis=-1)
    sa = (b, t, H_A, HEAD_DIM)
    sb = (b, t, H_B, HEAD_DIM)
    logf = jax.nn.log_sigmoid(f.astype(jnp.float32) + b_f.astype(jnp.float32))
    return qa.reshape(sa), ka.reshape(sa), va.reshape(sa), qb.reshape(sb), kb.reshape(sb), vb.reshape(sb), logf


def ab_mix_prompt(h, w_in, b_f, w_out, lam, lam_init, subln_g, t5_table):
    b, s, _ = h.shape
    qa, ka, va, qb, kb, vb, logf = ab_project(h, w_in, b_f)
    f_cum = jnp.cumsum(logf, axis=1)
    pos = jnp.arange(s)

    def block(args):
        qa_blk, qb_blk, f_blk, pos_blk = args
        oa = diff_attn(qa_blk, ka, va, pos_blk, pos, t5_table, lam, lam_init, subln_g)
        ob = fox_attn(qb_blk, kb, vb, f_blk, f_cum, pos_blk, pos)
        return jnp.concatenate([oa, ob], axis=-1)

    o = lax.map(block, (to_blocks(qa, Q_BLOCK), to_blocks(qb, Q_BLOCK), to_blocks(f_cum, Q_BLOCK), pos.reshape(-1, Q_BLOCK)))
    return from_blocks(o) @ w_out, (ka, va, kb, vb, logf)


def ab_mix_sample(h, ck_a, cv_a, ck_b, cv_b, clogf, w_in, b_f, w_out, lam, lam_init, subln_g, t5_table):
    t = h.shape[1]
    p = clogf.shape[1]
    qa, ka, va, qb, kb, vb, logf = ab_project(h, w_in, b_f)
    ka_all = jnp.concatenate([ck_a, ka], axis=1)
    va_all = jnp.concatenate([cv_a, va], axis=1)
    kb_all = jnp.concatenate([ck_b, kb], axis=1)
    vb_all = jnp.concatenate([cv_b, vb], axis=1)
    f_cum = jnp.cumsum(jnp.concatenate([clogf.astype(jnp.float32), logf], axis=1), axis=1)
    k_pos = jnp.arange(p + t)
    q_pos = p + jnp.arange(t)
    oa = diff_attn(qa, ka_all, va_all, q_pos, k_pos, t5_table, lam, lam_init, subln_g)
    ob = fox_attn(qb, kb_all, vb_all, f_cum[:, p:], f_cum, q_pos, k_pos)
    return jnp.concatenate([oa, ob], axis=-1) @ w_out, (ka, va, kb, vb, logf)


def c_project(h, w_in):
    b, t, _ = h.shape
    q, k, v = jnp.split(h @ w_in, 3, axis=-1)
    shp = (b, t, H_C, HEAD_DIM)
    return q.reshape(shp), k.reshape(shp), v.reshape(shp)


def c_mix_prompt(h, w_in, w_out, rel_table):
    b, s, _ = h.shape
    q, k, v = c_project(h, w_in)
    pad = jnp.zeros((b, C_BAND, H_C, HEAD_DIM), k.dtype)
    kp = jnp.concatenate([pad, k], axis=1)
    vp = jnp.concatenate([pad, v], axis=1)
    band = jnp.arange(C_BAND + CHUNK)

    def chunk_step(args):
        q_c, c = args
        start = c * CHUNK
        k_c = lax.dynamic_slice_in_dim(kp, start, C_BAND + CHUNK, axis=1)
        v_c = lax.dynamic_slice_in_dim(vp, start, C_BAND + CHUNK, axis=1)
        return band_attn(q_c, k_c, v_c, start + jnp.arange(CHUNK), start - C_BAND + band, rel_table)

    o = lax.map(chunk_step, (to_blocks(q, CHUNK), jnp.arange(s // CHUNK)))
    keep = min(C_BAND, s)
    return from_blocks(o) @ w_out, (k[:, s - keep:], v[:, s - keep:])


def c_mix_sample(h, ck, cv, past_len, w_in, w_out, rel_table):
    t = h.shape[1]
    lc = ck.shape[1]
    q, k, v = c_project(h, w_in)
    k_all = jnp.concatenate([ck, k], axis=1)
    v_all = jnp.concatenate([cv, v], axis=1)
    k_pos = past_len - lc + jnp.arange(lc + t)
    q_pos = past_len + jnp.arange(t)
    o = band_attn(q, k_all, v_all, q_pos, k_pos, rel_table)
    return o @ w_out, (k, v)


def setup_inputs(seed: int = 0) -> dict:
    key = jax.random.key(seed)
    ks = jax.random.split(key, 32)

    def nrm(k, shape, s=1.0):
        return s * jax.random.normal(k, shape, jnp.float32)

    inv = D_MODEL ** -0.5
    lc = min(C_BAND, PAST_LEN)
    return {
        'x_prompt': nrm(ks[0], (BATCH, SEQ, D_MODEL)),
        'x_sample': nrm(ks[1], (DEC_BATCH, DEC_SEQ, D_MODEL)),
        'cache_a_k': nrm(ks[2], (N_AB_LAYERS, DEC_BATCH, PAST_LEN, H_A, HEAD_DIM)),
        'cache_a_v': nrm(ks[3], (N_AB_LAYERS, DEC_BATCH, PAST_LEN, H_A, HEAD_DIM)),
        'cache_b_k': nrm(ks[4], (N_AB_LAYERS, DEC_BATCH, PAST_LEN, H_B, HEAD_DIM)),
        'cache_b_v': nrm(ks[5], (N_AB_LAYERS, DEC_BATCH, PAST_LEN, H_B, HEAD_DIM)),
        'cache_b_logf': jax.nn.log_sigmoid(FORGET_BIAS_INIT + nrm(ks[6], (N_AB_LAYERS, DEC_BATCH, PAST_LEN, H_B))),
        'cache_c_k': nrm(ks[7], (N_C_LAYERS, DEC_BATCH, lc, H_C, HEAD_DIM)),
        'cache_c_v': nrm(ks[8], (N_C_LAYERS, DEC_BATCH, lc, H_C, HEAD_DIM)),
        'c_prompt': nrm(ks[9], (BATCH, D_MODEL)),
        'c_sample': nrm(ks[10], (DEC_BATCH, D_MODEL)),
        'w_ada': nrm(ks[11], (DEPTH, D_MODEL, 9 * D_MODEL), 0.5 * inv),
        'b_ada': nrm(ks[12], (DEPTH, 9 * D_MODEL), 0.02),
        'norm_g': 1.0 + nrm(ks[13], (DEPTH, 3, D_MODEL), 0.01),
        'w_ffn_in': nrm(ks[14], (DEPTH, 2, D_MODEL, 2 * D_FF), inv),
        'w_ffn_out': nrm(ks[15], (DEPTH, 2, D_FF, D_MODEL), D_FF ** -0.5),
        'w_in_ab': nrm(ks[16], (N_AB_LAYERS, D_MODEL, AB_IN), inv),
        'b_forget': FORGET_BIAS_INIT + nrm(ks[17], (N_AB_LAYERS, H_B), 0.1),
        'w_out_ab': nrm(ks[18], (N_AB_LAYERS, AB_OUT, D_MODEL), AB_OUT ** -0.5),
        'lambda_q1': nrm(ks[19], (N_AB_LAYERS, DA_HALF), 0.1),
        'lambda_k1': nrm(ks[20], (N_AB_LAYERS, DA_HALF), 0.1),
        'lambda_q2': nrm(ks[21], (N_AB_LAYERS, DA_HALF), 0.1),
        'lambda_k2': nrm(ks[22], (N_AB_LAYERS, DA_HALF), 0.1),
        'subln_g': 1.0 + nrm(ks[23], (N_AB_LAYERS, HEAD_DIM), 0.01),
        't5_table': nrm(ks[24], (N_T5_BUCKETS, H_A), 0.5),
        'w_in_c': nrm(ks[25], (N_C_LAYERS, D_MODEL, 3 * C_W), inv),
        'w_out_c': nrm(ks[26], (N_C_LAYERS, C_W, D_MODEL), C_W ** -0.5),
        'c_rel_bias': nrm(ks[27], (N_C_LAYERS, 2 * REL_CLIP + 1, H_C), 0.5),
        'final_g': 1.0 + nrm(ks[28], (D_MODEL,), 0.01),
    }


def reference(x_prompt, x_sample, cache_a_k, cache_a_v, cache_b_k, cache_b_v, cache_b_logf, cache_c_k, cache_c_v, c_prompt, c_sample, w_ada, b_ada, norm_g, w_ffn_in, w_ffn_out, w_in_ab, b_forget, w_out_ab, lambda_q1, lambda_k1, lambda_q2, lambda_k2, subln_g, t5_table, w_in_c, w_out_c, c_rel_bias, final_g):
    past_len = cache_b_logf.shape[2]
    xp, xs = x_prompt, x_sample
    ab_p, ab_s, c_p, c_s = [], [], [], []
    for l in range(DEPTH):
        i = l // 2
        mp = ada_mods(c_prompt, w_ada[l], b_ada[l])
        ms = ada_mods(c_sample, w_ada[l], b_ada[l])
        xp = ffn_residual(xp, mp[0], norm_g[l, 0], w_ffn_in[l, 0], w_ffn_out[l, 0])
        xs = ffn_residual(xs, ms[0], norm_g[l, 0], w_ffn_in[l, 0], w_ffn_out[l, 0])
        hp = modulate(xp, norm_g[l, 1], mp[1][0], mp[1][1])
        hs = modulate(xs, norm_g[l, 1], ms[1][0], ms[1][1])
        if l % 2 == 0:
            lam_init = 0.8 - 0.6 * math.exp(-0.3 * l)
            lam = (jnp.exp(jnp.sum(lambda_q1[i].astype(jnp.float32) * lambda_k1[i].astype(jnp.float32)))
                   - jnp.exp(jnp.sum(lambda_q2[i].astype(jnp.float32) * lambda_k2[i].astype(jnp.float32))) + lam_init)
            yp, st_p = ab_mix_prompt(hp, w_in_ab[i], b_forget[i], w_out_ab[i], lam, lam_init, subln_g[i], t5_table)
            ys, st_s = ab_mix_sample(hs, cache_a_k[i], cache_a_v[i], cache_b_k[i], cache_b_v[i], cache_b_logf[i],
                                     w_in_ab[i], b_forget[i], w_out_ab[i], lam, lam_init, subln_g[i], t5_table)
            ab_p.append(st_p)
            ab_s.append(st_s)
        else:
            yp, st_p = c_mix_prompt(hp, w_in_c[i], w_out_c[i], c_rel_bias[i])
            ys, st_s = c_mix_sample(hs, cache_c_k[i], cache_c_v[i], past_len, w_in_c[i], w_out_c[i], c_rel_bias[i])
            c_p.append(st_p)
            c_s.append(st_s)
        xp = xp + mp[1][2][:, None, :] * yp
        xs = xs + ms[1][2][:, None, :] * ys
        xp = ffn_residual(xp, mp[2], norm_g[l, 2], w_ffn_in[l, 1], w_ffn_out[l, 1])
        xs = ffn_residual(xs, ms[2], norm_g[l, 2], w_ffn_in[l, 1], w_ffn_out[l, 1])
    y_prompt = rmsnorm(xp, final_g)
    y_sample = rmsnorm(xs, final_g)
    a_k_p = jnp.stack([st[0] for st in ab_p])
    a_v_p = jnp.stack([st[1] for st in ab_p])
    b_k_p = jnp.stack([st[2] for st in ab_p])
    b_v_p = jnp.stack([st[3] for st in ab_p])
    b_f_p = jnp.stack([st[4] for st in ab_p])
    c_k_p = jnp.stack([st[0] for st in c_p])
    c_v_p = jnp.stack([st[1] for st in c_p])
    a_k_s = jnp.stack([st[0] for st in ab_s])
    a_v_s = jnp.stack([st[1] for st in ab_s])
    b_k_s = jnp.stack([st[2] for st in ab_s])
    b_v_s = jnp.stack([st[3] for st in ab_s])
    b_f_s = jnp.stack([st[4] for st in ab_s])
    c_k_s = jnp.stack([st[0] for st in c_s])
    c_v_s = jnp.stack([st[1] for st in c_s])
    return (y_prompt, y_sample, a_k_p, a_v_p, b_k_p, b_v_p, b_f_p, c_k_p, c_v_p, a_k_s, a_v_s, b_k_s, b_v_s, b_f_s, c_k_s, c_v_s)
```

```python
import functools
import math

import numpy as np
import jax
import jax.numpy as jnp
from jax import lax
from jax.experimental import pallas as pl
from jax.experimental.pallas import tpu as pltpu

F32 = jnp.float32
BF16 = jnp.bfloat16

CHUNK = 64
HEAD_DIM = 128
DA_HALF = HEAD_DIM // 2
N_T5_BUCKETS = 32
T5_MAX_DIST = 128
C_PREV_CHUNKS = 8
C_BAND = C_PREV_CHUNKS * CHUNK
REL_CLIP = 128
EPS = 1e-6
NEG_INF = -1e30

VMEM_LIMIT_BYTES = 56 * 1024 * 1024
LANES = 128

ROW_TILE = 512
FF_TILE = 512
PROJ_COL_TILE = 1024
ADA_COL_TILE = 1024
DIFF_TILE = 256
FOX_TILE = 512
BAND_TILE = 256
SAMPLE_NEAR = 256

_NT = (((1,), (1,)), ((), ()))


def _params(*sem):
    return pltpu.CompilerParams(dimension_semantics=sem, vmem_limit_bytes=VMEM_LIMIT_BYTES)


def _modulated_norm(x, g, shift, scale):
    y = x * lax.rsqrt(jnp.mean(x * x, axis=-1, keepdims=True) + EPS) * g
    return y * (1.0 + scale) + shift


def _ada_kernel(c_ref, w_ref, b_ref, o_ref):
    c = c_ref[...]
    a = (c * jax.nn.sigmoid(c)).astype(BF16)
    o_ref[0] = jnp.dot(a, w_ref[0].astype(BF16), preferred_element_type=F32) + b_ref[0]


def _ada_mods(c_all, w_ada, b_ada):
    depth, d, n = w_ada.shape
    r = c_all.shape[0]
    tn = math.gcd(n, ADA_COL_TILE)
    return pl.pallas_call(
        _ada_kernel,
        grid=(depth, n // tn),
        in_specs=[
            pl.BlockSpec((r, d), lambda l, j: (0, 0)),
            pl.BlockSpec((1, d, tn), lambda l, j: (l, 0, j)),
            pl.BlockSpec((1, 1, tn), lambda l, j: (l, 0, j)),
        ],
        out_specs=pl.BlockSpec((1, r, tn), lambda l, j: (l, 0, j)),
        out_shape=jax.ShapeDtypeStruct((depth, r, n), F32),
        compiler_params=_params("arbitrary", "arbitrary"),
        name="ada_mods",
    )(c_all, w_ada, b_ada.reshape(depth, 1, n))


def _row_blocking(x):
    b, t, _ = x.shape
    if t >= ROW_TILE:
        assert t % ROW_TILE == 0
        return 1, ROW_TILE, t // ROW_TILE, b * (t // ROW_TILE)
    assert (b * t) % 8 == 0
    return b, t, 1, 1


def _x_index(nb, tps):
    if nb == 1:
        return lambda i, j: (i // tps, i % tps, 0)
    return lambda i, j: (0, 0, 0)


def _mod_spec(nb, tps, d, slot):
    if nb == 1:
        return pl.BlockSpec((None, 1, 1, d), lambda i, j: (slot, i // tps, 0, 0))
    return pl.BlockSpec((None, nb, 1, d), lambda i, j: (slot, 0, 0, 0))


def _ffn_kernel(*refs, nb, tt, n_ff, final):
    if final:
        (x_ref, shift_ref, scale_ref, gate_ref, g_ref, wg_ref, wu_ref, wo_ref, fg_ref,
         o_ref, h_scr, acc_scr) = refs
    else:
        (x_ref, shift_ref, scale_ref, gate_ref, g_ref, wg_ref, wu_ref, wo_ref,
         o_ref, h_scr, acc_scr) = refs
    f = pl.program_id(1)
    d = x_ref.shape[-1]

    @pl.when(f == 0)
    def _():
        h = _modulated_norm(x_ref[...], g_ref[...], shift_ref[...], scale_ref[...])
        h_scr[...] = h.reshape(nb * tt, d).astype(BF16)

    h = h_scr[...]
    gg = jnp.dot(h, wg_ref[...], preferred_element_type=F32)
    uu = jnp.dot(h, wu_ref[...], preferred_element_type=F32)
    a = (gg * jax.nn.sigmoid(gg) * uu).astype(BF16)
    part = jnp.dot(a, wo_ref[...], preferred_element_type=F32)

    @pl.when(f == 0)
    def _():
        acc_scr[...] = part

    @pl.when(f > 0)
    def _():
        acc_scr[...] += part

    @pl.when(f == n_ff - 1)
    def _():
        y = x_ref[...] + 0.5 * gate_ref[...] * acc_scr[...].reshape(nb, tt, d)
        if final:
            y = y * lax.rsqrt(jnp.mean(y * y, axis=-1, keepdims=True) + EPS) * fg_ref[...]
        o_ref[...] = y


def _ffn(x, mods, slots, g, w_in_p, w_out_p, final_g=None):
    b, t, d = x.shape
    nb, tt, tps, nrows = _row_blocking(x)
    ffp = w_out_p.shape[0]
    n_ff = ffp // FF_TILE
    final = final_g is not None
    in_specs = [
        pl.BlockSpec((nb, tt, d), _x_index(nb, tps)),
        _mod_spec(nb, tps, d, slots[0]),
        _mod_spec(nb, tps, d, slots[1]),
        _mod_spec(nb, tps, d, slots[2]),
        pl.BlockSpec((1, d), lambda i, j: (0, 0)),
        pl.BlockSpec((d, FF_TILE), lambda i, j: (0, j)),
        pl.BlockSpec((d, FF_TILE), lambda i, j: (0, j + n_ff)),
        pl.BlockSpec((FF_TILE, d), lambda i, j: (j, 0)),
    ]
    args = [x, mods, mods, mods, g.reshape(1, d), w_in_p, w_in_p, w_out_p]
    if final:
        in_specs.append(pl.BlockSpec((1, d), lambda i, j: (0, 0)))
        args.append(final_g.reshape(1, d))
    return pl.pallas_call(
        functools.partial(_ffn_kernel, nb=nb, tt=tt, n_ff=n_ff, final=final),
        grid=(nrows, n_ff),
        in_specs=in_specs,
        out_specs=pl.BlockSpec((nb, tt, d), _x_index(nb, tps)),
        out_shape=jax.ShapeDtypeStruct((b, t, d), F32),
        scratch_shapes=[pltpu.VMEM((nb * tt, d), BF16), pltpu.VMEM((nb * tt, d), F32)],
        compiler_params=_params("parallel", "arbitrary"),
        name="ffn",
    )(*args)


def _prep_ffn_weights(w_in, w_out):
    ff = w_out.shape[0]
    ffp = -(-ff // FF_TILE) * FF_TILE
    pad = ffp - ff
    wg = jnp.pad(w_in[:, :ff].astype(BF16), ((0, 0), (0, pad)))
    wu = jnp.pad(w_in[:, ff:].astype(BF16), ((0, 0), (0, pad)))
    return jnp.concatenate([wg, wu], axis=1), jnp.pad(w_out.astype(BF16), ((0, pad), (0, 0)))


def _log_sigmoid(x):
    return jnp.minimum(x, 0.0) - jnp.log1p(jnp.exp(-jnp.abs(x)))


def _proj_kernel(*refs, nb, tt, tn, tps, taps, has_f):
    x_ref, shift_ref, scale_ref, g_ref, w_ref = refs[:5]
    pos = 5
    if has_f:
        wf_ref, bf_ref = refs[5:7]
        pos = 7
    obf_ref = refs[pos]
    tap_refs = refs[pos + 1:pos + 1 + len({tp[0] for tp in taps})]
    pos = pos + 1 + len(tap_refs)
    if has_f:
        logf_ref = refs[pos]
        pos += 1
    h_scr = refs[pos]
    i = pl.program_id(0)
    j = pl.program_id(1)
    d = x_ref.shape[-1]

    @pl.when(j == 0)
    def _():
        h = _modulated_norm(x_ref[...], g_ref[...], shift_ref[...], scale_ref[...])
        h_scr[...] = h.reshape(nb * tt, d).astype(BF16)
        if has_f:
            fr = lax.dot_general(wf_ref[...], h_scr[...], _NT, preferred_element_type=F32)
            logf_ref[...] = _log_sigmoid(fr[:logf_ref.shape[0]] + bf_ref[...])

    y = jnp.dot(h_scr[...], w_ref[...], preferred_element_type=F32)
    obf_ref[...] = y.astype(BF16).reshape(nb, tt, tn)
    for out_idx, jval, col0, last_rows_only in taps:
        cond = j == jval
        if last_rows_only and tps > 1:
            cond = jnp.logical_and(cond, i % tps == tps - 1)

        @pl.when(cond)
        def _(out_idx=out_idx, col0=col0):
            tap_refs[out_idx][:, :, col0:col0 + tn] = y.reshape(nb, tt, tn)


def _seg_taps(tn, width, segments, last_rows_only):
    per = width // tn
    assert per * tn == width
    return [(o, sidx * per + s, s * tn, last_rows_only) for o, sidx in enumerate(segments) for s in range(per)]


def _proj(x, mods, slots, g, w_bf, tn, tap_shapes, taps, wf=None, bf=None):
    b, t, d = x.shape
    n = w_bf.shape[1]
    nb, tt, tps, nrows = _row_blocking(x)
    has_f = wf is not None
    in_specs = [
        pl.BlockSpec((nb, tt, d), _x_index(nb, tps)),
        _mod_spec(nb, tps, d, slots[0]),
        _mod_spec(nb, tps, d, slots[1]),
        pl.BlockSpec((1, d), lambda i, j: (0, 0)),
        pl.BlockSpec((d, tn), lambda i, j: (0, j)),
    ]
    args = [x, mods, mods, g.reshape(1, d), w_bf]
    if has_f:
        in_specs += [pl.BlockSpec(wf.shape, lambda i, j: (0, 0)), pl.BlockSpec(bf.shape, lambda i, j: (0, 0))]
        args += [wf, bf]
    xi = _x_index(nb, tps)
    out_specs = [pl.BlockSpec((nb, tt, tn), lambda i, j: xi(i, j)[:2] + (j,))]
    out_shape = [jax.ShapeDtypeStruct((b, t, n), BF16)]
    for shp, follows_rows in tap_shapes:
        if follows_rows:
            out_specs.append(pl.BlockSpec((nb, tt, shp[2]), xi))
        elif nb == 1:
            out_specs.append(pl.BlockSpec((1, shp[1], shp[2]), lambda i, j: (i // tps, 0, 0)))
        else:
            out_specs.append(pl.BlockSpec(shp, lambda i, j: (0, 0, 0)))
        out_shape.append(jax.ShapeDtypeStruct(shp, F32))
    if has_f:
        nf = bf.shape[0]
        out_specs.append(pl.BlockSpec((nf, nb * tt), lambda i, j: (0, i)))
        out_shape.append(jax.ShapeDtypeStruct((nf, b * t), F32))
    return pl.pallas_call(
        functools.partial(_proj_kernel, nb=nb, tt=tt, tn=tn, tps=tps, taps=tuple(taps), has_f=has_f),
        grid=(nrows, n // tn),
        in_specs=in_specs,
        out_specs=out_specs,
        out_shape=out_shape,
        scratch_shapes=[pltpu.VMEM((nb * tt, d), BF16)],
        compiler_params=_params("arbitrary", "arbitrary"),
        name="mixer_in_proj",
    )(*args)


def _outproj_kernel(*refs, nb, tt, n_in):
    o_refs = refs[:n_in]
    w_ref, x_ref, gate_ref, out_ref = refs[n_in:]
    d = x_ref.shape[-1]
    acc = None
    row0 = 0
    for o_ref in o_refs:
        wd = o_ref.shape[-1]
        part = jnp.dot(o_ref[...].reshape(nb * tt, wd), w_ref[row0:row0 + wd, :], preferred_element_type=F32)
        acc = part if acc is None else acc + part
        row0 += wd
    out_ref[...] = x_ref[...] + gate_ref[...] * acc.reshape(nb, tt, d)


def _outproj(o_list, w_bf, x, mods, gate_slot):
    b, t, d = x.shape
    nb, tt, tps, nrows = _row_blocking(x)
    xi = _x_index(nb, tps)
    in_specs = [pl.BlockSpec((nb, tt, o.shape[-1]), xi) for o in o_list]
    in_specs += [
        pl.BlockSpec(w_bf.shape, lambda i, j: (0, 0)),
        pl.BlockSpec((nb, tt, d), xi),
        _mod_spec(nb, tps, d, gate_slot),
    ]
    return pl.pallas_call(
        functools.partial(_outproj_kernel, nb=nb, tt=tt, n_in=len(o_list)),
        grid=(nrows, 1),
        in_specs=in_specs,
        out_specs=pl.BlockSpec((nb, tt, d), xi),
        out_shape=jax.ShapeDtypeStruct((b, t, d), F32),
        compiler_params=_params("parallel", "arbitrary"),
        name="mixer_out_proj",
    )(*o_list, w_bf, x, mods)


def _cumsum_kernel(x_ref, c_ref, o_ref):
    x = x_ref[0]
    n = x.shape[1]
    lane = lax.broadcasted_iota(jnp.int32, x.shape, 1)
    s = 1
    while s < n:
        x = x + jnp.where(lane >= s, pltpu.roll(x, s, axis=1), 0.0)
        s *= 2
    o_ref[0] = x + c_ref[0][:, :1]


def _cumsum(x, carry):
    g, h, n = x.shape
    return pl.pallas_call(
        _cumsum_kernel,
        grid=(g,),
        in_specs=[pl.BlockSpec((1, h, n), lambda i: (i, 0, 0)), pl.BlockSpec((1, h, LANES), lambda i: (i, 0, 0))],
        out_specs=pl.BlockSpec((1, h, n), lambda i: (i, 0, 0)),
        out_shape=jax.ShapeDtypeStruct((g, h, n), F32),
        compiler_params=_params("arbitrary"),
        name="logf_cumsum",
    )(x, carry)


def _softmax_init(m_scr, l_scr, acc_scr):
    m_scr[...] = jnp.full(m_scr.shape, NEG_INF, F32)
    l_scr[...] = jnp.zeros(l_scr.shape, F32)
    acc_scr[...] = jnp.zeros(acc_scr.shape, F32)


def _softmax_step(s, v, m_scr, l_scr, acc_scr):
    m_prev = m_scr[...]
    m_new = jnp.maximum(m_prev, jnp.max(s, axis=1, keepdims=True))
    alpha = jnp.exp(m_prev - m_new)
    p = jnp.exp(s - m_new)
    l_scr[...] = alpha * l_scr[...] + jnp.sum(p, axis=1, keepdims=True)
    acc_scr[...] = alpha * acc_scr[...] + jnp.dot(p.astype(BF16), v, preferred_element_type=F32)
    m_scr[...] = m_new


def _stack_diff_queries(q):
    lane = lax.broadcasted_iota(jnp.int32, q.shape, 1)
    zero = jnp.zeros_like(q)
    qs = jnp.concatenate([jnp.where(lane < DA_HALF, q, zero), jnp.where(lane >= DA_HALF, q, zero)], axis=0)
    return qs * jnp.asarray(DA_HALF ** -0.5, BF16)


def _diff_finalize(o, lam, g, out_scale):
    tq = o.shape[0] // 2
    od = o[:tq] - lam * o[tq:]
    return od * lax.rsqrt(jnp.mean(od * od, axis=-1, keepdims=True) + EPS) * g * out_scale


def _diff_p_kernel(lam_ref, q_ref, k_ref, v_ref, bias_ref, g_ref, o_ref, m_scr, l_scr, acc_scr, *, tq, out_scale):
    qi = pl.program_id(2)
    qs = _stack_diff_queries(q_ref[0])
    _softmax_init(m_scr, l_scr, acc_scr)

    def tile(j, bias):
        start = pl.multiple_of(j * tq, tq)
        k = k_ref[0, pl.ds(start, tq), :]
        v = v_ref[0, pl.ds(start, tq), :]
        s = lax.dot_general(qs, k, _NT, preferred_element_type=F32)
        if bias is not None:
            s = s + jnp.concatenate([bias, bias], axis=0)
        _softmax_step(s, v, m_scr, l_scr, acc_scr)

    def far(j, carry):
        tile(j, None)
        return carry

    lax.fori_loop(0, jnp.maximum(qi - 1, 0), far, 0)

    @pl.when(qi >= 1)
    def _():
        tile(qi - 1, bias_ref[0, 1])

    tile(qi, bias_ref[0, 0])
    o = acc_scr[...] / l_scr[...]
    o_ref[0] = _diff_finalize(o, lam_ref[0], g_ref[...], out_scale).astype(BF16)


def _diff_prompt(lam, qkv, bias, g, n_heads, out_scale):
    b, t, _ = qkv.shape
    tq = DIFF_TILE
    return pl.pallas_call(
        functools.partial(_diff_p_kernel, tq=tq, out_scale=out_scale),
        grid=(b, n_heads, t // tq),
        in_specs=[
            pl.BlockSpec(memory_space=pltpu.SMEM),
            pl.BlockSpec((1, tq, HEAD_DIM), lambda bi, h, qi: (bi, qi, h)),
            pl.BlockSpec((1, t, HEAD_DIM), lambda bi, h, qi: (bi, 0, n_heads + h)),
            pl.BlockSpec((1, t, HEAD_DIM), lambda bi, h, qi: (bi, 0, 2 * n_heads + h)),
            pl.BlockSpec((1, 2, tq, tq), lambda bi, h, qi: (h, 0, 0, 0)),
            pl.BlockSpec((1, HEAD_DIM), lambda bi, h, qi: (0, 0)),
        ],
        out_specs=pl.BlockSpec((1, tq, HEAD_DIM), lambda bi, h, qi: (bi, qi, h)),
        out_shape=jax.ShapeDtypeStruct((b, t, n_heads * HEAD_DIM), BF16),
        scratch_shapes=[pltpu.VMEM((2 * tq, 1), F32), pltpu.VMEM((2 * tq, 1), F32), pltpu.VMEM((2 * tq, HEAD_DIM), F32)],
        compiler_params=_params("parallel", "parallel", "arbitrary"),
        name="diff_attn_prompt",
    )(lam, qkv, qkv, qkv, bias, g)


def _fox_p_kernel(fref_ref, q_ref, k_ref, v_ref, f_ref, o_ref, m_scr, l_scr, acc_scr, *, tq):
    bi = pl.program_id(0)
    h = pl.program_id(1)
    qi = pl.program_id(2)
    q = q_ref[0]
    fref = fref_ref[(bi * pl.num_programs(1) + h) * pl.num_programs(2) + qi]
    _softmax_init(m_scr, l_scr, acc_scr)

    def tile(j, causal):
        start = pl.multiple_of(j * tq, tq)
        k = k_ref[0, pl.ds(start, tq), :]
        v = v_ref[0, pl.ds(start, tq), :]
        s = lax.dot_general(q, k, _NT, preferred_element_type=F32) * (HEAD_DIM ** -0.5)
        s = s + (fref - f_ref[0, 0, j])
        if causal:
            row = lax.broadcasted_iota(jnp.int32, s.shape, 0)
            col = lax.broadcasted_iota(jnp.int32, s.shape, 1)
            s = jnp.where(col <= row, s, NEG_INF)
        _softmax_step(s, v, m_scr, l_scr, acc_scr)

    def far(j, carry):
        tile(j, False)
        return carry

    lax.fori_loop(0, qi, far, 0)
    tile(qi, True)
    o_ref[0] = (acc_scr[...] / l_scr[...]).astype(BF16)


def _fox_prompt(fref, qkv, fcum, n_heads, col0):
    b, t, _ = qkv.shape
    tq = FOX_TILE
    nq = t // tq
    return pl.pallas_call(
        functools.partial(_fox_p_kernel, tq=tq),
        grid=(b, n_heads, nq),
        in_specs=[
            pl.BlockSpec(memory_space=pltpu.SMEM),
            pl.BlockSpec((1, tq, HEAD_DIM), lambda bi, h, qi: (bi, qi, col0 + h)),
            pl.BlockSpec((1, t, HEAD_DIM), lambda bi, h, qi: (bi, 0, col0 + n_heads + h)),
            pl.BlockSpec((1, t, HEAD_DIM), lambda bi, h, qi: (bi, 0, col0 + 2 * n_heads + h)),
            pl.BlockSpec((1, 1, nq, 1, tq), lambda bi, h, qi: (bi, h, 0, 0, 0)),
        ],
        out_specs=pl.BlockSpec((1, tq, HEAD_DIM), lambda bi, h, qi: (bi, qi, h)),
        out_shape=jax.ShapeDtypeStruct((b, t, n_heads * HEAD_DIM), BF16),
        scratch_shapes=[pltpu.VMEM((tq, 1), F32), pltpu.VMEM((tq, 1), F32), pltpu.VMEM((tq, HEAD_DIM), F32)],
        compiler_params=_params("parallel", "parallel", "arbitrary"),
        name="fox_attn_prompt",
    )(fref, qkv, qkv, qkv, fcum)


def _joint_softmax_pv(s_c, s_n, vc, vn):
    m = jnp.maximum(jnp.max(s_c, axis=1, keepdims=True), jnp.max(s_n, axis=1, keepdims=True))
    p_c = jnp.exp(s_c - m)
    p_n = jnp.exp(s_n - m)
    l = jnp.sum(p_c, axis=1, keepdims=True) + jnp.sum(p_n, axis=1, keepdims=True)
    acc = jnp.dot(p_c.astype(BF16), vc, preferred_element_type=F32)
    acc = acc + jnp.dot(p_n.astype(BF16), vn, preferred_element_type=F32)
    return acc / l


def _diff_s_kernel(lam_ref, q_ref, kn_ref, vn_ref, kc_ref, vc_ref, bnear_ref, bnew_ref, g_ref, o_ref, *, near, out_scale):
    qs = _stack_diff_queries(q_ref[0])
    kc = kc_ref[0].astype(BF16)
    vc = vc_ref[0].astype(BF16)
    p = kc.shape[0]
    s_c = lax.dot_general(qs, kc, _NT, preferred_element_type=F32)
    bnear = bnear_ref[0]
    s_c = jnp.concatenate([s_c[:, :p - near], s_c[:, p - near:] + jnp.concatenate([bnear, bnear], axis=0)], axis=1)
    bnew = bnew_ref[0]
    s_n = lax.dot_general(qs, kn_ref[0], _NT, preferred_element_type=F32) + jnp.concatenate([bnew, bnew], axis=0)
    o = _joint_softmax_pv(s_c, s_n, vc, vn_ref[0])
    o_ref[0] = _diff_finalize(o, lam_ref[0], g_ref[...], out_scale).astype(BF16)


def _diff_sample(lam, qkv, kc, vc, bnear, bnew, g, n_heads, out_scale):
    b, t, _ = qkv.shape
    p = kc.shape[1]
    near = bnear.shape[-1]
    hd = HEAD_DIM
    return pl.pallas_call(
        functools.partial(_diff_s_kernel, near=near, out_scale=out_scale),
        grid=(b, n_heads),
        in_specs=[
            pl.BlockSpec(memory_space=pltpu.SMEM),
            pl.BlockSpec((1, t, hd), lambda bi, h: (bi, 0, h)),
            pl.BlockSpec((1, t, hd), lambda bi, h: (bi, 0, n_heads + h)),
            pl.BlockSpec((1, t, hd), lambda bi, h: (bi, 0, 2 * n_heads + h)),
            pl.BlockSpec((1, p, hd), lambda bi, h: (bi, 0, h)),
            pl.BlockSpec((1, p, hd), lambda bi, h: (bi, 0, h)),
            pl.BlockSpec((1, t, near), lambda bi, h: (h, 0, 0)),
            pl.BlockSpec((1, t, t), lambda bi, h: (h, 0, 0)),
            pl.BlockSpec((1, hd), lambda bi, h: (0, 0)),
        ],
        out_specs=pl.BlockSpec((1, t, hd), lambda bi, h: (bi, 0, h)),
        out_shape=jax.ShapeDtypeStruct((b, t, n_heads * hd), BF16),
        compiler_params=_params("parallel", "parallel"),
        name="diff_attn_sample",
    )(lam, qkv, qkv, qkv, kc, vc, bnear, bnew, g)


def _fox_s_kernel(fref_ref, q_ref, kn_ref, vn_ref, kc_ref, vc_ref, fc_ref, fn_ref, o_ref):
    bi = pl.program_id(0)
    h = pl.program_id(1)
    q = q_ref[0]
    t = q.shape[0]
    fref = fref_ref[bi * pl.num_programs(1) + h]
    scale = HEAD_DIM ** -0.5
    s_c = lax.dot_general(q, kc_ref[0].astype(BF16), _NT, preferred_element_type=F32) * scale + (fref - fc_ref[0, 0])
    s_n = lax.dot_general(q, kn_ref[0], _NT, preferred_element_type=F32) * scale + (fref - fn_ref[0, 0][:, :t])
    row = lax.broadcasted_iota(jnp.int32, s_n.shape, 0)
    col = lax.broadcasted_iota(jnp.int32, s_n.shape, 1)
    s_n = jnp.where(col <= row, s_n, NEG_INF)
    o_ref[0] = _joint_softmax_pv(s_c, s_n, vc_ref[0].astype(BF16), vn_ref[0]).astype(BF16)


def _fox_sample(fref, qkv, kc, vc, fc, fn, n_heads, col0):
    b, t, _ = qkv.shape
    p = kc.shape[1]
    hd = HEAD_DIM
    return pl.pallas_call(
        _fox_s_kernel,
        grid=(b, n_heads),
        in_specs=[
            pl.BlockSpec(memory_space=pltpu.SMEM),
            pl.BlockSpec((1, t, hd), lambda bi, h: (bi, 0, col0 + h)),
            pl.BlockSpec((1, t, hd), lambda bi, h: (bi, 0, col0 + n_heads + h)),
            pl.BlockSpec((1, t, hd), lambda bi, h: (bi, 0, col0 + 2 * n_heads + h)),
            pl.BlockSpec((1, p, hd), lambda bi, h: (bi, 0, h)),
            pl.BlockSpec((1, p, hd), lambda bi, h: (bi, 0, h)),
            pl.BlockSpec((1, 1, 1, p), lambda bi, h: (bi, h, 0, 0)),
            pl.BlockSpec((1, 1, 1, fn.shape[-1]), lambda bi, h: (bi, h, 0, 0)),
        ],
        out_specs=pl.BlockSpec((1, t, hd), lambda bi, h: (bi, 0, h)),
        out_shape=jax.ShapeDtypeStruct((b, t, n_heads * hd), BF16),
        compiler_params=_params("parallel", "parallel"),
        name="fox_attn_sample",
    )(fref, qkv, qkv, qkv, kc, vc, fc, fn)


def _band_p_kernel(q_ref, k_ref, v_ref, bm_ref, o_ref, *, tq, nkb):
    i = pl.program_id(2)
    q = q_ref[0]
    scale = HEAD_DIM ** -0.5
    ss, vs = [], []
    for blk in range(nkb):
        jb = i - (nkb - 1) + blk
        start = pl.multiple_of(jnp.maximum(jb, 0) * tq, tq)
        k = k_ref[0, pl.ds(start, tq), :]
        s = lax.dot_general(q, k, _NT, preferred_element_type=F32) * scale + bm_ref[0, :, blk * tq:(blk + 1) * tq]
        if blk < nkb - 1:
            s = s + jnp.where(jb < 0, NEG_INF, 0.0)
        ss.append(s)
        vs.append(v_ref[0, pl.ds(start, tq), :])
    m = functools.reduce(jnp.maximum, [jnp.max(s, axis=1, keepdims=True) for s in ss])
    l = None
    acc = None
    for s, v in zip(ss, vs):
        p = jnp.exp(s - m)
        ls = jnp.sum(p, axis=1, keepdims=True)
        a = jnp.dot(p.astype(BF16), v, preferred_element_type=F32)
        l = ls if l is None else l + ls
        acc = a if acc is None else acc + a
    o_ref[0] = (acc / l).astype(BF16)


def _band_prompt(qkv, bm, n_heads):
    b, t, _ = qkv.shape
    tq = BAND_TILE
    nkb = bm.shape[-1] // tq
    return pl.pallas_call(
        functools.partial(_band_p_kernel, tq=tq, nkb=nkb),
        grid=(b, n_heads, t // tq),
        in_specs=[
            pl.BlockSpec((1, tq, HEAD_DIM), lambda bi, h, i: (bi, i, h)),
            pl.BlockSpec((1, t, HEAD_DIM), lambda bi, h, i: (bi, 0, n_heads + h)),
            pl.BlockSpec((1, t, HEAD_DIM), lambda bi, h, i: (bi, 0, 2 * n_heads + h)),
            pl.BlockSpec((1, tq, nkb * tq), lambda bi, h, i: (h, 0, 0)),
        ],
        out_specs=pl.BlockSpec((1, tq, HEAD_DIM), lambda bi, h, i: (bi, i, h)),
        out_shape=jax.ShapeDtypeStruct((b, t, n_heads * HEAD_DIM), BF16),
        compiler_params=_params("parallel", "parallel", "arbitrary"),
        name="band_attn_prompt",
    )(qkv, qkv, qkv, bm)


def _band_s_kernel(q_ref, kn_ref, vn_ref, kc_ref, vc_ref, bmc_ref, bmn_ref, o_ref):
    q = q_ref[0]
    scale = HEAD_DIM ** -0.5
    s_c = lax.dot_general(q, kc_ref[0].astype(BF16), _NT, preferred_element_type=F32) * scale + bmc_ref[0]
    s_n = lax.dot_general(q, kn_ref[0], _NT, preferred_element_type=F32) * scale + bmn_ref[0]
    o_ref[0] = _joint_softmax_pv(s_c, s_n, vc_ref[0].astype(BF16), vn_ref[0]).astype(BF16)


def _band_sample(qkv, kc, vc, bmc, bmn, n_heads):
    b, t, _ = qkv.shape
    lc = kc.shape[1]
    hd = HEAD_DIM
    return pl.pallas_call(
        _band_s_kernel,
        grid=(b, n_heads),
        in_specs=[
            pl.BlockSpec((1, t, hd), lambda bi, h: (bi, 0, h)),
            pl.BlockSpec((1, t, hd), lambda bi, h: (bi, 0, n_heads + h)),
            pl.BlockSpec((1, t, hd), lambda bi, h: (bi, 0, 2 * n_heads + h)),
            pl.BlockSpec((1, lc, hd), lambda bi, h: (bi, 0, h)),
            pl.BlockSpec((1, lc, hd), lambda bi, h: (bi, 0, h)),
            pl.BlockSpec((1, t, lc), lambda bi, h: (h, 0, 0)),
            pl.BlockSpec((1, t, t), lambda bi, h: (h, 0, 0)),
        ],
        out_specs=pl.BlockSpec((1, t, hd), lambda bi, h: (bi, 0, h)),
        out_shape=jax.ShapeDtypeStruct((b, t, n_heads * hd), BF16),
        compiler_params=_params("parallel", "parallel"),
        name="band_attn_sample",
    )(qkv, qkv, qkv, kc, vc, bmc, bmn)


def _t5_bucket_np(rel):
    nb = N_T5_BUCKETS // 2
    max_exact = nb // 2
    n = np.abs(rel)
    nf = np.maximum(n, 1).astype(np.float64)
    large = max_exact + (np.log(nf / max_exact) / math.log(T5_MAX_DIST / max_exact) * (nb - max_exact)).astype(np.int64)
    large = np.minimum(large, nb - 1)
    return np.where(rel > 0, nb, 0) + np.where(n < max_exact, n, large)


def _t5_bias_tiles(t5_table, q_pos, k_pos, far_bucket):
    rel = k_pos[None, :] - q_pos[:, None]
    mask = (k_pos[None, :] // CHUNK) <= (q_pos[:, None] // CHUNK)
    bias = jnp.transpose(t5_table.astype(F32)[_t5_bucket_np(rel)], (2, 0, 1))
    bias = bias - t5_table.astype(F32)[far_bucket][:, None, None]
    return jnp.where(mask[None], bias, NEG_INF)


def _band_bias_tiles(rel_table, q_pos, k_pos):
    rel = np.clip(k_pos[None, :] - q_pos[:, None], -REL_CLIP, REL_CLIP) + REL_CLIP
    qc = q_pos[:, None] // CHUNK
    kc = k_pos[None, :] // CHUNK
    mask = (kc <= qc) & (qc - kc <= C_PREV_CHUNKS) & (k_pos[None, :] >= 0)
    bias = jnp.transpose(rel_table.astype(F32)[rel], (2, 0, 1))
    return jnp.where(mask[None], bias, NEG_INF)


def kernel(x_prompt, x_sample, cache_a_k, cache_a_v, cache_b_k, cache_b_v, cache_b_logf, cache_c_k, cache_c_v, c_prompt, c_sample, w_ada, b_ada, norm_g, w_ffn_in, w_ffn_out, w_in_ab, b_forget, w_out_ab, lambda_q1, lambda_k1, lambda_q2, lambda_k2, subln_g, t5_table, w_in_c, w_out_c, c_rel_bias, final_g):
    depth = w_ada.shape[0]
    bsz, seq, d = x_prompt.shape
    dbsz, dseq, _ = x_sample.shape
    past = cache_b_logf.shape[2]
    h_a = cache_a_k.shape[3]
    h_b = cache_b_k.shape[3]
    h_c = cache_c_k.shape[3]
    wa, wb, cw = h_a * HEAD_DIM, h_b * HEAD_DIM, h_c * HEAD_DIM
    assert bsz == dbsz and wa == wb and 3 * wa + 3 * wb == 3 * cw
    assert DIFF_TILE >= T5_MAX_DIST and SAMPLE_NEAR >= T5_MAX_DIST and past >= SAMPLE_NEAR
    far_bucket = (N_T5_BUCKETS // 2) - 1

    mods = _ada_mods(jnp.concatenate([c_prompt, c_sample], axis=0), w_ada, b_ada)
    mods = mods.reshape(depth, 2, bsz, 9, d).transpose(0, 3, 1, 2, 4).reshape(depth * 9 * 2, bsz, 1, d)

    def slot(l, k, grp):
        return (l * 9 + k) * 2 + grp

    xs = [x_prompt, x_sample]
    ab_states = [[], []]
    c_states = [[], []]
    for l in range(depth):
        i = l // 2
        last = l == depth - 1
        w1 = _prep_ffn_weights(w_ffn_in[l, 0], w_ffn_out[l, 0])
        w2 = _prep_ffn_weights(w_ffn_in[l, 1], w_ffn_out[l, 1])
        for grp in range(2):
            xs[grp] = _ffn(xs[grp], mods, [slot(l, k, grp) for k in range(3)], norm_g[l, 0], *w1)

        if l % 2 == 0:
            lam_init = 0.8 - 0.6 * math.exp(-0.3 * l)
            lam = (jnp.exp(jnp.sum(lambda_q1[i].astype(F32) * lambda_k1[i].astype(F32)))
                   - jnp.exp(jnp.sum(lambda_q2[i].astype(F32) * lambda_k2[i].astype(F32))) + lam_init).reshape(1)
            n_main = 3 * wa + 3 * wb
            w_main = w_in_ab[i][:, :n_main].astype(BF16)
            wf = jnp.pad(w_in_ab[i][:, n_main:].T.astype(BF16), ((0, 16 - h_b), (0, 0)))
            bfo = b_forget[i].astype(F32).reshape(h_b, 1)
            w_out = w_out_ab[i].astype(BF16)
            g_sub = subln_g[i].astype(F32).reshape(1, HEAD_DIM)
            tn = min(PROJ_COL_TILE, wa)
            taps = _seg_taps(tn, wa, (1, 2, 4, 5), False)
            for grp in range(2):
                x = xs[grp]
                b, t, _ = x.shape
                tap_shapes = [((b, t, wa), True)] * 4
                qkv, ka, va, kb, vb, logf_t = _proj(
                    x, mods, [slot(l, 3, grp), slot(l, 4, grp)], norm_g[l, 1], w_main, tn, tap_shapes, taps, wf, bfo)
                logf = logf_t.reshape(h_b, b, t).transpose(1, 2, 0)
                ab_states[grp].append((ka.reshape(b, t, h_a, HEAD_DIM), va.reshape(b, t, h_a, HEAD_DIM),
                                       kb.reshape(b, t, h_b, HEAD_DIM), vb.reshape(b, t, h_b, HEAD_DIM), logf))
                logf_bh = logf_t.reshape(h_b, b, t).transpose(1, 0, 2)
                if grp == 0:
                    pos = np.arange(DIFF_TILE)
                    bias = jnp.stack([
                        _t5_bias_tiles(t5_table, pos, pos, far_bucket),
                        _t5_bias_tiles(t5_table, pos + DIFF_TILE, pos, far_bucket)], axis=1)
                    o_a = _diff_prompt(lam, qkv, bias, g_sub, h_a, 1.0 - lam_init)
                    fcum = _cumsum(logf_bh, jnp.zeros((b, h_b, LANES), F32))
                    nq = t // FOX_TILE
                    fref = fcum[:, :, ::FOX_TILE].reshape(-1)
                    o_b = _fox_prompt(fref, qkv, fcum.reshape(b, h_b, nq, 1, FOX_TILE), h_b, 3 * h_a)
                else:
                    q_pos = past + np.arange(t)
                    bnear = _t5_bias_tiles(t5_table, q_pos, past - SAMPLE_NEAR + np.arange(SAMPLE_NEAR), far_bucket)
                    bnew = _t5_bias_tiles(t5_table, q_pos, q_pos, far_bucket)
                    kc = cache_a_k[i].reshape(b, past, wa)
                    vc = cache_a_v[i].reshape(b, past, wa)
                    o_a = _diff_sample(lam, qkv, kc, vc, bnear, bnew, g_sub, h_a, 1.0 - lam_init)
                    fc = _cumsum(cache_b_logf[i].astype(F32).transpose(0, 2, 1), jnp.zeros((b, h_b, LANES), F32))
                    carry = fc[:, :, past - 1:past]
                    fn = _cumsum(jnp.pad(logf_bh, ((0, 0), (0, 0), (0, LANES - t))),
                                 jnp.broadcast_to(carry, (b, h_b, LANES)))
                    o_b = _fox_sample(carry.reshape(-1), qkv, cache_b_k[i].reshape(b, past, wb),
                                      cache_b_v[i].reshape(b, past, wb), fc.reshape(b, h_b, 1, past),
                                      fn.reshape(b, h_b, 1, LANES), h_b, 3 * h_a)
                xs[grp] = _outproj([o_a, o_b], w_out, x, mods, slot(l, 5, grp))
        else:
            w_in = w_in_c[i].astype(BF16)
            w_out = w_out_c[i].astype(BF16)
            tn = min(PROJ_COL_TILE, cw)
            taps = _seg_taps(tn, cw, (1, 2), True)
            for grp in range(2):
                x = xs[grp]
                b, t, _ = x.shape
                keep = min(C_BAND, t)
                assert keep == min(t, ROW_TILE)
                tap_shapes = [((b, keep, cw), False)] * 2
                qkv, k_new, v_new = _proj(
                    x, mods, [slot(l, 3, grp), slot(l, 4, grp)], norm_g[l, 1], w_in, tn, tap_shapes, taps)
                c_states[grp].append((k_new.reshape(b, keep, h_c, HEAD_DIM), v_new.reshape(b, keep, h_c, HEAD_DIM)))
                if grp == 0:
                    nkb = C_BAND // BAND_TILE + 1
                    assert (nkb - 1) * BAND_TILE == C_BAND
                    q_pos = C_BAND + np.arange(BAND_TILE)
                    bm = _band_bias_tiles(c_rel_bias[i], q_pos, np.arange(nkb * BAND_TILE))
                    o_c = _band_prompt(qkv, bm, h_c)
                else:
                    lc = cache_c_k.shape[2]
                    q_pos = past + np.arange(t)
                    bmc = _band_bias_tiles(c_rel_bias[i], q_pos, past - lc + np.arange(lc))
                    bmn = _band_bias_tiles(c_rel_bias[i], q_pos, q_pos)
                    o_c = _band_sample(qkv, cache_c_k[i].reshape(b, lc, cw), cache_c_v[i].reshape(b, lc, cw), bmc, bmn, h_c)
                xs[grp] = _outproj([o_c], w_out, x, mods, slot(l, 5, grp))

        for grp in range(2):
            xs[grp] = _ffn(xs[grp], mods, [slot(l, k, grp) for k in (6, 7, 8)], norm_g[l, 2], *w2,
                           final_g=final_g if last else None)

    outs = [xs[0], xs[1]]
    for grp in range(2):
        st = ab_states[grp]
        outs += [jnp.stack([s[k] for s in st]) for k in range(5)]
        st = c_states[grp]
        outs += [jnp.stack([s[k] for s in st]) for k in range(2)]
    return tuple(outs)
```

```python
import functools
import math

import numpy as np
import jax
import jax.numpy as jnp
from jax import lax
from jax.experimental import pallas as pl
from jax.experimental.pallas import tpu as pltpu

F32 = jnp.float32
BF16 = jnp.bfloat16

CHUNK = 64
HEAD_DIM = 128
DA_HALF = HEAD_DIM // 2
N_T5_BUCKETS = 32
T5_MAX_DIST = 128
C_PREV_CHUNKS = 8
C_BAND = C_PREV_CHUNKS * CHUNK
REL_CLIP = 128
EPS = 1e-6
NEG_INF = -1e30

VMEM_LIMIT_BYTES = 56 * 1024 * 1024
LANES = 128

ROW_TILE = 512
FF_TILE = 512
PROJ_COL_TILE = 1024
ADA_COL_TILE = 1024
BAND_TILE = 256
BAND_HEADS = 2
ATTN_GROUP = 512
SAMPLE_NEAR = 256

_NT = (((1,), (1,)), ((), ()))


def _params(*sem, flags=None):
    return pltpu.CompilerParams(dimension_semantics=sem, vmem_limit_bytes=VMEM_LIMIT_BYTES, flags=flags)


def _modulated_norm(x, g, shift, scale):
    y = x * lax.rsqrt(jnp.mean(x * x, axis=-1, keepdims=True) + EPS) * g
    return y * (1.0 + scale) + shift


def _ada_kernel(c_ref, w_ref, b_ref, o_ref):
    c = c_ref[...]
    a = (c * jax.nn.sigmoid(c)).astype(BF16)
    o_ref[0] = jnp.dot(a, w_ref[0].astype(BF16), preferred_element_type=F32) + b_ref[0]


def _ada_mods(c_all, w_ada, b_ada):
    depth, d, n = w_ada.shape
    r = c_all.shape[0]
    tn = math.gcd(n, ADA_COL_TILE)
    return pl.pallas_call(
        _ada_kernel,
        grid=(depth, n // tn),
        in_specs=[
            pl.BlockSpec((r, d), lambda l, j: (0, 0)),
            pl.BlockSpec((1, d, tn), lambda l, j: (l, 0, j)),
            pl.BlockSpec((1, 1, tn), lambda l, j: (l, 0, j)),
        ],
        out_specs=pl.BlockSpec((1, r, tn), lambda l, j: (l, 0, j)),
        out_shape=jax.ShapeDtypeStruct((depth, r, n), F32),
        compiler_params=_params("arbitrary", "arbitrary"),
        name="ada_mods",
    )(c_all, w_ada, b_ada.reshape(depth, 1, n))


def _row_blocking(x):
    b, t, _ = x.shape
    if t >= ROW_TILE:
        assert t % ROW_TILE == 0
        return 1, ROW_TILE, t // ROW_TILE, b * (t // ROW_TILE)
    assert (b * t) % 8 == 0
    return b, t, 1, 1


def _x_index(nb, tps):
    if nb == 1:
        return lambda i, j: (i // tps, i % tps, 0)
    return lambda i, j: (0, 0, 0)


def _mod_spec(nb, tps, d, slot):
    if nb == 1:
        return pl.BlockSpec((None, 1, 1, d), lambda i, j: (slot, i // tps, 0, 0))
    return pl.BlockSpec((None, nb, 1, d), lambda i, j: (slot, 0, 0, 0))


def _ffn_kernel(*refs, nb, tt, n_ff, final):
    if final:
        (x_ref, shift_ref, scale_ref, gate_ref, g_ref, wg_ref, wu_ref, wo_ref, fg_ref,
         o_ref, h_scr, acc_scr) = refs
    else:
        (x_ref, shift_ref, scale_ref, gate_ref, g_ref, wg_ref, wu_ref, wo_ref,
         o_ref, h_scr, acc_scr) = refs
    f = pl.program_id(1)
    d = x_ref.shape[-1]

    @pl.when(f == 0)
    def _():
        h = _modulated_norm(x_ref[...], g_ref[...], shift_ref[...], scale_ref[...])
        h_scr[...] = h.reshape(nb * tt, d).astype(BF16)

    h = h_scr[...]
    gg = jnp.dot(h, wg_ref[...], preferred_element_type=F32)
    uu = jnp.dot(h, wu_ref[...], preferred_element_type=F32)
    a = (gg * jax.nn.sigmoid(gg) * uu).astype(BF16)
    part = jnp.dot(a, wo_ref[...], preferred_element_type=F32)

    @pl.when(f == 0)
    def _():
        acc_scr[...] = part

    @pl.when(f > 0)
    def _():
        acc_scr[...] += part

    @pl.when(f == n_ff - 1)
    def _():
        y = x_ref[...] + 0.5 * gate_ref[...] * acc_scr[...].reshape(nb, tt, d)
        if final:
            y = y * lax.rsqrt(jnp.mean(y * y, axis=-1, keepdims=True) + EPS) * fg_ref[...]
        o_ref[...] = y


def _ffn(x, mods, slots, g, w_in_p, w_out_p, final_g=None):
    b, t, d = x.shape
    nb, tt, tps, nrows = _row_blocking(x)
    ffp = w_out_p.shape[0]
    n_ff = ffp // FF_TILE
    final = final_g is not None
    in_specs = [
        pl.BlockSpec((nb, tt, d), _x_index(nb, tps)),
        _mod_spec(nb, tps, d, slots[0]),
        _mod_spec(nb, tps, d, slots[1]),
        _mod_spec(nb, tps, d, slots[2]),
        pl.BlockSpec((1, d), lambda i, j: (0, 0)),
        pl.BlockSpec((d, FF_TILE), lambda i, j: (0, j)),
        pl.BlockSpec((d, FF_TILE), lambda i, j: (0, j + n_ff)),
        pl.BlockSpec((FF_TILE, d), lambda i, j: (j, 0)),
    ]
    args = [x, mods, mods, mods, g.reshape(1, d), w_in_p, w_in_p, w_out_p]
    if final:
        in_specs.append(pl.BlockSpec((1, d), lambda i, j: (0, 0)))
        args.append(final_g.reshape(1, d))
    return pl.pallas_call(
        functools.partial(_ffn_kernel, nb=nb, tt=tt, n_ff=n_ff, final=final),
        grid=(nrows, n_ff),
        in_specs=in_specs,
        out_specs=pl.BlockSpec((nb, tt, d), _x_index(nb, tps)),
        out_shape=jax.ShapeDtypeStruct((b, t, d), F32),
        scratch_shapes=[pltpu.VMEM((nb * tt, d), BF16), pltpu.VMEM((nb * tt, d), F32)],
        compiler_params=_params("parallel", "arbitrary"),
        name="ffn",
    )(*args)


def _prep_ffn_weights(w_in, w_out):
    ff = w_out.shape[0]
    ffp = -(-ff // FF_TILE) * FF_TILE
    pad = ffp - ff
    wg = jnp.pad(w_in[:, :ff].astype(BF16), ((0, 0), (0, pad)))
    wu = jnp.pad(w_in[:, ff:].astype(BF16), ((0, 0), (0, pad)))
    return jnp.concatenate([wg, wu], axis=1), jnp.pad(w_out.astype(BF16), ((0, pad), (0, 0)))


def _log_sigmoid(x):
    return jnp.minimum(x, 0.0) - jnp.log1p(jnp.exp(-jnp.abs(x)))


def _proj_kernel(*refs, nb, tt, tn, tps, taps, has_f, vt_cols, vt_tile):
    x_ref, shift_ref, scale_ref, g_ref, w_ref = refs[:5]
    pos = 5
    if has_f:
        wf_ref, bf_ref = refs[5:7]
        pos = 7
    obf_ref = refs[pos]
    tap_refs = refs[pos + 1:pos + 1 + len({tp[0] for tp in taps})]
    pos = pos + 1 + len(tap_refs)
    if vt_cols:
        vt_ref = refs[pos]
        pos += 1
    if has_f:
        logf_ref = refs[pos]
        pos += 1
    h_scr = refs[pos]
    i = pl.program_id(0)
    j = pl.program_id(1)
    d = x_ref.shape[-1]

    @pl.when(j == 0)
    def _():
        h = _modulated_norm(x_ref[...], g_ref[...], shift_ref[...], scale_ref[...])
        h_scr[...] = h.reshape(nb * tt, d).astype(BF16)
        if has_f:
            fr = lax.dot_general(wf_ref[...], h_scr[...], _NT, preferred_element_type=F32)
            logf_ref[...] = _log_sigmoid(fr[:logf_ref.shape[0]] + bf_ref[...])

    y = jnp.dot(h_scr[...], w_ref[...], preferred_element_type=F32)
    obf_ref[...] = y.astype(BF16).reshape(nb, tt, tn)
    for out_idx, jval, col0, last_rows_only in taps:
        cond = j == jval
        if last_rows_only and tps > 1:
            cond = jnp.logical_and(cond, i % tps == tps - 1)

        @pl.when(cond)
        def _(out_idx=out_idx, col0=col0):
            tap_refs[out_idx][:, :, col0:col0 + tn] = y.reshape(nb, tt, tn)

    for jval, head0 in vt_cols:
        @pl.when(j == jval)
        def _(head0=head0):
            for hh in range(tn // HEAD_DIM):
                yt = y[:, hh * HEAD_DIM:(hh + 1) * HEAD_DIM].T.astype(BF16)
                for s in range(tt // vt_tile):
                    vt_ref[0, head0 + hh, s] = yt[:, s * vt_tile:(s + 1) * vt_tile]


def _seg_taps(tn, width, segments, last_rows_only):
    per = width // tn
    assert per * tn == width
    return [(o, sidx * per + s, s * tn, last_rows_only) for o, sidx in enumerate(segments) for s in range(per)]


def _vt_cols(tn, width, segments):
    per = width // tn
    hpb = tn // HEAD_DIM
    return [(sidx * per + s, (o * per + s) * hpb) for o, sidx in enumerate(segments) for s in range(per)]


def _proj(x, mods, slots, g, w_bf, tn, tap_shapes, taps, wf=None, bf=None, vt_cols=(), vt_tile=None):
    b, t, d = x.shape
    n = w_bf.shape[1]
    nb, tt, tps, nrows = _row_blocking(x)
    has_f = wf is not None
    assert not vt_cols or nb == 1
    in_specs = [
        pl.BlockSpec((nb, tt, d), _x_index(nb, tps)),
        _mod_spec(nb, tps, d, slots[0]),
        _mod_spec(nb, tps, d, slots[1]),
        pl.BlockSpec((1, d), lambda i, j: (0, 0)),
        pl.BlockSpec((d, tn), lambda i, j: (0, j)),
    ]
    args = [x, mods, mods, g.reshape(1, d), w_bf]
    if has_f:
        in_specs += [pl.BlockSpec(wf.shape, lambda i, j: (0, 0)), pl.BlockSpec(bf.shape, lambda i, j: (0, 0))]
        args += [wf, bf]
    xi = _x_index(nb, tps)
    out_specs = [pl.BlockSpec((nb, tt, tn), lambda i, j: xi(i, j)[:2] + (j,))]
    out_shape = [jax.ShapeDtypeStruct((b, t, n), BF16)]
    for shp, follows_rows in tap_shapes:
        if follows_rows:
            out_specs.append(pl.BlockSpec((nb, tt, shp[2]), xi))
        elif nb == 1:
            out_specs.append(pl.BlockSpec((1, shp[1], shp[2]), lambda i, j: (i // tps, 0, 0)))
        else:
            out_specs.append(pl.BlockSpec(shp, lambda i, j: (0, 0, 0)))
        out_shape.append(jax.ShapeDtypeStruct(shp, F32))
    if vt_cols:
        n_vh = len(vt_cols) * (tn // HEAD_DIM)
        out_specs.append(pl.BlockSpec((1, n_vh, tt // vt_tile, HEAD_DIM, vt_tile), lambda i, j: (i // tps, 0, i % tps, 0, 0)))
        out_shape.append(jax.ShapeDtypeStruct((b, n_vh, t // vt_tile, HEAD_DIM, vt_tile), BF16))
    if has_f:
        nf = bf.shape[0]
        out_specs.append(pl.BlockSpec((nf, nb * tt), lambda i, j: (0, i)))
        out_shape.append(jax.ShapeDtypeStruct((nf, b * t), F32))
    return pl.pallas_call(
        functools.partial(_proj_kernel, nb=nb, tt=tt, tn=tn, tps=tps, taps=tuple(taps), has_f=has_f,
                          vt_cols=tuple(vt_cols), vt_tile=vt_tile),
        grid=(nrows, n // tn),
        in_specs=in_specs,
        out_specs=out_specs,
        out_shape=out_shape,
        scratch_shapes=[pltpu.VMEM((nb * tt, d), BF16)],
        compiler_params=_params("arbitrary", "arbitrary"),
        name="mixer_in_proj",
    )(*args)


def _outproj_kernel(*refs, nb, tt, n_in):
    o_refs = refs[:n_in]
    w_ref, x_ref, gate_ref, out_ref = refs[n_in:]
    d = x_ref.shape[-1]
    acc = None
    row0 = 0
    for o_ref in o_refs:
        wd = o_ref.shape[-1]
        part = jnp.dot(o_ref[...].reshape(nb * tt, wd), w_ref[row0:row0 + wd, :], preferred_element_type=F32)
        acc = part if acc is None else acc + part
        row0 += wd
    out_ref[...] = x_ref[...] + gate_ref[...] * acc.reshape(nb, tt, d)


def _outproj(o_list, w_bf, x, mods, gate_slot):
    b, t, d = x.shape
    nb, tt, tps, nrows = _row_blocking(x)
    xi = _x_index(nb, tps)
    in_specs = [pl.BlockSpec((nb, tt, o.shape[-1]), xi) for o in o_list]
    in_specs += [
        pl.BlockSpec(w_bf.shape, lambda i, j: (0, 0)),
        pl.BlockSpec((nb, tt, d), xi),
        _mod_spec(nb, tps, d, gate_slot),
    ]
    return pl.pallas_call(
        functools.partial(_outproj_kernel, nb=nb, tt=tt, n_in=len(o_list)),
        grid=(nrows, 1),
        in_specs=in_specs,
        out_specs=pl.BlockSpec((nb, tt, d), xi),
        out_shape=jax.ShapeDtypeStruct((b, t, d), F32),
        compiler_params=_params("parallel", "arbitrary"),
        name="mixer_out_proj",
    )(*o_list, w_bf, x, mods)


def _cumsum_kernel(x_ref, c_ref, o_ref):
    x = x_ref[0]
    n = x.shape[1]
    lane = lax.broadcasted_iota(jnp.int32, x.shape, 1)
    s = 1
    while s < n:
        x = x + jnp.where(lane >= s, pltpu.roll(x, s, axis=1), 0.0)
        s *= 2
    o_ref[0] = x + c_ref[0][:, :1]


def _cumsum(x, carry):
    g, h, n = x.shape
    return pl.pallas_call(
        _cumsum_kernel,
        grid=(g,),
        in_specs=[pl.BlockSpec((1, h, n), lambda i: (i, 0, 0)), pl.BlockSpec((1, h, LANES), lambda i: (i, 0, 0))],
        out_specs=pl.BlockSpec((1, h, n), lambda i: (i, 0, 0)),
        out_shape=jax.ShapeDtypeStruct((g, h, n), F32),
        compiler_params=_params("arbitrary"),
        name="logf_cumsum",
    )(x, carry)


def _softmax_init(m_scr, l_scr, acc_scr):
    m_scr[...] = jnp.full(m_scr.shape, NEG_INF, F32)
    l_scr[...] = jnp.zeros(l_scr.shape, F32)
    acc_scr[...] = jnp.zeros(acc_scr.shape, F32)


def _softmax_update_t(s, vt, m_prev, l_prev, acc_prev):
    m_new = jnp.maximum(m_prev, jnp.max(s, axis=0, keepdims=True))
    alpha = jnp.exp(m_prev - m_new)
    p = jnp.exp(s - m_new)
    l_new = alpha * l_prev + jnp.sum(p, axis=0, keepdims=True)
    acc_new = alpha * acc_prev + jnp.dot(vt, p.astype(BF16), preferred_element_type=F32)
    return m_new, l_new, acc_new


def _grouped_softmax_tile(score_fns, vt, m_scr, l_scr, acc_scr):
    n = len(score_fns)
    g = m_scr.shape[1] // n
    sls = [slice(i * g, (i + 1) * g) for i in range(n)]
    state = [(m_scr[:, sl], l_scr[:, sl], acc_scr[:, sl]) for sl in sls]
    new = []
    s_next = score_fns[0]()
    for i in range(n):
        s_cur = s_next
        if i + 1 < n:
            s_next = score_fns[i + 1]()
        new.append(_softmax_update_t(s_cur, vt, *state[i]))
    for sl, (m_new, l_new, acc_new) in zip(sls, new):
        m_scr[:, sl] = m_new
        l_scr[:, sl] = l_new
        acc_scr[:, sl] = acc_new


def _toeplitz_values(tbl_ref, idx, n_vals, nh, h, ref_row):
    ref_val = tbl_ref[ref_row * nh + h]

    def body(v, acc):
        return jnp.where(idx == v, tbl_ref[v * nh + h] - ref_val, acc)

    return lax.fori_loop(0, n_vals, body, jnp.zeros(idx.shape, F32))


def _stack_diff_queries(q):
    lane = lax.broadcasted_iota(jnp.int32, q.shape, 1)
    zero = jnp.zeros_like(q)
    qs = jnp.concatenate([jnp.where(lane < DA_HALF, q, zero), jnp.where(lane >= DA_HALF, q, zero)], axis=0)
    return qs * jnp.asarray(DA_HALF ** -0.5, BF16)


def _diff_finalize(o, lam, g, out_scale):
    tq = o.shape[0] // 2
    od = o[:tq] - lam * o[tq:]
    return od * lax.rsqrt(jnp.mean(od * od, axis=-1, keepdims=True) + EPS) * g * out_scale


def _diff_p_kernel(lam_ref, tbl_ref, q_ref, k_ref, vt_ref, idx_ref, base_ref, g_ref, o_ref,
                   m_scr, l_scr, acc_scr, bias_scr, *, tq, nh, far_bucket, out_scale):
    h = pl.program_id(0)
    bi = pl.program_id(1)
    qi = pl.program_id(2)
    nblk = tq // LANES

    @pl.when(jnp.logical_and(bi == 0, qi == 0))
    def _():
        t = _toeplitz_values(tbl_ref, idx_ref[...], N_T5_BUCKETS, nh, h, far_bucket)
        bias_scr[...] = base_ref[...]
        for blk in range(nblk):
            sl = slice(blk * LANES, (blk + 1) * LANES)
            bias_scr[0, sl, sl] += t[0]
            if blk >= 1:
                bias_scr[0, (blk - 1) * LANES:blk * LANES, sl] += t[1]
        bias_scr[1, (nblk - 1) * LANES:, :LANES] += t[1]

    qs = _stack_diff_queries(q_ref[0])
    _softmax_init(m_scr, l_scr, acc_scr)

    def tile(j, bias_idx):
        k = k_ref[0, pl.ds(pl.multiple_of(j * tq, tq), tq), :]
        vt = vt_ref[0, 0, j]

        def scores(grp):
            q0 = grp * ATTN_GROUP
            s = lax.dot_general(k, qs[q0:q0 + ATTN_GROUP], _NT, preferred_element_type=F32)
            if bias_idx is not None:
                s = s + bias_scr[bias_idx, :, q0 % tq:q0 % tq + ATTN_GROUP]
            return s

        _grouped_softmax_tile([functools.partial(scores, grp) for grp in range(2 * tq // ATTN_GROUP)],
                              vt, m_scr, l_scr, acc_scr)

    def far(j, carry):
        tile(j, None)
        return carry

    lax.fori_loop(0, jnp.maximum(qi - 1, 0), far, 0)

    @pl.when(qi >= 1)
    def _():
        tile(qi - 1, 1)

    tile(qi, 0)
    o = (acc_scr[...] / l_scr[...]).T
    o_ref[0] = _diff_finalize(o, lam_ref[0], g_ref[...], out_scale).astype(BF16)


def _diff_prompt(lam, tbl, qkv, vt, g, n_heads, out_scale):
    b, t, _ = qkv.shape
    tq = vt.shape[-1]
    far_bucket = N_T5_BUCKETS // 2 - 1
    pos = np.arange(LANES)
    idx = np.stack([_t5_bucket_np(pos[:, None] - pos[None, :]),
                    _t5_bucket_np(pos[:, None] - pos[None, :] - LANES)]).astype(np.int32)
    kq = np.arange(tq)
    base = np.zeros((2, tq, tq), np.float32)
    base[0] = np.where((kq[:, None] // CHUNK) <= (kq[None, :] // CHUNK), 0.0, NEG_INF)
    return pl.pallas_call(
        functools.partial(_diff_p_kernel, tq=tq, nh=n_heads, far_bucket=far_bucket, out_scale=out_scale),
        grid=(n_heads, b, t // tq),
        in_specs=[
            pl.BlockSpec(memory_space=pltpu.SMEM),
            pl.BlockSpec(memory_space=pltpu.SMEM),
            pl.BlockSpec((1, tq, HEAD_DIM), lambda h, bi, qi: (bi, qi, h)),
            pl.BlockSpec((1, t, HEAD_DIM), lambda h, bi, qi: (bi, 0, n_heads + h)),
            pl.BlockSpec((1, 1, t // tq, HEAD_DIM, tq), lambda h, bi, qi: (bi, h, 0, 0, 0)),
            pl.BlockSpec(idx.shape, lambda h, bi, qi: (0, 0, 0)),
            pl.BlockSpec(base.shape, lambda h, bi, qi: (0, 0, 0)),
            pl.BlockSpec((1, HEAD_DIM), lambda h, bi, qi: (0, 0)),
        ],
        out_specs=pl.BlockSpec((1, tq, HEAD_DIM), lambda h, bi, qi: (bi, qi, h)),
        out_shape=jax.ShapeDtypeStruct((b, t, n_heads * HEAD_DIM), BF16),
        scratch_shapes=[pltpu.VMEM((1, 2 * tq), F32), pltpu.VMEM((1, 2 * tq), F32),
                        pltpu.VMEM((HEAD_DIM, 2 * tq), F32), pltpu.VMEM((2, tq, tq), F32)],
        compiler_params=_params("arbitrary", "arbitrary", "arbitrary"),
        name="diff_attn_prompt",
    )(lam, tbl, qkv, qkv, vt, jnp.asarray(idx), jnp.asarray(base), g)


def _fox_p_kernel(fref_ref, q_ref, k_ref, vt_ref, f_ref, o_ref, m_scr, l_scr, acc_scr, fcol_scr, *, tq):
    bi = pl.program_id(0)
    h = pl.program_id(1)
    qi = pl.program_id(2)
    nblk = tq // LANES

    @pl.when(qi == 0)
    def _():
        for c in range(f_ref.shape[2] * nblk):
            row = f_ref[0, 0, c // nblk][:, (c % nblk) * LANES:(c % nblk + 1) * LANES]
            fcol_scr[c * LANES:(c + 1) * LANES, :] = jnp.broadcast_to(row, (LANES, LANES)).T

    q = q_ref[0]
    fref = fref_ref[(bi * pl.num_programs(1) + h) * pl.num_programs(2) + qi]
    _softmax_init(m_scr, l_scr, acc_scr)

    def tile(j, causal):
        start = pl.multiple_of(j * tq, tq)
        k = k_ref[0, pl.ds(start, tq), :]
        vt = vt_ref[0, 0, j]
        decay = fref - fcol_scr[pl.ds(start, tq), :]
        decay = jnp.concatenate([decay] * (ATTN_GROUP // LANES), axis=1)

        def scores(grp):
            q0 = grp * ATTN_GROUP
            s = lax.dot_general(k, q[q0:q0 + ATTN_GROUP], _NT, preferred_element_type=F32) * (HEAD_DIM ** -0.5) + decay
            if causal:
                key = lax.broadcasted_iota(jnp.int32, s.shape, 0)
                qry = lax.broadcasted_iota(jnp.int32, s.shape, 1) + q0
                s = jnp.where(key <= qry, s, NEG_INF)
            return s

        _grouped_softmax_tile([functools.partial(scores, grp) for grp in range(tq // ATTN_GROUP)],
                              vt, m_scr, l_scr, acc_scr)

    def far(j, carry):
        tile(j, False)
        return carry

    lax.fori_loop(0, qi, far, 0)
    tile(qi, True)
    o_ref[0] = (acc_scr[...] / l_scr[...]).T.astype(BF16)


def _fox_prompt(fref, qkv, vt, fcum, n_heads, col0, vt_head0):
    b, t, _ = qkv.shape
    tq = vt.shape[-1]
    nq = t // tq
    return pl.pallas_call(
        functools.partial(_fox_p_kernel, tq=tq),
        grid=(b, n_heads, nq),
        in_specs=[
            pl.BlockSpec(memory_space=pltpu.SMEM),
            pl.BlockSpec((1, tq, HEAD_DIM), lambda bi, h, qi: (bi, qi, col0 + h)),
            pl.BlockSpec((1, t, HEAD_DIM), lambda bi, h, qi: (bi, 0, col0 + n_heads + h)),
            pl.BlockSpec((1, 1, nq, HEAD_DIM, tq), lambda bi, h, qi: (bi, vt_head0 + h, 0, 0, 0)),
            pl.BlockSpec((1, 1, nq, 1, tq), lambda bi, h, qi: (bi, h, 0, 0, 0)),
        ],
        out_specs=pl.BlockSpec((1, tq, HEAD_DIM), lambda bi, h, qi: (bi, qi, h)),
        out_shape=jax.ShapeDtypeStruct((b, t, n_heads * HEAD_DIM), BF16),
        scratch_shapes=[pltpu.VMEM((1, tq), F32), pltpu.VMEM((1, tq), F32), pltpu.VMEM((HEAD_DIM, tq), F32),
                        pltpu.VMEM((t, LANES), F32)],
        compiler_params=_params("arbitrary", "arbitrary", "arbitrary"),
        name="fox_attn_prompt",
    )(fref, qkv, qkv, vt, fcum)


def _joint_softmax_pv(s_c, s_n, vc, vn):
    m = jnp.maximum(jnp.max(s_c, axis=1, keepdims=True), jnp.max(s_n, axis=1, keepdims=True))
    p_c = jnp.exp(s_c - m)
    p_n = jnp.exp(s_n - m)
    l = jnp.sum(p_c, axis=1, keepdims=True) + jnp.sum(p_n, axis=1, keepdims=True)
    acc = jnp.dot(p_c.astype(BF16), vc, preferred_element_type=F32)
    acc = acc + jnp.dot(p_n.astype(BF16), vn, preferred_element_type=F32)
    return acc / l


def _diff_s_kernel(lam_ref, q_ref, kn_ref, vn_ref, kc_ref, vc_ref, bnear_ref, bnew_ref, g_ref, o_ref, *, near, out_scale):
    qs = _stack_diff_queries(q_ref[0])
    kc = kc_ref[0].astype(BF16)
    vc = vc_ref[0].astype(BF16)
    p = kc.shape[0]
    s_c = lax.dot_general(qs, kc, _NT, preferred_element_type=F32)
    bnear = bnear_ref[0]
    s_c = jnp.concatenate([s_c[:, :p - near], s_c[:, p - near:] + jnp.concatenate([bnear, bnear], axis=0)], axis=1)
    bnew = bnew_ref[0]
    s_n = lax.dot_general(qs, kn_ref[0], _NT, preferred_element_type=F32) + jnp.concatenate([bnew, bnew], axis=0)
    o = _joint_softmax_pv(s_c, s_n, vc, vn_ref[0])
    o_ref[0] = _diff_finalize(o, lam_ref[0], g_ref[...], out_scale).astype(BF16)


def _diff_sample(lam, qkv, kc, vc, bnear, bnew, g, n_heads, out_scale):
    b, t, _ = qkv.shape
    p = kc.shape[1]
    near = bnear.shape[-1]
    hd = HEAD_DIM
    return pl.pallas_call(
        functools.partial(_diff_s_kernel, near=near, out_scale=out_scale),
        grid=(b, n_heads),
        in_specs=[
            pl.BlockSpec(memory_space=pltpu.SMEM),
            pl.BlockSpec((1, t, hd), lambda bi, h: (bi, 0, h)),
            pl.BlockSpec((1, t, hd), lambda bi, h: (bi, 0, n_heads + h)),
            pl.BlockSpec((1, t, hd), lambda bi, h: (bi, 0, 2 * n_heads + h)),
            pl.BlockSpec((1, p, hd), lambda bi, h: (bi, 0, h)),
            pl.BlockSpec((1, p, hd), lambda bi, h: (bi, 0, h)),
            pl.BlockSpec((1, t, near), lambda bi, h: (h, 0, 0)),
            pl.BlockSpec((1, t, t), lambda bi, h: (h, 0, 0)),
            pl.BlockSpec((1, hd), lambda bi, h: (0, 0)),
        ],
        out_specs=pl.BlockSpec((1, t, hd), lambda bi, h: (bi, 0, h)),
        out_shape=jax.ShapeDtypeStruct((b, t, n_heads * hd), BF16),
        compiler_params=_params("parallel", "parallel"),
        name="diff_attn_sample",
    )(lam, qkv, qkv, qkv, kc, vc, bnear, bnew, g)


def _fox_s_kernel(fref_ref, q_ref, kn_ref, vn_ref, kc_ref, vc_ref, fc_ref, fn_ref, o_ref):
    bi = pl.program_id(0)
    h = pl.program_id(1)
    q = q_ref[0]
    t = q.shape[0]
    fref = fref_ref[bi * pl.num_programs(1) + h]
    scale = HEAD_DIM ** -0.5
    s_c = lax.dot_general(q, kc_ref[0].astype(BF16), _NT, preferred_element_type=F32) * scale + (fref - fc_ref[0, 0])
    s_n = lax.dot_general(q, kn_ref[0], _NT, preferred_element_type=F32) * scale + (fref - fn_ref[0, 0][:, :t])
    row = lax.broadcasted_iota(jnp.int32, s_n.shape, 0)
    col = lax.broadcasted_iota(jnp.int32, s_n.shape, 1)
    s_n = jnp.where(col <= row, s_n, NEG_INF)
    o_ref[0] = _joint_softmax_pv(s_c, s_n, vc_ref[0].astype(BF16), vn_ref[0]).astype(BF16)


def _fox_sample(fref, qkv, kc, vc, fc, fn, n_heads, col0):
    b, t, _ = qkv.shape
    p = kc.shape[1]
    hd = HEAD_DIM
    return pl.pallas_call(
        _fox_s_kernel,
        grid=(b, n_heads),
        in_specs=[
            pl.BlockSpec(memory_space=pltpu.SMEM),
            pl.BlockSpec((1, t, hd), lambda bi, h: (bi, 0, col0 + h)),
            pl.BlockSpec((1, t, hd), lambda bi, h: (bi, 0, col0 + n_heads + h)),
            pl.BlockSpec((1, t, hd), lambda bi, h: (bi, 0, col0 + 2 * n_heads + h)),
            pl.BlockSpec((1, p, hd), lambda bi, h: (bi, 0, h)),
            pl.BlockSpec((1, p, hd), lambda bi, h: (bi, 0, h)),
            pl.BlockSpec((1, 1, 1, p), lambda bi, h: (bi, h, 0, 0)),
            pl.BlockSpec((1, 1, 1, fn.shape[-1]), lambda bi, h: (bi, h, 0, 0)),
        ],
        out_specs=pl.BlockSpec((1, t, hd), lambda bi, h: (bi, 0, h)),
        out_shape=jax.ShapeDtypeStruct((b, t, n_heads * hd), BF16),
        compiler_params=_params("parallel", "parallel"),
        name="fox_attn_sample",
    )(fref, qkv, qkv, qkv, kc, vc, fc, fn)


def _band_p_kernel(tbl_ref, q_ref, k_ref, vt_ref, idx_ref, base_ref, o_ref, bm_scr, *, tq, nkb, nh, hps):
    hg = pl.program_id(0)
    bi = pl.program_id(1)
    i = pl.program_id(2)
    nblk = tq // LANES
    qoff = (nkb - 1) * nblk

    @pl.when(jnp.logical_and(bi == 0, i == 0))
    def _():
        for e in range(hps):
            t = _toeplitz_values(tbl_ref, idx_ref[...], 2 * REL_CLIP + 1, nh, hg * hps + e, 0)
            bm_scr[e] = base_ref[...]
            for bq in range(nblk):
                kd = qoff + bq
                bm_scr[e, kd * LANES:(kd + 1) * LANES, bq * LANES:(bq + 1) * LANES] += t[0]
                bm_scr[e, (kd - 1) * LANES:kd * LANES, bq * LANES:(bq + 1) * LANES] += t[1]

    scale = HEAD_DIM ** -0.5
    outs = []
    for e in range(hps):
        hs = slice(e * HEAD_DIM, (e + 1) * HEAD_DIM)
        q = q_ref[0, :, hs]
        ss, vts = [], []
        for blk in range(nkb):
            jb = i - (nkb - 1) + blk
            jc = jnp.maximum(jb, 0)
            k = k_ref[0, pl.ds(pl.multiple_of(jc * tq, tq), tq), hs]
            s = lax.dot_general(k, q, _NT, preferred_element_type=F32) * scale + bm_scr[e, blk * tq:(blk + 1) * tq, :]
            if blk < nkb - 1:
                s = s + jnp.where(jb < 0, NEG_INF, 0.0)
            ss.append(s)
            vts.append(vt_ref[0, e, jc])
        m = functools.reduce(jnp.maximum, [jnp.max(s, axis=0, keepdims=True) for s in ss])
        l = None
        acc = None
        for s, vt in zip(ss, vts):
            p = jnp.exp(s - m)
            ls = jnp.sum(p, axis=0, keepdims=True)
            a = jnp.dot(vt, p.astype(BF16), preferred_element_type=F32)
            l = ls if l is None else l + ls
            acc = a if acc is None else acc + a
        outs.append((acc / l).T.astype(BF16))
    o_ref[0] = jnp.concatenate(outs, axis=1)


def _band_prompt(tbl, qkv, vt, n_heads):
    b, t, _ = qkv.shape
    tq = vt.shape[-1]
    nkb = C_BAND // tq + 1
    assert (nkb - 1) * tq == C_BAND and tq % LANES == 0 and LANES >= REL_CLIP
    pos = np.arange(LANES)
    rel = pos[:, None] - pos[None, :]
    idx = (np.stack([np.clip(rel, -REL_CLIP, REL_CLIP), np.clip(rel - LANES, -REL_CLIP, REL_CLIP)]) + REL_CLIP).astype(np.int32)
    k_pos = np.arange(nkb * tq)[:, None]
    q_pos = C_BAND + np.arange(tq)[None, :]
    kc, qc = k_pos // CHUNK, q_pos // CHUNK
    base = np.where((kc <= qc) & (qc - kc <= C_PREV_CHUNKS), 0.0, NEG_INF).astype(np.float32)
    hps = BAND_HEADS
    ngrp = n_heads // hps
    assert ngrp * hps == n_heads
    wd = hps * HEAD_DIM
    return pl.pallas_call(
        functools.partial(_band_p_kernel, tq=tq, nkb=nkb, nh=n_heads, hps=hps),
        grid=(ngrp, b, t // tq),
        in_specs=[
            pl.BlockSpec(memory_space=pltpu.SMEM),
            pl.BlockSpec((1, tq, wd), lambda h, bi, i: (bi, i, h)),
            pl.BlockSpec((1, t, wd), lambda h, bi, i: (bi, 0, ngrp + h)),
            pl.BlockSpec((1, hps, t // tq, HEAD_DIM, tq), lambda h, bi, i: (bi, h, 0, 0, 0)),
            pl.BlockSpec(idx.shape, lambda h, bi, i: (0, 0, 0)),
            pl.BlockSpec(base.shape, lambda h, bi, i: (0, 0)),
        ],
        out_specs=pl.BlockSpec((1, tq, wd), lambda h, bi, i: (bi, i, h)),
        out_shape=jax.ShapeDtypeStruct((b, t, n_heads * HEAD_DIM), BF16),
        scratch_shapes=[pltpu.VMEM((hps, nkb * tq, tq), F32)],
        compiler_params=_params("arbitrary", "arbitrary", "arbitrary"),
        name="band_attn_prompt",
    )(tbl, qkv, qkv, vt, jnp.asarray(idx), jnp.asarray(base))


def _band_s_kernel(q_ref, kn_ref, vn_ref, kc_ref, vc_ref, bmc_ref, bmn_ref, o_ref):
    q = q_ref[0]
    scale = HEAD_DIM ** -0.5
    s_c = lax.dot_general(q, kc_ref[0].astype(BF16), _NT, preferred_element_type=F32) * scale + bmc_ref[0]
    s_n = lax.dot_general(q, kn_ref[0], _NT, preferred_element_type=F32) * scale + bmn_ref[0]
    o_ref[0] = _joint_softmax_pv(s_c, s_n, vc_ref[0].astype(BF16), vn_ref[0]).astype(BF16)


def _band_sample(qkv, kc, vc, bmc, bmn, n_heads):
    b, t, _ = qkv.shape
    lc = kc.shape[1]
    hd = HEAD_DIM
    return pl.pallas_call(
        _band_s_kernel,
        grid=(b, n_heads),
        in_specs=[
            pl.BlockSpec((1, t, hd), lambda bi, h: (bi, 0, h)),
            pl.BlockSpec((1, t, hd), lambda bi, h: (bi, 0, n_heads + h)),
            pl.BlockSpec((1, t, hd), lambda bi, h: (bi, 0, 2 * n_heads + h)),
            pl.BlockSpec((1, lc, hd), lambda bi, h: (bi, 0, h)),
            pl.BlockSpec((1, lc, hd), lambda bi, h: (bi, 0, h)),
            pl.BlockSpec((1, t, lc), lambda bi, h: (h, 0, 0)),
            pl.BlockSpec((1, t, t), lambda bi, h: (h, 0, 0)),
        ],
        out_specs=pl.BlockSpec((1, t, hd), lambda bi, h: (bi, 0, h)),
        out_shape=jax.ShapeDtypeStruct((b, t, n_heads * hd), BF16),
        compiler_params=_params("parallel", "parallel"),
        name="band_attn_sample",
    )(qkv, qkv, qkv, kc, vc, bmc, bmn)


def _t5_bucket_np(rel):
    nb = N_T5_BUCKETS // 2
    max_exact = nb // 2
    n = np.abs(rel)
    nf = np.maximum(n, 1).astype(np.float64)
    large = max_exact + (np.log(nf / max_exact) / math.log(T5_MAX_DIST / max_exact) * (nb - max_exact)).astype(np.int64)
    large = np.minimum(large, nb - 1)
    return np.where(rel > 0, nb, 0) + np.where(n < max_exact, n, large)


def _t5_bias_tiles(t5_table, q_pos, k_pos, far_bucket):
    rel = k_pos[None, :] - q_pos[:, None]
    mask = (k_pos[None, :] // CHUNK) <= (q_pos[:, None] // CHUNK)
    bias = jnp.transpose(t5_table.astype(F32)[_t5_bucket_np(rel)], (2, 0, 1))
    bias = bias - t5_table.astype(F32)[far_bucket][:, None, None]
    return jnp.where(mask[None], bias, NEG_INF)


def _band_bias_tiles(rel_table, q_pos, k_pos):
    rel = np.clip(k_pos[None, :] - q_pos[:, None], -REL_CLIP, REL_CLIP) + REL_CLIP
    qc = q_pos[:, None] // CHUNK
    kc = k_pos[None, :] // CHUNK
    mask = (kc <= qc) & (qc - kc <= C_PREV_CHUNKS) & (k_pos[None, :] >= 0)
    bias = jnp.transpose(rel_table.astype(F32)[rel], (2, 0, 1))
    return jnp.where(mask[None], bias, NEG_INF)


def kernel(x_prompt, x_sample, cache_a_k, cache_a_v, cache_b_k, cache_b_v, cache_b_logf, cache_c_k, cache_c_v, c_prompt, c_sample, w_ada, b_ada, norm_g, w_ffn_in, w_ffn_out, w_in_ab, b_forget, w_out_ab, lambda_q1, lambda_k1, lambda_q2, lambda_k2, subln_g, t5_table, w_in_c, w_out_c, c_rel_bias, final_g):
    depth = w_ada.shape[0]
    bsz, seq, d = x_prompt.shape
    dbsz, dseq, _ = x_sample.shape
    past = cache_b_logf.shape[2]
    h_a = cache_a_k.shape[3]
    h_b = cache_b_k.shape[3]
    h_c = cache_c_k.shape[3]
    wa, wb, cw = h_a * HEAD_DIM, h_b * HEAD_DIM, h_c * HEAD_DIM
    assert bsz == dbsz and wa == wb and 3 * wa + 3 * wb == 3 * cw
    assert LANES >= T5_MAX_DIST and SAMPLE_NEAR >= T5_MAX_DIST and past >= SAMPLE_NEAR
    far_bucket = (N_T5_BUCKETS // 2) - 1

    mods = _ada_mods(jnp.concatenate([c_prompt, c_sample], axis=0), w_ada, b_ada)
    mods = mods.reshape(depth, 2, bsz, 9, d).transpose(0, 3, 1, 2, 4).reshape(depth * 9 * 2, bsz, 1, d)

    def slot(l, k, grp):
        return (l * 9 + k) * 2 + grp

    xs = [x_prompt, x_sample]
    ab_states = [[], []]
    c_states = [[], []]
    for l in range(depth):
        i = l // 2
        last = l == depth - 1
        w1 = _prep_ffn_weights(w_ffn_in[l, 0], w_ffn_out[l, 0])
        w2 = _prep_ffn_weights(w_ffn_in[l, 1], w_ffn_out[l, 1])
        for grp in range(2):
            xs[grp] = _ffn(xs[grp], mods, [slot(l, k, grp) for k in range(3)], norm_g[l, 0], *w1)

        if l % 2 == 0:
            lam_init = 0.8 - 0.6 * math.exp(-0.3 * l)
            lam = (jnp.exp(jnp.sum(lambda_q1[i].astype(F32) * lambda_k1[i].astype(F32)))
                   - jnp.exp(jnp.sum(lambda_q2[i].astype(F32) * lambda_k2[i].astype(F32))) + lam_init).reshape(1)
            n_main = 3 * wa + 3 * wb
            w_main = w_in_ab[i][:, :n_main].astype(BF16)
            wf = jnp.pad(w_in_ab[i][:, n_main:].T.astype(BF16), ((0, 16 - h_b), (0, 0)))
            bfo = b_forget[i].astype(F32).reshape(h_b, 1)
            w_out = w_out_ab[i].astype(BF16)
            g_sub = subln_g[i].astype(F32).reshape(1, HEAD_DIM)
            tn = min(PROJ_COL_TILE, wa)
            taps = _seg_taps(tn, wa, (1, 2, 4, 5), False)
            for grp in range(2):
                x = xs[grp]
                b, t, _ = x.shape
                tap_shapes = [((b, t, wa), True)] * 4
                if grp == 0:
                    qkv, ka, va, kb, vb, vt, logf_t = _proj(
                        x, mods, [slot(l, 3, grp), slot(l, 4, grp)], norm_g[l, 1], w_main, tn, tap_shapes, taps,
                        wf, bfo, vt_cols=_vt_cols(tn, wa, (2, 5)), vt_tile=ROW_TILE)
                else:
                    qkv, ka, va, kb, vb, logf_t = _proj(
                        x, mods, [slot(l, 3, grp), slot(l, 4, grp)], norm_g[l, 1], w_main, tn, tap_shapes, taps,
                        wf, bfo)
                logf = logf_t.reshape(h_b, b, t).transpose(1, 2, 0)
                ab_states[grp].append((ka.reshape(b, t, h_a, HEAD_DIM), va.reshape(b, t, h_a, HEAD_DIM),
                                       kb.reshape(b, t, h_b, HEAD_DIM), vb.reshape(b, t, h_b, HEAD_DIM), logf))
                logf_bh = logf_t.reshape(h_b, b, t).transpose(1, 0, 2)
                if grp == 0:
                    o_a = _diff_prompt(lam, t5_table.astype(F32).reshape(-1), qkv, vt, g_sub, h_a, 1.0 - lam_init)
                    fcum = _cumsum(logf_bh, jnp.zeros((b, h_b, LANES), F32))
                    nq = t // ROW_TILE
                    fref = fcum[:, :, ::ROW_TILE].reshape(-1)
                    o_b = _fox_prompt(fref, qkv, vt, fcum.reshape(b, h_b, nq, 1, ROW_TILE), h_b, 3 * h_a, h_a)
                else:
                    q_pos = past + np.arange(t)
                    bnear = _t5_bias_tiles(t5_table, q_pos, past - SAMPLE_NEAR + np.arange(SAMPLE_NEAR), far_bucket)
                    bnew = _t5_bias_tiles(t5_table, q_pos, q_pos, far_bucket)
                    kc = cache_a_k[i].reshape(b, past, wa)
                    vc = cache_a_v[i].reshape(b, past, wa)
                    o_a = _diff_sample(lam, qkv, kc, vc, bnear, bnew, g_sub, h_a, 1.0 - lam_init)
                    fc = _cumsum(cache_b_logf[i].astype(F32).transpose(0, 2, 1), jnp.zeros((b, h_b, LANES), F32))
                    carry = fc[:, :, past - 1:past]
                    fn = _cumsum(jnp.pad(logf_bh, ((0, 0), (0, 0), (0, LANES - t))),
                                 jnp.broadcast_to(carry, (b, h_b, LANES)))
                    o_b = _fox_sample(carry.reshape(-1), qkv, cache_b_k[i].reshape(b, past, wb),
                                      cache_b_v[i].reshape(b, past, wb), fc.reshape(b, h_b, 1, past),
                                      fn.reshape(b, h_b, 1, LANES), h_b, 3 * h_a)
                xs[grp] = _outproj([o_a, o_b], w_out, x, mods, slot(l, 5, grp))
        else:
            w_in = w_in_c[i].astype(BF16)
            w_out = w_out_c[i].astype(BF16)
            tn = min(PROJ_COL_TILE, cw)
            taps = _seg_taps(tn, cw, (1, 2), True)
            for grp in range(2):
                x = xs[grp]
                b, t, _ = x.shape
                keep = min(C_BAND, t)
                assert keep == min(t, ROW_TILE)
                tap_shapes = [((b, keep, cw), False)] * 2
                if grp == 0:
                    qkv, k_new, v_new, vt = _proj(
                        x, mods, [slot(l, 3, grp), slot(l, 4, grp)], norm_g[l, 1], w_in, tn, tap_shapes, taps,
                        vt_cols=_vt_cols(tn, cw, (2,)), vt_tile=BAND_TILE)
                else:
                    qkv, k_new, v_new = _proj(
                        x, mods, [slot(l, 3, grp), slot(l, 4, grp)], norm_g[l, 1], w_in, tn, tap_shapes, taps)
                c_states[grp].append((k_new.reshape(b, keep, h_c, HEAD_DIM), v_new.reshape(b, keep, h_c, HEAD_DIM)))
                if grp == 0:
                    o_c = _band_prompt(c_rel_bias[i].astype(F32).reshape(-1), qkv, vt, h_c)
                else:
                    lc = cache_c_k.shape[2]
                    q_pos = past + np.arange(t)
                    bmc = _band_bias_tiles(c_rel_bias[i], q_pos, past - lc + np.arange(lc))
                    bmn = _band_bias_tiles(c_rel_bias[i], q_pos, q_pos)
                    o_c = _band_sample(qkv, cache_c_k[i].reshape(b, lc, cw), cache_c_v[i].reshape(b, lc, cw), bmc, bmn, h_c)
                xs[grp] = _outproj([o_c], w_out, x, mods, slot(l, 5, grp))

        for grp in range(2):
            xs[grp] = _ffn(xs[grp], mods, [slot(l, k, grp) for k in (6, 7, 8)], norm_g[l, 2], *w2,
                           final_g=final_g if last else None)

    outs = [xs[0], xs[1]]
    for grp in range(2):
        st = ab_states[grp]
        outs += [jnp.stack([s[k] for s in st]) for k in range(5)]
        st = c_states[grp]
        outs += [jnp.stack([s[k] for s in st]) for k in range(2)]
    return tuple(outs)
```

```python
import functools
import math

import numpy as np
import jax
import jax.numpy as jnp
from jax import lax
from jax.experimental import pallas as pl
from jax.experimental.pallas import tpu as pltpu

F32 = jnp.float32
BF16 = jnp.bfloat16

CHUNK = 64
HEAD_DIM = 128
DA_HALF = HEAD_DIM // 2
N_T5_BUCKETS = 32
T5_MAX_DIST = 128
C_PREV_CHUNKS = 8
C_BAND = C_PREV_CHUNKS * CHUNK
REL_CLIP = 128
EPS = 1e-6
NEG_INF = -1e30

VMEM_LIMIT_BYTES = 56 * 1024 * 1024
LANES = 128

ROW_TILE = 512
FF_TILE = 512
FFN_OUT_TILE = 512
FFN_ROW_SPLIT = 2
PROJ_COL_TILE = 1024
ADA_COL_TILE = 1024
BAND_TILE = 256
BAND_HEADS = 2
ATTN_GROUP = 512
SAMPLE_NEAR = 256

_NT = (((1,), (1,)), ((), ()))


def _params(*sem, flags=None):
    return pltpu.CompilerParams(dimension_semantics=sem, vmem_limit_bytes=VMEM_LIMIT_BYTES, flags=flags)


def _modulated_norm(x, g, shift, scale):
    y = x * lax.rsqrt(jnp.mean(x * x, axis=-1, keepdims=True) + EPS) * g
    return y * (1.0 + scale) + shift


def _ada_kernel(c_ref, w_ref, b_ref, o_ref):
    c = c_ref[...]
    a = (c * jax.nn.sigmoid(c)).astype(BF16)
    o_ref[0] = jnp.dot(a, w_ref[0].astype(BF16), preferred_element_type=F32) + b_ref[0]


def _ada_mods(c_all, w_ada, b_ada):
    depth, d, n = w_ada.shape
    r = c_all.shape[0]
    tn = math.gcd(n, ADA_COL_TILE)
    return pl.pallas_call(
        _ada_kernel,
        grid=(depth, n // tn),
        in_specs=[
            pl.BlockSpec((r, d), lambda l, j: (0, 0)),
            pl.BlockSpec((1, d, tn), lambda l, j: (l, 0, j)),
            pl.BlockSpec((1, 1, tn), lambda l, j: (l, 0, j)),
        ],
        out_specs=pl.BlockSpec((1, r, tn), lambda l, j: (l, 0, j)),
        out_shape=jax.ShapeDtypeStruct((depth, r, n), F32),
        compiler_params=_params("arbitrary", "arbitrary"),
        name="ada_mods",
    )(c_all, w_ada, b_ada.reshape(depth, 1, n))


def _row_blocking(x):
    b, t, _ = x.shape
    if t >= ROW_TILE:
        assert t % ROW_TILE == 0
        return 1, ROW_TILE, t // ROW_TILE, b * (t // ROW_TILE)
    assert (b * t) % 8 == 0
    return b, t, 1, 1


def _x_index(nb, tps):
    if nb == 1:
        return lambda i, j: (i // tps, i % tps, 0)
    return lambda i, j: (0, 0, 0)


def _mod_spec(nb, tps, d, slot):
    if nb == 1:
        return pl.BlockSpec((None, 1, 1, d), lambda i, j: (slot, i // tps, 0, 0))
    return pl.BlockSpec((None, nb, 1, d), lambda i, j: (slot, 0, 0, 0))


def _ffn_kernel(*refs, nb, tt, n_ff, n_out, final):
    if final:
        (x_ref, shift_ref, scale_ref, gate_ref, g_ref, wg_ref, wu_ref, wo_ref, fg_ref,
         o_ref, h_scr, a_scr) = refs
    else:
        (x_ref, shift_ref, scale_ref, gate_ref, g_ref, wg_ref, wu_ref, wo_ref,
         o_ref, h_scr, a_scr) = refs
    j = pl.program_id(1)
    d = x_ref.shape[-1]
    tm = nb * tt
    tf = wg_ref.shape[1]
    tn = wo_ref.shape[1]

    @pl.when(j == 0)
    def _():
        h = _modulated_norm(x_ref[...], g_ref[...], shift_ref[...], scale_ref[...])
        h_scr[...] = h.reshape(tm, d).astype(BF16)

    @pl.when(j < n_ff)
    def _():
        rows = tm // FFN_ROW_SPLIT

        def products(r):
            hr = h_scr[r * rows:(r + 1) * rows, :]
            return (jnp.dot(hr, wg_ref[...], preferred_element_type=F32),
                    jnp.dot(hr, wu_ref[...], preferred_element_type=F32))

        parts = []
        cur = products(0)
        for r in range(FFN_ROW_SPLIT):
            nxt = products(r + 1) if r + 1 < FFN_ROW_SPLIT else None
            gg, uu = cur
            parts.append((gg * jax.nn.sigmoid(gg) * uu).astype(BF16))
            cur = nxt
        a_scr[j] = jnp.concatenate(parts, axis=0)

    @pl.when(j >= n_ff)
    def _():
        a = jnp.concatenate([a_scr[f] for f in range(n_ff)], axis=1)
        acc = jnp.dot(a, wo_ref[...], preferred_element_type=F32).reshape(nb, tt, tn)
        for n in range(n_out):
            @pl.when(j == n_ff + n)
            def _(n=n):
                cs = slice(n * tn, (n + 1) * tn)
                o_ref[:, :, cs] = x_ref[:, :, cs] + 0.5 * gate_ref[:, :, cs] * acc

    if final:
        @pl.when(j == n_ff + n_out - 1)
        def _():
            y = o_ref[...]
            o_ref[...] = y * lax.rsqrt(jnp.mean(y * y, axis=-1, keepdims=True) + EPS) * fg_ref[...]


def _ffn(x, mods, slots, g, w_in_p, w_out_p, final_g=None):
    b, t, d = x.shape
    nb, tt, tps, nrows = _row_blocking(x)
    ffp = w_out_p.shape[0]
    n_ff = ffp // FF_TILE
    tn = math.gcd(d, FFN_OUT_TILE)
    n_out = d // tn
    final = final_g is not None
    in_specs = [
        pl.BlockSpec((nb, tt, d), _x_index(nb, tps)),
        _mod_spec(nb, tps, d, slots[0]),
        _mod_spec(nb, tps, d, slots[1]),
        _mod_spec(nb, tps, d, slots[2]),
        pl.BlockSpec((1, d), lambda i, j: (0, 0)),
        pl.BlockSpec((d, FF_TILE), lambda i, j: (0, jnp.minimum(j, n_ff - 1))),
        pl.BlockSpec((d, FF_TILE), lambda i, j: (0, jnp.minimum(j, n_ff - 1) + n_ff)),
        pl.BlockSpec((ffp, tn), lambda i, j: (0, jnp.maximum(j - n_ff, 0))),
    ]
    args = [x, mods, mods, mods, g.reshape(1, d), w_in_p, w_in_p, w_out_p]
    if final:
        in_specs.append(pl.BlockSpec((1, d), lambda i, j: (0, 0)))
        args.append(final_g.reshape(1, d))
    return pl.pallas_call(
        functools.partial(_ffn_kernel, nb=nb, tt=tt, n_ff=n_ff, n_out=n_out, final=final),
        grid=(nrows, n_ff + n_out),
        in_specs=in_specs,
        out_specs=pl.BlockSpec((nb, tt, d), _x_index(nb, tps)),
        out_shape=jax.ShapeDtypeStruct((b, t, d), F32),
        scratch_shapes=[pltpu.VMEM((nb * tt, d), BF16), pltpu.VMEM((n_ff, nb * tt, FF_TILE), BF16)],
        compiler_params=_params("parallel", "arbitrary"),
        name="ffn",
    )(*args)


def _prep_ffn_weights(w_in, w_out):
    ff = w_out.shape[0]
    ffp = -(-ff // FF_TILE) * FF_TILE
    pad = ffp - ff
    wg = jnp.pad(w_in[:, :ff].astype(BF16), ((0, 0), (0, pad)))
    wu = jnp.pad(w_in[:, ff:].astype(BF16), ((0, 0), (0, pad)))
    return jnp.concatenate([wg, wu], axis=1), jnp.pad(w_out.astype(BF16), ((0, pad), (0, 0)))


def _log_sigmoid(x):
    return jnp.minimum(x, 0.0) - jnp.log1p(jnp.exp(-jnp.abs(x)))


def _proj_kernel(*refs, nb, tt, tn, tps, taps, has_f, vt_cols, vt_tile):
    x_ref, shift_ref, scale_ref, g_ref, w_ref = refs[:5]
    pos = 5
    if has_f:
        wf_ref, bf_ref = refs[5:7]
        pos = 7
    obf_ref = refs[pos]
    tap_refs = refs[pos + 1:pos + 1 + len({tp[0] for tp in taps})]
    pos = pos + 1 + len(tap_refs)
    if vt_cols:
        vt_ref = refs[pos]
        pos += 1
    if has_f:
        logf_ref = refs[pos]
        pos += 1
    h_scr = refs[pos]
    i = pl.program_id(0)
    j = pl.program_id(1)
    d = x_ref.shape[-1]

    @pl.when(j == 0)
    def _():
        h = _modulated_norm(x_ref[...], g_ref[...], shift_ref[...], scale_ref[...])
        h_scr[...] = h.reshape(nb * tt, d).astype(BF16)
        if has_f:
            fr = lax.dot_general(wf_ref[...], h_scr[...], _NT, preferred_element_type=F32)
            logf_ref[...] = _log_sigmoid(fr[:logf_ref.shape[0]] + bf_ref[...])

    y = jnp.dot(h_scr[...], w_ref[...], preferred_element_type=F32)
    obf_ref[...] = y.astype(BF16).reshape(nb, tt, tn)
    for out_idx, jval, col0, last_rows_only in taps:
        cond = j == jval
        if last_rows_only and tps > 1:
            cond = jnp.logical_and(cond, i % tps == tps - 1)

        @pl.when(cond)
        def _(out_idx=out_idx, col0=col0):
            tap_refs[out_idx][:, :, col0:col0 + tn] = y.reshape(nb, tt, tn)

    for jval, head0 in vt_cols:
        @pl.when(j == jval)
        def _(head0=head0):
            for hh in range(tn // HEAD_DIM):
                yt = y[:, hh * HEAD_DIM:(hh + 1) * HEAD_DIM].T.astype(BF16)
                for s in range(tt // vt_tile):
                    vt_ref[0, head0 + hh, s] = yt[:, s * vt_tile:(s + 1) * vt_tile]


def _seg_taps(tn, width, segments, last_rows_only):
    per = width // tn
    assert per * tn == width
    return [(o, sidx * per + s, s * tn, last_rows_only) for o, sidx in enumerate(segments) for s in range(per)]


def _vt_cols(tn, width, segments):
    per = width // tn
    hpb = tn // HEAD_DIM
    return [(sidx * per + s, (o * per + s) * hpb) for o, sidx in enumerate(segments) for s in range(per)]


def _proj(x, mods, slots, g, w_bf, tn, tap_shapes, taps, wf=None, bf=None, vt_cols=(), vt_tile=None):
    b, t, d = x.shape
    n = w_bf.shape[1]
    nb, tt, tps, nrows = _row_blocking(x)
    has_f = wf is not None
    assert not vt_cols or nb == 1
    in_specs = [
        pl.BlockSpec((nb, tt, d), _x_index(nb, tps)),
        _mod_spec(nb, tps, d, slots[0]),
        _mod_spec(nb, tps, d, slots[1]),
        pl.BlockSpec((1, d), lambda i, j: (0, 0)),
        pl.BlockSpec((d, tn), lambda i, j: (0, j)),
    ]
    args = [x, mods, mods, g.reshape(1, d), w_bf]
    if has_f:
        in_specs += [pl.BlockSpec(wf.shape, lambda i, j: (0, 0)), pl.BlockSpec(bf.shape, lambda i, j: (0, 0))]
        args += [wf, bf]
    xi = _x_index(nb, tps)
    out_specs = [pl.BlockSpec((nb, tt, tn), lambda i, j: xi(i, j)[:2] + (j,))]
    out_shape = [jax.ShapeDtypeStruct((b, t, n), BF16)]
    for shp, follows_rows in tap_shapes:
        if follows_rows:
            out_specs.append(pl.BlockSpec((nb, tt, shp[2]), xi))
        elif nb == 1:
            out_specs.append(pl.BlockSpec((1, shp[1], shp[2]), lambda i, j: (i // tps, 0, 0)))
        else:
            out_specs.append(pl.BlockSpec(shp, lambda i, j: (0, 0, 0)))
        out_shape.append(jax.ShapeDtypeStruct(shp, F32))
    if vt_cols:
        n_vh = len(vt_cols) * (tn // HEAD_DIM)
        out_specs.append(pl.BlockSpec((1, n_vh, tt // vt_tile, HEAD_DIM, vt_tile), lambda i, j: (i // tps, 0, i % tps, 0, 0)))
        out_shape.append(jax.ShapeDtypeStruct((b, n_vh, t // vt_tile, HEAD_DIM, vt_tile), BF16))
    if has_f:
        nf = bf.shape[0]
        out_specs.append(pl.BlockSpec((nf, nb * tt), lambda i, j: (0, i)))
        out_shape.append(jax.ShapeDtypeStruct((nf, b * t), F32))
    return pl.pallas_call(
        functools.partial(_proj_kernel, nb=nb, tt=tt, tn=tn, tps=tps, taps=tuple(taps), has_f=has_f,
                          vt_cols=tuple(vt_cols), vt_tile=vt_tile),
        grid=(nrows, n // tn),
        in_specs=in_specs,
        out_specs=out_specs,
        out_shape=out_shape,
        scratch_shapes=[pltpu.VMEM((nb * tt, d), BF16)],
        compiler_params=_params("arbitrary", "arbitrary"),
        name="mixer_in_proj",
    )(*args)


def _outproj_kernel(*refs, nb, tt, n_in):
    o_refs = refs[:n_in]
    w_ref, x_ref, gate_ref, out_ref = refs[n_in:]
    d = x_ref.shape[-1]
    acc = None
    row0 = 0
    for o_ref in o_refs:
        wd = o_ref.shape[-1]
        part = jnp.dot(o_ref[...].reshape(nb * tt, wd), w_ref[row0:row0 + wd, :], preferred_element_type=F32)
        acc = part if acc is None else acc + part
        row0 += wd
    out_ref[...] = x_ref[...] + gate_ref[...] * acc.reshape(nb, tt, d)


def _outproj(o_list, w_bf, x, mods, gate_slot):
    b, t, d = x.shape
    nb, tt, tps, nrows = _row_blocking(x)
    xi = _x_index(nb, tps)
    in_specs = [pl.BlockSpec((nb, tt, o.shape[-1]), xi) for o in o_list]
    in_specs += [
        pl.BlockSpec(w_bf.shape, lambda i, j: (0, 0)),
        pl.BlockSpec((nb, tt, d), xi),
        _mod_spec(nb, tps, d, gate_slot),
    ]
    return pl.pallas_call(
        functools.partial(_outproj_kernel, nb=nb, tt=tt, n_in=len(o_list)),
        grid=(nrows, 1),
        in_specs=in_specs,
        out_specs=pl.BlockSpec((nb, tt, d), xi),
        out_shape=jax.ShapeDtypeStruct((b, t, d), F32),
        compiler_params=_params("parallel", "arbitrary"),
        name="mixer_out_proj",
    )(*o_list, w_bf, x, mods)


def _cumsum_kernel(x_ref, c_ref, o_ref):
    x = x_ref[0]
    n = x.shape[1]
    lane = lax.broadcasted_iota(jnp.int32, x.shape, 1)
    s = 1
    while s < n:
        x = x + jnp.where(lane >= s, pltpu.roll(x, s, axis=1), 0.0)
        s *= 2
    o_ref[0] = x + c_ref[0][:, :1]


def _cumsum(x, carry):
    g, h, n = x.shape
    return pl.pallas_call(
        _cumsum_kernel,
        grid=(g,),
        in_specs=[pl.BlockSpec((1, h, n), lambda i: (i, 0, 0)), pl.BlockSpec((1, h, LANES), lambda i: (i, 0, 0))],
        out_specs=pl.BlockSpec((1, h, n), lambda i: (i, 0, 0)),
        out_shape=jax.ShapeDtypeStruct((g, h, n), F32),
        compiler_params=_params("arbitrary"),
        name="logf_cumsum",
    )(x, carry)


def _softmax_init(m_scr, l_scr, acc_scr):
    m_scr[...] = jnp.full(m_scr.shape, NEG_INF, F32)
    l_scr[...] = jnp.zeros(l_scr.shape, F32)
    acc_scr[...] = jnp.zeros(acc_scr.shape, F32)


def _softmax_update_t(s, s_max, vt, m_prev, l_prev, acc_prev):
    m_new = jnp.maximum(m_prev, s_max)
    alpha = jnp.exp(m_prev - m_new)
    p = jnp.exp(s - m_new)
    l_new = alpha * l_prev + jnp.sum(p, axis=0, keepdims=True)
    acc_new = alpha * acc_prev + jnp.dot(vt, p.astype(BF16), preferred_element_type=F32)
    return m_new, l_new, acc_new


def _with_max(s):
    return s, jnp.max(s, axis=0, keepdims=True)


def _consume_tile(scores, vt, m_scr, l_scr, acc_scr, next_scores=None):
    n = len(scores) // 2
    g = m_scr.shape[1] // n
    sls = [slice(i * g, (i + 1) * g) for i in range(n)]
    state = [(m_scr[:, sl], l_scr[:, sl], acc_scr[:, sl]) for sl in sls]
    new, nxt = [], []
    for i in range(n):
        if next_scores is not None:
            nxt.extend(next_scores(i))
        new.append(_softmax_update_t(scores[2 * i], scores[2 * i + 1], vt, *state[i]))
    for sl, (m_new, l_new, acc_new) in zip(sls, new):
        m_scr[:, sl] = m_new
        l_scr[:, sl] = l_new
        acc_scr[:, sl] = acc_new
    return tuple(nxt)


def _toeplitz_values(tbl_ref, idx, n_vals, nh, h, ref_row):
    ref_val = tbl_ref[ref_row * nh + h]

    def body(v, acc):
        return jnp.where(idx == v, tbl_ref[v * nh + h] - ref_val, acc)

    return lax.fori_loop(0, n_vals, body, jnp.zeros(idx.shape, F32))


def _stack_diff_queries(q):
    lane = lax.broadcasted_iota(jnp.int32, q.shape, 1)
    zero = jnp.zeros_like(q)
    qs = jnp.concatenate([jnp.where(lane < DA_HALF, q, zero), jnp.where(lane >= DA_HALF, q, zero)], axis=0)
    return qs * jnp.asarray(DA_HALF ** -0.5, BF16)


def _diff_finalize(o, lam, g, out_scale):
    tq = o.shape[0] // 2
    od = o[:tq] - lam * o[tq:]
    return od * lax.rsqrt(jnp.mean(od * od, axis=-1, keepdims=True) + EPS) * g * out_scale


def _diff_p_kernel(lam_ref, tbl_ref, q_ref, k_ref, vt_ref, idx_ref, base_ref, g_ref, o_ref,
                   m_scr, l_scr, acc_scr, bias_scr, *, tq, nh, far_bucket, out_scale):
    h = pl.program_id(0)
    bi = pl.program_id(1)
    qi = pl.program_id(2)
    nblk = tq // LANES

    @pl.when(jnp.logical_and(bi == 0, qi == 0))
    def _():
        t = _toeplitz_values(tbl_ref, idx_ref[...], N_T5_BUCKETS, nh, h, far_bucket)
        bias_scr[...] = base_ref[...]
        for blk in range(nblk):
            sl = slice(blk * LANES, (blk + 1) * LANES)
            bias_scr[0, sl, sl] += t[0]
            if blk >= 1:
                bias_scr[0, (blk - 1) * LANES:blk * LANES, sl] += t[1]
        bias_scr[1, (nblk - 1) * LANES:, :LANES] += t[1]

    qs = _stack_diff_queries(q_ref[0])
    _softmax_init(m_scr, l_scr, acc_scr)
    ngrp = 2 * tq // ATTN_GROUP

    def scores(j, slot):
        k = k_ref[0, pl.ds(pl.multiple_of(j * tq, tq), tq), :]

        def group(i):
            s = lax.dot_general(k, qs[i * ATTN_GROUP:(i + 1) * ATTN_GROUP], _NT, preferred_element_type=F32)
            if slot is not None:
                q0 = (i * ATTN_GROUP) % tq
                s = s + bias_scr[slot, :, q0:q0 + ATTN_GROUP]
            return _with_max(s)
        return group

    n_far = jnp.maximum(qi - 1, 0)
    before_slot = jnp.where(qi == 0, 2, 1)

    @pl.when(n_far >= 1)
    def _():
        first = scores(0, None)
        cur = tuple(x for i in range(ngrp) for x in first(i))

        def far(j, cur):
            return _consume_tile(cur, vt_ref[0, 0, j], m_scr, l_scr, acc_scr, scores(j + 1, None))

        cur = lax.fori_loop(0, n_far - 1, far, cur)
        _consume_tile(cur, vt_ref[0, 0, n_far - 1], m_scr, l_scr, acc_scr)

    before = scores(n_far, before_slot)
    cur = tuple(x for i in range(ngrp) for x in before(i))
    cur = _consume_tile(cur, vt_ref[0, 0, n_far], m_scr, l_scr, acc_scr, scores(qi, 0))
    _consume_tile(cur, vt_ref[0, 0, qi], m_scr, l_scr, acc_scr)
    o = (acc_scr[...] / l_scr[...]).T
    o_ref[0] = _diff_finalize(o, lam_ref[0], g_ref[...], out_scale).astype(BF16)


def _diff_prompt(lam, tbl, qkv, vt, g, n_heads, out_scale):
    b, t, _ = qkv.shape
    tq = vt.shape[-1]
    far_bucket = N_T5_BUCKETS // 2 - 1
    pos = np.arange(LANES)
    idx = np.stack([_t5_bucket_np(pos[:, None] - pos[None, :]),
                    _t5_bucket_np(pos[:, None] - pos[None, :] - LANES)]).astype(np.int32)
    kq = np.arange(tq)
    base = np.zeros((3, tq, tq), np.float32)
    base[0] = np.where((kq[:, None] // CHUNK) <= (kq[None, :] // CHUNK), 0.0, NEG_INF)
    base[2] = NEG_INF
    return pl.pallas_call(
        functools.partial(_diff_p_kernel, tq=tq, nh=n_heads, far_bucket=far_bucket, out_scale=out_scale),
        grid=(n_heads, b, t // tq),
        in_specs=[
            pl.BlockSpec(memory_space=pltpu.SMEM),
            pl.BlockSpec(memory_space=pltpu.SMEM),
            pl.BlockSpec((1, tq, HEAD_DIM), lambda h, bi, qi: (bi, qi, h)),
            pl.BlockSpec((1, t, HEAD_DIM), lambda h, bi, qi: (bi, 0, n_heads + h)),
            pl.BlockSpec((1, 1, t // tq, HEAD_DIM, tq), lambda h, bi, qi: (bi, h, 0, 0, 0)),
            pl.BlockSpec(idx.shape, lambda h, bi, qi: (0, 0, 0)),
            pl.BlockSpec(base.shape, lambda h, bi, qi: (0, 0, 0)),
            pl.BlockSpec((1, HEAD_DIM), lambda h, bi, qi: (0, 0)),
        ],
        out_specs=pl.BlockSpec((1, tq, HEAD_DIM), lambda h, bi, qi: (bi, qi, h)),
        out_shape=jax.ShapeDtypeStruct((b, t, n_heads * HEAD_DIM), BF16),
        scratch_shapes=[pltpu.VMEM((1, 2 * tq), F32), pltpu.VMEM((1, 2 * tq), F32),
                        pltpu.VMEM((HEAD_DIM, 2 * tq), F32), pltpu.VMEM((3, tq, tq), F32)],
        compiler_params=_params("arbitrary", "arbitrary", "arbitrary"),
        name="diff_attn_prompt",
    )(lam, tbl, qkv, qkv, vt, jnp.asarray(idx), jnp.asarray(base), g)


def _fox_p_kernel(fref_ref, q_ref, k_ref, vt_ref, f_ref, o_ref, m_scr, l_scr, acc_scr, fcol_scr, *, tq):
    bi = pl.program_id(0)
    h = pl.program_id(1)
    qi = pl.program_id(2)
    nblk = tq // LANES

    @pl.when(qi == 0)
    def _():
        for c in range(f_ref.shape[2] * nblk):
            row = f_ref[0, 0, c // nblk][:, (c % nblk) * LANES:(c % nblk + 1) * LANES]
            fcol_scr[c * LANES:(c + 1) * LANES, :] = jnp.broadcast_to(row, (LANES, LANES)).T

    q = q_ref[0]
    fref = fref_ref[(bi * pl.num_programs(1) + h) * pl.num_programs(2) + qi]
    _softmax_init(m_scr, l_scr, acc_scr)

    ngrp = tq // ATTN_GROUP

    def scores(j, causal):
        start = pl.multiple_of(j * tq, tq)
        k = k_ref[0, pl.ds(start, tq), :]
        decay = fref - fcol_scr[pl.ds(start, tq), :]
        decay = jnp.concatenate([decay] * (ATTN_GROUP // LANES), axis=1)

        def group(i):
            s = lax.dot_general(k, q[i * ATTN_GROUP:(i + 1) * ATTN_GROUP], _NT, preferred_element_type=F32)
            s = s * (HEAD_DIM ** -0.5) + decay
            if causal:
                key = lax.broadcasted_iota(jnp.int32, s.shape, 0)
                qry = lax.broadcasted_iota(jnp.int32, s.shape, 1) + i * ATTN_GROUP
                s = jnp.where(key <= qry, s, NEG_INF)
            return _with_max(s)
        return group

    @pl.when(qi >= 1)
    def _():
        first = scores(0, False)
        cur = tuple(x for i in range(ngrp) for x in first(i))

        def far(j, cur):
            return _consume_tile(cur, vt_ref[0, 0, j], m_scr, l_scr, acc_scr, scores(j + 1, False))

        cur = lax.fori_loop(0, qi - 1, far, cur)
        _consume_tile(cur, vt_ref[0, 0, qi - 1], m_scr, l_scr, acc_scr)

    diag = scores(qi, True)
    _consume_tile(tuple(x for i in range(ngrp) for x in diag(i)), vt_ref[0, 0, qi], m_scr, l_scr, acc_scr)
    o_ref[0] = (acc_scr[...] / l_scr[...]).T.astype(BF16)


def _fox_prompt(fref, qkv, vt, fcum, n_heads, col0, vt_head0):
    b, t, _ = qkv.shape
    tq = vt.shape[-1]
    nq = t // tq
    return pl.pallas_call(
        functools.partial(_fox_p_kernel, tq=tq),
        grid=(b, n_heads, nq),
        in_specs=[
            pl.BlockSpec(memory_space=pltpu.SMEM),
            pl.BlockSpec((1, tq, HEAD_DIM), lambda bi, h, qi: (bi, qi, col0 + h)),
            pl.BlockSpec((1, t, HEAD_DIM), lambda bi, h, qi: (bi, 0, col0 + n_heads + h)),
            pl.BlockSpec((1, 1, nq, HEAD_DIM, tq), lambda bi, h, qi: (bi, vt_head0 + h, 0, 0, 0)),
            pl.BlockSpec((1, 1, nq, 1, tq), lambda bi, h, qi: (bi, h, 0, 0, 0)),
        ],
        out_specs=pl.BlockSpec((1, tq, HEAD_DIM), lambda bi, h, qi: (bi, qi, h)),
        out_shape=jax.ShapeDtypeStruct((b, t, n_heads * HEAD_DIM), BF16),
        scratch_shapes=[pltpu.VMEM((1, tq), F32), pltpu.VMEM((1, tq), F32), pltpu.VMEM((HEAD_DIM, tq), F32),
                        pltpu.VMEM((t, LANES), F32)],
        compiler_params=_params("arbitrary", "arbitrary", "arbitrary"),
        name="fox_attn_prompt",
    )(fref, qkv, qkv, vt, fcum)


def _joint_softmax_pv(s_c, s_n, vc, vn):
    m = jnp.maximum(jnp.max(s_c, axis=1, keepdims=True), jnp.max(s_n, axis=1, keepdims=True))
    p_c = jnp.exp(s_c - m)
    p_n = jnp.exp(s_n - m)
    l = jnp.sum(p_c, axis=1, keepdims=True) + jnp.sum(p_n, axis=1, keepdims=True)
    acc = jnp.dot(p_c.astype(BF16), vc, preferred_element_type=F32)
    acc = acc + jnp.dot(p_n.astype(BF16), vn, preferred_element_type=F32)
    return acc / l


def _diff_s_kernel(lam_ref, q_ref, kn_ref, vn_ref, kc_ref, vc_ref, bnear_ref, bnew_ref, g_ref, o_ref, *, near, out_scale):
    qs = _stack_diff_queries(q_ref[0])
    kc = kc_ref[0].astype(BF16)
    vc = vc_ref[0].astype(BF16)
    p = kc.shape[0]
    s_c = lax.dot_general(qs, kc, _NT, preferred_element_type=F32)
    bnear = bnear_ref[0]
    s_c = jnp.concatenate([s_c[:, :p - near], s_c[:, p - near:] + jnp.concatenate([bnear, bnear], axis=0)], axis=1)
    bnew = bnew_ref[0]
    s_n = lax.dot_general(qs, kn_ref[0], _NT, preferred_element_type=F32) + jnp.concatenate([bnew, bnew], axis=0)
    o = _joint_softmax_pv(s_c, s_n, vc, vn_ref[0])
    o_ref[0] = _diff_finalize(o, lam_ref[0], g_ref[...], out_scale).astype(BF16)


def _diff_sample(lam, qkv, kc, vc, bnear, bnew, g, n_heads, out_scale):
    b, t, _ = qkv.shape
    p = kc.shape[1]
    near = bnear.shape[-1]
    hd = HEAD_DIM
    return pl.pallas_call(
        functools.partial(_diff_s_kernel, near=near, out_scale=out_scale),
        grid=(b, n_heads),
        in_specs=[
            pl.BlockSpec(memory_space=pltpu.SMEM),
            pl.BlockSpec((1, t, hd), lambda bi, h: (bi, 0, h)),
            pl.BlockSpec((1, t, hd), lambda bi, h: (bi, 0, n_heads + h)),
            pl.BlockSpec((1, t, hd), lambda bi, h: (bi, 0, 2 * n_heads + h)),
            pl.BlockSpec((1, p, hd), lambda bi, h: (bi, 0, h)),
            pl.BlockSpec((1, p, hd), lambda bi, h: (bi, 0, h)),
            pl.BlockSpec((1, t, near), lambda bi, h: (h, 0, 0)),
            pl.BlockSpec((1, t, t), lambda bi, h: (h, 0, 0)),
            pl.BlockSpec((1, hd), lambda bi, h: (0, 0)),
        ],
        out_specs=pl.BlockSpec((1, t, hd), lambda bi, h: (bi, 0, h)),
        out_shape=jax.ShapeDtypeStruct((b, t, n_heads * hd), BF16),
        compiler_params=_params("parallel", "parallel"),
        name="diff_attn_sample",
    )(lam, qkv, qkv, qkv, kc, vc, bnear, bnew, g)


def _fox_s_kernel(fref_ref, q_ref, kn_ref, vn_ref, kc_ref, vc_ref, fc_ref, fn_ref, o_ref):
    bi = pl.program_id(0)
    h = pl.program_id(1)
    q = q_ref[0]
    t = q.shape[0]
    fref = fref_ref[bi * pl.num_programs(1) + h]
    scale = HEAD_DIM ** -0.5
    s_c = lax.dot_general(q, kc_ref[0].astype(BF16), _NT, preferred_element_type=F32) * scale + (fref - fc_ref[0, 0])
    s_n = lax.dot_general(q, kn_ref[0], _NT, preferred_element_type=F32) * scale + (fref - fn_ref[0, 0][:, :t])
    row = lax.broadcasted_iota(jnp.int32, s_n.shape, 0)
    col = lax.broadcasted_iota(jnp.int32, s_n.shape, 1)
    s_n = jnp.where(col <= row, s_n, NEG_INF)
    o_ref[0] = _joint_softmax_pv(s_c, s_n, vc_ref[0].astype(BF16), vn_ref[0]).astype(BF16)


def _fox_sample(fref, qkv, kc, vc, fc, fn, n_heads, col0):
    b, t, _ = qkv.shape
    p = kc.shape[1]
    hd = HEAD_DIM
    return pl.pallas_call(
        _fox_s_kernel,
        grid=(b, n_heads),
        in_specs=[
            pl.BlockSpec(memory_space=pltpu.SMEM),
            pl.BlockSpec((1, t, hd), lambda bi, h: (bi, 0, col0 + h)),
            pl.BlockSpec((1, t, hd), lambda bi, h: (bi, 0, col0 + n_heads + h)),
            pl.BlockSpec((1, t, hd), lambda bi, h: (bi, 0, col0 + 2 * n_heads + h)),
            pl.BlockSpec((1, p, hd), lambda bi, h: (bi, 0, h)),
            pl.BlockSpec((1, p, hd), lambda bi, h: (bi, 0, h)),
            pl.BlockSpec((1, 1, 1, p), lambda bi, h: (bi, h, 0, 0)),
            pl.BlockSpec((1, 1, 1, fn.shape[-1]), lambda bi, h: (bi, h, 0, 0)),
        ],
        out_specs=pl.BlockSpec((1, t, hd), lambda bi, h: (bi, 0, h)),
        out_shape=jax.ShapeDtypeStruct((b, t, n_heads * hd), BF16),
        compiler_params=_params("parallel", "parallel"),
        name="fox_attn_sample",
    )(fref, qkv, qkv, qkv, kc, vc, fc, fn)


def _band_p_kernel(tbl_ref, q_ref, k_ref, vt_ref, idx_ref, base_ref, o_ref, bm_scr, *, tq, nkb, nh, hps):
    hg = pl.program_id(0)
    bi = pl.program_id(1)
    i = pl.program_id(2)
    nblk = tq // LANES
    qoff = (nkb - 1) * nblk

    @pl.when(jnp.logical_and(bi == 0, i == 0))
    def _():
        for e in range(hps):
            t = _toeplitz_values(tbl_ref, idx_ref[...], 2 * REL_CLIP + 1, nh, hg * hps + e, 0)
            bm_scr[e] = base_ref[...]
            for bq in range(nblk):
                kd = qoff + bq
                bm_scr[e, kd * LANES:(kd + 1) * LANES, bq * LANES:(bq + 1) * LANES] += t[0]
                bm_scr[e, (kd - 1) * LANES:kd * LANES, bq * LANES:(bq + 1) * LANES] += t[1]

    scale = HEAD_DIM ** -0.5
    outs = []
    for e in range(hps):
        hs = slice(e * HEAD_DIM, (e + 1) * HEAD_DIM)
        q = q_ref[0, :, hs]
        ss, vts = [], []
        for blk in range(nkb):
            jb = i - (nkb - 1) + blk
            jc = jnp.maximum(jb, 0)
            k = k_ref[0, pl.ds(pl.multiple_of(jc * tq, tq), tq), hs]
            s = lax.dot_general(k, q, _NT, preferred_element_type=F32) * scale + bm_scr[e, blk * tq:(blk + 1) * tq, :]
            if blk < nkb - 1:
                s = s + jnp.where(jb < 0, NEG_INF, 0.0)
            ss.append(s)
            vts.append(vt_ref[0, e, jc])
        m = functools.reduce(jnp.maximum, [jnp.max(s, axis=0, keepdims=True) for s in ss])
        l = None
        acc = None
        for s, vt in zip(ss, vts):
            p = jnp.exp(s - m)
            ls = jnp.sum(p, axis=0, keepdims=True)
            a = jnp.dot(vt, p.astype(BF16), preferred_element_type=F32)
            l = ls if l is None else l + ls
            acc = a if acc is None else acc + a
        outs.append((acc / l).T.astype(BF16))
    o_ref[0] = jnp.concatenate(outs, axis=1)


def _band_prompt(tbl, qkv, vt, n_heads):
    b, t, _ = qkv.shape
    tq = vt.shape[-1]
    nkb = C_BAND // tq + 1
    assert (nkb - 1) * tq == C_BAND and tq % LANES == 0 and LANES >= REL_CLIP
    pos = np.arange(LANES)
    rel = pos[:, None] - pos[None, :]
    idx = (np.stack([np.clip(rel, -REL_CLIP, REL_CLIP), np.clip(rel - LANES, -REL_CLIP, REL_CLIP)]) + REL_CLIP).astype(np.int32)
    k_pos = np.arange(nkb * tq)[:, None]
    q_pos = C_BAND + np.arange(tq)[None, :]
    kc, qc = k_pos // CHUNK, q_pos // CHUNK
    base = np.where((kc <= qc) & (qc - kc <= C_PREV_CHUNKS), 0.0, NEG_INF).astype(np.float32)
    hps = BAND_HEADS
    ngrp = n_heads // hps
    assert ngrp * hps == n_heads
    wd = hps * HEAD_DIM
    return pl.pallas_call(
        functools.partial(_band_p_kernel, tq=tq, nkb=nkb, nh=n_heads, hps=hps),
        grid=(ngrp, b, t // tq),
        in_specs=[
            pl.BlockSpec(memory_space=pltpu.SMEM),
            pl.BlockSpec((1, tq, wd), lambda h, bi, i: (bi, i, h)),
            pl.BlockSpec((1, t, wd), lambda h, bi, i: (bi, 0, ngrp + h)),
            pl.BlockSpec((1, hps, t // tq, HEAD_DIM, tq), lambda h, bi, i: (bi, h, 0, 0, 0)),
            pl.BlockSpec(idx.shape, lambda h, bi, i: (0, 0, 0)),
            pl.BlockSpec(base.shape, lambda h, bi, i: (0, 0)),
        ],
        out_specs=pl.BlockSpec((1, tq, wd), lambda h, bi, i: (bi, i, h)),
        out_shape=jax.ShapeDtypeStruct((b, t, n_heads * HEAD_DIM), BF16),
        scratch_shapes=[pltpu.VMEM((hps, nkb * tq, tq), F32)],
        compiler_params=_params("arbitrary", "arbitrary", "arbitrary"),
        name="band_attn_prompt",
    )(tbl, qkv, qkv, vt, jnp.asarray(idx), jnp.asarray(base))


def _band_s_kernel(q_ref, kn_ref, vn_ref, kc_ref, vc_ref, bmc_ref, bmn_ref, o_ref):
    q = q_ref[0]
    scale = HEAD_DIM ** -0.5
    s_c = lax.dot_general(q, kc_ref[0].astype(BF16), _NT, preferred_element_type=F32) * scale + bmc_ref[0]
    s_n = lax.dot_general(q, kn_ref[0], _NT, preferred_element_type=F32) * scale + bmn_ref[0]
    o_ref[0] = _joint_softmax_pv(s_c, s_n, vc_ref[0].astype(BF16), vn_ref[0]).astype(BF16)


def _band_sample(qkv, kc, vc, bmc, bmn, n_heads):
    b, t, _ = qkv.shape
    lc = kc.shape[1]
    hd = HEAD_DIM
    return pl.pallas_call(
        _band_s_kernel,
        grid=(b, n_heads),
        in_specs=[
            pl.BlockSpec((1, t, hd), lambda bi, h: (bi, 0, h)),
            pl.BlockSpec((1, t, hd), lambda bi, h: (bi, 0, n_heads + h)),
            pl.BlockSpec((1, t, hd), lambda bi, h: (bi, 0, 2 * n_heads + h)),
            pl.BlockSpec((1, lc, hd), lambda bi, h: (bi, 0, h)),
            pl.BlockSpec((1, lc, hd), lambda bi, h: (bi, 0, h)),
            pl.BlockSpec((1, t, lc), lambda bi, h: (h, 0, 0)),
            pl.BlockSpec((1, t, t), lambda bi, h: (h, 0, 0)),
        ],
        out_specs=pl.BlockSpec((1, t, hd), lambda bi, h: (bi, 0, h)),
        out_shape=jax.ShapeDtypeStruct((b, t, n_heads * hd), BF16),
        compiler_params=_params("parallel", "parallel"),
        name="band_attn_sample",
    )(qkv, qkv, qkv, kc, vc, bmc, bmn)


def _t5_bucket_np(rel):
    nb = N_T5_BUCKETS // 2
    max_exact = nb // 2
    n = np.abs(rel)
    nf = np.maximum(n, 1).astype(np.float64)
    large = max_exact + (np.log(nf / max_exact) / math.log(T5_MAX_DIST / max_exact) * (nb - max_exact)).astype(np.int64)
    large = np.minimum(large, nb - 1)
    return np.where(rel > 0, nb, 0) + np.where(n < max_exact, n, large)


def _t5_bias_tiles(t5_table, q_pos, k_pos, far_bucket):
    rel = k_pos[None, :] - q_pos[:, None]
    mask = (k_pos[None, :] // CHUNK) <= (q_pos[:, None] // CHUNK)
    bias = jnp.transpose(t5_table.astype(F32)[_t5_bucket_np(rel)], (2, 0, 1))
    bias = bias - t5_table.astype(F32)[far_bucket][:, None, None]
    return jnp.where(mask[None], bias, NEG_INF)


def _band_bias_tiles(rel_table, q_pos, k_pos):
    rel = np.clip(k_pos[None, :] - q_pos[:, None], -REL_CLIP, REL_CLIP) + REL_CLIP
    qc = q_pos[:, None] // CHUNK
    kc = k_pos[None, :] // CHUNK
    mask = (kc <= qc) & (qc - kc <= C_PREV_CHUNKS) & (k_pos[None, :] >= 0)
    bias = jnp.transpose(rel_table.astype(F32)[rel], (2, 0, 1))
    return jnp.where(mask[None], bias, NEG_INF)


def kernel(x_prompt, x_sample, cache_a_k, cache_a_v, cache_b_k, cache_b_v, cache_b_logf, cache_c_k, cache_c_v, c_prompt, c_sample, w_ada, b_ada, norm_g, w_ffn_in, w_ffn_out, w_in_ab, b_forget, w_out_ab, lambda_q1, lambda_k1, lambda_q2, lambda_k2, subln_g, t5_table, w_in_c, w_out_c, c_rel_bias, final_g):
    depth = w_ada.shape[0]
    bsz, seq, d = x_prompt.shape
    dbsz, dseq, _ = x_sample.shape
    past = cache_b_logf.shape[2]
    h_a = cache_a_k.shape[3]
    h_b = cache_b_k.shape[3]
    h_c = cache_c_k.shape[3]
    wa, wb, cw = h_a * HEAD_DIM, h_b * HEAD_DIM, h_c * HEAD_DIM
    assert bsz == dbsz and wa == wb and 3 * wa + 3 * wb == 3 * cw
    assert LANES >= T5_MAX_DIST and SAMPLE_NEAR >= T5_MAX_DIST and past >= SAMPLE_NEAR
    far_bucket = (N_T5_BUCKETS // 2) - 1

    mods = _ada_mods(jnp.concatenate([c_prompt, c_sample], axis=0), w_ada, b_ada)
    mods = mods.reshape(depth, 2, bsz, 9, d).transpose(0, 3, 1, 2, 4).reshape(depth * 9 * 2, bsz, 1, d)

    def slot(l, k, grp):
        return (l * 9 + k) * 2 + grp

    xs = [x_prompt, x_sample]
    ab_states = [[], []]
    c_states = [[], []]
    for l in range(depth):
        i = l // 2
        last = l == depth - 1
        w1 = _prep_ffn_weights(w_ffn_in[l, 0], w_ffn_out[l, 0])
        w2 = _prep_ffn_weights(w_ffn_in[l, 1], w_ffn_out[l, 1])
        for grp in range(2):
            xs[grp] = _ffn(xs[grp], mods, [slot(l, k, grp) for k in range(3)], norm_g[l, 0], *w1)

        if l % 2 == 0:
            lam_init = 0.8 - 0.6 * math.exp(-0.3 * l)
            lam = (jnp.exp(jnp.sum(lambda_q1[i].astype(F32) * lambda_k1[i].astype(F32)))
                   - jnp.exp(jnp.sum(lambda_q2[i].astype(F32) * lambda_k2[i].astype(F32))) + lam_init).reshape(1)
            n_main = 3 * wa + 3 * wb
            w_main = w_in_ab[i][:, :n_main].astype(BF16)
            wf = jnp.pad(w_in_ab[i][:, n_main:].T.astype(BF16), ((0, 16 - h_b), (0, 0)))
            bfo = b_forget[i].astype(F32).reshape(h_b, 1)
            w_out = w_out_ab[i].astype(BF16)
            g_sub = subln_g[i].astype(F32).reshape(1, HEAD_DIM)
            tn = min(PROJ_COL_TILE, wa)
            taps = _seg_taps(tn, wa, (1, 2, 4, 5), False)
            for grp in range(2):
                x = xs[grp]
                b, t, _ = x.shape
                tap_shapes = [((b, t, wa), True)] * 4
                if grp == 0:
                    qkv, ka, va, kb, vb, vt, logf_t = _proj(
                        x, mods, [slot(l, 3, grp), slot(l, 4, grp)], norm_g[l, 1], w_main, tn, tap_shapes, taps,
                        wf, bfo, vt_cols=_vt_cols(tn, wa, (2, 5)), vt_tile=ROW_TILE)
                else:
                    qkv, ka, va, kb, vb, logf_t = _proj(
                        x, mods, [slot(l, 3, grp), slot(l, 4, grp)], norm_g[l, 1], w_main, tn, tap_shapes, taps,
                        wf, bfo)
                logf = logf_t.reshape(h_b, b, t).transpose(1, 2, 0)
                ab_states[grp].append((ka.reshape(b, t, h_a, HEAD_DIM), va.reshape(b, t, h_a, HEAD_DIM),
                                       kb.reshape(b, t, h_b, HEAD_DIM), vb.reshape(b, t, h_b, HEAD_DIM), logf))
                logf_bh = logf_t.reshape(h_b, b, t).transpose(1, 0, 2)
                if grp == 0:
                    o_a = _diff_prompt(lam, t5_table.astype(F32).reshape(-1), qkv, vt, g_sub, h_a, 1.0 - lam_init)
                    fcum = _cumsum(logf_bh, jnp.zeros((b, h_b, LANES), F32))
                    nq = t // ROW_TILE
                    fref = fcum[:, :, ::ROW_TILE].reshape(-1)
                    o_b = _fox_prompt(fref, qkv, vt, fcum.reshape(b, h_b, nq, 1, ROW_TILE), h_b, 3 * h_a, h_a)
                else:
                    q_pos = past + np.arange(t)
                    bnear = _t5_bias_tiles(t5_table, q_pos, past - SAMPLE_NEAR + np.arange(SAMPLE_NEAR), far_bucket)
                    bnew = _t5_bias_tiles(t5_table, q_pos, q_pos, far_bucket)
                    kc = cache_a_k[i].reshape(b, past, wa)
                    vc = cache_a_v[i].reshape(b, past, wa)
                    o_a = _diff_sample(lam, qkv, kc, vc, bnear, bnew, g_sub, h_a, 1.0 - lam_init)
                    fc = _cumsum(cache_b_logf[i].astype(F32).transpose(0, 2, 1), jnp.zeros((b, h_b, LANES), F32))
                    carry = fc[:, :, past - 1:past]
                    fn = _cumsum(jnp.pad(logf_bh, ((0, 0), (0, 0), (0, LANES - t))),
                                 jnp.broadcast_to(carry, (b, h_b, LANES)))
                    o_b = _fox_sample(carry.reshape(-1), qkv, cache_b_k[i].reshape(b, past, wb),
                                      cache_b_v[i].reshape(b, past, wb), fc.reshape(b, h_b, 1, past),
                                      fn.reshape(b, h_b, 1, LANES), h_b, 3 * h_a)
                xs[grp] = _outproj([o_a, o_b], w_out, x, mods, slot(l, 5, grp))
        else:
            w_in = w_in_c[i].astype(BF16)
            w_out = w_out_c[i].astype(BF16)
            tn = min(PROJ_COL_TILE, cw)
            taps = _seg_taps(tn, cw, (1, 2), True)
            for grp in range(2):
                x = xs[grp]
                b, t, _ = x.shape
                keep = min(C_BAND, t)
                assert keep == min(t, ROW_TILE)
                tap_shapes = [((b, keep, cw), False)] * 2
                if grp == 0:
                    qkv, k_new, v_new, vt = _proj(
                        x, mods, [slot(l, 3, grp), slot(l, 4, grp)], norm_g[l, 1], w_in, tn, tap_shapes, taps,
                        vt_cols=_vt_cols(tn, cw, (2,)), vt_tile=BAND_TILE)
                else:
                    qkv, k_new, v_new = _proj(
                        x, mods, [slot(l, 3, grp), slot(l, 4, grp)], norm_g[l, 1], w_in, tn, tap_shapes, taps)
                c_states[grp].append((k_new.reshape(b, keep, h_c, HEAD_DIM), v_new.reshape(b, keep, h_c, HEAD_DIM)))
                if grp == 0:
                    o_c = _band_prompt(c_rel_bias[i].astype(F32).reshape(-1), qkv, vt, h_c)
                else:
                    lc = cache_c_k.shape[2]
                    q_pos = past + np.arange(t)
                    bmc = _band_bias_tiles(c_rel_bias[i], q_pos, past - lc + np.arange(lc))
                    bmn = _band_bias_tiles(c_rel_bias[i], q_pos, q_pos)
                    o_c = _band_sample(qkv, cache_c_k[i].reshape(b, lc, cw), cache_c_v[i].reshape(b, lc, cw), bmc, bmn, h_c)
                xs[grp] = _outproj([o_c], w_out, x, mods, slot(l, 5, grp))

        for grp in range(2):
            xs[grp] = _ffn(xs[grp], mods, [slot(l, k, grp) for k in (6, 7, 8)], norm_g[l, 2], *w2,
                           final_g=final_g if last else None)

    outs = [xs[0], xs[1]]
    for grp in range(2):
        st = ab_states[grp]
        outs += [jnp.stack([s[k] for s in st]) for k in range(5)]
        st = c_states[grp]
        outs += [jnp.stack([s[k] for s in st]) for k in range(2)]
    return tuple(outs)
```

```python
import functools
import math

import numpy as np
import jax
import jax.numpy as jnp
from jax import lax
from jax.experimental import pallas as pl
from jax.experimental.pallas import tpu as pltpu

F32 = jnp.float32
BF16 = jnp.bfloat16

CHUNK = 64
HEAD_DIM = 128
DA_HALF = HEAD_DIM // 2
N_T5_BUCKETS = 32
T5_MAX_DIST = 128
C_PREV_CHUNKS = 8
C_BAND = C_PREV_CHUNKS * CHUNK
REL_CLIP = 128
EPS = 1e-6
NEG_INF = -1e30

VMEM_LIMIT_BYTES = 56 * 1024 * 1024
LANES = 128

ROW_TILE = 512
FF_TILE = 512
FFN_OUT_TILE = 512
FFN_ROW_SPLIT = 2
PROJ_COL_TILE = 1024
ADA_COL_TILE = 1024
BAND_TILE = 256
BAND_HEADS = 2
ATTN_GROUP = 512
SAMPLE_NEAR = 256

_NT = (((1,), (1,)), ((), ()))


def _params(*sem, flags=None):
    return pltpu.CompilerParams(dimension_semantics=sem, vmem_limit_bytes=VMEM_LIMIT_BYTES, flags=flags)


def _modulated_norm(x, g, shift, scale):
    y = x * lax.rsqrt(jnp.mean(x * x, axis=-1, keepdims=True) + EPS) * g
    return y * (1.0 + scale) + shift


def _ada_kernel(c_ref, w_ref, b_ref, o_ref):
    c = c_ref[...]
    a = (c * jax.nn.sigmoid(c)).astype(BF16)
    o_ref[0] = jnp.dot(a, w_ref[0].astype(BF16), preferred_element_type=F32) + b_ref[0]


def _ada_mods(c_all, w_ada, b_ada):
    depth, d, n = w_ada.shape
    r = c_all.shape[0]
    tn = math.gcd(n, ADA_COL_TILE)
    return pl.pallas_call(
        _ada_kernel,
        grid=(depth, n // tn),
        in_specs=[
            pl.BlockSpec((r, d), lambda l, j: (0, 0)),
            pl.BlockSpec((1, d, tn), lambda l, j: (l, 0, j)),
            pl.BlockSpec((1, 1, tn), lambda l, j: (l, 0, j)),
        ],
        out_specs=pl.BlockSpec((1, r, tn), lambda l, j: (l, 0, j)),
        out_shape=jax.ShapeDtypeStruct((depth, r, n), F32),
        compiler_params=_params("arbitrary", "arbitrary"),
        name="ada_mods",
    )(c_all, w_ada, b_ada.reshape(depth, 1, n))


def _row_blocking(x):
    b, t, _ = x.shape
    if t >= ROW_TILE:
        assert t % ROW_TILE == 0
        return 1, ROW_TILE, t // ROW_TILE, b * (t // ROW_TILE)
    assert (b * t) % 8 == 0
    return b, t, 1, 1


def _x_index(nb, tps):
    if nb == 1:
        return lambda i, j: (i // tps, i % tps, 0)
    return lambda i, j: (0, 0, 0)


def _mod_spec(nb, tps, d, slot):
    if nb == 1:
        return pl.BlockSpec((None, 1, 1, d), lambda i, j: (slot, i // tps, 0, 0))
    return pl.BlockSpec((None, nb, 1, d), lambda i, j: (slot, 0, 0, 0))


def _ffn_kernel(*refs, nb, tt, n_ff, n_out, final):
    if final:
        (x_ref, shift_ref, scale_ref, gate_ref, g_ref, wg_ref, wu_ref, wo_ref, fg_ref,
         o_ref, h_scr, a_scr) = refs
    else:
        (x_ref, shift_ref, scale_ref, gate_ref, g_ref, wg_ref, wu_ref, wo_ref,
         o_ref, h_scr, a_scr) = refs
    j = pl.program_id(1)
    d = x_ref.shape[-1]
    tm = nb * tt
    tf = wg_ref.shape[1]
    tn = wo_ref.shape[1]

    @pl.when(j == 0)
    def _():
        h = _modulated_norm(x_ref[...], g_ref[...], shift_ref[...], scale_ref[...])
        h_scr[...] = h.reshape(tm, d).astype(BF16)

    @pl.when(j < n_ff)
    def _():
        rows = tm // FFN_ROW_SPLIT

        def products(r):
            hr = h_scr[r * rows:(r + 1) * rows, :]
            return (jnp.dot(hr, wg_ref[...], preferred_element_type=F32),
                    jnp.dot(hr, wu_ref[...], preferred_element_type=F32))

        parts = []
        cur = products(0)
        for r in range(FFN_ROW_SPLIT):
            nxt = products(r + 1) if r + 1 < FFN_ROW_SPLIT else None
            gg, uu = cur
            parts.append((gg * jax.nn.sigmoid(gg) * uu).astype(BF16))
            cur = nxt
        a_scr[j] = jnp.concatenate(parts, axis=0)

    @pl.when(j >= n_ff)
    def _():
        a = jnp.concatenate([a_scr[f] for f in range(n_ff)], axis=1)
        acc = jnp.dot(a, wo_ref[...], preferred_element_type=F32).reshape(nb, tt, tn)
        for n in range(n_out):
            @pl.when(j == n_ff + n)
            def _(n=n):
                cs = slice(n * tn, (n + 1) * tn)
                o_ref[:, :, cs] = x_ref[:, :, cs] + 0.5 * gate_ref[:, :, cs] * acc

    if final:
        @pl.when(j == n_ff + n_out - 1)
        def _():
            y = o_ref[...]
            o_ref[...] = y * lax.rsqrt(jnp.mean(y * y, axis=-1, keepdims=True) + EPS) * fg_ref[...]


def _ffn(x, mods, slots, g, w_in_p, w_out_p, final_g=None):
    b, t, d = x.shape
    nb, tt, tps, nrows = _row_blocking(x)
    n_out, ffp, tn = w_out_p.shape
    n_ff = ffp // FF_TILE
    final = final_g is not None
    in_specs = [
        pl.BlockSpec((nb, tt, d), _x_index(nb, tps)),
        _mod_spec(nb, tps, d, slots[0]),
        _mod_spec(nb, tps, d, slots[1]),
        _mod_spec(nb, tps, d, slots[2]),
        pl.BlockSpec((1, d), lambda i, j: (0, 0)),
        pl.BlockSpec((None, d, FF_TILE), lambda i, j: (jnp.minimum(j, n_ff - 1), 0, 0)),
        pl.BlockSpec((None, d, FF_TILE), lambda i, j: (jnp.minimum(j, n_ff - 1) + n_ff, 0, 0)),
        pl.BlockSpec((None, ffp, tn), lambda i, j: (jnp.maximum(j - n_ff, 0), 0, 0)),
    ]
    args = [x, mods, mods, mods, g.reshape(1, d), w_in_p, w_in_p, w_out_p]
    if final:
        in_specs.append(pl.BlockSpec((1, d), lambda i, j: (0, 0)))
        args.append(final_g.reshape(1, d))
    return pl.pallas_call(
        functools.partial(_ffn_kernel, nb=nb, tt=tt, n_ff=n_ff, n_out=n_out, final=final),
        grid=(nrows, n_ff + n_out),
        in_specs=in_specs,
        out_specs=pl.BlockSpec((nb, tt, d), _x_index(nb, tps)),
        out_shape=jax.ShapeDtypeStruct((b, t, d), F32),
        scratch_shapes=[pltpu.VMEM((nb * tt, d), BF16), pltpu.VMEM((n_ff, nb * tt, FF_TILE), BF16)],
        compiler_params=_params("parallel", "arbitrary"),
        name="ffn",
    )(*args)


def _prep_ffn_weights(w_in, w_out):
    ff = w_out.shape[0]
    ffp = -(-ff // FF_TILE) * FF_TILE
    pad = ffp - ff
    wg = jnp.pad(w_in[:, :ff].astype(BF16), ((0, 0), (0, pad)))
    wu = jnp.pad(w_in[:, ff:].astype(BF16), ((0, 0), (0, pad)))
    w_out_p = jnp.pad(w_out.astype(BF16), ((0, pad), (0, 0)))
    return _col_blocks(jnp.concatenate([wg, wu], axis=1), FF_TILE), _col_blocks(w_out_p, math.gcd(w_out.shape[1], FFN_OUT_TILE))


def _col_blocks(w, tn):
    k, n = w.shape
    return w.reshape(k, n // tn, tn).transpose(1, 0, 2)


def _log_sigmoid(x):
    return jnp.minimum(x, 0.0) - jnp.log1p(jnp.exp(-jnp.abs(x)))


def _proj_kernel(*refs, nb, tt, tn, tps, taps, has_f, vt_cols, vt_tile):
    x_ref, shift_ref, scale_ref, g_ref, w_ref = refs[:5]
    pos = 5
    if has_f:
        wf_ref, bf_ref = refs[5:7]
        pos = 7
    obf_ref = refs[pos]
    tap_refs = refs[pos + 1:pos + 1 + len({tp[0] for tp in taps})]
    pos = pos + 1 + len(tap_refs)
    if vt_cols:
        vt_ref = refs[pos]
        pos += 1
    if has_f:
        logf_ref = refs[pos]
        pos += 1
    h_scr = refs[pos]
    i = pl.program_id(0)
    j = pl.program_id(1)
    d = x_ref.shape[-1]

    @pl.when(j == 0)
    def _():
        h = _modulated_norm(x_ref[...], g_ref[...], shift_ref[...], scale_ref[...])
        h_scr[...] = h.reshape(nb * tt, d).astype(BF16)
        if has_f:
            fr = lax.dot_general(wf_ref[...], h_scr[...], _NT, preferred_element_type=F32)
            logf_ref[...] = _log_sigmoid(fr[:logf_ref.shape[0]] + bf_ref[...])

    y = jnp.dot(h_scr[...], w_ref[...], preferred_element_type=F32)
    obf_ref[...] = y.astype(BF16).reshape(nb, tt, tn)
    for out_idx, jval, col0, last_rows_only in taps:
        cond = j == jval
        if last_rows_only and tps > 1:
            cond = jnp.logical_and(cond, i % tps == tps - 1)

        @pl.when(cond)
        def _(out_idx=out_idx, col0=col0):
            tap_refs[out_idx][:, :, col0:col0 + tn] = y.reshape(nb, tt, tn)

    for jval, head0 in vt_cols:
        @pl.when(j == jval)
        def _(head0=head0):
            for hh in range(tn // HEAD_DIM):
                yt = y[:, hh * HEAD_DIM:(hh + 1) * HEAD_DIM].T.astype(BF16)
                for s in range(tt // vt_tile):
                    vt_ref[0, head0 + hh, s] = yt[:, s * vt_tile:(s + 1) * vt_tile]


def _seg_taps(tn, width, segments, last_rows_only):
    per = width // tn
    assert per * tn == width
    return [(o, sidx * per + s, s * tn, last_rows_only) for o, sidx in enumerate(segments) for s in range(per)]


def _vt_cols(tn, width, segments):
    per = width // tn
    hpb = tn // HEAD_DIM
    return [(sidx * per + s, (o * per + s) * hpb) for o, sidx in enumerate(segments) for s in range(per)]


def _proj(x, mods, slots, g, w_bf, tn, tap_shapes, taps, wf=None, bf=None, vt_cols=(), vt_tile=None):
    b, t, d = x.shape
    n = w_bf.shape[1]
    nb, tt, tps, nrows = _row_blocking(x)
    has_f = wf is not None
    assert not vt_cols or nb == 1
    in_specs = [
        pl.BlockSpec((nb, tt, d), _x_index(nb, tps)),
        _mod_spec(nb, tps, d, slots[0]),
        _mod_spec(nb, tps, d, slots[1]),
        pl.BlockSpec((1, d), lambda i, j: (0, 0)),
        pl.BlockSpec((None, d, tn), lambda i, j: (j, 0, 0)),
    ]
    args = [x, mods, mods, g.reshape(1, d), _col_blocks(w_bf, tn)]
    if has_f:
        in_specs += [pl.BlockSpec(wf.shape, lambda i, j: (0, 0)), pl.BlockSpec(bf.shape, lambda i, j: (0, 0))]
        args += [wf, bf]
    xi = _x_index(nb, tps)
    out_specs = [pl.BlockSpec((nb, tt, tn), lambda i, j: xi(i, j)[:2] + (j,))]
    out_shape = [jax.ShapeDtypeStruct((b, t, n), BF16)]
    for shp, follows_rows in tap_shapes:
        if follows_rows:
            out_specs.append(pl.BlockSpec((nb, tt, shp[2]), xi))
        elif nb == 1:
            out_specs.append(pl.BlockSpec((1, shp[1], shp[2]), lambda i, j: (i // tps, 0, 0)))
        else:
            out_specs.append(pl.BlockSpec(shp, lambda i, j: (0, 0, 0)))
        out_shape.append(jax.ShapeDtypeStruct(shp, F32))
    if vt_cols:
        n_vh = len(vt_cols) * (tn // HEAD_DIM)
        out_specs.append(pl.BlockSpec((1, n_vh, tt // vt_tile, HEAD_DIM, vt_tile), lambda i, j: (i // tps, 0, i % tps, 0, 0)))
        out_shape.append(jax.ShapeDtypeStruct((b, n_vh, t // vt_tile, HEAD_DIM, vt_tile), BF16))
    if has_f:
        nf = bf.shape[0]
        out_specs.append(pl.BlockSpec((nf, nb * tt), lambda i, j: (0, i)))
        out_shape.append(jax.ShapeDtypeStruct((nf, b * t), F32))
    return pl.pallas_call(
        functools.partial(_proj_kernel, nb=nb, tt=tt, tn=tn, tps=tps, taps=tuple(taps), has_f=has_f,
                          vt_cols=tuple(vt_cols), vt_tile=vt_tile),
        grid=(nrows, n // tn),
        in_specs=in_specs,
        out_specs=out_specs,
        out_shape=out_shape,
        scratch_shapes=[pltpu.VMEM((nb * tt, d), BF16)],
        compiler_params=_params("arbitrary", "arbitrary"),
        name="mixer_in_proj",
    )(*args)


def _outproj_kernel(*refs, nb, tt, n_in):
    o_refs = refs[:n_in]
    w_ref, x_ref, gate_ref, out_ref = refs[n_in:]
    d = x_ref.shape[-1]
    acc = None
    row0 = 0
    for o_ref in o_refs:
        wd = o_ref.shape[-1]
        part = jnp.dot(o_ref[...].reshape(nb * tt, wd), w_ref[row0:row0 + wd, :], preferred_element_type=F32)
        acc = part if acc is None else acc + part
        row0 += wd
    out_ref[...] = x_ref[...] + gate_ref[...] * acc.reshape(nb, tt, d)


def _outproj(o_list, w_bf, x, mods, gate_slot):
    b, t, d = x.shape
    nb, tt, tps, nrows = _row_blocking(x)
    xi = _x_index(nb, tps)
    in_specs = [pl.BlockSpec((nb, tt, o.shape[-1]), xi) for o in o_list]
    in_specs += [
        pl.BlockSpec(w_bf.shape, lambda i, j: (0, 0)),
        pl.BlockSpec((nb, tt, d), xi),
        _mod_spec(nb, tps, d, gate_slot),
    ]
    return pl.pallas_call(
        functools.partial(_outproj_kernel, nb=nb, tt=tt, n_in=len(o_list)),
        grid=(nrows, 1),
        in_specs=in_specs,
        out_specs=pl.BlockSpec((nb, tt, d), xi),
        out_shape=jax.ShapeDtypeStruct((b, t, d), F32),
        compiler_params=_params("parallel", "arbitrary"),
        name="mixer_out_proj",
    )(*o_list, w_bf, x, mods)


def _cumsum_kernel(x_ref, c_ref, o_ref):
    x = x_ref[0]
    n = x.shape[1]
    lane = lax.broadcasted_iota(jnp.int32, x.shape, 1)
    s = 1
    while s < n:
        x = x + jnp.where(lane >= s, pltpu.roll(x, s, axis=1), 0.0)
        s *= 2
    o_ref[0] = x + c_ref[0][:, :1]


def _cumsum(x, carry):
    g, h, n = x.shape
    return pl.pallas_call(
        _cumsum_kernel,
        grid=(g,),
        in_specs=[pl.BlockSpec((1, h, n), lambda i: (i, 0, 0)), pl.BlockSpec((1, h, LANES), lambda i: (i, 0, 0))],
        out_specs=pl.BlockSpec((1, h, n), lambda i: (i, 0, 0)),
        out_shape=jax.ShapeDtypeStruct((g, h, n), F32),
        compiler_params=_params("arbitrary"),
        name="logf_cumsum",
    )(x, carry)


def _softmax_init(m_scr, l_scr, acc_scr):
    m_scr[...] = jnp.full(m_scr.shape, NEG_INF, F32)
    l_scr[...] = jnp.zeros(l_scr.shape, F32)
    acc_scr[...] = jnp.zeros(acc_scr.shape, F32)


def _softmax_update_t(s, s_max, vt, m_prev, l_prev, acc_prev):
    m_new = jnp.maximum(m_prev, s_max)
    alpha = jnp.exp(m_prev - m_new)
    p = jnp.exp(s - m_new)
    l_new = alpha * l_prev + jnp.sum(p, axis=0, keepdims=True)
    acc_new = alpha * acc_prev + jnp.dot(vt, p.astype(BF16), preferred_element_type=F32)
    return m_new, l_new, acc_new


def _with_max(s):
    return s, jnp.max(s, axis=0, keepdims=True)


def _consume_tile(scores, vt, m_scr, l_scr, acc_scr, next_scores=None):
    n = len(scores) // 2
    g = m_scr.shape[1] // n
    sls = [slice(i * g, (i + 1) * g) for i in range(n)]
    state = [(m_scr[:, sl], l_scr[:, sl], acc_scr[:, sl]) for sl in sls]
    new, nxt = [], []
    for i in range(n):
        if next_scores is not None:
            nxt.extend(next_scores(i))
        new.append(_softmax_update_t(scores[2 * i], scores[2 * i + 1], vt, *state[i]))
    for sl, (m_new, l_new, acc_new) in zip(sls, new):
        m_scr[:, sl] = m_new
        l_scr[:, sl] = l_new
        acc_scr[:, sl] = acc_new
    return tuple(nxt)


def _toeplitz_values(tbl_ref, idx, n_vals, nh, h, ref_row):
    ref_val = tbl_ref[ref_row * nh + h]

    def body(v, acc):
        return jnp.where(idx == v, tbl_ref[v * nh + h] - ref_val, acc)

    return lax.fori_loop(0, n_vals, body, jnp.zeros(idx.shape, F32))


def _stack_diff_queries(q):
    lane = lax.broadcasted_iota(jnp.int32, q.shape, 1)
    zero = jnp.zeros_like(q)
    qs = jnp.concatenate([jnp.where(lane < DA_HALF, q, zero), jnp.where(lane >= DA_HALF, q, zero)], axis=0)
    return qs * jnp.asarray(DA_HALF ** -0.5, BF16)


def _diff_finalize(o, lam, g, out_scale):
    tq = o.shape[0] // 2
    od = o[:tq] - lam * o[tq:]
    return od * lax.rsqrt(jnp.mean(od * od, axis=-1, keepdims=True) + EPS) * g * out_scale


def _diff_p_kernel(lam_ref, tbl_ref, q_ref, k_ref, vt_ref, idx_ref, base_ref, g_ref, o_ref,
                   m_scr, l_scr, acc_scr, bias_scr, *, tq, nh, far_bucket, out_scale):
    h = pl.program_id(0)
    bi = pl.program_id(1)
    qi = pl.program_id(2)
    nblk = tq // LANES

    @pl.when(jnp.logical_and(bi == 0, qi == 0))
    def _():
        t = _toeplitz_values(tbl_ref, idx_ref[...], N_T5_BUCKETS, nh, h, far_bucket)
        bias_scr[...] = base_ref[...]
        for blk in range(nblk):
            sl = slice(blk * LANES, (blk + 1) * LANES)
            bias_scr[0, sl, sl] += t[0]
            if blk >= 1:
                bias_scr[0, (blk - 1) * LANES:blk * LANES, sl] += t[1]
        bias_scr[1, (nblk - 1) * LANES:, :LANES] += t[1]

    qs = _stack_diff_queries(q_ref[0])
    _softmax_init(m_scr, l_scr, acc_scr)
    ngrp = 2 * tq // ATTN_GROUP

    def scores(j, slot):
        k = k_ref[0, pl.ds(pl.multiple_of(j * tq, tq), tq), :]

        def group(i):
            s = lax.dot_general(k, qs[i * ATTN_GROUP:(i + 1) * ATTN_GROUP], _NT, preferred_element_type=F32)
            if slot is not None:
                q0 = (i * ATTN_GROUP) % tq
                s = s + bias_scr[slot, :, q0:q0 + ATTN_GROUP]
            return _with_max(s)
        return group

    n_far = jnp.maximum(qi - 1, 0)
    before_slot = jnp.where(qi == 0, 2, 1)

    @pl.when(n_far >= 1)
    def _():
        first = scores(0, None)
        cur = tuple(x for i in range(ngrp) for x in first(i))

        def far(j, cur):
            return _consume_tile(cur, vt_ref[0, 0, j], m_scr, l_scr, acc_scr, scores(j + 1, None))

        cur = lax.fori_loop(0, n_far - 1, far, cur)
        _consume_tile(cur, vt_ref[0, 0, n_far - 1], m_scr, l_scr, acc_scr)

    before = scores(n_far, before_slot)
    cur = tuple(x for i in range(ngrp) for x in before(i))
    cur = _consume_tile(cur, vt_ref[0, 0, n_far], m_scr, l_scr, acc_scr, scores(qi, 0))
    _consume_tile(cur, vt_ref[0, 0, qi], m_scr, l_scr, acc_scr)
    o = (acc_scr[...] / l_scr[...]).T
    o_ref[0] = _diff_finalize(o, lam_ref[0], g_ref[...], out_scale).astype(BF16)


def _diff_prompt(lam, tbl, qkv, vt, g, n_heads, out_scale):
    b, t, _ = qkv.shape
    tq = vt.shape[-1]
    far_bucket = N_T5_BUCKETS // 2 - 1
    pos = np.arange(LANES)
    idx = np.stack([_t5_bucket_np(pos[:, None] - pos[None, :]),
                    _t5_bucket_np(pos[:, None] - pos[None, :] - LANES)]).astype(np.int32)
    kq = np.arange(tq)
    base = np.zeros((3, tq, tq), np.float32)
    base[0] = np.where((kq[:, None] // CHUNK) <= (kq[None, :] // CHUNK), 0.0, NEG_INF)
    base[2] = NEG_INF
    return pl.pallas_call(
        functools.partial(_diff_p_kernel, tq=tq, nh=n_heads, far_bucket=far_bucket, out_scale=out_scale),
        grid=(n_heads, b, t // tq),
        in_specs=[
            pl.BlockSpec(memory_space=pltpu.SMEM),
            pl.BlockSpec(memory_space=pltpu.SMEM),
            pl.BlockSpec((1, tq, HEAD_DIM), lambda h, bi, qi: (bi, qi, h)),
            pl.BlockSpec((1, t, HEAD_DIM), lambda h, bi, qi: (bi, 0, n_heads + h)),
            pl.BlockSpec((1, 1, t // tq, HEAD_DIM, tq), lambda h, bi, qi: (bi, h, 0, 0, 0)),
            pl.BlockSpec(idx.shape, lambda h, bi, qi: (0, 0, 0)),
            pl.BlockSpec(base.shape, lambda h, bi, qi: (0, 0, 0)),
            pl.BlockSpec((1, HEAD_DIM), lambda h, bi, qi: (0, 0)),
        ],
        out_specs=pl.BlockSpec((1, tq, HEAD_DIM), lambda h, bi, qi: (bi, qi, h)),
        out_shape=jax.ShapeDtypeStruct((b, t, n_heads * HEAD_DIM), BF16),
        scratch_shapes=[pltpu.VMEM((1, 2 * tq), F32), pltpu.VMEM((1, 2 * tq), F32),
                        pltpu.VMEM((HEAD_DIM, 2 * tq), F32), pltpu.VMEM((3, tq, tq), F32)],
        compiler_params=_params("arbitrary", "arbitrary", "arbitrary"),
        name="diff_attn_prompt",
    )(lam, tbl, qkv, qkv, vt, jnp.asarray(idx), jnp.asarray(base), g)


def _fox_p_kernel(fref_ref, q_ref, k_ref, vt_ref, f_ref, o_ref, m_scr, l_scr, acc_scr, fcol_scr, *, tq):
    bi = pl.program_id(0)
    h = pl.program_id(1)
    qi = pl.program_id(2)
    nblk = tq // LANES

    @pl.when(qi == 0)
    def _():
        for c in range(f_ref.shape[2] * nblk):
            row = f_ref[0, 0, c // nblk][:, (c % nblk) * LANES:(c % nblk + 1) * LANES]
            fcol_scr[c * LANES:(c + 1) * LANES, :] = jnp.broadcast_to(row, (LANES, LANES)).T

    q = q_ref[0]
    fref = fref_ref[(bi * pl.num_programs(1) + h) * pl.num_programs(2) + qi]
    _softmax_init(m_scr, l_scr, acc_scr)

    ngrp = tq // ATTN_GROUP

    def scores(j, causal):
        start = pl.multiple_of(j * tq, tq)
        k = k_ref[0, pl.ds(start, tq), :]
        decay = fref - fcol_scr[pl.ds(start, tq), :]
        decay = jnp.concatenate([decay] * (ATTN_GROUP // LANES), axis=1)

        def group(i):
            s = lax.dot_general(k, q[i * ATTN_GROUP:(i + 1) * ATTN_GROUP], _NT, preferred_element_type=F32)
            s = s * (HEAD_DIM ** -0.5) + decay
            if causal:
                key = lax.broadcasted_iota(jnp.int32, s.shape, 0)
                qry = lax.broadcasted_iota(jnp.int32, s.shape, 1) + i * ATTN_GROUP
                s = jnp.where(key <= qry, s, NEG_INF)
            return _with_max(s)
        return group

    @pl.when(qi >= 1)
    def _():
        first = scores(0, False)
        cur = tuple(x for i in range(ngrp) for x in first(i))

        def far(j, cur):
            return _consume_tile(cur, vt_ref[0, 0, j], m_scr, l_scr, acc_scr, scores(j + 1, False))

        cur = lax.fori_loop(0, qi - 1, far, cur)
        _consume_tile(cur, vt_ref[0, 0, qi - 1], m_scr, l_scr, acc_scr)

    diag = scores(qi, True)
    _consume_tile(tuple(x for i in range(ngrp) for x in diag(i)), vt_ref[0, 0, qi], m_scr, l_scr, acc_scr)
    o_ref[0] = (acc_scr[...] / l_scr[...]).T.astype(BF16)


def _fox_prompt(fref, qkv, vt, fcum, n_heads, col0, vt_head0):
    b, t, _ = qkv.shape
    tq = vt.shape[-1]
    nq = t // tq
    return pl.pallas_call(
        functools.partial(_fox_p_kernel, tq=tq),
        grid=(b, n_heads, nq),
        in_specs=[
            pl.BlockSpec(memory_space=pltpu.SMEM),
            pl.BlockSpec((1, tq, HEAD_DIM), lambda bi, h, qi: (bi, qi, col0 + h)),
            pl.BlockSpec((1, t, HEAD_DIM), lambda bi, h, qi: (bi, 0, col0 + n_heads + h)),
            pl.BlockSpec((1, 1, nq, HEAD_DIM, tq), lambda bi, h, qi: (bi, vt_head0 + h, 0, 0, 0)),
            pl.BlockSpec((1, 1, nq, 1, tq), lambda bi, h, qi: (bi, h, 0, 0, 0)),
        ],
        out_specs=pl.BlockSpec((1, tq, HEAD_DIM), lambda bi, h, qi: (bi, qi, h)),
        out_shape=jax.ShapeDtypeStruct((b, t, n_heads * HEAD_DIM), BF16),
        scratch_shapes=[pltpu.VMEM((1, tq), F32), pltpu.VMEM((1, tq), F32), pltpu.VMEM((HEAD_DIM, tq), F32),
                        pltpu.VMEM((t, LANES), F32)],
        compiler_params=_params("arbitrary", "arbitrary", "arbitrary"),
        name="fox_attn_prompt",
    )(fref, qkv, qkv, vt, fcum)


def _joint_softmax_pv(s_c, s_n, vc, vn):
    m = jnp.maximum(jnp.max(s_c, axis=1, keepdims=True), jnp.max(s_n, axis=1, keepdims=True))
    p_c = jnp.exp(s_c - m)
    p_n = jnp.exp(s_n - m)
    l = jnp.sum(p_c, axis=1, keepdims=True) + jnp.sum(p_n, axis=1, keepdims=True)
    acc = jnp.dot(p_c.astype(BF16), vc, preferred_element_type=F32)
    acc = acc + jnp.dot(p_n.astype(BF16), vn, preferred_element_type=F32)
    return acc / l


def _diff_s_kernel(lam_ref, q_ref, kn_ref, vn_ref, kc_ref, vc_ref, bnear_ref, bnew_ref, g_ref, o_ref, *, near, out_scale):
    qs = _stack_diff_queries(q_ref[0])
    kc = kc_ref[0].astype(BF16)
    vc = vc_ref[0].astype(BF16)
    p = kc.shape[0]
    s_c = lax.dot_general(qs, kc, _NT, preferred_element_type=F32)
    bnear = bnear_ref[0]
    s_c = jnp.concatenate([s_c[:, :p - near], s_c[:, p - near:] + jnp.concatenate([bnear, bnear], axis=0)], axis=1)
    bnew = bnew_ref[0]
    s_n = lax.dot_general(qs, kn_ref[0], _NT, preferred_element_type=F32) + jnp.concatenate([bnew, bnew], axis=0)
    o = _joint_softmax_pv(s_c, s_n, vc, vn_ref[0])
    o_ref[0] = _diff_finalize(o, lam_ref[0], g_ref[...], out_scale).astype(BF16)


def _diff_sample(lam, qkv, kc, vc, bnear, bnew, g, n_heads, out_scale):
    b, t, _ = qkv.shape
    p = kc.shape[1]
    near = bnear.shape[-1]
    hd = HEAD_DIM
    return pl.pallas_call(
        functools.partial(_diff_s_kernel, near=near, out_scale=out_scale),
        grid=(b, n_heads),
        in_specs=[
            pl.BlockSpec(memory_space=pltpu.SMEM),
            pl.BlockSpec((1, t, hd), lambda bi, h: (bi, 0, h)),
            pl.BlockSpec((1, t, hd), lambda bi, h: (bi, 0, n_heads + h)),
            pl.BlockSpec((1, t, hd), lambda bi, h: (bi, 0, 2 * n_heads + h)),
            pl.BlockSpec((1, p, hd), lambda bi, h: (bi, 0, h)),
            pl.BlockSpec((1, p, hd), lambda bi, h: (bi, 0, h)),
            pl.BlockSpec((1, t, near), lambda bi, h: (h, 0, 0)),
            pl.BlockSpec((1, t, t), lambda bi, h: (h, 0, 0)),
            pl.BlockSpec((1, hd), lambda bi, h: (0, 0)),
        ],
        out_specs=pl.BlockSpec((1, t, hd), lambda bi, h: (bi, 0, h)),
        out_shape=jax.ShapeDtypeStruct((b, t, n_heads * hd), BF16),
        compiler_params=_params("parallel", "parallel"),
        name="diff_attn_sample",
    )(lam, qkv, qkv, qkv, kc, vc, bnear, bnew, g)


def _fox_s_kernel(fref_ref, q_ref, kn_ref, vn_ref, kc_ref, vc_ref, fc_ref, fn_ref, o_ref):
    bi = pl.program_id(0)
    h = pl.program_id(1)
    q = q_ref[0]
    t = q.shape[0]
    fref = fref_ref[bi * pl.num_programs(1) + h]
    scale = HEAD_DIM ** -0.5
    s_c = lax.dot_general(q, kc_ref[0].astype(BF16), _NT, preferred_element_type=F32) * scale + (fref - fc_ref[0, 0])
    s_n = lax.dot_general(q, kn_ref[0], _NT, preferred_element_type=F32) * scale + (fref - fn_ref[0, 0][:, :t])
    row = lax.broadcasted_iota(jnp.int32, s_n.shape, 0)
    col = lax.broadcasted_iota(jnp.int32, s_n.shape, 1)
    s_n = jnp.where(col <= row, s_n, NEG_INF)
    o_ref[0] = _joint_softmax_pv(s_c, s_n, vc_ref[0].astype(BF16), vn_ref[0]).astype(BF16)


def _fox_sample(fref, qkv, kc, vc, fc, fn, n_heads, col0):
    b, t, _ = qkv.shape
    p = kc.shape[1]
    hd = HEAD_DIM
    return pl.pallas_call(
        _fox_s_kernel,
        grid=(b, n_heads),
        in_specs=[
            pl.BlockSpec(memory_space=pltpu.SMEM),
            pl.BlockSpec((1, t, hd), lambda bi, h: (bi, 0, col0 + h)),
            pl.BlockSpec((1, t, hd), lambda bi, h: (bi, 0, col0 + n_heads + h)),
            pl.BlockSpec((1, t, hd), lambda bi, h: (bi, 0, col0 + 2 * n_heads + h)),
            pl.BlockSpec((1, p, hd), lambda bi, h: (bi, 0, h)),
            pl.BlockSpec((1, p, hd), lambda bi, h: (bi, 0, h)),
            pl.BlockSpec((1, 1, 1, p), lambda bi, h: (bi, h, 0, 0)),
            pl.BlockSpec((1, 1, 1, fn.shape[-1]), lambda bi, h: (bi, h, 0, 0)),
        ],
        out_specs=pl.BlockSpec((1, t, hd), lambda bi, h: (bi, 0, h)),
        out_shape=jax.ShapeDtypeStruct((b, t, n_heads * hd), BF16),
        compiler_params=_params("parallel", "parallel"),
        name="fox_attn_sample",
    )(fref, qkv, qkv, qkv, kc, vc, fc, fn)


def _band_p_kernel(tbl_ref, q_ref, k_ref, vt_ref, idx_ref, base_ref, o_ref, bm_scr, *, tq, nkb, nh, hps):
    hg = pl.program_id(0)
    bi = pl.program_id(1)
    i = pl.program_id(2)
    nblk = tq // LANES
    qoff = (nkb - 1) * nblk

    @pl.when(jnp.logical_and(bi == 0, i == 0))
    def _():
        for e in range(hps):
            t = _toeplitz_values(tbl_ref, idx_ref[...], 2 * REL_CLIP + 1, nh, hg * hps + e, 0)
            bm_scr[e] = base_ref[...]
            for bq in range(nblk):
                kd = qoff + bq
                bm_scr[e, kd * LANES:(kd + 1) * LANES, bq * LANES:(bq + 1) * LANES] += t[0]
                bm_scr[e, (kd - 1) * LANES:kd * LANES, bq * LANES:(bq + 1) * LANES] += t[1]

    scale = HEAD_DIM ** -0.5
    outs = []
    for e in range(hps):
        hs = slice(e * HEAD_DIM, (e + 1) * HEAD_DIM)
        q = q_ref[0, :, hs]
        ss, vts = [], []
        for blk in range(nkb):
            jb = i - (nkb - 1) + blk
            jc = jnp.maximum(jb, 0)
            k = k_ref[0, pl.ds(pl.multiple_of(jc * tq, tq), tq), hs]
            s = lax.dot_general(k, q, _NT, preferred_element_type=F32) * scale + bm_scr[e, blk * tq:(blk + 1) * tq, :]
            if blk < nkb - 1:
                s = s + jnp.where(jb < 0, NEG_INF, 0.0)
            ss.append(s)
            vts.append(vt_ref[0, e, jc])
        m = functools.reduce(jnp.maximum, [jnp.max(s, axis=0, keepdims=True) for s in ss])
        l = None
        acc = None
        for s, vt in zip(ss, vts):
            p = jnp.exp(s - m)
            ls = jnp.sum(p, axis=0, keepdims=True)
            a = jnp.dot(vt, p.astype(BF16), preferred_element_type=F32)
            l = ls if l is None else l + ls
            acc = a if acc is None else acc + a
        outs.append((acc / l).T.astype(BF16))
    o_ref[0] = jnp.concatenate(outs, axis=1)


def _band_prompt(tbl, qkv, vt, n_heads):
    b, t, _ = qkv.shape
    tq = vt.shape[-1]
    nkb = C_BAND // tq + 1
    assert (nkb - 1) * tq == C_BAND and tq % LANES == 0 and LANES >= REL_CLIP
    pos = np.arange(LANES)
    rel = pos[:, None] - pos[None, :]
    idx = (np.stack([np.clip(rel, -REL_CLIP, REL_CLIP), np.clip(rel - LANES, -REL_CLIP, REL_CLIP)]) + REL_CLIP).astype(np.int32)
    k_pos = np.arange(nkb * tq)[:, None]
    q_pos = C_BAND + np.arange(tq)[None, :]
    kc, qc = k_pos // CHUNK, q_pos // CHUNK
    base = np.where((kc <= qc) & (qc - kc <= C_PREV_CHUNKS), 0.0, NEG_INF).astype(np.float32)
    hps = BAND_HEADS
    ngrp = n_heads // hps
    assert ngrp * hps == n_heads
    wd = hps * HEAD_DIM
    return pl.pallas_call(
        functools.partial(_band_p_kernel, tq=tq, nkb=nkb, nh=n_heads, hps=hps),
        grid=(ngrp, b, t // tq),
        in_specs=[
            pl.BlockSpec(memory_space=pltpu.SMEM),
            pl.BlockSpec((1, tq, wd), lambda h, bi, i: (bi, i, h)),
            pl.BlockSpec((1, t, wd), lambda h, bi, i: (bi, 0, ngrp + h)),
            pl.BlockSpec((1, hps, t // tq, HEAD_DIM, tq), lambda h, bi, i: (bi, h, 0, 0, 0)),
            pl.BlockSpec(idx.shape, lambda h, bi, i: (0, 0, 0)),
            pl.BlockSpec(base.shape, lambda h, bi, i: (0, 0)),
        ],
        out_specs=pl.BlockSpec((1, tq, wd), lambda h, bi, i: (bi, i, h)),
        out_shape=jax.ShapeDtypeStruct((b, t, n_heads * HEAD_DIM), BF16),
        scratch_shapes=[pltpu.VMEM((hps, nkb * tq, tq), F32)],
        compiler_params=_params("arbitrary", "arbitrary", "arbitrary"),
        name="band_attn_prompt",
    )(tbl, qkv, qkv, vt, jnp.asarray(idx), jnp.asarray(base))


def _band_s_kernel(q_ref, kn_ref, vn_ref, kc_ref, vc_ref, bmc_ref, bmn_ref, o_ref):
    q = q_ref[0]
    scale = HEAD_DIM ** -0.5
    s_c = lax.dot_general(q, kc_ref[0].astype(BF16), _NT, preferred_element_type=F32) * scale + bmc_ref[0]
    s_n = lax.dot_general(q, kn_ref[0], _NT, preferred_element_type=F32) * scale + bmn_ref[0]
    o_ref[0] = _joint_softmax_pv(s_c, s_n, vc_ref[0].astype(BF16), vn_ref[0]).astype(BF16)


def _band_sample(qkv, kc, vc, bmc, bmn, n_heads):
    b, t, _ = qkv.shape
    lc = kc.shape[1]
    hd = HEAD_DIM
    return pl.pallas_call(
        _band_s_kernel,
        grid=(b, n_heads),
        in_specs=[
            pl.BlockSpec((1, t, hd), lambda bi, h: (bi, 0, h)),
            pl.BlockSpec((1, t, hd), lambda bi, h: (bi, 0, n_heads + h)),
            pl.BlockSpec((1, t, hd), lambda bi, h: (bi, 0, 2 * n_heads + h)),
            pl.BlockSpec((1, lc, hd), lambda bi, h: (bi, 0, h)),
            pl.BlockSpec((1, lc, hd), lambda bi, h: (bi, 0, h)),
            pl.BlockSpec((1, t, lc), lambda bi, h: (h, 0, 0)),
            pl.BlockSpec((1, t, t), lambda bi, h: (h, 0, 0)),
        ],
        out_specs=pl.BlockSpec((1, t, hd), lambda bi, h: (bi, 0, h)),
        out_shape=jax.ShapeDtypeStruct((b, t, n_heads * hd), BF16),
        compiler_params=_params("parallel", "parallel"),
        name="band_attn_sample",
    )(qkv, qkv, qkv, kc, vc, bmc, bmn)


def _t5_bucket_np(rel):
    nb = N_T5_BUCKETS // 2
    max_exact = nb // 2
    n = np.abs(rel)
    nf = np.maximum(n, 1).astype(np.float64)
    large = max_exact + (np.log(nf / max_exact) / math.log(T5_MAX_DIST / max_exact) * (nb - max_exact)).astype(np.int64)
    large = np.minimum(large, nb - 1)
    return np.where(rel > 0, nb, 0) + np.where(n < max_exact, n, large)


def _t5_bias_tiles(t5_table, q_pos, k_pos, far_bucket):
    rel = k_pos[None, :] - q_pos[:, None]
    mask = (k_pos[None, :] // CHUNK) <= (q_pos[:, None] // CHUNK)
    bias = jnp.transpose(t5_table.astype(F32)[_t5_bucket_np(rel)], (2, 0, 1))
    bias = bias - t5_table.astype(F32)[far_bucket][:, None, None]
    return jnp.where(mask[None], bias, NEG_INF)


def _band_bias_tiles(rel_table, q_pos, k_pos):
    rel = np.clip(k_pos[None, :] - q_pos[:, None], -REL_CLIP, REL_CLIP) + REL_CLIP
    qc = q_pos[:, None] // CHUNK
    kc = k_pos[None, :] // CHUNK
    mask = (kc <= qc) & (qc - kc <= C_PREV_CHUNKS) & (k_pos[None, :] >= 0)
    bias = jnp.transpose(rel_table.astype(F32)[rel], (2, 0, 1))
    return jnp.where(mask[None], bias, NEG_INF)


def kernel(x_prompt, x_sample, cache_a_k, cache_a_v, cache_b_k, cache_b_v, cache_b_logf, cache_c_k, cache_c_v, c_prompt, c_sample, w_ada, b_ada, norm_g, w_ffn_in, w_ffn_out, w_in_ab, b_forget, w_out_ab, lambda_q1, lambda_k1, lambda_q2, lambda_k2, subln_g, t5_table, w_in_c, w_out_c, c_rel_bias, final_g):
    depth = w_ada.shape[0]
    bsz, seq, d = x_prompt.shape
    dbsz, dseq, _ = x_sample.shape
    past = cache_b_logf.shape[2]
    h_a = cache_a_k.shape[3]
    h_b = cache_b_k.shape[3]
    h_c = cache_c_k.shape[3]
    wa, wb, cw = h_a * HEAD_DIM, h_b * HEAD_DIM, h_c * HEAD_DIM
    assert bsz == dbsz and wa == wb and 3 * wa + 3 * wb == 3 * cw
    assert LANES >= T5_MAX_DIST and SAMPLE_NEAR >= T5_MAX_DIST and past >= SAMPLE_NEAR
    far_bucket = (N_T5_BUCKETS // 2) - 1

    mods = _ada_mods(jnp.concatenate([c_prompt, c_sample], axis=0), w_ada, b_ada)
    mods = mods.reshape(depth, 2, bsz, 9, d).transpose(0, 3, 1, 2, 4).reshape(depth * 9 * 2, bsz, 1, d)

    def slot(l, k, grp):
        return (l * 9 + k) * 2 + grp

    xs = [x_prompt, x_sample]
    ab_states = [[], []]
    c_states = [[], []]
    for l in range(depth):
        i = l // 2
        last = l == depth - 1
        w1 = _prep_ffn_weights(w_ffn_in[l, 0], w_ffn_out[l, 0])
        w2 = _prep_ffn_weights(w_ffn_in[l, 1], w_ffn_out[l, 1])
        for grp in range(2):
            xs[grp] = _ffn(xs[grp], mods, [slot(l, k, grp) for k in range(3)], norm_g[l, 0], *w1)

        if l % 2 == 0:
            lam_init = 0.8 - 0.6 * math.exp(-0.3 * l)
            lam = (jnp.exp(jnp.sum(lambda_q1[i].astype(F32) * lambda_k1[i].astype(F32)))
                   - jnp.exp(jnp.sum(lambda_q2[i].astype(F32) * lambda_k2[i].astype(F32))) + lam_init).reshape(1)
            n_main = 3 * wa + 3 * wb
            w_main = w_in_ab[i][:, :n_main].astype(BF16)
            wf = jnp.pad(w_in_ab[i][:, n_main:].T.astype(BF16), ((0, 16 - h_b), (0, 0)))
            bfo = b_forget[i].astype(F32).reshape(h_b, 1)
            w_out = w_out_ab[i].astype(BF16)
            g_sub = subln_g[i].astype(F32).reshape(1, HEAD_DIM)
            tn = min(PROJ_COL_TILE, wa)
            taps = _seg_taps(tn, wa, (1, 2, 4, 5), False)
            for grp in range(2):
                x = xs[grp]
                b, t, _ = x.shape
                tap_shapes = [((b, t, wa), True)] * 4
                if grp == 0:
                    qkv, ka, va, kb, vb, vt, logf_t = _proj(
                        x, mods, [slot(l, 3, grp), slot(l, 4, grp)], norm_g[l, 1], w_main, tn, tap_shapes, taps,
                        wf, bfo, vt_cols=_vt_cols(tn, wa, (2, 5)), vt_tile=ROW_TILE)
                else:
                    qkv, ka, va, kb, vb, logf_t = _proj(
                        x, mods, [slot(l, 3, grp), slot(l, 4, grp)], norm_g[l, 1], w_main, tn, tap_shapes, taps,
                        wf, bfo)
                logf = logf_t.reshape(h_b, b, t).transpose(1, 2, 0)
                ab_states[grp].append((ka.reshape(b, t, h_a, HEAD_DIM), va.reshape(b, t, h_a, HEAD_DIM),
                                       kb.reshape(b, t, h_b, HEAD_DIM), vb.reshape(b, t, h_b, HEAD_DIM), logf))
                logf_bh = logf_t.reshape(h_b, b, t).transpose(1, 0, 2)
                if grp == 0:
                    o_a = _diff_prompt(lam, t5_table.astype(F32).reshape(-1), qkv, vt, g_sub, h_a, 1.0 - lam_init)
                    fcum = _cumsum(logf_bh, jnp.zeros((b, h_b, LANES), F32))
                    nq = t // ROW_TILE
                    fref = fcum[:, :, ::ROW_TILE].reshape(-1)
                    o_b = _fox_prompt(fref, qkv, vt, fcum.reshape(b, h_b, nq, 1, ROW_TILE), h_b, 3 * h_a, h_a)
                else:
                    q_pos = past + np.arange(t)
                    bnear = _t5_bias_tiles(t5_table, q_pos, past - SAMPLE_NEAR + np.arange(SAMPLE_NEAR), far_bucket)
                    bnew = _t5_bias_tiles(t5_table, q_pos, q_pos, far_bucket)
                    kc = cache_a_k[i].reshape(b, past, wa)
                    vc = cache_a_v[i].reshape(b, past, wa)
                    o_a = _diff_sample(lam, qkv, kc, vc, bnear, bnew, g_sub, h_a, 1.0 - lam_init)
                    fc = _cumsum(cache_b_logf[i].astype(F32).transpose(0, 2, 1), jnp.zeros((b, h_b, LANES), F32))
                    carry = fc[:, :, past - 1:past]
                    fn = _cumsum(jnp.pad(logf_bh, ((0, 0), (0, 0), (0, LANES - t))),
                                 jnp.broadcast_to(carry, (b, h_b, LANES)))
                    o_b = _fox_sample(carry.reshape(-1), qkv, cache_b_k[i].reshape(b, past, wb),
                                      cache_b_v[i].reshape(b, past, wb), fc.reshape(b, h_b, 1, past),
                                      fn.reshape(b, h_b, 1, LANES), h_b, 3 * h_a)
                xs[grp] = _outproj([o_a, o_b], w_out, x, mods, slot(l, 5, grp))
        else:
            w_in = w_in_c[i].astype(BF16)
            w_out = w_out_c[i].astype(BF16)
            tn = min(PROJ_COL_TILE, cw)
            taps = _seg_taps(tn, cw, (1, 2), True)
            for grp in range(2):
                x = xs[grp]
                b, t, _ = x.shape
                keep = min(C_BAND, t)
                assert keep == min(t, ROW_TILE)
                tap_shapes = [((b, keep, cw), False)] * 2
                if grp == 0:
                    qkv, k_new, v_new, vt = _proj(
                        x, mods, [slot(l, 3, grp), slot(l, 4, grp)], norm_g[l, 1], w_in, tn, tap_shapes, taps,
                        vt_cols=_vt_cols(tn, cw, (2,)), vt_tile=BAND_TILE)
                else:
                    qkv, k_new, v_new = _proj(
                        x, mods, [slot(l, 3, grp), slot(l, 4, grp)], norm_g[l, 1], w_in, tn, tap_shapes, taps)
                c_states[grp].append((k_new.reshape(b, keep, h_c, HEAD_DIM), v_new.reshape(b, keep, h_c, HEAD_DIM)))
                if grp == 0:
                    o_c = _band_prompt(c_rel_bias[i].astype(F32).reshape(-1), qkv, vt, h_c)
                else:
                    lc = cache_c_k.shape[2]
                    q_pos = past + np.arange(t)
                    bmc = _band_bias_tiles(c_rel_bias[i], q_pos, past - lc + np.arange(lc))
                    bmn = _band_bias_tiles(c_rel_bias[i], q_pos, q_pos)
                    o_c = _band_sample(qkv, cache_c_k[i].reshape(b, lc, cw), cache_c_v[i].reshape(b, lc, cw), bmc, bmn, h_c)
                xs[grp] = _outproj([o_c], w_out, x, mods, slot(l, 5, grp))

        for grp in range(2):
            xs[grp] = _ffn(xs[grp], mods, [slot(l, k, grp) for k in (6, 7, 8)], norm_g[l, 2], *w2,
                           final_g=final_g if last else None)

    outs = [xs[0], xs[1]]
    for grp in range(2):
        st = ab_states[grp]
        outs += [jnp.stack([s[k] for s in st]) for k in range(5)]
        st = c_states[grp]
        outs += [jnp.stack([s[k] for s in st]) for k in range(2)]
    return tuple(outs)
```

```python
import functools
import math

import numpy as np
import jax
import jax.numpy as jnp
from jax import lax
from jax.experimental import pallas as pl
from jax.experimental.pallas import tpu as pltpu

F32 = jnp.float32
BF16 = jnp.bfloat16

CHUNK = 64
HEAD_DIM = 128
DA_HALF = HEAD_DIM // 2
N_T5_BUCKETS = 32
T5_MAX_DIST = 128
C_PREV_CHUNKS = 8
C_BAND = C_PREV_CHUNKS * CHUNK
REL_CLIP = 128
EPS = 1e-6
NEG_INF = -1e30
LOG2E = math.log2(math.e)
VT_ROWS = HEAD_DIM + 16

VMEM_LIMIT_BYTES = 56 * 1024 * 1024
LANES = 128

ROW_TILE = 512
FF_TILE = 512
FFN_OUT_TILE = 512
FFN_ROW_SPLIT = 2
PROJ_COL_TILE = 1024
ADA_COL_TILE = 1024
BAND_TILE = 256
BAND_HEADS = 4
ATTN_GROUP = 512
SAMPLE_NEAR = 256

_NT = (((1,), (1,)), ((), ()))


def _params(*sem, flags=None):
    return pltpu.CompilerParams(dimension_semantics=sem, vmem_limit_bytes=VMEM_LIMIT_BYTES, flags=flags)


def _modulated_norm(x, g, shift, scale):
    y = x * lax.rsqrt(jnp.mean(x * x, axis=-1, keepdims=True) + EPS) * g
    return y * (1.0 + scale) + shift


def _ada_kernel(c_ref, w_ref, b_ref, o_ref):
    c = c_ref[...]
    a = (c * jax.nn.sigmoid(c)).astype(BF16)
    o_ref[0] = jnp.dot(a, w_ref[0].astype(BF16), preferred_element_type=F32) + b_ref[0]


def _ada_mods(c_all, w_ada, b_ada):
    depth, d, n = w_ada.shape
    r = c_all.shape[0]
    tn = math.gcd(n, ADA_COL_TILE)
    return pl.pallas_call(
        _ada_kernel,
        grid=(depth, n // tn),
        in_specs=[
            pl.BlockSpec((r, d), lambda l, j: (0, 0)),
            pl.BlockSpec((1, d, tn), lambda l, j: (l, 0, j)),
            pl.BlockSpec((1, 1, tn), lambda l, j: (l, 0, j)),
        ],
        out_specs=pl.BlockSpec((1, r, tn), lambda l, j: (l, 0, j)),
        out_shape=jax.ShapeDtypeStruct((depth, r, n), F32),
        compiler_params=_params("arbitrary", "arbitrary"),
        name="ada_mods",
    )(c_all, w_ada, b_ada.reshape(depth, 1, n))


def _row_blocking(x):
    b, t, _ = x.shape
    if t >= ROW_TILE:
        assert t % ROW_TILE == 0
        return 1, ROW_TILE, t // ROW_TILE, b * (t // ROW_TILE)
    assert (b * t) % 8 == 0
    return b, t, 1, 1


def _x_index(nb, tps):
    if nb == 1:
        return lambda i, j: (i // tps, i % tps, 0)
    return lambda i, j: (0, 0, 0)


def _mod_spec(nb, tps, d, slot):
    if nb == 1:
        return pl.BlockSpec((None, 1, 1, d), lambda i, j: (slot, i // tps, 0, 0))
    return pl.BlockSpec((None, nb, 1, d), lambda i, j: (slot, 0, 0, 0))


def _ffn_kernel(*refs, nb, tt, n_ff, n_out, final):
    if final:
        (x_ref, shift_ref, scale_ref, gate_ref, g_ref, wg_ref, wu_ref, wo_ref, fg_ref,
         o_ref, h_scr, a_scr) = refs
    else:
        (x_ref, shift_ref, scale_ref, gate_ref, g_ref, wg_ref, wu_ref, wo_ref,
         o_ref, h_scr, a_scr) = refs
    j = pl.program_id(1)
    d = x_ref.shape[-1]
    tm = nb * tt
    tf = wg_ref.shape[1]
    tn = wo_ref.shape[1]

    @pl.when(j == 0)
    def _():
        h = _modulated_norm(x_ref[...], g_ref[...], shift_ref[...], scale_ref[...])
        h_scr[...] = h.reshape(tm, d).astype(BF16)

    @pl.when(j < n_ff)
    def _():
        rows = tm // FFN_ROW_SPLIT

        def products(r):
            hr = h_scr[r * rows:(r + 1) * rows, :]
            return (jnp.dot(hr, wg_ref[...], preferred_element_type=F32),
                    jnp.dot(hr, wu_ref[...], preferred_element_type=F32))

        parts = []
        cur = products(0)
        for r in range(FFN_ROW_SPLIT):
            nxt = products(r + 1) if r + 1 < FFN_ROW_SPLIT else None
            gg, uu = cur
            parts.append((gg * jax.nn.sigmoid(gg) * uu).astype(BF16))
            cur = nxt
        a_scr[j] = jnp.concatenate(parts, axis=0)

    @pl.when(j >= n_ff)
    def _():
        a = jnp.concatenate([a_scr[f] for f in range(n_ff)], axis=1)
        acc = jnp.dot(a, wo_ref[...], preferred_element_type=F32).reshape(nb, tt, tn)
        for n in range(n_out):
            @pl.when(j == n_ff + n)
            def _(n=n):
                cs = slice(n * tn, (n + 1) * tn)
                o_ref[:, :, cs] = x_ref[:, :, cs] + 0.5 * gate_ref[:, :, cs] * acc

    if final:
        @pl.when(j == n_ff + n_out - 1)
        def _():
            y = o_ref[...]
            o_ref[...] = y * lax.rsqrt(jnp.mean(y * y, axis=-1, keepdims=True) + EPS) * fg_ref[...]


def _ffn(x, mods, slots, g, w_in_p, w_out_p, final_g=None):
    b, t, d = x.shape
    nb, tt, tps, nrows = _row_blocking(x)
    ffp = w_out_p.shape[0]
    n_ff = ffp // FF_TILE
    tn = math.gcd(d, FFN_OUT_TILE)
    n_out = d // tn
    final = final_g is not None
    in_specs = [
        pl.BlockSpec((nb, tt, d), _x_index(nb, tps)),
        _mod_spec(nb, tps, d, slots[0]),
        _mod_spec(nb, tps, d, slots[1]),
        _mod_spec(nb, tps, d, slots[2]),
        pl.BlockSpec((1, d), lambda i, j: (0, 0)),
        pl.BlockSpec((d, FF_TILE), lambda i, j: (0, jnp.minimum(j, n_ff - 1))),
        pl.BlockSpec((d, FF_TILE), lambda i, j: (0, jnp.minimum(j, n_ff - 1) + n_ff)),
        pl.BlockSpec((ffp, tn), lambda i, j: (0, jnp.maximum(j - n_ff, 0))),
    ]
    args = [x, mods, mods, mods, g.reshape(1, d), w_in_p, w_in_p, w_out_p]
    if final:
        in_specs.append(pl.BlockSpec((1, d), lambda i, j: (0, 0)))
        args.append(final_g.reshape(1, d))
    return pl.pallas_call(
        functools.partial(_ffn_kernel, nb=nb, tt=tt, n_ff=n_ff, n_out=n_out, final=final),
        grid=(nrows, n_ff + n_out),
        in_specs=in_specs,
        out_specs=pl.BlockSpec((nb, tt, d), _x_index(nb, tps)),
        out_shape=jax.ShapeDtypeStruct((b, t, d), F32),
        scratch_shapes=[pltpu.VMEM((nb * tt, d), BF16), pltpu.VMEM((n_ff, nb * tt, FF_TILE), BF16)],
        compiler_params=_params("parallel", "arbitrary"),
        name="ffn",
    )(*args)


def _prep_ffn_weights(w_in, w_out):
    ff = w_out.shape[0]
    ffp = -(-ff // FF_TILE) * FF_TILE
    pad = ffp - ff
    wg = jnp.pad(w_in[:, :ff].astype(BF16), ((0, 0), (0, pad)))
    wu = jnp.pad(w_in[:, ff:].astype(BF16), ((0, 0), (0, pad)))
    return jnp.concatenate([wg, wu], axis=1), jnp.pad(w_out.astype(BF16), ((0, pad), (0, 0)))


def _log_sigmoid(x):
    return jnp.minimum(x, 0.0) - jnp.log1p(jnp.exp(-jnp.abs(x)))


def _proj_kernel(*refs, nb, tt, tn, tps, taps, has_f, vt_cols, vt_tile):
    x_ref, shift_ref, scale_ref, g_ref, w_ref, cs_ref = refs[:6]
    pos = 6
    if has_f:
        wf_ref, bf_ref = refs[6:8]
        pos = 8
    obf_ref = refs[pos]
    tap_refs = refs[pos + 1:pos + 1 + len({tp[0] for tp in taps})]
    pos = pos + 1 + len(tap_refs)
    if vt_cols:
        vt_ref = refs[pos]
        pos += 1
    if has_f:
        logf_ref = refs[pos]
        pos += 1
    h_scr = refs[pos]
    i = pl.program_id(0)
    j = pl.program_id(1)
    d = x_ref.shape[-1]

    @pl.when(j == 0)
    def _():
        h = _modulated_norm(x_ref[...], g_ref[...], shift_ref[...], scale_ref[...])
        h_scr[...] = h.reshape(nb * tt, d).astype(BF16)
        if has_f:
            fr = lax.dot_general(wf_ref[...], h_scr[...], _NT, preferred_element_type=F32)
            logf_ref[...] = _log_sigmoid(fr[:logf_ref.shape[0]] + bf_ref[...])

    y = jnp.dot(h_scr[...], w_ref[...], preferred_element_type=F32)
    obf_ref[...] = (y * cs_ref[...]).astype(BF16).reshape(nb, tt, tn)
    for out_idx, jval, col0, last_rows_only in taps:
        cond = j == jval
        if last_rows_only and tps > 1:
            cond = jnp.logical_and(cond, i % tps == tps - 1)

        @pl.when(cond)
        def _(out_idx=out_idx, col0=col0):
            h0 = col0 // HEAD_DIM
            tap_refs[out_idx][:, :, h0:h0 + tn // HEAD_DIM, :] = y.reshape(nb, tt, tn // HEAD_DIM, HEAD_DIM)

    for jval, head0 in vt_cols:
        @pl.when(j == jval)
        def _(head0=head0):
            for hh in range(tn // HEAD_DIM):
                yt = y[:, hh * HEAD_DIM:(hh + 1) * HEAD_DIM].T.astype(BF16)
                yt = jnp.concatenate([yt, jnp.ones((VT_ROWS - HEAD_DIM, yt.shape[1]), BF16)], axis=0)
                for s in range(tt // vt_tile):
                    vt_ref[0, head0 + hh, s] = yt[:, s * vt_tile:(s + 1) * vt_tile]


def _seg_taps(tn, width, segments, last_rows_only):
    per = width // tn
    assert per * tn == width
    return [(o, sidx * per + s, s * tn, last_rows_only) for o, sidx in enumerate(segments) for s in range(per)]


def _vt_cols(tn, width, segments):
    per = width // tn
    hpb = tn // HEAD_DIM
    return [(sidx * per + s, (o * per + s) * hpb) for o, sidx in enumerate(segments) for s in range(per)]


def _proj(x, mods, slots, g, w_bf, col_scale, tn, tap_shapes, taps, wf=None, bf=None, vt_cols=(), vt_tile=None):
    b, t, d = x.shape
    n = w_bf.shape[1]
    nb, tt, tps, nrows = _row_blocking(x)
    has_f = wf is not None
    assert not vt_cols or nb == 1
    in_specs = [
        pl.BlockSpec((nb, tt, d), _x_index(nb, tps)),
        _mod_spec(nb, tps, d, slots[0]),
        _mod_spec(nb, tps, d, slots[1]),
        pl.BlockSpec((1, d), lambda i, j: (0, 0)),
        pl.BlockSpec((d, tn), lambda i, j: (0, j)),
        pl.BlockSpec((1, tn), lambda i, j: (0, j)),
    ]
    args = [x, mods, mods, g.reshape(1, d), w_bf, jnp.asarray(col_scale, F32).reshape(1, n)]
    if has_f:
        in_specs += [pl.BlockSpec(wf.shape, lambda i, j: (0, 0)), pl.BlockSpec(bf.shape, lambda i, j: (0, 0))]
        args += [wf, bf]
    xi = _x_index(nb, tps)
    out_specs = [pl.BlockSpec((nb, tt, tn), lambda i, j: xi(i, j)[:2] + (j,))]
    out_shape = [jax.ShapeDtypeStruct((b, t, n), BF16)]
    for shp, follows_rows in tap_shapes:
        if follows_rows:
            out_specs.append(pl.BlockSpec((nb, tt) + shp[2:], lambda i, j: xi(i, j)[:2] + (0, 0)))
        elif nb == 1:
            out_specs.append(pl.BlockSpec((1,) + shp[1:], lambda i, j: (i // tps, 0, 0, 0)))
        else:
            out_specs.append(pl.BlockSpec(shp, lambda i, j: (0, 0, 0, 0)))
        out_shape.append(jax.ShapeDtypeStruct(shp, F32))
    if vt_cols:
        n_vh = len(vt_cols) * (tn // HEAD_DIM)
        out_specs.append(pl.BlockSpec((1, n_vh, tt // vt_tile, VT_ROWS, vt_tile), lambda i, j: (i // tps, 0, i % tps, 0, 0)))
        out_shape.append(jax.ShapeDtypeStruct((b, n_vh, t // vt_tile, VT_ROWS, vt_tile), BF16))
    if has_f:
        nf = bf.shape[0]
        out_specs.append(pl.BlockSpec((nf, nb * tt), lambda i, j: (0, i)))
        out_shape.append(jax.ShapeDtypeStruct((nf, b * t), F32))
    return pl.pallas_call(
        functools.partial(_proj_kernel, nb=nb, tt=tt, tn=tn, tps=tps, taps=tuple(taps), has_f=has_f,
                          vt_cols=tuple(vt_cols), vt_tile=vt_tile),
        grid=(nrows, n // tn),
        in_specs=in_specs,
        out_specs=out_specs,
        out_shape=out_shape,
        scratch_shapes=[pltpu.VMEM((nb * tt, d), BF16)],
        compiler_params=_params("arbitrary", "arbitrary"),
        name="mixer_in_proj",
    )(*args)


def _outproj_kernel(*refs, nb, tt, n_in):
    o_refs = refs[:n_in]
    w_ref, x_ref, gate_ref, out_ref = refs[n_in:]
    d = x_ref.shape[-1]
    acc = None
    row0 = 0
    for o_ref in o_refs:
        wd = o_ref.shape[-1]
        part = jnp.dot(o_ref[...].reshape(nb * tt, wd), w_ref[row0:row0 + wd, :], preferred_element_type=F32)
        acc = part if acc is None else acc + part
        row0 += wd
    out_ref[...] = x_ref[...] + gate_ref[...] * acc.reshape(nb, tt, d)


def _outproj(o_list, w_bf, x, mods, gate_slot):
    b, t, d = x.shape
    nb, tt, tps, nrows = _row_blocking(x)
    xi = _x_index(nb, tps)
    in_specs = [pl.BlockSpec((nb, tt, o.shape[-1]), xi) for o in o_list]
    in_specs += [
        pl.BlockSpec(w_bf.shape, lambda i, j: (0, 0)),
        pl.BlockSpec((nb, tt, d), xi),
        _mod_spec(nb, tps, d, gate_slot),
    ]
    return pl.pallas_call(
        functools.partial(_outproj_kernel, nb=nb, tt=tt, n_in=len(o_list)),
        grid=(nrows, 1),
        in_specs=in_specs,
        out_specs=pl.BlockSpec((nb, tt, d), xi),
        out_shape=jax.ShapeDtypeStruct((b, t, d), F32),
        compiler_params=_params("parallel", "arbitrary"),
        name="mixer_out_proj",
    )(*o_list, w_bf, x, mods)


def _cumsum_kernel(x_ref, c_ref, o_ref):
    x = x_ref[0]
    n = x.shape[1]
    lane = lax.broadcasted_iota(jnp.int32, x.shape, 1)
    s = 1
    while s < n:
        x = x + jnp.where(lane >= s, pltpu.roll(x, s, axis=1), 0.0)
        s *= 2
    o_ref[0] = x + c_ref[0][:, :1]


def _cumsum(x, carry):
    g, h, n = x.shape
    return pl.pallas_call(
        _cumsum_kernel,
        grid=(g,),
        in_specs=[pl.BlockSpec((1, h, n), lambda i: (i, 0, 0)), pl.BlockSpec((1, h, LANES), lambda i: (i, 0, 0))],
        out_specs=pl.BlockSpec((1, h, n), lambda i: (i, 0, 0)),
        out_shape=jax.ShapeDtypeStruct((g, h, n), F32),
        compiler_params=_params("arbitrary"),
        name="logf_cumsum",
    )(x, carry)


def _softmax_init(m_scr, acc_scr):
    m_scr[...] = jnp.full(m_scr.shape, NEG_INF, F32)
    acc_scr[...] = jnp.zeros(acc_scr.shape, F32)


def _softmax_update_t(s, s_max, vt, m_prev, acc_prev):
    m_new = jnp.maximum(m_prev, s_max)
    alpha = jnp.exp2(m_prev - m_new)
    p = jnp.exp2(s - m_new)
    acc_new = alpha * acc_prev + jnp.dot(vt, p.astype(BF16), preferred_element_type=F32)
    return m_new, acc_new


def _normalised_t(acc):
    return (acc[:HEAD_DIM] / acc[HEAD_DIM:HEAD_DIM + 1]).T


def _with_max(s):
    return s, jnp.max(s, axis=0, keepdims=True)


def _consume_tile(scores, vt, m_scr, acc_scr, next_scores=None):
    n = len(scores) // 2
    g = m_scr.shape[1] // n
    sls = [slice(i * g, (i + 1) * g) for i in range(n)]
    state = [(m_scr[:, sl], acc_scr[:, sl]) for sl in sls]
    new, nxt = [], []
    for i in range(n):
        if next_scores is not None:
            nxt.extend(next_scores(i))
        new.append(_softmax_update_t(scores[2 * i], scores[2 * i + 1], vt, *state[i]))
    for sl, (m_new, acc_new) in zip(sls, new):
        m_scr[:, sl] = m_new
        acc_scr[:, sl] = acc_new
    return tuple(nxt)


def _toeplitz_values(tbl_ref, idx, n_vals, nh, h, ref_row):
    ref_val = tbl_ref[ref_row * nh + h]

    def body(v, acc):
        return jnp.where(idx == v, tbl_ref[v * nh + h] - ref_val, acc)

    return lax.fori_loop(0, n_vals, body, jnp.zeros(idx.shape, F32))


def _stack_diff_queries(q):
    lane = lax.broadcasted_iota(jnp.int32, q.shape, 1)
    zero = jnp.zeros_like(q)
    qs = jnp.concatenate([jnp.where(lane < DA_HALF, q, zero), jnp.where(lane >= DA_HALF, q, zero)], axis=0)
    return qs


def _diff_finalize(o, lam, g, out_scale):
    tq = o.shape[0] // 2
    od = o[:tq] - lam * o[tq:]
    return od * lax.rsqrt(jnp.mean(od * od, axis=-1, keepdims=True) + EPS) * g * out_scale


def _diff_p_kernel(lam_ref, tbl_ref, q_ref, k_ref, vt_ref, idx_ref, base_ref, g_ref, o_ref,
                   m_scr, acc_scr, bias_scr, *, tq, nh, far_bucket, out_scale):
    h = pl.program_id(0)
    bi = pl.program_id(1)
    qi = pl.program_id(2)
    nblk = tq // LANES

    @pl.when(jnp.logical_and(bi == 0, qi == 0))
    def _():
        t = _toeplitz_values(tbl_ref, idx_ref[...], N_T5_BUCKETS, nh, h, far_bucket) * LOG2E
        bias_scr[...] = base_ref[...]
        for blk in range(nblk):
            sl = slice(blk * LANES, (blk + 1) * LANES)
            bias_scr[0, sl, sl] += t[0]
            if blk >= 1:
                bias_scr[0, (blk - 1) * LANES:blk * LANES, sl] += t[1]
        bias_scr[1, (nblk - 1) * LANES:, :LANES] += t[1]

    qs = _stack_diff_queries(q_ref[0])
    _softmax_init(m_scr, acc_scr)
    ngrp = 2 * tq // ATTN_GROUP

    def scores(j, slot):
        k = k_ref[0, pl.ds(pl.multiple_of(j * tq, tq), tq), :]

        def group(i):
            s = lax.dot_general(k, qs[i * ATTN_GROUP:(i + 1) * ATTN_GROUP], _NT, preferred_element_type=F32)
            if slot is not None:
                q0 = (i * ATTN_GROUP) % tq
                s = s + bias_scr[slot, :, q0:q0 + ATTN_GROUP]
            return _with_max(s)
        return group

    n_far = jnp.maximum(qi - 1, 0)
    before_slot = jnp.where(qi == 0, 2, 1)

    @pl.when(n_far >= 1)
    def _():
        first = scores(0, None)
        cur = tuple(x for i in range(ngrp) for x in first(i))

        def far(j, cur):
            return _consume_tile(cur, vt_ref[0, 0, j], m_scr, acc_scr, scores(j + 1, None))

        cur = lax.fori_loop(0, n_far - 1, far, cur)
        _consume_tile(cur, vt_ref[0, 0, n_far - 1], m_scr, acc_scr)

    before = scores(n_far, before_slot)
    cur = tuple(x for i in range(ngrp) for x in before(i))
    cur = _consume_tile(cur, vt_ref[0, 0, n_far], m_scr, acc_scr, scores(qi, 0))
    _consume_tile(cur, vt_ref[0, 0, qi], m_scr, acc_scr)
    o = _normalised_t(acc_scr[...])
    o_ref[0] = _diff_finalize(o, lam_ref[0], g_ref[...], out_scale).astype(BF16)


def _diff_prompt(lam, tbl, qkv, vt, g, n_heads, out_scale):
    b, t, _ = qkv.shape
    tq = vt.shape[-1]
    far_bucket = N_T5_BUCKETS // 2 - 1
    pos = np.arange(LANES)
    idx = np.stack([_t5_bucket_np(pos[:, None] - pos[None, :]),
                    _t5_bucket_np(pos[:, None] - pos[None, :] - LANES)]).astype(np.int32)
    kq = np.arange(tq)
    base = np.zeros((3, tq, tq), np.float32)
    base[0] = np.where((kq[:, None] // CHUNK) <= (kq[None, :] // CHUNK), 0.0, NEG_INF)
    base[2] = NEG_INF
    return pl.pallas_call(
        functools.partial(_diff_p_kernel, tq=tq, nh=n_heads, far_bucket=far_bucket, out_scale=out_scale),
        grid=(n_heads, b, t // tq),
        in_specs=[
            pl.BlockSpec(memory_space=pltpu.SMEM),
            pl.BlockSpec(memory_space=pltpu.SMEM),
            pl.BlockSpec((1, tq, HEAD_DIM), lambda h, bi, qi: (bi, qi, h)),
            pl.BlockSpec((1, t, HEAD_DIM), lambda h, bi, qi: (bi, 0, n_heads + h)),
            pl.BlockSpec((1, 1, t // tq, VT_ROWS, tq), lambda h, bi, qi: (bi, h, 0, 0, 0)),
            pl.BlockSpec(idx.shape, lambda h, bi, qi: (0, 0, 0)),
            pl.BlockSpec(base.shape, lambda h, bi, qi: (0, 0, 0)),
            pl.BlockSpec((1, HEAD_DIM), lambda h, bi, qi: (0, 0)),
        ],
        out_specs=pl.BlockSpec((1, tq, HEAD_DIM), lambda h, bi, qi: (bi, qi, h)),
        out_shape=jax.ShapeDtypeStruct((b, t, n_heads * HEAD_DIM), BF16),
        scratch_shapes=[pltpu.VMEM((1, 2 * tq), F32), pltpu.VMEM((VT_ROWS, 2 * tq), F32),
                        pltpu.VMEM((3, tq, tq), F32)],
        compiler_params=_params("arbitrary", "arbitrary", "arbitrary"),
        name="diff_attn_prompt",
    )(lam, tbl, qkv, qkv, vt, jnp.asarray(idx), jnp.asarray(base), g)


def _fox_p_kernel(fref_ref, q_ref, k_ref, vt_ref, f_ref, o_ref, m_scr, acc_scr, fcol_scr, *, tq):
    bi = pl.program_id(0)
    h = pl.program_id(1)
    qi = pl.program_id(2)
    nblk = tq // LANES

    @pl.when(qi == 0)
    def _():
        for c in range(f_ref.shape[2] * nblk):
            row = f_ref[0, 0, c // nblk][:, (c % nblk) * LANES:(c % nblk + 1) * LANES]
            fcol_scr[c * LANES:(c + 1) * LANES, :] = jnp.broadcast_to(row, (LANES, LANES)).T

    q = q_ref[0]
    fref = fref_ref[(bi * pl.num_programs(1) + h) * pl.num_programs(2) + qi]
    _softmax_init(m_scr, acc_scr)

    ngrp = tq // ATTN_GROUP

    def scores(j, causal):
        start = pl.multiple_of(j * tq, tq)
        k = k_ref[0, pl.ds(start, tq), :]
        decay = (fref - fcol_scr[pl.ds(start, tq), :]) * LOG2E
        decay = jnp.concatenate([decay] * (ATTN_GROUP // LANES), axis=1)

        def group(i):
            s = lax.dot_general(k, q[i * ATTN_GROUP:(i + 1) * ATTN_GROUP], _NT, preferred_element_type=F32)
            s = s + decay
            if causal:
                key = lax.broadcasted_iota(jnp.int32, s.shape, 0)
                qry = lax.broadcasted_iota(jnp.int32, s.shape, 1) + i * ATTN_GROUP
                s = jnp.where(key <= qry, s, NEG_INF)
            return _with_max(s)
        return group

    @pl.when(qi >= 1)
    def _():
        first = scores(0, False)
        cur = tuple(x for i in range(ngrp) for x in first(i))

        def far(j, cur):
            return _consume_tile(cur, vt_ref[0, 0, j], m_scr, acc_scr, scores(j + 1, False))

        cur = lax.fori_loop(0, qi - 1, far, cur)
        _consume_tile(cur, vt_ref[0, 0, qi - 1], m_scr, acc_scr)

    diag = scores(qi, True)
    _consume_tile(tuple(x for i in range(ngrp) for x in diag(i)), vt_ref[0, 0, qi], m_scr, acc_scr)
    o_ref[0] = _normalised_t(acc_scr[...]).astype(BF16)


def _fox_prompt(fref, qkv, vt, fcum, n_heads, col0, vt_head0):
    b, t, _ = qkv.shape
    tq = vt.shape[-1]
    nq = t // tq
    return pl.pallas_call(
        functools.partial(_fox_p_kernel, tq=tq),
        grid=(b, n_heads, nq),
        in_specs=[
            pl.BlockSpec(memory_space=pltpu.SMEM),
            pl.BlockSpec((1, tq, HEAD_DIM), lambda bi, h, qi: (bi, qi, col0 + h)),
            pl.BlockSpec((1, t, HEAD_DIM), lambda bi, h, qi: (bi, 0, col0 + n_heads + h)),
            pl.BlockSpec((1, 1, nq, VT_ROWS, tq), lambda bi, h, qi: (bi, vt_head0 + h, 0, 0, 0)),
            pl.BlockSpec((1, 1, nq, 1, tq), lambda bi, h, qi: (bi, h, 0, 0, 0)),
        ],
        out_specs=pl.BlockSpec((1, tq, HEAD_DIM), lambda bi, h, qi: (bi, qi, h)),
        out_shape=jax.ShapeDtypeStruct((b, t, n_heads * HEAD_DIM), BF16),
        scratch_shapes=[pltpu.VMEM((1, tq), F32), pltpu.VMEM((VT_ROWS, tq), F32),
                        pltpu.VMEM((t, LANES), F32)],
        compiler_params=_params("arbitrary", "arbitrary", "arbitrary"),
        name="fox_attn_prompt",
    )(fref, qkv, qkv, vt, fcum)


def _joint_softmax_pv(s_c, s_n, vc, vn):
    m = jnp.maximum(jnp.max(s_c, axis=1, keepdims=True), jnp.max(s_n, axis=1, keepdims=True))
    p_c = jnp.exp2(s_c - m)
    p_n = jnp.exp2(s_n - m)
    l = jnp.sum(p_c, axis=1, keepdims=True) + jnp.sum(p_n, axis=1, keepdims=True)
    acc = jnp.dot(p_c.astype(BF16), vc, preferred_element_type=F32)
    acc = acc + jnp.dot(p_n.astype(BF16), vn, preferred_element_type=F32)
    return acc / l


def _diff_s_kernel(lam_ref, q_ref, kn_ref, vn_ref, kc_ref, vc_ref, bnear_ref, bnew_ref, g_ref, o_ref, *, near, out_scale):
    qs = _stack_diff_queries(q_ref[0])
    kc = kc_ref[0].astype(BF16)
    vc = vc_ref[0].astype(BF16)
    p = kc.shape[0]
    s_c = lax.dot_general(qs, kc, _NT, preferred_element_type=F32)
    bnear = bnear_ref[0]
    s_c = jnp.concatenate([s_c[:, :p - near], s_c[:, p - near:] + jnp.concatenate([bnear, bnear], axis=0)], axis=1)
    bnew = bnew_ref[0]
    s_n = lax.dot_general(qs, kn_ref[0], _NT, preferred_element_type=F32) + jnp.concatenate([bnew, bnew], axis=0)
    o = _joint_softmax_pv(s_c, s_n, vc, vn_ref[0])
    o_ref[0] = _diff_finalize(o, lam_ref[0], g_ref[...], out_scale).astype(BF16)


def _diff_sample(lam, qkv, kc, vc, bnear, bnew, g, n_heads, out_scale):
    b, t, _ = qkv.shape
    p = kc.shape[1]
    near = bnear.shape[-1]
    hd = HEAD_DIM
    return pl.pallas_call(
        functools.partial(_diff_s_kernel, near=near, out_scale=out_scale),
        grid=(b, n_heads),
        in_specs=[
            pl.BlockSpec(memory_space=pltpu.SMEM),
            pl.BlockSpec((1, t, hd), lambda bi, h: (bi, 0, h)),
            pl.BlockSpec((1, t, hd), lambda bi, h: (bi, 0, n_heads + h)),
            pl.BlockSpec((1, t, hd), lambda bi, h: (bi, 0, 2 * n_heads + h)),
            pl.BlockSpec((1, p, hd), lambda bi, h: (bi, 0, h)),
            pl.BlockSpec((1, p, hd), lambda bi, h: (bi, 0, h)),
            pl.BlockSpec((1, t, near), lambda bi, h: (h, 0, 0)),
            pl.BlockSpec((1, t, t), lambda bi, h: (h, 0, 0)),
            pl.BlockSpec((1, hd), lambda bi, h: (0, 0)),
        ],
        out_specs=pl.BlockSpec((1, t, hd), lambda bi, h: (bi, 0, h)),
        out_shape=jax.ShapeDtypeStruct((b, t, n_heads * hd), BF16),
        compiler_params=_params("parallel", "parallel"),
        name="diff_attn_sample",
    )(lam, qkv, qkv, qkv, kc, vc, bnear, bnew, g)


def _fox_s_kernel(fref_ref, q_ref, kn_ref, vn_ref, kc_ref, vc_ref, fc_ref, fn_ref, o_ref):
    bi = pl.program_id(0)
    h = pl.program_id(1)
    q = q_ref[0]
    t = q.shape[0]
    fref = fref_ref[bi * pl.num_programs(1) + h]
    s_c = lax.dot_general(q, kc_ref[0].astype(BF16), _NT, preferred_element_type=F32) + (fref - fc_ref[0, 0]) * LOG2E
    s_n = lax.dot_general(q, kn_ref[0], _NT, preferred_element_type=F32) + (fref - fn_ref[0, 0][:, :t]) * LOG2E
    row = lax.broadcasted_iota(jnp.int32, s_n.shape, 0)
    col = lax.broadcasted_iota(jnp.int32, s_n.shape, 1)
    s_n = jnp.where(col <= row, s_n, NEG_INF)
    o_ref[0] = _joint_softmax_pv(s_c, s_n, vc_ref[0].astype(BF16), vn_ref[0]).astype(BF16)


def _fox_sample(fref, qkv, kc, vc, fc, fn, n_heads, col0):
    b, t, _ = qkv.shape
    p = kc.shape[1]
    hd = HEAD_DIM
    return pl.pallas_call(
        _fox_s_kernel,
        grid=(b, n_heads),
        in_specs=[
            pl.BlockSpec(memory_space=pltpu.SMEM),
            pl.BlockSpec((1, t, hd), lambda bi, h: (bi, 0, col0 + h)),
            pl.BlockSpec((1, t, hd), lambda bi, h: (bi, 0, col0 + n_heads + h)),
            pl.BlockSpec((1, t, hd), lambda bi, h: (bi, 0, col0 + 2 * n_heads + h)),
            pl.BlockSpec((1, p, hd), lambda bi, h: (bi, 0, h)),
            pl.BlockSpec((1, p, hd), lambda bi, h: (bi, 0, h)),
            pl.BlockSpec((1, 1, 1, p), lambda bi, h: (bi, h, 0, 0)),
            pl.BlockSpec((1, 1, 1, fn.shape[-1]), lambda bi, h: (bi, h, 0, 0)),
        ],
        out_specs=pl.BlockSpec((1, t, hd), lambda bi, h: (bi, 0, h)),
        out_shape=jax.ShapeDtypeStruct((b, t, n_heads * hd), BF16),
        compiler_params=_params("parallel", "parallel"),
        name="fox_attn_sample",
    )(fref, qkv, qkv, qkv, kc, vc, fc, fn)


def _band_p_kernel(tbl_ref, q_ref, k_ref, vt_ref, idx_ref, base_ref, o_ref, bm_scr, *, tq, nkb, nh, hps):
    hg = pl.program_id(0)
    bi = pl.program_id(1)
    i = pl.program_id(2)
    nblk = tq // LANES
    qoff = (nkb - 1) * nblk

    @pl.when(jnp.logical_and(bi == 0, i == 0))
    def _():
        for e in range(hps):
            t = _toeplitz_values(tbl_ref, idx_ref[...], 2 * REL_CLIP + 1, nh, hg * hps + e, 0)
            bm_scr[e] = base_ref[...]
            for bq in range(nblk):
                kd = qoff + bq
                bm_scr[e, kd * LANES:(kd + 1) * LANES, bq * LANES:(bq + 1) * LANES] += t[0] * LOG2E
                bm_scr[e, (kd - 1) * LANES:kd * LANES, bq * LANES:(bq + 1) * LANES] += t[1] * LOG2E

    def head_scores(e):
        hs = slice(e * HEAD_DIM, (e + 1) * HEAD_DIM)
        q = q_ref[0, :, hs]
        ss = []
        for blk in range(nkb):
            jb = i - (nkb - 1) + blk
            jc = jnp.maximum(jb, 0)
            k = k_ref[0, pl.ds(pl.multiple_of(jc * tq, tq), tq), hs]
            s = lax.dot_general(k, q, _NT, preferred_element_type=F32) + bm_scr[e, blk * tq:(blk + 1) * tq, :]
            if blk < nkb - 1:
                s = s + jnp.where(jb < 0, NEG_INF, 0.0)
            ss.append(s)
        return ss, functools.reduce(jnp.maximum, [jnp.max(s, axis=0, keepdims=True) for s in ss])

    outs = []
    nxt = head_scores(0)
    for e in range(hps):
        ss, m = nxt
        if e + 1 < hps:
            nxt = head_scores(e + 1)
        acc = None
        for blk, s in enumerate(ss):
            jc = jnp.maximum(i - (nkb - 1) + blk, 0)
            a = jnp.dot(vt_ref[0, e, jc], jnp.exp2(s - m).astype(BF16), preferred_element_type=F32)
            acc = a if acc is None else acc + a
        outs.append(_normalised_t(acc).astype(BF16))
    o_ref[0] = jnp.concatenate(outs, axis=1)


def _band_prompt(tbl, qkv, vt, n_heads):
    b, t, _ = qkv.shape
    tq = vt.shape[-1]
    nkb = C_BAND // tq + 1
    assert (nkb - 1) * tq == C_BAND and tq % LANES == 0 and LANES >= REL_CLIP
    pos = np.arange(LANES)
    rel = pos[:, None] - pos[None, :]
    idx = (np.stack([np.clip(rel, -REL_CLIP, REL_CLIP), np.clip(rel - LANES, -REL_CLIP, REL_CLIP)]) + REL_CLIP).astype(np.int32)
    k_pos = np.arange(nkb * tq)[:, None]
    q_pos = C_BAND + np.arange(tq)[None, :]
    kc, qc = k_pos // CHUNK, q_pos // CHUNK
    base = np.where((kc <= qc) & (qc - kc <= C_PREV_CHUNKS), 0.0, NEG_INF).astype(np.float32)
    hps = BAND_HEADS
    ngrp = n_heads // hps
    assert ngrp * hps == n_heads
    wd = hps * HEAD_DIM
    return pl.pallas_call(
        functools.partial(_band_p_kernel, tq=tq, nkb=nkb, nh=n_heads, hps=hps),
        grid=(ngrp, b, t // tq),
        in_specs=[
            pl.BlockSpec(memory_space=pltpu.SMEM),
            pl.BlockSpec((1, tq, wd), lambda h, bi, i: (bi, i, h)),
            pl.BlockSpec((1, t, wd), lambda h, bi, i: (bi, 0, ngrp + h)),
            pl.BlockSpec((1, hps, t // tq, VT_ROWS, tq), lambda h, bi, i: (bi, h, 0, 0, 0)),
            pl.BlockSpec(idx.shape, lambda h, bi, i: (0, 0, 0)),
            pl.BlockSpec(base.shape, lambda h, bi, i: (0, 0)),
        ],
        out_specs=pl.BlockSpec((1, tq, wd), lambda h, bi, i: (bi, i, h)),
        out_shape=jax.ShapeDtypeStruct((b, t, n_heads * HEAD_DIM), BF16),
        scratch_shapes=[pltpu.VMEM((hps, nkb * tq, tq), F32)],
        compiler_params=_params("arbitrary", "arbitrary", "arbitrary"),
        name="band_attn_prompt",
    )(tbl, qkv, qkv, vt, jnp.asarray(idx), jnp.asarray(base))


def _band_s_kernel(q_ref, kn_ref, vn_ref, kc_ref, vc_ref, bmc_ref, bmn_ref, o_ref):
    q = q_ref[0]
    s_c = lax.dot_general(q, kc_ref[0].astype(BF16), _NT, preferred_element_type=F32) + bmc_ref[0]
    s_n = lax.dot_general(q, kn_ref[0], _NT, preferred_element_type=F32) + bmn_ref[0]
    o_ref[0] = _joint_softmax_pv(s_c, s_n, vc_ref[0].astype(BF16), vn_ref[0]).astype(BF16)


def _band_sample(qkv, kc, vc, bmc, bmn, n_heads):
    b, t, _ = qkv.shape
    lc = kc.shape[1]
    hd = HEAD_DIM
    return pl.pallas_call(
        _band_s_kernel,
        grid=(b, n_heads),
        in_specs=[
            pl.BlockSpec((1, t, hd), lambda bi, h: (bi, 0, h)),
            pl.BlockSpec((1, t, hd), lambda bi, h: (bi, 0, n_heads + h)),
            pl.BlockSpec((1, t, hd), lambda bi, h: (bi, 0, 2 * n_heads + h)),
            pl.BlockSpec((1, lc, hd), lambda bi, h: (bi, 0, h)),
            pl.BlockSpec((1, lc, hd), lambda bi, h: (bi, 0, h)),
            pl.BlockSpec((1, t, lc), lambda bi, h: (h, 0, 0)),
            pl.BlockSpec((1, t, t), lambda bi, h: (h, 0, 0)),
        ],
        out_specs=pl.BlockSpec((1, t, hd), lambda bi, h: (bi, 0, h)),
        out_shape=jax.ShapeDtypeStruct((b, t, n_heads * hd), BF16),
        compiler_params=_params("parallel", "parallel"),
        name="band_attn_sample",
    )(qkv, qkv, qkv, kc, vc, bmc, bmn)


def _t5_bucket_np(rel):
    nb = N_T5_BUCKETS // 2
    max_exact = nb // 2
    n = np.abs(rel)
    nf = np.maximum(n, 1).astype(np.float64)
    large = max_exact + (np.log(nf / max_exact) / math.log(T5_MAX_DIST / max_exact) * (nb - max_exact)).astype(np.int64)
    large = np.minimum(large, nb - 1)
    return np.where(rel > 0, nb, 0) + np.where(n < max_exact, n, large)


def _t5_bias_tiles(t5_table, q_pos, k_pos, far_bucket):
    rel = k_pos[None, :] - q_pos[:, None]
    mask = (k_pos[None, :] // CHUNK) <= (q_pos[:, None] // CHUNK)
    bias = jnp.transpose(t5_table.astype(F32)[_t5_bucket_np(rel)], (2, 0, 1))
    bias = (bias - t5_table.astype(F32)[far_bucket][:, None, None]) * LOG2E
    return jnp.where(mask[None], bias, NEG_INF)


def _band_bias_tiles(rel_table, q_pos, k_pos):
    rel = np.clip(k_pos[None, :] - q_pos[:, None], -REL_CLIP, REL_CLIP) + REL_CLIP
    qc = q_pos[:, None] // CHUNK
    kc = k_pos[None, :] // CHUNK
    mask = (kc <= qc) & (qc - kc <= C_PREV_CHUNKS) & (k_pos[None, :] >= 0)
    bias = jnp.transpose(rel_table.astype(F32)[rel], (2, 0, 1)) * LOG2E
    return jnp.where(mask[None], bias, NEG_INF)


def kernel(x_prompt, x_sample, cache_a_k, cache_a_v, cache_b_k, cache_b_v, cache_b_logf, cache_c_k, cache_c_v, c_prompt, c_sample, w_ada, b_ada, norm_g, w_ffn_in, w_ffn_out, w_in_ab, b_forget, w_out_ab, lambda_q1, lambda_k1, lambda_q2, lambda_k2, subln_g, t5_table, w_in_c, w_out_c, c_rel_bias, final_g):
    depth = w_ada.shape[0]
    bsz, seq, d = x_prompt.shape
    dbsz, dseq, _ = x_sample.shape
    past = cache_b_logf.shape[2]
    h_a = cache_a_k.shape[3]
    h_b = cache_b_k.shape[3]
    h_c = cache_c_k.shape[3]
    wa, wb, cw = h_a * HEAD_DIM, h_b * HEAD_DIM, h_c * HEAD_DIM
    assert bsz == dbsz and wa == wb and 3 * wa + 3 * wb == 3 * cw
    assert LANES >= T5_MAX_DIST and SAMPLE_NEAR >= T5_MAX_DIST and past >= SAMPLE_NEAR
    far_bucket = (N_T5_BUCKETS // 2) - 1

    mods = _ada_mods(jnp.concatenate([c_prompt, c_sample], axis=0), w_ada, b_ada)
    mods = mods.reshape(depth, 2, bsz, 9, d).transpose(0, 3, 1, 2, 4).reshape(depth * 9 * 2, bsz, 1, d)

    def slot(l, k, grp):
        return (l * 9 + k) * 2 + grp

    xs = [x_prompt, x_sample]
    ab_states = [[], []]
    c_states = [[], []]
    for l in range(depth):
        i = l // 2
        last = l == depth - 1
        w1 = _prep_ffn_weights(w_ffn_in[l, 0], w_ffn_out[l, 0])
        w2 = _prep_ffn_weights(w_ffn_in[l, 1], w_ffn_out[l, 1])
        for grp in range(2):
            xs[grp] = _ffn(xs[grp], mods, [slot(l, k, grp) for k in range(3)], norm_g[l, 0], *w1)

        if l % 2 == 0:
            lam_init = 0.8 - 0.6 * math.exp(-0.3 * l)
            lam = (jnp.exp(jnp.sum(lambda_q1[i].astype(F32) * lambda_k1[i].astype(F32)))
                   - jnp.exp(jnp.sum(lambda_q2[i].astype(F32) * lambda_k2[i].astype(F32))) + lam_init).reshape(1)
            n_main = 3 * wa + 3 * wb
            w_main = w_in_ab[i][:, :n_main].astype(BF16)
            wf = jnp.pad(w_in_ab[i][:, n_main:].T.astype(BF16), ((0, 16 - h_b), (0, 0)))
            bfo = b_forget[i].astype(F32).reshape(h_b, 1)
            w_out = w_out_ab[i].astype(BF16)
            g_sub = subln_g[i].astype(F32).reshape(1, HEAD_DIM)
            tn = min(PROJ_COL_TILE, wa)
            taps = _seg_taps(tn, wa, (1, 2, 4, 5), False)
            cs = np.ones((n_main,), np.float32)
            cs[:wa] = DA_HALF ** -0.5 * LOG2E
            cs[3 * wa:3 * wa + wb] = HEAD_DIM ** -0.5 * LOG2E
            for grp in range(2):
                x = xs[grp]
                b, t, _ = x.shape
                tap_shapes = [((b, t, h_a, HEAD_DIM), True)] * 4
                if grp == 0:
                    qkv, ka, va, kb, vb, vt, logf_t = _proj(
                        x, mods, [slot(l, 3, grp), slot(l, 4, grp)], norm_g[l, 1], w_main, cs, tn, tap_shapes, taps,
                        wf, bfo, vt_cols=_vt_cols(tn, wa, (2, 5)), vt_tile=ROW_TILE)
                else:
                    qkv, ka, va, kb, vb, logf_t = _proj(
                        x, mods, [slot(l, 3, grp), slot(l, 4, grp)], norm_g[l, 1], w_main, cs, tn, tap_shapes, taps,
                        wf, bfo)
                logf = logf_t.reshape(h_b, b, t).transpose(1, 2, 0)
                ab_states[grp].append((ka, va, kb, vb, logf))
                logf_bh = logf_t.reshape(h_b, b, t).transpose(1, 0, 2)
                if grp == 0:
                    o_a = _diff_prompt(lam, t5_table.astype(F32).reshape(-1), qkv, vt, g_sub, h_a, 1.0 - lam_init)
                    fcum = _cumsum(logf_bh, jnp.zeros((b, h_b, LANES), F32))
                    nq = t // ROW_TILE
                    fref = fcum[:, :, ::ROW_TILE].reshape(-1)
                    o_b = _fox_prompt(fref, qkv, vt, fcum.reshape(b, h_b, nq, 1, ROW_TILE), h_b, 3 * h_a, h_a)
                else:
                    q_pos = past + np.arange(t)
                    bnear = _t5_bias_tiles(t5_table, q_pos, past - SAMPLE_NEAR + np.arange(SAMPLE_NEAR), far_bucket)
                    bnew = _t5_bias_tiles(t5_table, q_pos, q_pos, far_bucket)
                    kc = cache_a_k[i].reshape(b, past, wa)
                    vc = cache_a_v[i].reshape(b, past, wa)
                    o_a = _diff_sample(lam, qkv, kc, vc, bnear, bnew, g_sub, h_a, 1.0 - lam_init)
                    fc = _cumsum(cache_b_logf[i].astype(F32).transpose(0, 2, 1), jnp.zeros((b, h_b, LANES), F32))
                    carry = fc[:, :, past - 1:past]
                    fn = _cumsum(jnp.pad(logf_bh, ((0, 0), (0, 0), (0, LANES - t))),
                                 jnp.broadcast_to(carry, (b, h_b, LANES)))
                    o_b = _fox_sample(carry.reshape(-1), qkv, cache_b_k[i].reshape(b, past, wb),
                                      cache_b_v[i].reshape(b, past, wb), fc.reshape(b, h_b, 1, past),
                                      fn.reshape(b, h_b, 1, LANES), h_b, 3 * h_a)
                xs[grp] = _outproj([o_a, o_b], w_out, x, mods, slot(l, 5, grp))
        else:
            w_in = w_in_c[i].astype(BF16)
            w_out = w_out_c[i].astype(BF16)
            tn = min(PROJ_COL_TILE, cw)
            taps = _seg_taps(tn, cw, (1, 2), True)
            cs = np.ones((3 * cw,), np.float32)
            cs[:cw] = HEAD_DIM ** -0.5 * LOG2E
            for grp in range(2):
                x = xs[grp]
                b, t, _ = x.shape
                keep = min(C_BAND, t)
                assert keep == min(t, ROW_TILE)
                tap_shapes = [((b, keep, h_c, HEAD_DIM), False)] * 2
                if grp == 0:
                    qkv, k_new, v_new, vt = _proj(
                        x, mods, [slot(l, 3, grp), slot(l, 4, grp)], norm_g[l, 1], w_in, cs, tn, tap_shapes, taps,
                        vt_cols=_vt_cols(tn, cw, (2,)), vt_tile=BAND_TILE)
                else:
                    qkv, k_new, v_new = _proj(
                        x, mods, [slot(l, 3, grp), slot(l, 4, grp)], norm_g[l, 1], w_in, cs, tn, tap_shapes, taps)
                c_states[grp].append((k_new, v_new))
                if grp == 0:
                    o_c = _band_prompt(c_rel_bias[i].astype(F32).reshape(-1), qkv, vt, h_c)
                else:
                    lc = cache_c_k.shape[2]
                    q_pos = past + np.arange(t)
                    bmc = _band_bias_tiles(c_rel_bias[i], q_pos, past - lc + np.arange(lc))
                    bmn = _band_bias_tiles(c_rel_bias[i], q_pos, q_pos)
                    o_c = _band_sample(qkv, cache_c_k[i].reshape(b, lc, cw), cache_c_v[i].reshape(b, lc, cw), bmc, bmn, h_c)
                xs[grp] = _outproj([o_c], w_out, x, mods, slot(l, 5, grp))

        for grp in range(2):
            xs[grp] = _ffn(xs[grp], mods, [slot(l, k, grp) for k in (6, 7, 8)], norm_g[l, 2], *w2,
                           final_g=final_g if last else None)

    outs = [xs[0], xs[1]]
    for grp in range(2):
        st = ab_states[grp]
        outs += [jnp.stack([s[k] for s in st]) for k in range(5)]
        st = c_states[grp]
        outs += [jnp.stack([s[k] for s in st]) for k in range(2)]
    return tuple(outs)
```

```python
import functools
import math

import numpy as np
import jax
import jax.numpy as jnp
from jax import lax
from jax.experimental import pallas as pl
from jax.experimental.pallas import tpu as pltpu

F32 = jnp.float32
BF16 = jnp.bfloat16

CHUNK = 64
HEAD_DIM = 128
DA_HALF = HEAD_DIM // 2
N_T5_BUCKETS = 32
T5_MAX_DIST = 128
C_PREV_CHUNKS = 8
C_BAND = C_PREV_CHUNKS * CHUNK
REL_CLIP = 128
EPS = 1e-6
NEG_INF = -1e30
LOG2E = math.log2(math.e)
VT_ROWS = HEAD_DIM + 16

VMEM_LIMIT_BYTES = 56 * 1024 * 1024
LANES = 128

ROW_TILE = 512
FF_TILE = 512
FFN_OUT_TILE = 512
FFN_ROW_SPLIT = 2
PROJ_ROW_SPLIT = 2
PROJ_COL_TILE = 1024
ADA_COL_TILE = 1024
BAND_TILE = 256
BAND_HEADS = 4
ATTN_GROUP = 512
SAMPLE_NEAR = 256

_NT = (((1,), (1,)), ((), ()))


def _params(*sem, flags=None):
    return pltpu.CompilerParams(dimension_semantics=sem, vmem_limit_bytes=VMEM_LIMIT_BYTES, flags=flags)


def _modulated_norm(x, g, shift, scale):
    y = x * lax.rsqrt(jnp.mean(x * x, axis=-1, keepdims=True) + EPS) * g
    return y * (1.0 + scale) + shift


def _normed_rows(x_ref, shift_ref, scale_ref, g_ref, h_scr, n_split, r):
    nb, tt, d = x_ref.shape
    rows = nb * tt // n_split
    if nb == 1:
        x, sh, sc = x_ref[:, r * rows:(r + 1) * rows, :], shift_ref[...], scale_ref[...]
    else:
        seqs = slice(r * (nb // n_split), (r + 1) * (nb // n_split))
        x, sh, sc = x_ref[seqs], shift_ref[seqs], scale_ref[seqs]
    hr = _modulated_norm(x, g_ref[...], sh, sc).reshape(rows, d).astype(BF16)
    h_scr[r * rows:(r + 1) * rows, :] = hr
    return hr


def _ada_kernel(c_ref, w_ref, b_ref, o_ref):
    c = c_ref[...]
    a = (c * jax.nn.sigmoid(c)).astype(BF16)
    o_ref[0] = jnp.dot(a, w_ref[0].astype(BF16), preferred_element_type=F32) + b_ref[0]


def _ada_mods(c_all, w_ada, b_ada):
    depth, d, n = w_ada.shape
    r = c_all.shape[0]
    tn = math.gcd(n, ADA_COL_TILE)
    return pl.pallas_call(
        _ada_kernel,
        grid=(depth, n // tn),
        in_specs=[
            pl.BlockSpec((r, d), lambda l, j: (0, 0)),
            pl.BlockSpec((1, d, tn), lambda l, j: (l, 0, j)),
            pl.BlockSpec((1, 1, tn), lambda l, j: (l, 0, j)),
        ],
        out_specs=pl.BlockSpec((1, r, tn), lambda l, j: (l, 0, j)),
        out_shape=jax.ShapeDtypeStruct((depth, r, n), F32),
        compiler_params=_params("arbitrary", "arbitrary"),
        name="ada_mods",
    )(c_all, w_ada, b_ada.reshape(depth, 1, n))


def _row_blocking(x):
    b, t, _ = x.shape
    if t >= ROW_TILE:
        assert t % ROW_TILE == 0
        return 1, ROW_TILE, t // ROW_TILE, b * (t // ROW_TILE)
    assert (b * t) % 8 == 0
    return b, t, 1, 1


def _x_index(nb, tps):
    if nb == 1:
        return lambda i, j: (i // tps, i % tps, 0)
    return lambda i, j: (0, 0, 0)


def _mod_spec(nb, tps, d, slot):
    if nb == 1:
        return pl.BlockSpec((None, 1, 1, d), lambda i, j: (slot, i // tps, 0, 0))
    return pl.BlockSpec((None, nb, 1, d), lambda i, j: (slot, 0, 0, 0))


def _ffn_kernel(*refs, nb, tt, n_ff, n_out, final):
    if final:
        (x_ref, shift_ref, scale_ref, gate_ref, g_ref, wg_ref, wu_ref, wo_ref, fg_ref,
         o_ref, h_scr, a_scr) = refs
    else:
        (x_ref, shift_ref, scale_ref, gate_ref, g_ref, wg_ref, wu_ref, wo_ref,
         o_ref, h_scr, a_scr) = refs
    j = pl.program_id(1)
    d = x_ref.shape[-1]
    tm = nb * tt
    tf = wg_ref.shape[1]
    tn = wo_ref.shape[1]

    rows = tm // FFN_ROW_SPLIT
    normed_rows = functools.partial(_normed_rows, x_ref, shift_ref, scale_ref, g_ref, h_scr, FFN_ROW_SPLIT)

    def hidden_chunk(rows_of):
        def products(r):
            hr = rows_of(r)
            return (jnp.dot(hr, wg_ref[...], preferred_element_type=F32),
                    jnp.dot(hr, wu_ref[...], preferred_element_type=F32))

        parts = []
        cur = products(0)
        for r in range(FFN_ROW_SPLIT):
            nxt = products(r + 1) if r + 1 < FFN_ROW_SPLIT else None
            gg, uu = cur
            parts.append((gg * jax.nn.sigmoid(gg) * uu).astype(BF16))
            cur = nxt
        a_scr[j] = jnp.concatenate(parts, axis=0)

    @pl.when(j == 0)
    def _():
        hidden_chunk(normed_rows)

    @pl.when(jnp.logical_and(j > 0, j < n_ff))
    def _():
        hidden_chunk(lambda r: h_scr[r * rows:(r + 1) * rows, :])

    @pl.when(j >= n_ff)
    def _():
        a = jnp.concatenate([a_scr[f] for f in range(n_ff)], axis=1)
        acc = jnp.dot(a, wo_ref[...], preferred_element_type=F32).reshape(nb, tt, tn)
        for n in range(n_out):
            @pl.when(j == n_ff + n)
            def _(n=n):
                cs = slice(n * tn, (n + 1) * tn)
                o_ref[:, :, cs] = x_ref[:, :, cs] + 0.5 * gate_ref[:, :, cs] * acc

    if final:
        @pl.when(j == n_ff + n_out - 1)
        def _():
            y = o_ref[...]
            o_ref[...] = y * lax.rsqrt(jnp.mean(y * y, axis=-1, keepdims=True) + EPS) * fg_ref[...]


def _ffn(x, mods, slots, g, w_in_p, w_out_p, final_g=None):
    b, t, d = x.shape
    nb, tt, tps, nrows = _row_blocking(x)
    ffp = w_out_p.shape[0]
    n_ff = ffp // FF_TILE
    tn = math.gcd(d, FFN_OUT_TILE)
    n_out = d // tn
    final = final_g is not None
    in_specs = [
        pl.BlockSpec((nb, tt, d), _x_index(nb, tps)),
        _mod_spec(nb, tps, d, slots[0]),
        _mod_spec(nb, tps, d, slots[1]),
        _mod_spec(nb, tps, d, slots[2]),
        pl.BlockSpec((1, d), lambda i, j: (0, 0)),
        pl.BlockSpec((d, FF_TILE), lambda i, j: (0, jnp.minimum(j, n_ff - 1))),
        pl.BlockSpec((d, FF_TILE), lambda i, j: (0, jnp.minimum(j, n_ff - 1) + n_ff)),
        pl.BlockSpec((ffp, tn), lambda i, j: (0, jnp.maximum(j - n_ff, 0))),
    ]
    args = [x, mods, mods, mods, g.reshape(1, d), w_in_p, w_in_p, w_out_p]
    if final:
        in_specs.append(pl.BlockSpec((1, d), lambda i, j: (0, 0)))
        args.append(final_g.reshape(1, d))
    return pl.pallas_call(
        functools.partial(_ffn_kernel, nb=nb, tt=tt, n_ff=n_ff, n_out=n_out, final=final),
        grid=(nrows, n_ff + n_out),
        in_specs=in_specs,
        out_specs=pl.BlockSpec((nb, tt, d), _x_index(nb, tps)),
        out_shape=jax.ShapeDtypeStruct((b, t, d), F32),
        scratch_shapes=[pltpu.VMEM((nb * tt, d), BF16), pltpu.VMEM((n_ff, nb * tt, FF_TILE), BF16)],
        compiler_params=_params("parallel", "arbitrary"),
        name="ffn",
    )(*args)


def _prep_ffn_weights(w_in, w_out):
    ff = w_out.shape[0]
    ffp = -(-ff // FF_TILE) * FF_TILE
    pad = ffp - ff
    wg = jnp.pad(w_in[:, :ff].astype(BF16), ((0, 0), (0, pad)))
    wu = jnp.pad(w_in[:, ff:].astype(BF16), ((0, 0), (0, pad)))
    return jnp.concatenate([wg, wu], axis=1), jnp.pad(w_out.astype(BF16), ((0, pad), (0, 0)))


def _log_sigmoid(x):
    return jnp.minimum(x, 0.0) - jnp.log1p(jnp.exp(-jnp.abs(x)))


def _proj_kernel(*refs, nb, tt, tn, tps, taps, has_f, vt_cols, vt_tile):
    x_ref, shift_ref, scale_ref, g_ref, w_ref, cs_ref = refs[:6]
    pos = 6
    if has_f:
        wf_ref, bf_ref = refs[6:8]
        pos = 8
    obf_ref = refs[pos]
    tap_refs = refs[pos + 1:pos + 1 + len({tp[0] for tp in taps})]
    pos = pos + 1 + len(tap_refs)
    if vt_cols:
        vt_ref = refs[pos]
        pos += 1
    if has_f:
        logf_ref = refs[pos]
        pos += 1
    h_scr = refs[pos]
    i = pl.program_id(0)
    j = pl.program_id(1)
    d = x_ref.shape[-1]

    def store_attention_copy(y):
        obf_ref[...] = (y * cs_ref[...]).astype(BF16).reshape(nb, tt, tn)

    assert all(tp[1] > 0 for tp in taps) and all(vc[0] > 0 for vc in vt_cols)

    @pl.when(j == 0)
    def _():
        hs = []
        ys = []
        for r in range(PROJ_ROW_SPLIT):
            hs.append(_normed_rows(x_ref, shift_ref, scale_ref, g_ref, h_scr, PROJ_ROW_SPLIT, r))
            ys.append(jnp.dot(hs[-1], w_ref[...], preferred_element_type=F32))
        store_attention_copy(jnp.concatenate(ys, axis=0))
        if has_f:
            fr = lax.dot_general(wf_ref[...], jnp.concatenate(hs, axis=0), _NT, preferred_element_type=F32)
            logf_ref[...] = _log_sigmoid(fr[:logf_ref.shape[0]] + bf_ref[...])

    @pl.when(j > 0)
    def _():
        y = jnp.dot(h_scr[...], w_ref[...], preferred_element_type=F32)
        store_attention_copy(y)
        _proj_taps(y, i, j, tap_refs, vt_ref if vt_cols else None, nb=nb, tt=tt, tn=tn, tps=tps, taps=taps,
                   vt_cols=vt_cols, vt_tile=vt_tile)


def _proj_taps(y, i, j, tap_refs, vt_ref, *, nb, tt, tn, tps, taps, vt_cols, vt_tile):
    for out_idx, jval, col0, last_rows_only in taps:
        cond = j == jval
        if last_rows_only and tps > 1:
            cond = jnp.logical_and(cond, i % tps == tps - 1)

        @pl.when(cond)
        def _(out_idx=out_idx, col0=col0):
            tap_refs[out_idx][:, :, col0:col0 + tn] = y.reshape(nb, tt, tn)

    for jval, head0 in vt_cols:
        @pl.when(j == jval)
        def _(head0=head0):
            for hh in range(tn // HEAD_DIM):
                yt = y[:, hh * HEAD_DIM:(hh + 1) * HEAD_DIM].T.astype(BF16)
                yt = jnp.concatenate([yt, jnp.ones((VT_ROWS - HEAD_DIM, yt.shape[1]), BF16)], axis=0)
                for s in range(tt // vt_tile):
                    vt_ref[0, head0 + hh, s] = yt[:, s * vt_tile:(s + 1) * vt_tile]


def _seg_taps(tn, width, segments, last_rows_only):
    per = width // tn
    assert per * tn == width
    return [(o, sidx * per + s, s * tn, last_rows_only) for o, sidx in enumerate(segments) for s in range(per)]


def _vt_cols(tn, width, segments):
    per = width // tn
    hpb = tn // HEAD_DIM
    return [(sidx * per + s, (o * per + s) * hpb) for o, sidx in enumerate(segments) for s in range(per)]


def _proj(x, mods, slots, g, w_bf, col_scale, tn, tap_shapes, taps, wf=None, bf=None, vt_cols=(), vt_tile=None):
    b, t, d = x.shape
    n = w_bf.shape[1]
    nb, tt, tps, nrows = _row_blocking(x)
    has_f = wf is not None
    assert not vt_cols or nb == 1
    in_specs = [
        pl.BlockSpec((nb, tt, d), _x_index(nb, tps)),
        _mod_spec(nb, tps, d, slots[0]),
        _mod_spec(nb, tps, d, slots[1]),
        pl.BlockSpec((1, d), lambda i, j: (0, 0)),
        pl.BlockSpec((d, tn), lambda i, j: (0, j)),
        pl.BlockSpec((1, tn), lambda i, j: (0, j)),
    ]
    args = [x, mods, mods, g.reshape(1, d), w_bf, jnp.asarray(col_scale, F32).reshape(1, n)]
    if has_f:
        in_specs += [pl.BlockSpec(wf.shape, lambda i, j: (0, 0)), pl.BlockSpec(bf.shape, lambda i, j: (0, 0))]
        args += [wf, bf]
    xi = _x_index(nb, tps)
    out_specs = [pl.BlockSpec((nb, tt, tn), lambda i, j: xi(i, j)[:2] + (j,))]
    out_shape = [jax.ShapeDtypeStruct((b, t, n), BF16)]
    for shp, follows_rows in tap_shapes:
        if follows_rows:
            out_specs.append(pl.BlockSpec((nb, tt, shp[2]), xi))
        elif nb == 1:
            out_specs.append(pl.BlockSpec((1, shp[1], shp[2]), lambda i, j: (i // tps, 0, 0)))
        else:
            out_specs.append(pl.BlockSpec(shp, lambda i, j: (0, 0, 0)))
        out_shape.append(jax.ShapeDtypeStruct(shp, F32))
    if vt_cols:
        n_vh = len(vt_cols) * (tn // HEAD_DIM)
        out_specs.append(pl.BlockSpec((1, n_vh, tt // vt_tile, VT_ROWS, vt_tile), lambda i, j: (i // tps, 0, i % tps, 0, 0)))
        out_shape.append(jax.ShapeDtypeStruct((b, n_vh, t // vt_tile, VT_ROWS, vt_tile), BF16))
    if has_f:
        nf = bf.shape[0]
        out_specs.append(pl.BlockSpec((nf, nb * tt), lambda i, j: (0, i)))
        out_shape.append(jax.ShapeDtypeStruct((nf, b * t), F32))
    return pl.pallas_call(
        functools.partial(_proj_kernel, nb=nb, tt=tt, tn=tn, tps=tps, taps=tuple(taps), has_f=has_f,
                          vt_cols=tuple(vt_cols), vt_tile=vt_tile),
        grid=(nrows, n // tn),
        in_specs=in_specs,
        out_specs=out_specs,
        out_shape=out_shape,
        scratch_shapes=[pltpu.VMEM((nb * tt, d), BF16)],
        compiler_params=_params("arbitrary", "arbitrary"),
        name="mixer_in_proj",
    )(*args)


def _outproj_kernel(*refs, nb, tt, n_in):
    o_refs = refs[:n_in]
    w_ref, x_ref, gate_ref, out_ref = refs[n_in:]
    d = x_ref.shape[-1]
    acc = None
    row0 = 0
    for o_ref in o_refs:
        wd = o_ref.shape[-1]
        part = jnp.dot(o_ref[...].reshape(nb * tt, wd), w_ref[row0:row0 + wd, :], preferred_element_type=F32)
        acc = part if acc is None else acc + part
        row0 += wd
    out_ref[...] = x_ref[...] + gate_ref[...] * acc.reshape(nb, tt, d)


def _outproj(o_list, w_bf, x, mods, gate_slot):
    b, t, d = x.shape
    nb, tt, tps, nrows = _row_blocking(x)
    xi = _x_index(nb, tps)
    in_specs = [pl.BlockSpec((nb, tt, o.shape[-1]), xi) for o in o_list]
    in_specs += [
        pl.BlockSpec(w_bf.shape, lambda i, j: (0, 0)),
        pl.BlockSpec((nb, tt, d), xi),
        _mod_spec(nb, tps, d, gate_slot),
    ]
    return pl.pallas_call(
        functools.partial(_outproj_kernel, nb=nb, tt=tt, n_in=len(o_list)),
        grid=(nrows, 1),
        in_specs=in_specs,
        out_specs=pl.BlockSpec((nb, tt, d), xi),
        out_shape=jax.ShapeDtypeStruct((b, t, d), F32),
        compiler_params=_params("parallel", "arbitrary"),
        name="mixer_out_proj",
    )(*o_list, w_bf, x, mods)


def _cumsum_kernel(x_ref, c_ref, o_ref):
    x = x_ref[0]
    n = x.shape[1]
    lane = lax.broadcasted_iota(jnp.int32, x.shape, 1)
    s = 1
    while s < n:
        x = x + jnp.where(lane >= s, pltpu.roll(x, s, axis=1), 0.0)
        s *= 2
    o_ref[0] = x + c_ref[0][:, :1]


def _cumsum(x, carry):
    g, h, n = x.shape
    return pl.pallas_call(
        _cumsum_kernel,
        grid=(g,),
        in_specs=[pl.BlockSpec((1, h, n), lambda i: (i, 0, 0)), pl.BlockSpec((1, h, LANES), lambda i: (i, 0, 0))],
        out_specs=pl.BlockSpec((1, h, n), lambda i: (i, 0, 0)),
        out_shape=jax.ShapeDtypeStruct((g, h, n), F32),
        compiler_params=_params("arbitrary"),
        name="logf_cumsum",
    )(x, carry)


def _softmax_init(m_scr, acc_scr):
    m_scr[...] = jnp.full(m_scr.shape, NEG_INF, F32)
    acc_scr[...] = jnp.zeros(acc_scr.shape, F32)


def _softmax_update_t(s, s_max, vt, m_prev, acc_prev):
    m_new = jnp.maximum(m_prev, s_max)
    alpha = jnp.exp2(m_prev - m_new)
    p = jnp.exp2(s - m_new)
    acc_new = alpha * acc_prev + jnp.dot(vt, p.astype(BF16), preferred_element_type=F32)
    return m_new, acc_new


def _normalised_t(acc):
    return (acc[:HEAD_DIM] / acc[HEAD_DIM:HEAD_DIM + 1]).T


def _with_max(s):
    return s, jnp.max(s, axis=0, keepdims=True)


def _consume_tile(scores, vt, m_scr, acc_scr, next_scores=None):
    n = len(scores) // 2
    g = m_scr.shape[1] // n
    sls = [slice(i * g, (i + 1) * g) for i in range(n)]
    state = [(m_scr[:, sl], acc_scr[:, sl]) for sl in sls]
    new, nxt = [], []
    for i in range(n):
        if next_scores is not None:
            nxt.extend(next_scores(i))
        new.append(_softmax_update_t(scores[2 * i], scores[2 * i + 1], vt, *state[i]))
    for sl, (m_new, acc_new) in zip(sls, new):
        m_scr[:, sl] = m_new
        acc_scr[:, sl] = acc_new
    return tuple(nxt)


def _toeplitz_values(tbl_ref, idx, n_vals, nh, h, ref_row):
    ref_val = tbl_ref[ref_row * nh + h]

    def body(v, acc):
        return jnp.where(idx == v, tbl_ref[v * nh + h] - ref_val, acc)

    return lax.fori_loop(0, n_vals, body, jnp.zeros(idx.shape, F32))


def _stack_diff_queries(q):
    lane = lax.broadcasted_iota(jnp.int32, q.shape, 1)
    zero = jnp.zeros_like(q)
    qs = jnp.concatenate([jnp.where(lane < DA_HALF, q, zero), jnp.where(lane >= DA_HALF, q, zero)], axis=0)
    return qs


def _diff_finalize(o, lam, g, out_scale):
    tq = o.shape[0] // 2
    od = o[:tq] - lam * o[tq:]
    return od * lax.rsqrt(jnp.mean(od * od, axis=-1, keepdims=True) + EPS) * g * out_scale


def _diff_p_kernel(lam_ref, tbl_ref, q_ref, k_ref, vt_ref, idx_ref, base_ref, g_ref, o_ref,
                   m_scr, acc_scr, bias_scr, *, tq, nh, far_bucket, out_scale):
    h = pl.program_id(0)
    bi = pl.program_id(1)
    qi = pl.program_id(2)
    nblk = tq // LANES

    @pl.when(jnp.logical_and(bi == 0, qi == 0))
    def _():
        t = _toeplitz_values(tbl_ref, idx_ref[...], N_T5_BUCKETS, nh, h, far_bucket) * LOG2E
        bias_scr[...] = base_ref[...]
        for blk in range(nblk):
            sl = slice(blk * LANES, (blk + 1) * LANES)
            bias_scr[0, sl, sl] += t[0]
            if blk >= 1:
                bias_scr[0, (blk - 1) * LANES:blk * LANES, sl] += t[1]
        bias_scr[1, (nblk - 1) * LANES:, :LANES] += t[1]

    qs = _stack_diff_queries(q_ref[0])
    _softmax_init(m_scr, acc_scr)
    ngrp = 2 * tq // ATTN_GROUP

    def scores(j, slot):
        k = k_ref[0, pl.ds(pl.multiple_of(j * tq, tq), tq), :]

        def group(i):
            s = lax.dot_general(k, qs[i * ATTN_GROUP:(i + 1) * ATTN_GROUP], _NT, preferred_element_type=F32)
            if slot is not None:
                q0 = (i * ATTN_GROUP) % tq
                s = s + bias_scr[slot, :, q0:q0 + ATTN_GROUP]
            return _with_max(s)
        return group

    n_far = jnp.maximum(qi - 1, 0)
    before_slot = jnp.where(qi == 0, 2, 1)

    @pl.when(n_far >= 1)
    def _():
        first = scores(0, None)
        cur = tuple(x for i in range(ngrp) for x in first(i))

        def far(j, cur):
            return _consume_tile(cur, vt_ref[0, 0, j], m_scr, acc_scr, scores(j + 1, None))

        cur = lax.fori_loop(0, n_far - 1, far, cur)
        _consume_tile(cur, vt_ref[0, 0, n_far - 1], m_scr, acc_scr)

    before = scores(n_far, before_slot)
    cur = tuple(x for i in range(ngrp) for x in before(i))
    cur = _consume_tile(cur, vt_ref[0, 0, n_far], m_scr, acc_scr, scores(qi, 0))
    _consume_tile(cur, vt_ref[0, 0, qi], m_scr, acc_scr)
    o = _normalised_t(acc_scr[...])
    o_ref[0] = _diff_finalize(o, lam_ref[0], g_ref[...], out_scale).astype(BF16)


def _diff_prompt(lam, tbl, qkv, vt, g, n_heads, out_scale):
    b, t, _ = qkv.shape
    tq = vt.shape[-1]
    far_bucket = N_T5_BUCKETS // 2 - 1
    pos = np.arange(LANES)
    idx = np.stack([_t5_bucket_np(pos[:, None] - pos[None, :]),
                    _t5_bucket_np(pos[:, None] - pos[None, :] - LANES)]).astype(np.int32)
    kq = np.arange(tq)
    base = np.zeros((3, tq, tq), np.float32)
    base[0] = np.where((kq[:, None] // CHUNK) <= (kq[None, :] // CHUNK), 0.0, NEG_INF)
    base[2] = NEG_INF
    return pl.pallas_call(
        functools.partial(_diff_p_kernel, tq=tq, nh=n_heads, far_bucket=far_bucket, out_scale=out_scale),
        grid=(n_heads, b, t // tq),
        in_specs=[
            pl.BlockSpec(memory_space=pltpu.SMEM),
            pl.BlockSpec(memory_space=pltpu.SMEM),
            pl.BlockSpec((1, tq, HEAD_DIM), lambda h, bi, qi: (bi, qi, h)),
            pl.BlockSpec((1, t, HEAD_DIM), lambda h, bi, qi: (bi, 0, n_heads + h)),
            pl.BlockSpec((1, 1, t // tq, VT_ROWS, tq), lambda h, bi, qi: (bi, h, 0, 0, 0)),
            pl.BlockSpec(idx.shape, lambda h, bi, qi: (0, 0, 0)),
            pl.BlockSpec(base.shape, lambda h, bi, qi: (0, 0, 0)),
            pl.BlockSpec((1, HEAD_DIM), lambda h, bi, qi: (0, 0)),
        ],
        out_specs=pl.BlockSpec((1, tq, HEAD_DIM), lambda h, bi, qi: (bi, qi, h)),
        out_shape=jax.ShapeDtypeStruct((b, t, n_heads * HEAD_DIM), BF16),
        scratch_shapes=[pltpu.VMEM((1, 2 * tq), F32), pltpu.VMEM((VT_ROWS, 2 * tq), F32),
                        pltpu.VMEM((3, tq, tq), F32)],
        compiler_params=_params("arbitrary", "arbitrary", "arbitrary"),
        name="diff_attn_prompt",
    )(lam, tbl, qkv, qkv, vt, jnp.asarray(idx), jnp.asarray(base), g)


def _fox_p_kernel(fref_ref, q_ref, k_ref, vt_ref, f_ref, o_ref, m_scr, acc_scr, fcol_scr, *, tq):
    bi = pl.program_id(0)
    h = pl.program_id(1)
    qi = pl.program_id(2)
    nblk = tq // LANES

    @pl.when(qi == 0)
    def _():
        for c in range(f_ref.shape[2] * nblk):
            row = f_ref[0, 0, c // nblk][:, (c % nblk) * LANES:(c % nblk + 1) * LANES]
            fcol_scr[c * LANES:(c + 1) * LANES, :] = jnp.broadcast_to(row, (LANES, LANES)).T

    q = q_ref[0]
    fref = fref_ref[(bi * pl.num_programs(1) + h) * pl.num_programs(2) + qi]
    _softmax_init(m_scr, acc_scr)

    ngrp = tq // ATTN_GROUP

    def scores(j, causal):
        start = pl.multiple_of(j * tq, tq)
        k = k_ref[0, pl.ds(start, tq), :]
        decay = (fref - fcol_scr[pl.ds(start, tq), :]) * LOG2E
        decay = jnp.concatenate([decay] * (ATTN_GROUP // LANES), axis=1)

        def group(i):
            s = lax.dot_general(k, q[i * ATTN_GROUP:(i + 1) * ATTN_GROUP], _NT, preferred_element_type=F32)
            s = s + decay
            if causal:
                key = lax.broadcasted_iota(jnp.int32, s.shape, 0)
                qry = lax.broadcasted_iota(jnp.int32, s.shape, 1) + i * ATTN_GROUP
                s = jnp.where(key <= qry, s, NEG_INF)
            return _with_max(s)
        return group

    @pl.when(qi >= 1)
    def _():
        first = scores(0, False)
        cur = tuple(x for i in range(ngrp) for x in first(i))

        def far(j, cur):
            return _consume_tile(cur, vt_ref[0, 0, j], m_scr, acc_scr, scores(j + 1, False))

        cur = lax.fori_loop(0, qi - 1, far, cur)
        _consume_tile(cur, vt_ref[0, 0, qi - 1], m_scr, acc_scr)

    diag = scores(qi, True)
    _consume_tile(tuple(x for i in range(ngrp) for x in diag(i)), vt_ref[0, 0, qi], m_scr, acc_scr)
    o_ref[0] = _normalised_t(acc_scr[...]).astype(BF16)


def _fox_prompt(fref, qkv, vt, fcum, n_heads, col0, vt_head0):
    b, t, _ = qkv.shape
    tq = vt.shape[-1]
    nq = t // tq
    return pl.pallas_call(
        functools.partial(_fox_p_kernel, tq=tq),
        grid=(b, n_heads, nq),
        in_specs=[
            pl.BlockSpec(memory_space=pltpu.SMEM),
            pl.BlockSpec((1, tq, HEAD_DIM), lambda bi, h, qi: (bi, qi, col0 + h)),
            pl.BlockSpec((1, t, HEAD_DIM), lambda bi, h, qi: (bi, 0, col0 + n_heads + h)),
            pl.BlockSpec((1, 1, nq, VT_ROWS, tq), lambda bi, h, qi: (bi, vt_head0 + h, 0, 0, 0)),
            pl.BlockSpec((1, 1, nq, 1, tq), lambda bi, h, qi: (bi, h, 0, 0, 0)),
        ],
        out_specs=pl.BlockSpec((1, tq, HEAD_DIM), lambda bi, h, qi: (bi, qi, h)),
        out_shape=jax.ShapeDtypeStruct((b, t, n_heads * HEAD_DIM), BF16),
        scratch_shapes=[pltpu.VMEM((1, tq), F32), pltpu.VMEM((VT_ROWS, tq), F32),
                        pltpu.VMEM((t, LANES), F32)],
        compiler_params=_params("arbitrary", "arbitrary", "arbitrary"),
        name="fox_attn_prompt",
    )(fref, qkv, qkv, vt, fcum)


def _joint_softmax_pv(s_c, s_n, vc, vn):
    m = jnp.maximum(jnp.max(s_c, axis=1, keepdims=True), jnp.max(s_n, axis=1, keepdims=True))
    p_c = jnp.exp2(s_c - m)
    p_n = jnp.exp2(s_n - m)
    l = jnp.sum(p_c, axis=1, keepdims=True) + jnp.sum(p_n, axis=1, keepdims=True)
    acc = jnp.dot(p_c.astype(BF16), vc, preferred_element_type=F32)
    acc = acc + jnp.dot(p_n.astype(BF16), vn, preferred_element_type=F32)
    return acc / l


def _diff_s_kernel(lam_ref, q_ref, kn_ref, vn_ref, kc_ref, vc_ref, bnear_ref, bnew_ref, g_ref, o_ref, *, near, out_scale):
    qs = _stack_diff_queries(q_ref[0])
    kc = kc_ref[0]
    vc = vc_ref[0]
    p = kc.shape[0]
    s_c = lax.dot_general(qs, kc, _NT, preferred_element_type=F32)
    bnear = bnear_ref[0]
    s_c = jnp.concatenate([s_c[:, :p - near], s_c[:, p - near:] + jnp.concatenate([bnear, bnear], axis=0)], axis=1)
    bnew = bnew_ref[0]
    s_n = lax.dot_general(qs, kn_ref[0], _NT, preferred_element_type=F32) + jnp.concatenate([bnew, bnew], axis=0)
    o = _joint_softmax_pv(s_c, s_n, vc, vn_ref[0])
    o_ref[0] = _diff_finalize(o, lam_ref[0], g_ref[...], out_scale).astype(BF16)


def _diff_sample(lam, qkv, kc, vc, bnear, bnew, g, n_heads, out_scale):
    b, t, _ = qkv.shape
    p = kc.shape[1]
    near = bnear.shape[-1]
    hd = HEAD_DIM
    return pl.pallas_call(
        functools.partial(_diff_s_kernel, near=near, out_scale=out_scale),
        grid=(b, n_heads),
        in_specs=[
            pl.BlockSpec(memory_space=pltpu.SMEM),
            pl.BlockSpec((1, t, hd), lambda bi, h: (bi, 0, h)),
            pl.BlockSpec((1, t, hd), lambda bi, h: (bi, 0, n_heads + h)),
            pl.BlockSpec((1, t, hd), lambda bi, h: (bi, 0, 2 * n_heads + h)),
            pl.BlockSpec((1, p, hd), lambda bi, h: (bi, 0, h)),
            pl.BlockSpec((1, p, hd), lambda bi, h: (bi, 0, h)),
            pl.BlockSpec((1, t, near), lambda bi, h: (h, 0, 0)),
            pl.BlockSpec((1, t, t), lambda bi, h: (h, 0, 0)),
            pl.BlockSpec((1, hd), lambda bi, h: (0, 0)),
        ],
        out_specs=pl.BlockSpec((1, t, hd), lambda bi, h: (bi, 0, h)),
        out_shape=jax.ShapeDtypeStruct((b, t, n_heads * hd), BF16),
        compiler_params=_params("parallel", "parallel"),
        name="diff_attn_sample",
    )(lam, qkv, qkv, qkv, kc, vc, bnear, bnew, g)


def _fox_s_kernel(fref_ref, q_ref, kn_ref, vn_ref, kc_ref, vc_ref, fc_ref, fn_ref, o_ref):
    bi = pl.program_id(0)
    h = pl.program_id(1)
    q = q_ref[0]
    t = q.shape[0]
    fref = fref_ref[bi * pl.num_programs(1) + h]
    s_c = lax.dot_general(q, kc_ref[0], _NT, preferred_element_type=F32) + (fref - fc_ref[0, 0]) * LOG2E
    s_n = lax.dot_general(q, kn_ref[0], _NT, preferred_element_type=F32) + (fref - fn_ref[0, 0][:, :t]) * LOG2E
    row = lax.broadcasted_iota(jnp.int32, s_n.shape, 0)
    col = lax.broadcasted_iota(jnp.int32, s_n.shape, 1)
    s_n = jnp.where(col <= row, s_n, NEG_INF)
    o_ref[0] = _joint_softmax_pv(s_c, s_n, vc_ref[0], vn_ref[0]).astype(BF16)


def _fox_sample(fref, qkv, kc, vc, fc, fn, n_heads, col0):
    b, t, _ = qkv.shape
    p = kc.shape[1]
    hd = HEAD_DIM
    return pl.pallas_call(
        _fox_s_kernel,
        grid=(b, n_heads),
        in_specs=[
            pl.BlockSpec(memory_space=pltpu.SMEM),
            pl.BlockSpec((1, t, hd), lambda bi, h: (bi, 0, col0 + h)),
            pl.BlockSpec((1, t, hd), lambda bi, h: (bi, 0, col0 + n_heads + h)),
            pl.BlockSpec((1, t, hd), lambda bi, h: (bi, 0, col0 + 2 * n_heads + h)),
            pl.BlockSpec((1, p, hd), lambda bi, h: (bi, 0, h)),
            pl.BlockSpec((1, p, hd), lambda bi, h: (bi, 0, h)),
            pl.BlockSpec((1, 1, 1, p), lambda bi, h: (bi, h, 0, 0)),
            pl.BlockSpec((1, 1, 1, fn.shape[-1]), lambda bi, h: (bi, h, 0, 0)),
        ],
        out_specs=pl.BlockSpec((1, t, hd), lambda bi, h: (bi, 0, h)),
        out_shape=jax.ShapeDtypeStruct((b, t, n_heads * hd), BF16),
        compiler_params=_params("parallel", "parallel"),
        name="fox_attn_sample",
    )(fref, qkv, qkv, qkv, kc, vc, fc, fn)


def _band_p_kernel(tbl_ref, q_ref, k_ref, vt_ref, idx_ref, base_ref, o_ref, bm_scr, *, tq, nkb, nh, hps):
    hg = pl.program_id(0)
    bi = pl.program_id(1)
    i = pl.program_id(2)
    nblk = tq // LANES
    qoff = (nkb - 1) * nblk

    @pl.when(jnp.logical_and(bi == 0, i == 0))
    def _():
        for e in range(hps):
            t = _toeplitz_values(tbl_ref, idx_ref[...], 2 * REL_CLIP + 1, nh, hg * hps + e, 0)
            bm_scr[e] = base_ref[...]
            for bq in range(nblk):
                kd = qoff + bq
                bm_scr[e, kd * LANES:(kd + 1) * LANES, bq * LANES:(bq + 1) * LANES] += t[0] * LOG2E
                bm_scr[e, (kd - 1) * LANES:kd * LANES, bq * LANES:(bq + 1) * LANES] += t[1] * LOG2E

    def head_scores(e):
        hs = slice(e * HEAD_DIM, (e + 1) * HEAD_DIM)
        q = q_ref[0, :, hs]
        ss = []
        for blk in range(nkb):
            jb = i - (nkb - 1) + blk
            jc = jnp.maximum(jb, 0)
            k = k_ref[0, pl.ds(pl.multiple_of(jc * tq, tq), tq), hs]
            s = lax.dot_general(k, q, _NT, preferred_element_type=F32) + bm_scr[e, blk * tq:(blk + 1) * tq, :]
            if blk < nkb - 1:
                s = s + jnp.where(jb < 0, NEG_INF, 0.0)
            ss.append(s)
        return ss, functools.reduce(jnp.maximum, [jnp.max(s, axis=0, keepdims=True) for s in ss])

    outs = []
    nxt = head_scores(0)
    for e in range(hps):
        ss, m = nxt
        if e + 1 < hps:
            nxt = head_scores(e + 1)
        acc = None
        for blk, s in enumerate(ss):
            jc = jnp.maximum(i - (nkb - 1) + blk, 0)
            a = jnp.dot(vt_ref[0, e, jc], jnp.exp2(s - m).astype(BF16), preferred_element_type=F32)
            acc = a if acc is None else acc + a
        outs.append(_normalised_t(acc).astype(BF16))
    o_ref[0] = jnp.concatenate(outs, axis=1)


def _band_prompt(tbl, qkv, vt, n_heads):
    b, t, _ = qkv.shape
    tq = vt.shape[-1]
    nkb = C_BAND // tq + 1
    assert (nkb - 1) * tq == C_BAND and tq % LANES == 0 and LANES >= REL_CLIP
    pos = np.arange(LANES)
    rel = pos[:, None] - pos[None, :]
    idx = (np.stack([np.clip(rel, -REL_CLIP, REL_CLIP), np.clip(rel - LANES, -REL_CLIP, REL_CLIP)]) + REL_CLIP).astype(np.int32)
    k_pos = np.arange(nkb * tq)[:, None]
    q_pos = C_BAND + np.arange(tq)[None, :]
    kc, qc = k_pos // CHUNK, q_pos // CHUNK
    base = np.where((kc <= qc) & (qc - kc <= C_PREV_CHUNKS), 0.0, NEG_INF).astype(np.float32)
    hps = BAND_HEADS
    ngrp = n_heads // hps
    assert ngrp * hps == n_heads
    wd = hps * HEAD_DIM
    return pl.pallas_call(
        functools.partial(_band_p_kernel, tq=tq, nkb=nkb, nh=n_heads, hps=hps),
        grid=(ngrp, b, t // tq),
        in_specs=[
            pl.BlockSpec(memory_space=pltpu.SMEM),
            pl.BlockSpec((1, tq, wd), lambda h, bi, i: (bi, i, h)),
            pl.BlockSpec((1, t, wd), lambda h, bi, i: (bi, 0, ngrp + h)),
            pl.BlockSpec((1, hps, t // tq, VT_ROWS, tq), lambda h, bi, i: (bi, h, 0, 0, 0)),
            pl.BlockSpec(idx.shape, lambda h, bi, i: (0, 0, 0)),
            pl.BlockSpec(base.shape, lambda h, bi, i: (0, 0)),
        ],
        out_specs=pl.BlockSpec((1, tq, wd), lambda h, bi, i: (bi, i, h)),
        out_shape=jax.ShapeDtypeStruct((b, t, n_heads * HEAD_DIM), BF16),
        scratch_shapes=[pltpu.VMEM((hps, nkb * tq, tq), F32)],
        compiler_params=_params("arbitrary", "arbitrary", "arbitrary"),
        name="band_attn_prompt",
    )(tbl, qkv, qkv, vt, jnp.asarray(idx), jnp.asarray(base))


def _band_s_kernel(q_ref, kn_ref, vn_ref, kc_ref, vc_ref, bmc_ref, bmn_ref, o_ref):
    q = q_ref[0]
    s_c = lax.dot_general(q, kc_ref[0], _NT, preferred_element_type=F32) + bmc_ref[0]
    s_n = lax.dot_general(q, kn_ref[0], _NT, preferred_element_type=F32) + bmn_ref[0]
    o_ref[0] = _joint_softmax_pv(s_c, s_n, vc_ref[0], vn_ref[0]).astype(BF16)


def _band_sample(qkv, kc, vc, bmc, bmn, n_heads):
    b, t, _ = qkv.shape
    lc = kc.shape[1]
    hd = HEAD_DIM
    return pl.pallas_call(
        _band_s_kernel,
        grid=(b, n_heads),
        in_specs=[
            pl.BlockSpec((1, t, hd), lambda bi, h: (bi, 0, h)),
            pl.BlockSpec((1, t, hd), lambda bi, h: (bi, 0, n_heads + h)),
            pl.BlockSpec((1, t, hd), lambda bi, h: (bi, 0, 2 * n_heads + h)),
            pl.BlockSpec((1, lc, hd), lambda bi, h: (bi, 0, h)),
            pl.BlockSpec((1, lc, hd), lambda bi, h: (bi, 0, h)),
            pl.BlockSpec((1, t, lc), lambda bi, h: (h, 0, 0)),
            pl.BlockSpec((1, t, t), lambda bi, h: (h, 0, 0)),
        ],
        out_specs=pl.BlockSpec((1, t, hd), lambda bi, h: (bi, 0, h)),
        out_shape=jax.ShapeDtypeStruct((b, t, n_heads * hd), BF16),
        compiler_params=_params("parallel", "parallel"),
        name="band_attn_sample",
    )(qkv, qkv, qkv, kc, vc, bmc, bmn)


def _t5_bucket_np(rel):
    nb = N_T5_BUCKETS // 2
    max_exact = nb // 2
    n = np.abs(rel)
    nf = np.maximum(n, 1).astype(np.float64)
    large = max_exact + (np.log(nf / max_exact) / math.log(T5_MAX_DIST / max_exact) * (nb - max_exact)).astype(np.int64)
    large = np.minimum(large, nb - 1)
    return np.where(rel > 0, nb, 0) + np.where(n < max_exact, n, large)


def _t5_bias_tiles(t5_table, q_pos, k_pos, far_bucket):
    rel = k_pos[None, :] - q_pos[:, None]
    mask = (k_pos[None, :] // CHUNK) <= (q_pos[:, None] // CHUNK)
    bias = jnp.transpose(t5_table.astype(F32)[_t5_bucket_np(rel)], (2, 0, 1))
    bias = (bias - t5_table.astype(F32)[far_bucket][:, None, None]) * LOG2E
    return jnp.where(mask[None], bias, NEG_INF)


def _band_bias_tiles(rel_table, q_pos, k_pos):
    rel = np.clip(k_pos[None, :] - q_pos[:, None], -REL_CLIP, REL_CLIP) + REL_CLIP
    qc = q_pos[:, None] // CHUNK
    kc = k_pos[None, :] // CHUNK
    mask = (kc <= qc) & (qc - kc <= C_PREV_CHUNKS) & (k_pos[None, :] >= 0)
    bias = jnp.transpose(rel_table.astype(F32)[rel], (2, 0, 1)) * LOG2E
    return jnp.where(mask[None], bias, NEG_INF)


def kernel(x_prompt, x_sample, cache_a_k, cache_a_v, cache_b_k, cache_b_v, cache_b_logf, cache_c_k, cache_c_v, c_prompt, c_sample, w_ada, b_ada, norm_g, w_ffn_in, w_ffn_out, w_in_ab, b_forget, w_out_ab, lambda_q1, lambda_k1, lambda_q2, lambda_k2, subln_g, t5_table, w_in_c, w_out_c, c_rel_bias, final_g):
    depth = w_ada.shape[0]
    bsz, seq, d = x_prompt.shape
    dbsz, dseq, _ = x_sample.shape
    past = cache_b_logf.shape[2]
    h_a = cache_a_k.shape[3]
    h_b = cache_b_k.shape[3]
    h_c = cache_c_k.shape[3]
    wa, wb, cw = h_a * HEAD_DIM, h_b * HEAD_DIM, h_c * HEAD_DIM
    assert bsz == dbsz and wa == wb and 3 * wa + 3 * wb == 3 * cw
    assert LANES >= T5_MAX_DIST and SAMPLE_NEAR >= T5_MAX_DIST and past >= SAMPLE_NEAR
    far_bucket = (N_T5_BUCKETS // 2) - 1

    mods = _ada_mods(jnp.concatenate([c_prompt, c_sample], axis=0), w_ada, b_ada)
    mods = mods.reshape(depth, 2, bsz, 9, d).transpose(0, 3, 1, 2, 4).reshape(depth * 9 * 2, bsz, 1, d)

    def slot(l, k, grp):
        return (l * 9 + k) * 2 + grp

    xs = [x_prompt, x_sample]
    ab_states = [[], []]
    c_states = [[], []]
    for l in range(depth):
        i = l // 2
        last = l == depth - 1
        w1 = _prep_ffn_weights(w_ffn_in[l, 0], w_ffn_out[l, 0])
        w2 = _prep_ffn_weights(w_ffn_in[l, 1], w_ffn_out[l, 1])
        for grp in range(2):
            xs[grp] = _ffn(xs[grp], mods, [slot(l, k, grp) for k in range(3)], norm_g[l, 0], *w1)

        if l % 2 == 0:
            lam_init = 0.8 - 0.6 * math.exp(-0.3 * l)
            lam = (jnp.exp(jnp.sum(lambda_q1[i].astype(F32) * lambda_k1[i].astype(F32)))
                   - jnp.exp(jnp.sum(lambda_q2[i].astype(F32) * lambda_k2[i].astype(F32))) + lam_init).reshape(1)
            n_main = 3 * wa + 3 * wb
            w_main = w_in_ab[i][:, :n_main].astype(BF16)
            wf = jnp.pad(w_in_ab[i][:, n_main:].T.astype(BF16), ((0, 16 - h_b), (0, 0)))
            bfo = b_forget[i].astype(F32).reshape(h_b, 1)
            w_out = w_out_ab[i].astype(BF16)
            g_sub = subln_g[i].astype(F32).reshape(1, HEAD_DIM)
            tn = min(PROJ_COL_TILE, wa)
            taps = _seg_taps(tn, wa, (1, 2, 4, 5), False)
            cs = np.ones((n_main,), np.float32)
            cs[:wa] = DA_HALF ** -0.5 * LOG2E
            cs[3 * wa:3 * wa + wb] = HEAD_DIM ** -0.5 * LOG2E
            for grp in range(2):
                x = xs[grp]
                b, t, _ = x.shape
                tap_shapes = [((b, t, wa), True)] * 4
                if grp == 0:
                    qkv, ka, va, kb, vb, vt, logf_t = _proj(
                        x, mods, [slot(l, 3, grp), slot(l, 4, grp)], norm_g[l, 1], w_main, cs, tn, tap_shapes, taps,
                        wf, bfo, vt_cols=_vt_cols(tn, wa, (2, 5)), vt_tile=ROW_TILE)
                else:
                    qkv, ka, va, kb, vb, logf_t = _proj(
                        x, mods, [slot(l, 3, grp), slot(l, 4, grp)], norm_g[l, 1], w_main, cs, tn, tap_shapes, taps,
                        wf, bfo)
                logf = logf_t.reshape(h_b, b, t).transpose(1, 2, 0)
                ab_states[grp].append((ka.reshape(b, t, h_a, HEAD_DIM), va.reshape(b, t, h_a, HEAD_DIM),
                                       kb.reshape(b, t, h_b, HEAD_DIM), vb.reshape(b, t, h_b, HEAD_DIM), logf))
                logf_bh = logf_t.reshape(h_b, b, t).transpose(1, 0, 2)
                if grp == 0:
                    o_a = _diff_prompt(lam, t5_table.astype(F32).reshape(-1), qkv, vt, g_sub, h_a, 1.0 - lam_init)
                    fcum = _cumsum(logf_bh, jnp.zeros((b, h_b, LANES), F32))
                    nq = t // ROW_TILE
                    fref = fcum[:, :, ::ROW_TILE].reshape(-1)
                    o_b = _fox_prompt(fref, qkv, vt, fcum.reshape(b, h_b, nq, 1, ROW_TILE), h_b, 3 * h_a, h_a)
                else:
                    q_pos = past + np.arange(t)
                    bnear = _t5_bias_tiles(t5_table, q_pos, past - SAMPLE_NEAR + np.arange(SAMPLE_NEAR), far_bucket)
                    bnew = _t5_bias_tiles(t5_table, q_pos, q_pos, far_bucket)
                    kc = cache_a_k[i].astype(BF16).reshape(b, past, wa)
                    vc = cache_a_v[i].astype(BF16).reshape(b, past, wa)
                    o_a = _diff_sample(lam, qkv, kc, vc, bnear, bnew, g_sub, h_a, 1.0 - lam_init)
                    fc = _cumsum(cache_b_logf[i].astype(F32).transpose(0, 2, 1), jnp.zeros((b, h_b, LANES), F32))
                    carry = fc[:, :, past - 1:past]
                    fn = _cumsum(jnp.pad(logf_bh, ((0, 0), (0, 0), (0, LANES - t))),
                                 jnp.broadcast_to(carry, (b, h_b, LANES)))
                    o_b = _fox_sample(carry.reshape(-1), qkv, cache_b_k[i].astype(BF16).reshape(b, past, wb),
                                      cache_b_v[i].astype(BF16).reshape(b, past, wb), fc.reshape(b, h_b, 1, past),
                                      fn.reshape(b, h_b, 1, LANES), h_b, 3 * h_a)
                xs[grp] = _outproj([o_a, o_b], w_out, x, mods, slot(l, 5, grp))
        else:
            w_in = w_in_c[i].astype(BF16)
            w_out = w_out_c[i].astype(BF16)
            tn = min(PROJ_COL_TILE, cw)
            taps = _seg_taps(tn, cw, (1, 2), True)
            cs = np.ones((3 * cw,), np.float32)
            cs[:cw] = HEAD_DIM ** -0.5 * LOG2E
            for grp in range(2):
                x = xs[grp]
                b, t, _ = x.shape
                keep = min(C_BAND, t)
                assert keep == min(t, ROW_TILE)
                tap_shapes = [((b, keep, cw), False)] * 2
                if grp == 0:
                    qkv, k_new, v_new, vt = _proj(
                        x, mods, [slot(l, 3, grp), slot(l, 4, grp)], norm_g[l, 1], w_in, cs, tn, tap_shapes, taps,
                        vt_cols=_vt_cols(tn, cw, (2,)), vt_tile=BAND_TILE)
                else:
                    qkv, k_new, v_new = _proj(
                        x, mods, [slot(l, 3, grp), slot(l, 4, grp)], norm_g[l, 1], w_in, cs, tn, tap_shapes, taps)
                c_states[grp].append((k_new.reshape(b, keep, h_c, HEAD_DIM), v_new.reshape(b, keep, h_c, HEAD_DIM)))
                if grp == 0:
                    o_c = _band_prompt(c_rel_bias[i].astype(F32).reshape(-1), qkv, vt, h_c)
                else:
                    lc = cache_c_k.shape[2]
                    q_pos = past + np.arange(t)
                    bmc = _band_bias_tiles(c_rel_bias[i], q_pos, past - lc + np.arange(lc))
                    bmn = _band_bias_tiles(c_rel_bias[i], q_pos, q_pos)
                    o_c = _band_sample(qkv, cache_c_k[i].astype(BF16).reshape(b, lc, cw),
                                       cache_c_v[i].astype(BF16).reshape(b, lc, cw), bmc, bmn, h_c)
                xs[grp] = _outproj([o_c], w_out, x, mods, slot(l, 5, grp))

        for grp in range(2):
            xs[grp] = _ffn(xs[grp], mods, [slot(l, k, grp) for k in (6, 7, 8)], norm_g[l, 2], *w2,
                           final_g=final_g if last else None)

    outs = [xs[0], xs[1]]
    for grp in range(2):
        st = ab_states[grp]
        outs += [jnp.stack([s[k] for s in st]) for k in range(5)]
        st = c_states[grp]
        outs += [jnp.stack([s[k] for s in st]) for k in range(2)]
    return tuple(outs)
```

```python
import functools
import math

import numpy as np
import jax
import jax.numpy as jnp
from jax import lax
from jax.experimental import pallas as pl
from jax.experimental.pallas import tpu as pltpu

F32 = jnp.float32
BF16 = jnp.bfloat16

CHUNK = 64
HEAD_DIM = 128
DA_HALF = HEAD_DIM // 2
N_T5_BUCKETS = 32
T5_MAX_DIST = 128
C_PREV_CHUNKS = 8
C_BAND = C_PREV_CHUNKS * CHUNK
REL_CLIP = 128
EPS = 1e-6
NEG_INF = -1e30
LOG2E = math.log2(math.e)
VT_ROWS = HEAD_DIM + 16

VMEM_LIMIT_BYTES = 56 * 1024 * 1024
LANES = 128

ROW_TILE = 512
FF_TILE = 512
FFN_OUT_TILE = 512
FFN_ROW_SPLIT = 2
PROJ_ROW_SPLIT = 2
PROJ_COL_TILE = 1024
ADA_COL_TILE = 1024
BAND_TILE = 256
BAND_HEADS = 4
ATTN_GROUP = 512
SAMPLE_NEAR = 256

_NT = (((1,), (1,)), ((), ()))


def _params(*sem, flags=None):
    return pltpu.CompilerParams(dimension_semantics=sem, vmem_limit_bytes=VMEM_LIMIT_BYTES, flags=flags)


def _modulated_norm(x, g, shift, scale):
    y = x * lax.rsqrt(jnp.mean(x * x, axis=-1, keepdims=True) + EPS) * g
    return y * (1.0 + scale) + shift


def _normed_rows(x_ref, shift_ref, scale_ref, g_ref, h_scr, n_split, r):
    nb, tt, d = x_ref.shape
    rows = nb * tt // n_split
    if nb == 1:
        x, sh, sc = x_ref[:, r * rows:(r + 1) * rows, :], shift_ref[...], scale_ref[...]
    else:
        seqs = slice(r * (nb // n_split), (r + 1) * (nb // n_split))
        x, sh, sc = x_ref[seqs], shift_ref[seqs], scale_ref[seqs]
    hr = _modulated_norm(x, g_ref[...], sh, sc).reshape(rows, d).astype(BF16)
    h_scr[r * rows:(r + 1) * rows, :] = hr
    return hr


def _ada_kernel(c_ref, w_ref, b_ref, o_ref):
    c = c_ref[...]
    a = (c * jax.nn.sigmoid(c)).astype(BF16)
    o_ref[0] = jnp.dot(a, w_ref[0].astype(BF16), preferred_element_type=F32) + b_ref[0]


def _ada_mods(c_all, w_ada, b_ada):
    depth, d, n = w_ada.shape
    r = c_all.shape[0]
    tn = math.gcd(n, ADA_COL_TILE)
    return pl.pallas_call(
        _ada_kernel,
        grid=(depth, n // tn),
        in_specs=[
            pl.BlockSpec((r, d), lambda l, j: (0, 0)),
            pl.BlockSpec((1, d, tn), lambda l, j: (l, 0, j)),
            pl.BlockSpec((1, 1, tn), lambda l, j: (l, 0, j)),
        ],
        out_specs=pl.BlockSpec((1, r, tn), lambda l, j: (l, 0, j)),
        out_shape=jax.ShapeDtypeStruct((depth, r, n), F32),
        compiler_params=_params("arbitrary", "arbitrary"),
        name="ada_mods",
    )(c_all, w_ada, b_ada.reshape(depth, 1, n))


def _row_blocking(x):
    b, t, _ = x.shape
    if t >= ROW_TILE:
        assert t % ROW_TILE == 0
        return 1, ROW_TILE, t // ROW_TILE, b * (t // ROW_TILE)
    assert (b * t) % 8 == 0
    return b, t, 1, 1


def _x_index(nb, tps):
    if nb == 1:
        return lambda i, j: (i // tps, i % tps, 0)
    return lambda i, j: (0, 0, 0)


def _mod_spec(nb, tps, d, slot):
    if nb == 1:
        return pl.BlockSpec((None, 1, 1, d), lambda i, j: (slot, i // tps, 0, 0))
    return pl.BlockSpec((None, nb, 1, d), lambda i, j: (slot, 0, 0, 0))


def _ffn_kernel(*refs, nb, tt, n_ff, n_out, final):
    if final:
        (x_ref, shift_ref, scale_ref, gate_ref, g_ref, wg_ref, wu_ref, wo_ref, fg_ref,
         o_ref, h_scr, a_scr) = refs
    else:
        (x_ref, shift_ref, scale_ref, gate_ref, g_ref, wg_ref, wu_ref, wo_ref,
         o_ref, h_scr, a_scr) = refs
    j = pl.program_id(1)
    d = x_ref.shape[-1]
    tm = nb * tt
    tf = wg_ref.shape[1]
    tn = wo_ref.shape[1]

    rows = tm // FFN_ROW_SPLIT
    normed_rows = functools.partial(_normed_rows, x_ref, shift_ref, scale_ref, g_ref, h_scr, FFN_ROW_SPLIT)

    def hidden_chunk(rows_of):
        def products(r):
            hr = rows_of(r)
            return (jnp.dot(hr, wg_ref[...], preferred_element_type=F32),
                    jnp.dot(hr, wu_ref[...], preferred_element_type=F32))

        parts = []
        cur = products(0)
        for r in range(FFN_ROW_SPLIT):
            nxt = products(r + 1) if r + 1 < FFN_ROW_SPLIT else None
            gg, uu = cur
            parts.append((gg * jax.nn.sigmoid(gg) * uu).astype(BF16))
            cur = nxt
        a_scr[j] = jnp.concatenate(parts, axis=0)

    @pl.when(j == 0)
    def _():
        hidden_chunk(normed_rows)

    @pl.when(jnp.logical_and(j > 0, j < n_ff))
    def _():
        hidden_chunk(lambda r: h_scr[r * rows:(r + 1) * rows, :])

    @pl.when(j >= n_ff)
    def _():
        a = jnp.concatenate([a_scr[f] for f in range(n_ff)], axis=1)
        acc = jnp.dot(a, wo_ref[...], preferred_element_type=F32).reshape(nb, tt, tn)
        for n in range(n_out):
            @pl.when(j == n_ff + n)
            def _(n=n):
                cs = slice(n * tn, (n + 1) * tn)
                o_ref[:, :, cs] = x_ref[:, :, cs] + 0.5 * gate_ref[:, :, cs] * acc

    if final:
        @pl.when(j == n_ff + n_out - 1)
        def _():
            y = o_ref[...]
            o_ref[...] = y * lax.rsqrt(jnp.mean(y * y, axis=-1, keepdims=True) + EPS) * fg_ref[...]


def _ffn(x, mods, slots, g, w_in_p, w_out_p, final_g=None):
    b, t, d = x.shape
    nb, tt, tps, nrows = _row_blocking(x)
    ffp = w_out_p.shape[0]
    n_ff = ffp // FF_TILE
    tn = math.gcd(d, FFN_OUT_TILE)
    n_out = d // tn
    final = final_g is not None
    in_specs = [
        pl.BlockSpec((nb, tt, d), _x_index(nb, tps)),
        _mod_spec(nb, tps, d, slots[0]),
        _mod_spec(nb, tps, d, slots[1]),
        _mod_spec(nb, tps, d, slots[2]),
        pl.BlockSpec((1, d), lambda i, j: (0, 0)),
        pl.BlockSpec((d, FF_TILE), lambda i, j: (0, jnp.minimum(j, n_ff - 1))),
        pl.BlockSpec((d, FF_TILE), lambda i, j: (0, jnp.minimum(j, n_ff - 1) + n_ff)),
        pl.BlockSpec((ffp, tn), lambda i, j: (0, jnp.maximum(j - n_ff, 0))),
    ]
    args = [x, mods, mods, mods, g.reshape(1, d), w_in_p, w_in_p, w_out_p]
    if final:
        in_specs.append(pl.BlockSpec((1, d), lambda i, j: (0, 0)))
        args.append(final_g.reshape(1, d))
    return pl.pallas_call(
        functools.partial(_ffn_kernel, nb=nb, tt=tt, n_ff=n_ff, n_out=n_out, final=final),
        grid=(nrows, n_ff + n_out),
        in_specs=in_specs,
        out_specs=pl.BlockSpec((nb, tt, d), _x_index(nb, tps)),
        out_shape=jax.ShapeDtypeStruct((b, t, d), F32),
        scratch_shapes=[pltpu.VMEM((nb * tt, d), BF16), pltpu.VMEM((n_ff, nb * tt, FF_TILE), BF16)],
        compiler_params=_params("parallel", "arbitrary"),
        name="ffn",
    )(*args)


def _prep_ffn_weights(w_in, w_out):
    ff = w_out.shape[0]
    ffp = -(-ff // FF_TILE) * FF_TILE
    pad = ffp - ff
    d = w_in.shape[0]
    w_in_p = jnp.pad(w_in.reshape(d, 2, ff).astype(BF16), ((0, 0), (0, 0), (0, pad))).reshape(d, 2 * ffp)
    return w_in_p, jnp.pad(w_out.astype(BF16), ((0, pad), (0, 0)))


def _log_sigmoid(x):
    return jnp.minimum(x, 0.0) - jnp.log1p(jnp.exp(-jnp.abs(x)))


def _proj_kernel(*refs, nb, tt, tn, tps, taps, has_f, vt_cols, vt_tile):
    x_ref, shift_ref, scale_ref, g_ref, w_ref, cs_ref = refs[:6]
    pos = 6
    if has_f:
        wf_ref, bf_ref = refs[6:8]
        pos = 8
    obf_ref = refs[pos]
    tap_refs = refs[pos + 1:pos + 1 + len({tp[0] for tp in taps})]
    pos = pos + 1 + len(tap_refs)
    if vt_cols:
        vt_ref = refs[pos]
        pos += 1
    if has_f:
        logf_ref = refs[pos]
        pos += 1
    h_scr = refs[pos]
    i = pl.program_id(0)
    j = pl.program_id(1)
    d = x_ref.shape[-1]

    def store_attention_copy(y):
        obf_ref[...] = (y * cs_ref[...]).astype(BF16).reshape(nb, tt, tn)

    assert all(tp[1] > 0 for tp in taps) and all(vc[0] > 0 for vc in vt_cols)

    @pl.when(j == 0)
    def _():
        hs = []
        ys = []
        for r in range(PROJ_ROW_SPLIT):
            hs.append(_normed_rows(x_ref, shift_ref, scale_ref, g_ref, h_scr, PROJ_ROW_SPLIT, r))
            ys.append(jnp.dot(hs[-1], w_ref[...], preferred_element_type=F32))
        store_attention_copy(jnp.concatenate(ys, axis=0))
        if has_f:
            fr = lax.dot_general(wf_ref[...], jnp.concatenate(hs, axis=0), _NT, preferred_element_type=F32)
            logf_ref[...] = _log_sigmoid(fr[:logf_ref.shape[0]] + bf_ref[...])

    @pl.when(j > 0)
    def _():
        y = jnp.dot(h_scr[...], w_ref[...], preferred_element_type=F32)
        store_attention_copy(y)
        _proj_taps(y, i, j, tap_refs, vt_ref if vt_cols else None, nb=nb, tt=tt, tn=tn, tps=tps, taps=taps,
                   vt_cols=vt_cols, vt_tile=vt_tile)


def _proj_taps(y, i, j, tap_refs, vt_ref, *, nb, tt, tn, tps, taps, vt_cols, vt_tile):
    for out_idx, jval, col0, last_rows_only in taps:
        cond = j == jval
        if last_rows_only and tps > 1:
            cond = jnp.logical_and(cond, i % tps == tps - 1)

        @pl.when(cond)
        def _(out_idx=out_idx, col0=col0):
            tap_refs[out_idx][:, :, col0:col0 + tn] = y.reshape(nb, tt, tn)

    for jval, head0 in vt_cols:
        @pl.when(j == jval)
        def _(head0=head0):
            for hh in range(tn // HEAD_DIM):
                yt = y[:, hh * HEAD_DIM:(hh + 1) * HEAD_DIM].T.astype(BF16)
                yt = jnp.concatenate([yt, jnp.ones((VT_ROWS - HEAD_DIM, yt.shape[1]), BF16)], axis=0)
                for s in range(tt // vt_tile):
                    vt_ref[0, head0 + hh, s] = yt[:, s * vt_tile:(s + 1) * vt_tile]


def _seg_taps(tn, width, segments, last_rows_only):
    per = width // tn
    assert per * tn == width
    return [(o, sidx * per + s, s * tn, last_rows_only) for o, sidx in enumerate(segments) for s in range(per)]


def _vt_cols(tn, width, segments):
    per = width // tn
    hpb = tn // HEAD_DIM
    return [(sidx * per + s, (o * per + s) * hpb) for o, sidx in enumerate(segments) for s in range(per)]


def _proj(x, mods, slots, g, w_bf, col_scale, tn, tap_shapes, taps, wf=None, bf=None, vt_cols=(), vt_tile=None):
    b, t, d = x.shape
    n = w_bf.shape[1]
    nb, tt, tps, nrows = _row_blocking(x)
    has_f = wf is not None
    assert not vt_cols or nb == 1
    in_specs = [
        pl.BlockSpec((nb, tt, d), _x_index(nb, tps)),
        _mod_spec(nb, tps, d, slots[0]),
        _mod_spec(nb, tps, d, slots[1]),
        pl.BlockSpec((1, d), lambda i, j: (0, 0)),
        pl.BlockSpec((d, tn), lambda i, j: (0, j)),
        pl.BlockSpec((1, tn), lambda i, j: (0, j)),
    ]
    args = [x, mods, mods, g.reshape(1, d), w_bf, jnp.asarray(col_scale, F32).reshape(1, n)]
    if has_f:
        in_specs += [pl.BlockSpec(wf.shape, lambda i, j: (0, 0)), pl.BlockSpec(bf.shape, lambda i, j: (0, 0))]
        args += [wf, bf]
    xi = _x_index(nb, tps)
    out_specs = [pl.BlockSpec((nb, tt, tn), lambda i, j: xi(i, j)[:2] + (j,))]
    out_shape = [jax.ShapeDtypeStruct((b, t, n), BF16)]
    for shp, follows_rows in tap_shapes:
        if follows_rows:
            out_specs.append(pl.BlockSpec((nb, tt, shp[2]), xi))
        elif nb == 1:
            out_specs.append(pl.BlockSpec((1, shp[1], shp[2]), lambda i, j: (i // tps, 0, 0)))
        else:
            out_specs.append(pl.BlockSpec(shp, lambda i, j: (0, 0, 0)))
        out_shape.append(jax.ShapeDtypeStruct(shp, F32))
    if vt_cols:
        n_vh = len(vt_cols) * (tn // HEAD_DIM)
        out_specs.append(pl.BlockSpec((1, n_vh, tt // vt_tile, VT_ROWS, vt_tile), lambda i, j: (i // tps, 0, i % tps, 0, 0)))
        out_shape.append(jax.ShapeDtypeStruct((b, n_vh, t // vt_tile, VT_ROWS, vt_tile), BF16))
    if has_f:
        nf = bf.shape[0]
        out_specs.append(pl.BlockSpec((nf, nb * tt), lambda i, j: (0, i)))
        out_shape.append(jax.ShapeDtypeStruct((nf, b * t), F32))
    return pl.pallas_call(
        functools.partial(_proj_kernel, nb=nb, tt=tt, tn=tn, tps=tps, taps=tuple(taps), has_f=has_f,
                          vt_cols=tuple(vt_cols), vt_tile=vt_tile),
        grid=(nrows, n // tn),
        in_specs=in_specs,
        out_specs=out_specs,
        out_shape=out_shape,
        scratch_shapes=[pltpu.VMEM((nb * tt, d), BF16)],
        compiler_params=_params("arbitrary", "arbitrary"),
        name="mixer_in_proj",
    )(*args)


def _outproj_kernel(*refs, nb, tt, n_in):
    o_refs = refs[:n_in]
    w_ref, x_ref, gate_ref, out_ref = refs[n_in:]
    d = x_ref.shape[-1]
    acc = None
    row0 = 0
    for o_ref in o_refs:
        wd = o_ref.shape[-1]
        part = jnp.dot(o_ref[...].reshape(nb * tt, wd), w_ref[row0:row0 + wd, :], preferred_element_type=F32)
        acc = part if acc is None else acc + part
        row0 += wd
    out_ref[...] = x_ref[...] + gate_ref[...] * acc.reshape(nb, tt, d)


def _outproj(o_list, w_bf, x, mods, gate_slot):
    b, t, d = x.shape
    nb, tt, tps, nrows = _row_blocking(x)
    xi = _x_index(nb, tps)
    in_specs = [pl.BlockSpec((nb, tt, o.shape[-1]), xi) for o in o_list]
    in_specs += [
        pl.BlockSpec(w_bf.shape, lambda i, j: (0, 0)),
        pl.BlockSpec((nb, tt, d), xi),
        _mod_spec(nb, tps, d, gate_slot),
    ]
    return pl.pallas_call(
        functools.partial(_outproj_kernel, nb=nb, tt=tt, n_in=len(o_list)),
        grid=(nrows, 1),
        in_specs=in_specs,
        out_specs=pl.BlockSpec((nb, tt, d), xi),
        out_shape=jax.ShapeDtypeStruct((b, t, d), F32),
        compiler_params=_params("parallel", "arbitrary"),
        name="mixer_out_proj",
    )(*o_list, w_bf, x, mods)


def _cumsum_kernel(x_ref, c_ref, o_ref):
    x = x_ref[0]
    n = x.shape[1]
    lane = lax.broadcasted_iota(jnp.int32, x.shape, 1)
    s = 1
    while s < n:
        x = x + jnp.where(lane >= s, pltpu.roll(x, s, axis=1), 0.0)
        s *= 2
    o_ref[0] = x + c_ref[0][:, :1]


def _cumsum(x, carry):
    g, h, n = x.shape
    return pl.pallas_call(
        _cumsum_kernel,
        grid=(g,),
        in_specs=[pl.BlockSpec((1, h, n), lambda i: (i, 0, 0)), pl.BlockSpec((1, h, LANES), lambda i: (i, 0, 0))],
        out_specs=pl.BlockSpec((1, h, n), lambda i: (i, 0, 0)),
        out_shape=jax.ShapeDtypeStruct((g, h, n), F32),
        compiler_params=_params("arbitrary"),
        name="logf_cumsum",
    )(x, carry)


def _softmax_init(m_scr, acc_scr):
    m_scr[...] = jnp.full(m_scr.shape, NEG_INF, F32)
    acc_scr[...] = jnp.zeros(acc_scr.shape, F32)


def _softmax_update_t(s, s_max, vt, m_prev, acc_prev):
    m_new = jnp.maximum(m_prev, s_max)
    alpha = jnp.exp2(m_prev - m_new)
    p = jnp.exp2(s - m_new)
    acc_new = alpha * acc_prev + jnp.dot(vt, p.astype(BF16), preferred_element_type=F32)
    return m_new, acc_new


def _normalised_t(acc):
    return (acc[:HEAD_DIM] / acc[HEAD_DIM:HEAD_DIM + 1]).T


def _with_max(s):
    return s, jnp.max(s, axis=0, keepdims=True)


def _consume_tile(scores, vt, m_scr, acc_scr, next_scores=None):
    n = len(scores) // 2
    g = m_scr.shape[1] // n
    sls = [slice(i * g, (i + 1) * g) for i in range(n)]
    state = [(m_scr[:, sl], acc_scr[:, sl]) for sl in sls]
    new, nxt = [], []
    for i in range(n):
        if next_scores is not None:
            nxt.extend(next_scores(i))
        new.append(_softmax_update_t(scores[2 * i], scores[2 * i + 1], vt, *state[i]))
    for sl, (m_new, acc_new) in zip(sls, new):
        m_scr[:, sl] = m_new
        acc_scr[:, sl] = acc_new
    return tuple(nxt)


def _toeplitz_values(tbl_ref, idx, n_vals, nh, h, ref_row):
    ref_val = tbl_ref[ref_row * nh + h]

    def body(v, acc):
        return jnp.where(idx == v, tbl_ref[v * nh + h] - ref_val, acc)

    return lax.fori_loop(0, n_vals, body, jnp.zeros(idx.shape, F32))


def _stack_diff_queries(q):
    lane = lax.broadcasted_iota(jnp.int32, q.shape, 1)
    zero = jnp.zeros_like(q)
    qs = jnp.concatenate([jnp.where(lane < DA_HALF, q, zero), jnp.where(lane >= DA_HALF, q, zero)], axis=0)
    return qs


def _diff_finalize(o, lam, g, out_scale):
    tq = o.shape[0] // 2
    od = o[:tq] - lam * o[tq:]
    return od * lax.rsqrt(jnp.mean(od * od, axis=-1, keepdims=True) + EPS) * g * out_scale


def _diff_p_kernel(lam_ref, tbl_ref, q_ref, k_ref, vt_ref, idx_ref, base_ref, g_ref, o_ref,
                   m_scr, acc_scr, bias_scr, *, tq, nh, far_bucket, out_scale):
    h = pl.program_id(0)
    bi = pl.program_id(1)
    qi = pl.program_id(2)
    nblk = tq // LANES

    @pl.when(jnp.logical_and(bi == 0, qi == 0))
    def _():
        t = _toeplitz_values(tbl_ref, idx_ref[...], N_T5_BUCKETS, nh, h, far_bucket) * LOG2E
        bias_scr[...] = base_ref[...]
        for blk in range(nblk):
            sl = slice(blk * LANES, (blk + 1) * LANES)
            bias_scr[0, sl, sl] += t[0]
            if blk >= 1:
                bias_scr[0, (blk - 1) * LANES:blk * LANES, sl] += t[1]
        bias_scr[1, (nblk - 1) * LANES:, :LANES] += t[1]

    qs = _stack_diff_queries(q_ref[0])
    _softmax_init(m_scr, acc_scr)
    ngrp = 2 * tq // ATTN_GROUP

    def scores(j, slot):
        k = k_ref[0, pl.ds(pl.multiple_of(j * tq, tq), tq), :]

        def group(i):
            s = lax.dot_general(k, qs[i * ATTN_GROUP:(i + 1) * ATTN_GROUP], _NT, preferred_element_type=F32)
            if slot is not None:
                q0 = (i * ATTN_GROUP) % tq
                s = s + bias_scr[slot, :, q0:q0 + ATTN_GROUP]
            return _with_max(s)
        return group

    n_far = jnp.maximum(qi - 1, 0)
    before_slot = jnp.where(qi == 0, 2, 1)

    @pl.when(n_far >= 1)
    def _():
        first = scores(0, None)
        cur = tuple(x for i in range(ngrp) for x in first(i))

        def far(j, cur):
            return _consume_tile(cur, vt_ref[0, 0, j], m_scr, acc_scr, scores(j + 1, None))

        cur = lax.fori_loop(0, n_far - 1, far, cur)
        _consume_tile(cur, vt_ref[0, 0, n_far - 1], m_scr, acc_scr)

    before = scores(n_far, before_slot)
    cur = tuple(x for i in range(ngrp) for x in before(i))
    cur = _consume_tile(cur, vt_ref[0, 0, n_far], m_scr, acc_scr, scores(qi, 0))
    _consume_tile(cur, vt_ref[0, 0, qi], m_scr, acc_scr)
    o = _normalised_t(acc_scr[...])
    o_ref[0] = _diff_finalize(o, lam_ref[0], g_ref[...], out_scale).astype(BF16)


def _diff_prompt(lam, tbl, qkv, vt, g, n_heads, out_scale):
    b, t, _ = qkv.shape
    tq = vt.shape[-1]
    far_bucket = N_T5_BUCKETS // 2 - 1
    pos = np.arange(LANES)
    idx = np.stack([_t5_bucket_np(pos[:, None] - pos[None, :]),
                    _t5_bucket_np(pos[:, None] - pos[None, :] - LANES)]).astype(np.int32)
    kq = np.arange(tq)
    base = np.zeros((3, tq, tq), np.float32)
    base[0] = np.where((kq[:, None] // CHUNK) <= (kq[None, :] // CHUNK), 0.0, NEG_INF)
    base[2] = NEG_INF
    return pl.pallas_call(
        functools.partial(_diff_p_kernel, tq=tq, nh=n_heads, far_bucket=far_bucket, out_scale=out_scale),
        grid=(n_heads, b, t // tq),
        in_specs=[
            pl.BlockSpec(memory_space=pltpu.SMEM),
            pl.BlockSpec(memory_space=pltpu.SMEM),
            pl.BlockSpec((1, tq, HEAD_DIM), lambda h, bi, qi: (bi, qi, h)),
            pl.BlockSpec((1, t, HEAD_DIM), lambda h, bi, qi: (bi, 0, n_heads + h)),
            pl.BlockSpec((1, 1, t // tq, VT_ROWS, tq), lambda h, bi, qi: (bi, h, 0, 0, 0)),
            pl.BlockSpec(idx.shape, lambda h, bi, qi: (0, 0, 0)),
            pl.BlockSpec(base.shape, lambda h, bi, qi: (0, 0, 0)),
            pl.BlockSpec((1, HEAD_DIM), lambda h, bi, qi: (0, 0)),
        ],
        out_specs=pl.BlockSpec((1, tq, HEAD_DIM), lambda h, bi, qi: (bi, qi, h)),
        out_shape=jax.ShapeDtypeStruct((b, t, n_heads * HEAD_DIM), BF16),
        scratch_shapes=[pltpu.VMEM((1, 2 * tq), F32), pltpu.VMEM((VT_ROWS, 2 * tq), F32),
                        pltpu.VMEM((3, tq, tq), F32)],
        compiler_params=_params("arbitrary", "arbitrary", "arbitrary"),
        name="diff_attn_prompt",
    )(lam, tbl, qkv, qkv, vt, jnp.asarray(idx), jnp.asarray(base), g)


def _fox_p_kernel(fref_ref, q_ref, k_ref, vt_ref, f_ref, o_ref, m_scr, acc_scr, fcol_scr, *, tq):
    bi = pl.program_id(0)
    h = pl.program_id(1)
    qi = pl.program_id(2)
    nblk = tq // LANES

    @pl.when(qi == 0)
    def _():
        for c in range(f_ref.shape[2] * nblk):
            row = f_ref[0, 0, c // nblk][:, (c % nblk) * LANES:(c % nblk + 1) * LANES]
            fcol_scr[c * LANES:(c + 1) * LANES, :] = jnp.broadcast_to(row, (LANES, LANES)).T

    q = q_ref[0]
    fref = fref_ref[(bi * pl.num_programs(1) + h) * pl.num_programs(2) + qi]
    _softmax_init(m_scr, acc_scr)

    ngrp = tq // ATTN_GROUP

    def scores(j, causal):
        start = pl.multiple_of(j * tq, tq)
        k = k_ref[0, pl.ds(start, tq), :]
        decay = (fref - fcol_scr[pl.ds(start, tq), :]) * LOG2E
        decay = jnp.concatenate([decay] * (ATTN_GROUP // LANES), axis=1)

        def group(i):
            s = lax.dot_general(k, q[i * ATTN_GROUP:(i + 1) * ATTN_GROUP], _NT, preferred_element_type=F32)
            s = s + decay
            if causal:
                key = lax.broadcasted_iota(jnp.int32, s.shape, 0)
                qry = lax.broadcasted_iota(jnp.int32, s.shape, 1) + i * ATTN_GROUP
                s = jnp.where(key <= qry, s, NEG_INF)
            return _with_max(s)
        return group

    @pl.when(qi >= 1)
    def _():
        first = scores(0, False)
        cur = tuple(x for i in range(ngrp) for x in first(i))

        def far(j, cur):
            return _consume_tile(cur, vt_ref[0, 0, j], m_scr, acc_scr, scores(j + 1, False))

        cur = lax.fori_loop(0, qi - 1, far, cur)
        _consume_tile(cur, vt_ref[0, 0, qi - 1], m_scr, acc_scr)

    diag = scores(qi, True)
    _consume_tile(tuple(x for i in range(ngrp) for x in diag(i)), vt_ref[0, 0, qi], m_scr, acc_scr)
    o_ref[0] = _normalised_t(acc_scr[...]).astype(BF16)


def _fox_prompt(fref, qkv, vt, fcum, n_heads, col0, vt_head0):
    b, t, _ = qkv.shape
    tq = vt.shape[-1]
    nq = t // tq
    return pl.pallas_call(
        functools.partial(_fox_p_kernel, tq=tq),
        grid=(b, n_heads, nq),
        in_specs=[
            pl.BlockSpec(memory_space=pltpu.SMEM),
            pl.BlockSpec((1, tq, HEAD_DIM), lambda bi, h, qi: (bi, qi, col0 + h)),
            pl.BlockSpec((1, t, HEAD_DIM), lambda bi, h, qi: (bi, 0, col0 + n_heads + h)),
            pl.BlockSpec((1, 1, nq, VT_ROWS, tq), lambda bi, h, qi: (bi, vt_head0 + h, 0, 0, 0)),
            pl.BlockSpec((1, 1, nq, 1, tq), lambda bi, h, qi: (bi, h, 0, 0, 0)),
        ],
        out_specs=pl.BlockSpec((1, tq, HEAD_DIM), lambda bi, h, qi: (bi, qi, h)),
        out_shape=jax.ShapeDtypeStruct((b, t, n_heads * HEAD_DIM), BF16),
        scratch_shapes=[pltpu.VMEM((1, tq), F32), pltpu.VMEM((VT_ROWS, tq), F32),
                        pltpu.VMEM((t, LANES), F32)],
        compiler_params=_params("arbitrary", "arbitrary", "arbitrary"),
        name="fox_attn_prompt",
    )(fref, qkv, qkv, vt, fcum)


def _joint_softmax_pv(s_c, s_n, vc, vn):
    m = jnp.maximum(jnp.max(s_c, axis=1, keepdims=True), jnp.max(s_n, axis=1, keepdims=True))
    p_c = jnp.exp2(s_c - m)
    p_n = jnp.exp2(s_n - m)
    l = jnp.sum(p_c, axis=1, keepdims=True) + jnp.sum(p_n, axis=1, keepdims=True)
    acc = jnp.dot(p_c.astype(BF16), vc, preferred_element_type=F32)
    acc = acc + jnp.dot(p_n.astype(BF16), vn, preferred_element_type=F32)
    return acc / l


def _diff_s_kernel(lam_ref, q_ref, kn_ref, vn_ref, kc_ref, vc_ref, bnear_ref, bnew_ref, g_ref, o_ref, *, near, out_scale):
    qs = _stack_diff_queries(q_ref[0])
    kc = kc_ref[0].astype(BF16)
    vc = vc_ref[0].astype(BF16)
    p = kc.shape[0]
    s_c = lax.dot_general(qs, kc, _NT, preferred_element_type=F32)
    bnear = bnear_ref[0]
    s_c = jnp.concatenate([s_c[:, :p - near], s_c[:, p - near:] + jnp.concatenate([bnear, bnear], axis=0)], axis=1)
    bnew = bnew_ref[0]
    s_n = lax.dot_general(qs, kn_ref[0], _NT, preferred_element_type=F32) + jnp.concatenate([bnew, bnew], axis=0)
    o = _joint_softmax_pv(s_c, s_n, vc, vn_ref[0])
    o_ref[0] = _diff_finalize(o, lam_ref[0], g_ref[...], out_scale).astype(BF16)


def _diff_sample(lam, qkv, kc, vc, bnear, bnew, g, n_heads, out_scale):
    b, t, _ = qkv.shape
    p = kc.shape[1]
    near = bnear.shape[-1]
    hd = HEAD_DIM
    return pl.pallas_call(
        functools.partial(_diff_s_kernel, near=near, out_scale=out_scale),
        grid=(b, n_heads),
        in_specs=[
            pl.BlockSpec(memory_space=pltpu.SMEM),
            pl.BlockSpec((1, t, hd), lambda bi, h: (bi, 0, h)),
            pl.BlockSpec((1, t, hd), lambda bi, h: (bi, 0, n_heads + h)),
            pl.BlockSpec((1, t, hd), lambda bi, h: (bi, 0, 2 * n_heads + h)),
            pl.BlockSpec((1, p, hd), lambda bi, h: (bi, 0, h)),
            pl.BlockSpec((1, p, hd), lambda bi, h: (bi, 0, h)),
            pl.BlockSpec((1, t, near), lambda bi, h: (h, 0, 0)),
            pl.BlockSpec((1, t, t), lambda bi, h: (h, 0, 0)),
            pl.BlockSpec((1, hd), lambda bi, h: (0, 0)),
        ],
        out_specs=pl.BlockSpec((1, t, hd), lambda bi, h: (bi, 0, h)),
        out_shape=jax.ShapeDtypeStruct((b, t, n_heads * hd), BF16),
        compiler_params=_params("parallel", "parallel"),
        name="diff_attn_sample",
    )(lam, qkv, qkv, qkv, kc, vc, bnear, bnew, g)


def _fox_s_kernel(fref_ref, q_ref, kn_ref, vn_ref, kc_ref, vc_ref, fc_ref, fn_ref, o_ref):
    bi = pl.program_id(0)
    h = pl.program_id(1)
    q = q_ref[0]
    t = q.shape[0]
    fref = fref_ref[bi * pl.num_programs(1) + h]
    s_c = lax.dot_general(q, kc_ref[0].astype(BF16), _NT, preferred_element_type=F32) + (fref - fc_ref[0, 0]) * LOG2E
    s_n = lax.dot_general(q, kn_ref[0], _NT, preferred_element_type=F32) + (fref - fn_ref[0, 0][:, :t]) * LOG2E
    row = lax.broadcasted_iota(jnp.int32, s_n.shape, 0)
    col = lax.broadcasted_iota(jnp.int32, s_n.shape, 1)
    s_n = jnp.where(col <= row, s_n, NEG_INF)
    o_ref[0] = _joint_softmax_pv(s_c, s_n, vc_ref[0].astype(BF16), vn_ref[0]).astype(BF16)


def _fox_sample(fref, qkv, kc, vc, fc, fn, n_heads, col0):
    b, t, _ = qkv.shape
    p = kc.shape[1]
    hd = HEAD_DIM
    return pl.pallas_call(
        _fox_s_kernel,
        grid=(b, n_heads),
        in_specs=[
            pl.BlockSpec(memory_space=pltpu.SMEM),
            pl.BlockSpec((1, t, hd), lambda bi, h: (bi, 0, col0 + h)),
            pl.BlockSpec((1, t, hd), lambda bi, h: (bi, 0, col0 + n_heads + h)),
            pl.BlockSpec((1, t, hd), lambda bi, h: (bi, 0, col0 + 2 * n_heads + h)),
            pl.BlockSpec((1, p, hd), lambda bi, h: (bi, 0, h)),
            pl.BlockSpec((1, p, hd), lambda bi, h: (bi, 0, h)),
            pl.BlockSpec((1, 1, 1, p), lambda bi, h: (bi, h, 0, 0)),
            pl.BlockSpec((1, 1, 1, fn.shape[-1]), lambda bi, h: (bi, h, 0, 0)),
        ],
        out_specs=pl.BlockSpec((1, t, hd), lambda bi, h: (bi, 0, h)),
        out_shape=jax.ShapeDtypeStruct((b, t, n_heads * hd), BF16),
        compiler_params=_params("parallel", "parallel"),
        name="fox_attn_sample",
    )(fref, qkv, qkv, qkv, kc, vc, fc, fn)


def _band_p_kernel(tbl_ref, q_ref, k_ref, vt_ref, idx_ref, base_ref, o_ref, bm_scr, *, tq, nkb, nh, hps):
    hg = pl.program_id(0)
    bi = pl.program_id(1)
    i = pl.program_id(2)
    nblk = tq // LANES
    qoff = (nkb - 1) * nblk

    @pl.when(jnp.logical_and(bi == 0, i == 0))
    def _():
        for e in range(hps):
            t = _toeplitz_values(tbl_ref, idx_ref[...], 2 * REL_CLIP + 1, nh, hg * hps + e, 0)
            bm_scr[e] = base_ref[...]
            for bq in range(nblk):
                kd = qoff + bq
                bm_scr[e, kd * LANES:(kd + 1) * LANES, bq * LANES:(bq + 1) * LANES] += t[0] * LOG2E
                bm_scr[e, (kd - 1) * LANES:kd * LANES, bq * LANES:(bq + 1) * LANES] += t[1] * LOG2E

    def head_scores(e):
        hs = slice(e * HEAD_DIM, (e + 1) * HEAD_DIM)
        q = q_ref[0, :, hs]
        ss = []
        for blk in range(nkb):
            jb = i - (nkb - 1) + blk
            jc = jnp.maximum(jb, 0)
            k = k_ref[0, pl.ds(pl.multiple_of(jc * tq, tq), tq), hs]
            s = lax.dot_general(k, q, _NT, preferred_element_type=F32) + bm_scr[e, blk * tq:(blk + 1) * tq, :]
            if blk < nkb - 1:
                s = s + jnp.where(jb < 0, NEG_INF, 0.0)
            ss.append(s)
        return ss, functools.reduce(jnp.maximum, [jnp.max(s, axis=0, keepdims=True) for s in ss])

    outs = []
    nxt = head_scores(0)
    for e in range(hps):
        ss, m = nxt
        if e + 1 < hps:
            nxt = head_scores(e + 1)
        acc = None
        for blk, s in enumerate(ss):
            jc = jnp.maximum(i - (nkb - 1) + blk, 0)
            a = jnp.dot(vt_ref[0, e, jc], jnp.exp2(s - m).astype(BF16), preferred_element_type=F32)
            acc = a if acc is None else acc + a
        outs.append(_normalised_t(acc).astype(BF16))
    o_ref[0] = jnp.concatenate(outs, axis=1)


def _band_prompt(tbl, qkv, vt, n_heads):
    b, t, _ = qkv.shape
    tq = vt.shape[-1]
    nkb = C_BAND // tq + 1
    assert (nkb - 1) * tq == C_BAND and tq % LANES == 0 and LANES >= REL_CLIP
    pos = np.arange(LANES)
    rel = pos[:, None] - pos[None, :]
    idx = (np.stack([np.clip(rel, -REL_CLIP, REL_CLIP), np.clip(rel - LANES, -REL_CLIP, REL_CLIP)]) + REL_CLIP).astype(np.int32)
    k_pos = np.arange(nkb * tq)[:, None]
    q_pos = C_BAND + np.arange(tq)[None, :]
    kc, qc = k_pos // CHUNK, q_pos // CHUNK
    base = np.where((kc <= qc) & (qc - kc <= C_PREV_CHUNKS), 0.0, NEG_INF).astype(np.float32)
    hps = BAND_HEADS
    ngrp = n_heads // hps
    assert ngrp * hps == n_heads
    wd = hps * HEAD_DIM
    return pl.pallas_call(
        functools.partial(_band_p_kernel, tq=tq, nkb=nkb, nh=n_heads, hps=hps),
        grid=(ngrp, b, t // tq),
        in_specs=[
            pl.BlockSpec(memory_space=pltpu.SMEM),
            pl.BlockSpec((1, tq, wd), lambda h, bi, i: (bi, i, h)),
            pl.BlockSpec((1, t, wd), lambda h, bi, i: (bi, 0, ngrp + h)),
            pl.BlockSpec((1, hps, t // tq, VT_ROWS, tq), lambda h, bi, i: (bi, h, 0, 0, 0)),
            pl.BlockSpec(idx.shape, lambda h, bi, i: (0, 0, 0)),
            pl.BlockSpec(base.shape, lambda h, bi, i: (0, 0)),
        ],
        out_specs=pl.BlockSpec((1, tq, wd), lambda h, bi, i: (bi, i, h)),
        out_shape=jax.ShapeDtypeStruct((b, t, n_heads * HEAD_DIM), BF16),
        scratch_shapes=[pltpu.VMEM((hps, nkb * tq, tq), F32)],
        compiler_params=_params("arbitrary", "arbitrary", "arbitrary"),
        name="band_attn_prompt",
    )(tbl, qkv, qkv, vt, jnp.asarray(idx), jnp.asarray(base))


def _band_s_kernel(q_ref, kn_ref, vn_ref, kc_ref, vc_ref, bmc_ref, bmn_ref, o_ref, *, n_heads):
    outs = []
    for h in range(n_heads):
        hs = slice(h * HEAD_DIM, (h + 1) * HEAD_DIM)
        q = q_ref[0, :, hs]
        kc = kc_ref[0, :, h, :].astype(BF16)
        vc = vc_ref[0, :, h, :].astype(BF16)
        s_c = lax.dot_general(q, kc, _NT, preferred_element_type=F32) + bmc_ref[h]
        s_n = lax.dot_general(q, kn_ref[0, :, hs], _NT, preferred_element_type=F32) + bmn_ref[h]
        outs.append(_joint_softmax_pv(s_c, s_n, vc, vn_ref[0, :, hs]).astype(BF16))
    o_ref[0] = jnp.concatenate(outs, axis=1)


def _band_sample(qkv, kc, vc, bmc, bmn, n_heads):
    b, t, _ = qkv.shape
    lc = kc.shape[1]
    cw = n_heads * HEAD_DIM
    return pl.pallas_call(
        functools.partial(_band_s_kernel, n_heads=n_heads),
        grid=(b,),
        in_specs=[
            pl.BlockSpec((1, t, cw), lambda bi: (bi, 0, 0)),
            pl.BlockSpec((1, t, cw), lambda bi: (bi, 0, 1)),
            pl.BlockSpec((1, t, cw), lambda bi: (bi, 0, 2)),
            pl.BlockSpec((1, lc, n_heads, HEAD_DIM), lambda bi: (bi, 0, 0, 0)),
            pl.BlockSpec((1, lc, n_heads, HEAD_DIM), lambda bi: (bi, 0, 0, 0)),
            pl.BlockSpec(bmc.shape, lambda bi: (0, 0, 0)),
            pl.BlockSpec(bmn.shape, lambda bi: (0, 0, 0)),
        ],
        out_specs=pl.BlockSpec((1, t, cw), lambda bi: (bi, 0, 0)),
        out_shape=jax.ShapeDtypeStruct((b, t, cw), BF16),
        compiler_params=_params("parallel"),
        name="band_attn_sample",
    )(qkv, qkv, qkv, kc, vc, bmc, bmn)


def _t5_bucket_np(rel):
    nb = N_T5_BUCKETS // 2
    max_exact = nb // 2
    n = np.abs(rel)
    nf = np.maximum(n, 1).astype(np.float64)
    large = max_exact + (np.log(nf / max_exact) / math.log(T5_MAX_DIST / max_exact) * (nb - max_exact)).astype(np.int64)
    large = np.minimum(large, nb - 1)
    return np.where(rel > 0, nb, 0) + np.where(n < max_exact, n, large)


def _toeplitz(table, index_of_rel, q_pos, k_pos):
    r, c = len(q_pos), len(k_pos)
    assert (np.diff(q_pos) == 1).all() and (np.diff(k_pos) == 1).all()
    rels = (k_pos[0] - q_pos[0]) - (r - 1) + np.arange(r + c - 1)
    v = jnp.pad(table.astype(F32)[index_of_rel(rels)].T, ((0, 0), (0, 1)))
    x = jnp.tile(v, (1, r))[:, :r * (r + c - 1)].reshape(v.shape[0], r, r + c - 1)
    return x[:, :, r - 1:r - 1 + c]


def _t5_bias_tiles(t5_table, q_pos, k_pos, far_bucket):
    mask = (k_pos[None, :] // CHUNK) <= (q_pos[:, None] // CHUNK)
    bias = _toeplitz(t5_table, _t5_bucket_np, q_pos, k_pos)
    bias = (bias - t5_table.astype(F32)[far_bucket][:, None, None]) * LOG2E
    return jnp.where(mask[None], bias, NEG_INF)


def _band_bias_tiles(rel_table, q_pos, k_pos):
    qc = q_pos[:, None] // CHUNK
    kc = k_pos[None, :] // CHUNK
    mask = (kc <= qc) & (qc - kc <= C_PREV_CHUNKS) & (k_pos[None, :] >= 0)
    bias = _toeplitz(rel_table, lambda rel: np.clip(rel, -REL_CLIP, REL_CLIP) + REL_CLIP, q_pos, k_pos) * LOG2E
    return jnp.where(mask[None], bias, NEG_INF)


def kernel(x_prompt, x_sample, cache_a_k, cache_a_v, cache_b_k, cache_b_v, cache_b_logf, cache_c_k, cache_c_v, c_prompt, c_sample, w_ada, b_ada, norm_g, w_ffn_in, w_ffn_out, w_in_ab, b_forget, w_out_ab, lambda_q1, lambda_k1, lambda_q2, lambda_k2, subln_g, t5_table, w_in_c, w_out_c, c_rel_bias, final_g):
    depth = w_ada.shape[0]
    bsz, seq, d = x_prompt.shape
    dbsz, dseq, _ = x_sample.shape
    past = cache_b_logf.shape[2]
    h_a = cache_a_k.shape[3]
    h_b = cache_b_k.shape[3]
    h_c = cache_c_k.shape[3]
    wa, wb, cw = h_a * HEAD_DIM, h_b * HEAD_DIM, h_c * HEAD_DIM
    assert bsz == dbsz and wa == wb and 3 * wa + 3 * wb == 3 * cw
    assert LANES >= T5_MAX_DIST and SAMPLE_NEAR >= T5_MAX_DIST and past >= SAMPLE_NEAR
    far_bucket = (N_T5_BUCKETS // 2) - 1

    mods = _ada_mods(jnp.concatenate([c_prompt, c_sample], axis=0), w_ada, b_ada)
    mods = mods.reshape(depth, 2, bsz, 9, d).transpose(0, 3, 1, 2, 4).reshape(depth * 9 * 2, bsz, 1, d)

    def slot(l, k, grp):
        return (l * 9 + k) * 2 + grp

    xs = [x_prompt, x_sample]
    ab_states = [[], []]
    c_states = [[], []]
    for l in range(depth):
        i = l // 2
        last = l == depth - 1
        w1 = _prep_ffn_weights(w_ffn_in[l, 0], w_ffn_out[l, 0])
        w2 = _prep_ffn_weights(w_ffn_in[l, 1], w_ffn_out[l, 1])
        for grp in range(2):
            xs[grp] = _ffn(xs[grp], mods, [slot(l, k, grp) for k in range(3)], norm_g[l, 0], *w1)

        if l % 2 == 0:
            lam_init = 0.8 - 0.6 * math.exp(-0.3 * l)
            lam = (jnp.exp(jnp.sum(lambda_q1[i].astype(F32) * lambda_k1[i].astype(F32)))
                   - jnp.exp(jnp.sum(lambda_q2[i].astype(F32) * lambda_k2[i].astype(F32))) + lam_init).reshape(1)
            n_main = 3 * wa + 3 * wb
            w_main = w_in_ab[i][:, :n_main].astype(BF16)
            wf = jnp.pad(w_in_ab[i][:, n_main:].T.astype(BF16), ((0, 16 - h_b), (0, 0)))
            bfo = b_forget[i].astype(F32).reshape(h_b, 1)
            w_out = w_out_ab[i].astype(BF16)
            g_sub = subln_g[i].astype(F32).reshape(1, HEAD_DIM)
            tn = min(PROJ_COL_TILE, wa)
            taps = _seg_taps(tn, wa, (1, 2, 4, 5), False)
            cs = np.ones((n_main,), np.float32)
            cs[:wa] = DA_HALF ** -0.5 * LOG2E
            cs[3 * wa:3 * wa + wb] = HEAD_DIM ** -0.5 * LOG2E
            for grp in range(2):
                x = xs[grp]
                b, t, _ = x.shape
                tap_shapes = [((b, t, wa), True)] * 4
                if grp == 0:
                    qkv, ka, va, kb, vb, vt, logf_t = _proj(
                        x, mods, [slot(l, 3, grp), slot(l, 4, grp)], norm_g[l, 1], w_main, cs, tn, tap_shapes, taps,
                        wf, bfo, vt_cols=_vt_cols(tn, wa, (2, 5)), vt_tile=ROW_TILE)
                else:
                    qkv, ka, va, kb, vb, logf_t = _proj(
                        x, mods, [slot(l, 3, grp), slot(l, 4, grp)], norm_g[l, 1], w_main, cs, tn, tap_shapes, taps,
                        wf, bfo)
                logf = logf_t.reshape(h_b, b, t).transpose(1, 2, 0)
                ab_states[grp].append((ka.reshape(b, t, h_a, HEAD_DIM), va.reshape(b, t, h_a, HEAD_DIM),
                                       kb.reshape(b, t, h_b, HEAD_DIM), vb.reshape(b, t, h_b, HEAD_DIM), logf))
                logf_bh = logf_t.reshape(h_b, b, t).transpose(1, 0, 2)
                if grp == 0:
                    o_a = _diff_prompt(lam, t5_table.astype(F32).reshape(-1), qkv, vt, g_sub, h_a, 1.0 - lam_init)
                    fcum = _cumsum(logf_bh, jnp.zeros((b, h_b, LANES), F32))
                    nq = t // ROW_TILE
                    fref = fcum[:, :, ::ROW_TILE].reshape(-1)
                    o_b = _fox_prompt(fref, qkv, vt, fcum.reshape(b, h_b, nq, 1, ROW_TILE), h_b, 3 * h_a, h_a)
                else:
                    q_pos = past + np.arange(t)
                    bnear = _t5_bias_tiles(t5_table, q_pos, past - SAMPLE_NEAR + np.arange(SAMPLE_NEAR), far_bucket)
                    bnew = _t5_bias_tiles(t5_table, q_pos, q_pos, far_bucket)
                    kc = cache_a_k[i].reshape(b, past, wa)
                    vc = cache_a_v[i].reshape(b, past, wa)
                    o_a = _diff_sample(lam, qkv, kc, vc, bnear, bnew, g_sub, h_a, 1.0 - lam_init)
                    fc = _cumsum(cache_b_logf[i].astype(F32).transpose(0, 2, 1), jnp.zeros((b, h_b, LANES), F32))
                    carry = fc[:, :, past - 1:past]
                    fn = _cumsum(jnp.pad(logf_bh, ((0, 0), (0, 0), (0, LANES - t))),
                                 jnp.broadcast_to(carry, (b, h_b, LANES)))
                    o_b = _fox_sample(carry.reshape(-1), qkv, cache_b_k[i].reshape(b, past, wb),
                                      cache_b_v[i].reshape(b, past, wb), fc.reshape(b, h_b, 1, past),
                                      fn.reshape(b, h_b, 1, LANES), h_b, 3 * h_a)
                xs[grp] = _outproj([o_a, o_b], w_out, x, mods, slot(l, 5, grp))
        else:
            w_in = w_in_c[i].astype(BF16)
            w_out = w_out_c[i].astype(BF16)
            tn = min(PROJ_COL_TILE, cw)
            taps = _seg_taps(tn, cw, (1, 2), True)
            cs = np.ones((3 * cw,), np.float32)
            cs[:cw] = HEAD_DIM ** -0.5 * LOG2E
            for grp in range(2):
                x = xs[grp]
                b, t, _ = x.shape
                keep = min(C_BAND, t)
                assert keep == min(t, ROW_TILE)
                tap_shapes = [((b, keep, cw), False)] * 2
                if grp == 0:
                    qkv, k_new, v_new, vt = _proj(
                        x, mods, [slot(l, 3, grp), slot(l, 4, grp)], norm_g[l, 1], w_in, cs, tn, tap_shapes, taps,
                        vt_cols=_vt_cols(tn, cw, (2,)), vt_tile=BAND_TILE)
                else:
                    qkv, k_new, v_new = _proj(
                        x, mods, [slot(l, 3, grp), slot(l, 4, grp)], norm_g[l, 1], w_in, cs, tn, tap_shapes, taps)
                c_states[grp].append((k_new.reshape(b, keep, h_c, HEAD_DIM), v_new.reshape(b, keep, h_c, HEAD_DIM)))
                if grp == 0:
                    o_c = _band_prompt(c_rel_bias[i].astype(F32).reshape(-1), qkv, vt, h_c)
                else:
                    lc = cache_c_k.shape[2]
                    q_pos = past + np.arange(t)
                    bmc = _band_bias_tiles(c_rel_bias[i], q_pos, past - lc + np.arange(lc))
                    bmn = _band_bias_tiles(c_rel_bias[i], q_pos, q_pos)
                    o_c = _band_sample(qkv, cache_c_k[i], cache_c_v[i], bmc, bmn, h_c)
                xs[grp] = _outproj([o_c], w_out, x, mods, slot(l, 5, grp))

        for grp in range(2):
            xs[grp] = _ffn(xs[grp], mods, [slot(l, k, grp) for k in (6, 7, 8)], norm_g[l, 2], *w2,
                           final_g=final_g if last else None)

    outs = [xs[0], xs[1]]
    for grp in range(2):
        st = ab_states[grp]
        outs += [jnp.stack([s[k] for s in st]) for k in range(5)]
        st = c_states[grp]
        outs += [jnp.stack([s[k] for s in st]) for k in range(2)]
    return tuple(outs)
```

```python
import functools
import math

import numpy as np
import jax
import jax.numpy as jnp
from jax import lax
from jax.experimental import pallas as pl
from jax.experimental.pallas import tpu as pltpu

F32 = jnp.float32
BF16 = jnp.bfloat16

CHUNK = 64
HEAD_DIM = 128
DA_HALF = HEAD_DIM // 2
N_T5_BUCKETS = 32
T5_MAX_DIST = 128
C_PREV_CHUNKS = 8
C_BAND = C_PREV_CHUNKS * CHUNK
REL_CLIP = 128
EPS = 1e-6
NEG_INF = -1e30
LOG2E = math.log2(math.e)
VT_ROWS = HEAD_DIM + 16

VMEM_LIMIT_BYTES = 56 * 1024 * 1024
LANES = 128

ROW_TILE = 512
FF_TILE = 512
FFN_OUT_TILE = 512
FFN_ROW_SPLIT = 2
PROJ_ROW_SPLIT = 2
PROJ_COL_TILE = 1024
ADA_COL_TILE = 1024
BAND_TILE = 256
BAND_HEADS = 4
ATTN_GROUP = 512
SAMPLE_NEAR = 256

_NT = (((1,), (1,)), ((), ()))


def _params(*sem, flags=None):
    return pltpu.CompilerParams(dimension_semantics=sem, vmem_limit_bytes=VMEM_LIMIT_BYTES, flags=flags)


def _modulated_norm(x, g, shift, scale):
    y = x * lax.rsqrt(jnp.mean(x * x, axis=-1, keepdims=True) + EPS) * g
    return y * (1.0 + scale) + shift


def _normed_rows(x_ref, shift_ref, scale_ref, g_ref, h_scr, n_split, r):
    nb, tt, d = x_ref.shape
    rows = nb * tt // n_split
    if nb == 1:
        x, sh, sc = x_ref[:, r * rows:(r + 1) * rows, :], shift_ref[...], scale_ref[...]
    else:
        seqs = slice(r * (nb // n_split), (r + 1) * (nb // n_split))
        x, sh, sc = x_ref[seqs], shift_ref[seqs], scale_ref[seqs]
    hr = _modulated_norm(x, g_ref[...], sh, sc).reshape(rows, d).astype(BF16)
    h_scr[r * rows:(r + 1) * rows, :] = hr
    return hr


def _ada_kernel(c_ref, w_ref, b_ref, o_ref):
    c = c_ref[...]
    a = (c * jax.nn.sigmoid(c)).astype(BF16)
    o_ref[0] = jnp.dot(a, w_ref[0].astype(BF16), preferred_element_type=F32) + b_ref[0]


def _ada_mods(c_all, w_ada, b_ada):
    depth, d, n = w_ada.shape
    r = c_all.shape[0]
    tn = math.gcd(n, ADA_COL_TILE)
    return pl.pallas_call(
        _ada_kernel,
        grid=(depth, n // tn),
        in_specs=[
            pl.BlockSpec((r, d), lambda l, j: (0, 0)),
            pl.BlockSpec((1, d, tn), lambda l, j: (l, 0, j)),
            pl.BlockSpec((1, 1, tn), lambda l, j: (l, 0, j)),
        ],
        out_specs=pl.BlockSpec((1, r, tn), lambda l, j: (l, 0, j)),
        out_shape=jax.ShapeDtypeStruct((depth, r, n), F32),
        compiler_params=_params("arbitrary", "arbitrary"),
        name="ada_mods",
    )(c_all, w_ada, b_ada.reshape(depth, 1, n))


def _row_blocking(x):
    b, t, _ = x.shape
    if t >= ROW_TILE:
        assert t % ROW_TILE == 0
        return 1, ROW_TILE, t // ROW_TILE, b * (t // ROW_TILE)
    assert (b * t) % 8 == 0
    return b, t, 1, 1


def _x_index(nb, tps):
    if nb == 1:
        return lambda i, j: (i // tps, i % tps, 0)
    return lambda i, j: (0, 0, 0)


def _mod_spec(nb, tps, d, slot):
    if nb == 1:
        return pl.BlockSpec((None, 1, 1, d), lambda i, j: (slot, i // tps, 0, 0))
    return pl.BlockSpec((None, nb, 1, d), lambda i, j: (slot, 0, 0, 0))


def _ffn_kernel(*refs, nb, tt, n_ff, n_out, final):
    if final:
        (x_ref, shift_ref, scale_ref, gate_ref, g_ref, wg_ref, wu_ref, wo_ref, fg_ref,
         o_ref, h_scr, a_scr) = refs
    else:
        (x_ref, shift_ref, scale_ref, gate_ref, g_ref, wg_ref, wu_ref, wo_ref,
         o_ref, h_scr, a_scr) = refs
    j = pl.program_id(1)
    d = x_ref.shape[-1]
    tm = nb * tt
    tf = wg_ref.shape[1]
    tn = wo_ref.shape[1]

    rows = tm // FFN_ROW_SPLIT
    normed_rows = functools.partial(_normed_rows, x_ref, shift_ref, scale_ref, g_ref, h_scr, FFN_ROW_SPLIT)

    def hidden_chunk(rows_of):
        def products(r):
            hr = rows_of(r)
            return (jnp.dot(hr, wg_ref[...], preferred_element_type=F32),
                    jnp.dot(hr, wu_ref[...], preferred_element_type=F32))

        parts = []
        cur = products(0)
        for r in range(FFN_ROW_SPLIT):
            nxt = products(r + 1) if r + 1 < FFN_ROW_SPLIT else None
            gg, uu = cur
            parts.append((gg * jax.nn.sigmoid(gg) * uu).astype(BF16))
            cur = nxt
        a_scr[j] = jnp.concatenate(parts, axis=0)

    @pl.when(j == 0)
    def _():
        hidden_chunk(normed_rows)

    @pl.when(jnp.logical_and(j > 0, j < n_ff))
    def _():
        hidden_chunk(lambda r: h_scr[r * rows:(r + 1) * rows, :])

    @pl.when(j >= n_ff)
    def _():
        a = jnp.concatenate([a_scr[f] for f in range(n_ff)], axis=1)
        acc = jnp.dot(a, wo_ref[...], preferred_element_type=F32).reshape(nb, tt, tn)
        for n in range(n_out):
            @pl.when(j == n_ff + n)
            def _(n=n):
                cs = slice(n * tn, (n + 1) * tn)
                o_ref[:, :, cs] = x_ref[:, :, cs] + 0.5 * gate_ref[:, :, cs] * acc

    if final:
        @pl.when(j == n_ff + n_out - 1)
        def _():
            y = o_ref[...]
            o_ref[...] = y * lax.rsqrt(jnp.mean(y * y, axis=-1, keepdims=True) + EPS) * fg_ref[...]


def _ffn(x, mods, slots, g, wg_p, wu_p, w_out_p, final_g=None):
    b, t, d = x.shape
    nb, tt, tps, nrows = _row_blocking(x)
    ffp = w_out_p.shape[0]
    n_ff = ffp // FF_TILE
    tn = math.gcd(d, FFN_OUT_TILE)
    n_out = d // tn
    final = final_g is not None
    in_specs = [
        pl.BlockSpec((nb, tt, d), _x_index(nb, tps)),
        _mod_spec(nb, tps, d, slots[0]),
        _mod_spec(nb, tps, d, slots[1]),
        _mod_spec(nb, tps, d, slots[2]),
        pl.BlockSpec((1, d), lambda i, j: (0, 0)),
        pl.BlockSpec((d, FF_TILE), lambda i, j: (0, jnp.minimum(j, n_ff - 1))),
        pl.BlockSpec((d, FF_TILE), lambda i, j: (0, jnp.minimum(j, n_ff - 1))),
        pl.BlockSpec((ffp, tn), lambda i, j: (0, jnp.maximum(j - n_ff, 0))),
    ]
    args = [x, mods, mods, mods, g.reshape(1, d), wg_p, wu_p, w_out_p]
    if final:
        in_specs.append(pl.BlockSpec((1, d), lambda i, j: (0, 0)))
        args.append(final_g.reshape(1, d))
    return pl.pallas_call(
        functools.partial(_ffn_kernel, nb=nb, tt=tt, n_ff=n_ff, n_out=n_out, final=final),
        grid=(nrows, n_ff + n_out),
        in_specs=in_specs,
        out_specs=pl.BlockSpec((nb, tt, d), _x_index(nb, tps)),
        out_shape=jax.ShapeDtypeStruct((b, t, d), F32),
        scratch_shapes=[pltpu.VMEM((nb * tt, d), BF16), pltpu.VMEM((n_ff, nb * tt, FF_TILE), BF16)],
        compiler_params=_params("parallel", "arbitrary"),
        name="ffn",
    )(*args)


def _prep_ffn_weights(w_in, w_out):
    ff = w_out.shape[0]
    ffp = -(-ff // FF_TILE) * FF_TILE
    pad = ffp - ff
    wg = jnp.pad(w_in[:, :ff].astype(BF16), ((0, 0), (0, pad)))
    wu = jnp.pad(w_in[:, ff:].astype(BF16), ((0, 0), (0, pad)))
    return wg, wu, jnp.pad(w_out.astype(BF16), ((0, pad), (0, 0)))


def _log_sigmoid(x):
    return jnp.minimum(x, 0.0) - jnp.log1p(jnp.exp(-jnp.abs(x)))


def _proj_kernel(*refs, nb, tt, tn, tps, taps, has_f, vt_cols, vt_tile):
    x_ref, shift_ref, scale_ref, g_ref, w_ref, cs_ref = refs[:6]
    pos = 6
    if has_f:
        wf_ref, bf_ref = refs[6:8]
        pos = 8
    obf_ref = refs[pos]
    tap_refs = refs[pos + 1:pos + 1 + len({tp[0] for tp in taps})]
    pos = pos + 1 + len(tap_refs)
    if vt_cols:
        vt_ref = refs[pos]
        pos += 1
    if has_f:
        logf_ref = refs[pos]
        pos += 1
    h_scr = refs[pos]
    i = pl.program_id(0)
    j = pl.program_id(1)
    d = x_ref.shape[-1]

    def store_attention_copy(y):
        obf_ref[...] = (y * cs_ref[...]).astype(BF16).reshape(nb, tt, tn)

    assert all(tp[1] > 0 for tp in taps) and all(vc[0] > 0 for vc in vt_cols)

    @pl.when(j == 0)
    def _():
        hs = []
        ys = []
        for r in range(PROJ_ROW_SPLIT):
            hs.append(_normed_rows(x_ref, shift_ref, scale_ref, g_ref, h_scr, PROJ_ROW_SPLIT, r))
            ys.append(jnp.dot(hs[-1], w_ref[...], preferred_element_type=F32))
        store_attention_copy(jnp.concatenate(ys, axis=0))
        if has_f:
            fr = lax.dot_general(wf_ref[...], jnp.concatenate(hs, axis=0), _NT, preferred_element_type=F32)
            logf_ref[...] = _log_sigmoid(fr[:logf_ref.shape[0]] + bf_ref[...])

    @pl.when(j > 0)
    def _():
        y = jnp.dot(h_scr[...], w_ref[...], preferred_element_type=F32)
        store_attention_copy(y)
        _proj_taps(y, i, j, tap_refs, vt_ref if vt_cols else None, nb=nb, tt=tt, tn=tn, tps=tps, taps=taps,
                   vt_cols=vt_cols, vt_tile=vt_tile)


def _proj_taps(y, i, j, tap_refs, vt_ref, *, nb, tt, tn, tps, taps, vt_cols, vt_tile):
    for out_idx, jval, col0, last_rows_only in taps:
        cond = j == jval
        if last_rows_only and tps > 1:
            cond = jnp.logical_and(cond, i % tps == tps - 1)

        @pl.when(cond)
        def _(out_idx=out_idx, col0=col0):
            tap_refs[out_idx][:, :, col0:col0 + tn] = y.reshape(nb, tt, tn)

    for jval, head0 in vt_cols:
        @pl.when(j == jval)
        def _(head0=head0):
            for hh in range(tn // HEAD_DIM):
                yt = y[:, hh * HEAD_DIM:(hh + 1) * HEAD_DIM].T.astype(BF16)
                yt = jnp.concatenate([yt, jnp.ones((VT_ROWS - HEAD_DIM, yt.shape[1]), BF16)], axis=0)
                for s in range(tt // vt_tile):
                    vt_ref[0, head0 + hh, s] = yt[:, s * vt_tile:(s + 1) * vt_tile]


def _seg_taps(tn, width, segments, last_rows_only):
    per = width // tn
    assert per * tn == width
    return [(o, sidx * per + s, s * tn, last_rows_only) for o, sidx in enumerate(segments) for s in range(per)]


def _vt_cols(tn, width, segments):
    per = width // tn
    hpb = tn // HEAD_DIM
    return [(sidx * per + s, (o * per + s) * hpb) for o, sidx in enumerate(segments) for s in range(per)]


def _proj(x, mods, slots, g, w_bf, col_scale, tn, tap_shapes, taps, wf=None, bf=None, vt_cols=(), vt_tile=None):
    b, t, d = x.shape
    n = w_bf.shape[1]
    nb, tt, tps, nrows = _row_blocking(x)
    has_f = wf is not None
    assert not vt_cols or nb == 1
    in_specs = [
        pl.BlockSpec((nb, tt, d), _x_index(nb, tps)),
        _mod_spec(nb, tps, d, slots[0]),
        _mod_spec(nb, tps, d, slots[1]),
        pl.BlockSpec((1, d), lambda i, j: (0, 0)),
        pl.BlockSpec((d, tn), lambda i, j: (0, j)),
        pl.BlockSpec((1, tn), lambda i, j: (0, j)),
    ]
    args = [x, mods, mods, g.reshape(1, d), w_bf, jnp.asarray(col_scale, F32).reshape(1, n)]
    if has_f:
        in_specs += [pl.BlockSpec(wf.shape, lambda i, j: (0, 0)), pl.BlockSpec(bf.shape, lambda i, j: (0, 0))]
        args += [wf, bf]
    xi = _x_index(nb, tps)
    out_specs = [pl.BlockSpec((nb, tt, tn), lambda i, j: xi(i, j)[:2] + (j,))]
    out_shape = [jax.ShapeDtypeStruct((b, t, n), BF16)]
    for shp, follows_rows in tap_shapes:
        if follows_rows:
            out_specs.append(pl.BlockSpec((nb, tt, shp[2]), xi))
        elif nb == 1:
            out_specs.append(pl.BlockSpec((1, shp[1], shp[2]), lambda i, j: (i // tps, 0, 0)))
        else:
            out_specs.append(pl.BlockSpec(shp, lambda i, j: (0, 0, 0)))
        out_shape.append(jax.ShapeDtypeStruct(shp, F32))
    if vt_cols:
        n_vh = len(vt_cols) * (tn // HEAD_DIM)
        out_specs.append(pl.BlockSpec((1, n_vh, tt // vt_tile, VT_ROWS, vt_tile), lambda i, j: (i // tps, 0, i % tps, 0, 0)))
        out_shape.append(jax.ShapeDtypeStruct((b, n_vh, t // vt_tile, VT_ROWS, vt_tile), BF16))
    if has_f:
        nf = bf.shape[0]
        out_specs.append(pl.BlockSpec((nf, nb * tt), lambda i, j: (0, i)))
        out_shape.append(jax.ShapeDtypeStruct((nf, b * t), F32))
    return pl.pallas_call(
        functools.partial(_proj_kernel, nb=nb, tt=tt, tn=tn, tps=tps, taps=tuple(taps), has_f=has_f,
                          vt_cols=tuple(vt_cols), vt_tile=vt_tile),
        grid=(nrows, n // tn),
        in_specs=in_specs,
        out_specs=out_specs,
        out_shape=out_shape,
        scratch_shapes=[pltpu.VMEM((nb * tt, d), BF16)],
        compiler_params=_params("arbitrary", "arbitrary"),
        name="mixer_in_proj",
    )(*args)


def _outproj_kernel(*refs, nb, tt, n_in):
    o_refs = refs[:n_in]
    w_ref, x_ref, gate_ref, out_ref = refs[n_in:]
    d = x_ref.shape[-1]
    acc = None
    row0 = 0
    for o_ref in o_refs:
        wd = o_ref.shape[-1]
        part = jnp.dot(o_ref[...].reshape(nb * tt, wd), w_ref[row0:row0 + wd, :], preferred_element_type=F32)
        acc = part if acc is None else acc + part
        row0 += wd
    out_ref[...] = x_ref[...] + gate_ref[...] * acc.reshape(nb, tt, d)


def _outproj(o_list, w_bf, x, mods, gate_slot):
    b, t, d = x.shape
    nb, tt, tps, nrows = _row_blocking(x)
    xi = _x_index(nb, tps)
    in_specs = [pl.BlockSpec((nb, tt, o.shape[-1]), xi) for o in o_list]
    in_specs += [
        pl.BlockSpec(w_bf.shape, lambda i, j: (0, 0)),
        pl.BlockSpec((nb, tt, d), xi),
        _mod_spec(nb, tps, d, gate_slot),
    ]
    return pl.pallas_call(
        functools.partial(_outproj_kernel, nb=nb, tt=tt, n_in=len(o_list)),
        grid=(nrows, 1),
        in_specs=in_specs,
        out_specs=pl.BlockSpec((nb, tt, d), xi),
        out_shape=jax.ShapeDtypeStruct((b, t, d), F32),
        compiler_params=_params("parallel", "arbitrary"),
        name="mixer_out_proj",
    )(*o_list, w_bf, x, mods)


def _cumsum_kernel(x_ref, c_ref, o_ref):
    x = x_ref[0]
    n = x.shape[1]
    lane = lax.broadcasted_iota(jnp.int32, x.shape, 1)
    s = 1
    while s < n:
        x = x + jnp.where(lane >= s, pltpu.roll(x, s, axis=1), 0.0)
        s *= 2
    o_ref[0] = x + c_ref[0][:, :1]


def _cumsum(x, carry):
    g, h, n = x.shape
    return pl.pallas_call(
        _cumsum_kernel,
        grid=(g,),
        in_specs=[pl.BlockSpec((1, h, n), lambda i: (i, 0, 0)), pl.BlockSpec((1, h, LANES), lambda i: (i, 0, 0))],
        out_specs=pl.BlockSpec((1, h, n), lambda i: (i, 0, 0)),
        out_shape=jax.ShapeDtypeStruct((g, h, n), F32),
        compiler_params=_params("arbitrary"),
        name="logf_cumsum",
    )(x, carry)


def _softmax_init(m_scr, acc_scr):
    m_scr[...] = jnp.full(m_scr.shape, NEG_INF, F32)
    acc_scr[...] = jnp.zeros(acc_scr.shape, F32)


def _softmax_update_t(s, s_max, vt, m_prev, acc_prev):
    m_new = jnp.maximum(m_prev, s_max)
    alpha = jnp.exp2(m_prev - m_new)
    p = jnp.exp2(s - m_new)
    acc_new = alpha * acc_prev + jnp.dot(vt, p.astype(BF16), preferred_element_type=F32)
    return m_new, acc_new


def _normalised_t(acc):
    return (acc[:HEAD_DIM] / acc[HEAD_DIM:HEAD_DIM + 1]).T


def _with_max(s):
    return s, jnp.max(s, axis=0, keepdims=True)


def _consume_tile(scores, vt, m_scr, acc_scr, next_scores=None):
    n = len(scores) // 2
    g = m_scr.shape[1] // n
    sls = [slice(i * g, (i + 1) * g) for i in range(n)]
    state = [(m_scr[:, sl], acc_scr[:, sl]) for sl in sls]
    new, nxt = [], []
    for i in range(n):
        if next_scores is not None:
            nxt.extend(next_scores(i))
        new.append(_softmax_update_t(scores[2 * i], scores[2 * i + 1], vt, *state[i]))
    for sl, (m_new, acc_new) in zip(sls, new):
        m_scr[:, sl] = m_new
        acc_scr[:, sl] = acc_new
    return tuple(nxt)


def _toeplitz_values(tbl_ref, idx, n_vals, nh, h, ref_row):
    ref_val = tbl_ref[ref_row * nh + h]

    def body(v, acc):
        return jnp.where(idx == v, tbl_ref[v * nh + h] - ref_val, acc)

    return lax.fori_loop(0, n_vals, body, jnp.zeros(idx.shape, F32))


def _stack_diff_queries(q):
    lane = lax.broadcasted_iota(jnp.int32, q.shape, 1)
    zero = jnp.zeros_like(q)
    qs = jnp.concatenate([jnp.where(lane < DA_HALF, q, zero), jnp.where(lane >= DA_HALF, q, zero)], axis=0)
    return qs


def _diff_finalize(o, lam, g, out_scale):
    tq = o.shape[0] // 2
    od = o[:tq] - lam * o[tq:]
    return od * lax.rsqrt(jnp.mean(od * od, axis=-1, keepdims=True) + EPS) * g * out_scale


def _diff_p_kernel(lam_ref, tbl_ref, q_ref, k_ref, vt_ref, idx_ref, base_ref, g_ref, o_ref,
                   m_scr, acc_scr, bias_scr, *, tq, nh, far_bucket, out_scale):
    h = pl.program_id(0)
    bi = pl.program_id(1)
    qi = pl.program_id(2)
    nblk = tq // LANES

    @pl.when(jnp.logical_and(bi == 0, qi == 0))
    def _():
        t = _toeplitz_values(tbl_ref, idx_ref[...], N_T5_BUCKETS, nh, h, far_bucket) * LOG2E
        bias_scr[...] = base_ref[...]
        for blk in range(nblk):
            sl = slice(blk * LANES, (blk + 1) * LANES)
            bias_scr[0, sl, sl] += t[0]
            if blk >= 1:
                bias_scr[0, (blk - 1) * LANES:blk * LANES, sl] += t[1]
        bias_scr[1, (nblk - 1) * LANES:, :LANES] += t[1]

    qs = _stack_diff_queries(q_ref[0])
    _softmax_init(m_scr, acc_scr)
    ngrp = 2 * tq // ATTN_GROUP

    def scores(j, slot):
        k = k_ref[0, pl.ds(pl.multiple_of(j * tq, tq), tq), :]

        def group(i):
            s = lax.dot_general(k, qs[i * ATTN_GROUP:(i + 1) * ATTN_GROUP], _NT, preferred_element_type=F32)
            if slot is not None:
                q0 = (i * ATTN_GROUP) % tq
                s = s + bias_scr[slot, :, q0:q0 + ATTN_GROUP]
            return _with_max(s)
        return group

    n_far = jnp.maximum(qi - 1, 0)
    before_slot = jnp.where(qi == 0, 2, 1)

    def last_two(cur):
        cur = _consume_tile(cur, vt_ref[0, 0, n_far], m_scr, acc_scr, scores(qi, 0))
        _consume_tile(cur, vt_ref[0, 0, qi], m_scr, acc_scr)

    @pl.when(n_far >= 1)
    def _():
        first = scores(0, None)
        cur = tuple(x for i in range(ngrp) for x in first(i))

        def far(j, cur):
            return _consume_tile(cur, vt_ref[0, 0, j], m_scr, acc_scr, scores(j + 1, None))

        cur = lax.fori_loop(0, n_far - 1, far, cur)
        last_two(_consume_tile(cur, vt_ref[0, 0, n_far - 1], m_scr, acc_scr, scores(n_far, 1)))

    @pl.when(n_far == 0)
    def _():
        before = scores(n_far, before_slot)
        last_two(tuple(x for i in range(ngrp) for x in before(i)))

    o = _normalised_t(acc_scr[...])
    o_ref[0] = _diff_finalize(o, lam_ref[0], g_ref[...], out_scale).astype(BF16)


def _diff_prompt(lam, tbl, qkv, vt, g, n_heads, out_scale):
    b, t, _ = qkv.shape
    tq = vt.shape[-1]
    far_bucket = N_T5_BUCKETS // 2 - 1
    pos = np.arange(LANES)
    idx = np.stack([_t5_bucket_np(pos[:, None] - pos[None, :]),
                    _t5_bucket_np(pos[:, None] - pos[None, :] - LANES)]).astype(np.int32)
    kq = np.arange(tq)
    base = np.zeros((3, tq, tq), np.float32)
    base[0] = np.where((kq[:, None] // CHUNK) <= (kq[None, :] // CHUNK), 0.0, NEG_INF)
    base[2] = NEG_INF
    return pl.pallas_call(
        functools.partial(_diff_p_kernel, tq=tq, nh=n_heads, far_bucket=far_bucket, out_scale=out_scale),
        grid=(n_heads, b, t // tq),
        in_specs=[
            pl.BlockSpec(memory_space=pltpu.SMEM),
            pl.BlockSpec(memory_space=pltpu.SMEM),
            pl.BlockSpec((1, tq, HEAD_DIM), lambda h, bi, qi: (bi, qi, h)),
            pl.BlockSpec((1, t, HEAD_DIM), lambda h, bi, qi: (bi, 0, n_heads + h)),
            pl.BlockSpec((1, 1, t // tq, VT_ROWS, tq), lambda h, bi, qi: (bi, h, 0, 0, 0)),
            pl.BlockSpec(idx.shape, lambda h, bi, qi: (0, 0, 0)),
            pl.BlockSpec(base.shape, lambda h, bi, qi: (0, 0, 0)),
            pl.BlockSpec((1, HEAD_DIM), lambda h, bi, qi: (0, 0)),
        ],
        out_specs=pl.BlockSpec((1, tq, HEAD_DIM), lambda h, bi, qi: (bi, qi, h)),
        out_shape=jax.ShapeDtypeStruct((b, t, n_heads * HEAD_DIM), BF16),
        scratch_shapes=[pltpu.VMEM((1, 2 * tq), F32), pltpu.VMEM((VT_ROWS, 2 * tq), F32),
                        pltpu.VMEM((3, tq, tq), F32)],
        compiler_params=_params("arbitrary", "arbitrary", "arbitrary"),
        name="diff_attn_prompt",
    )(lam, tbl, qkv, qkv, vt, jnp.asarray(idx), jnp.asarray(base), g)


def _fox_p_kernel(fref_ref, q_ref, k_ref, vt_ref, f_ref, o_ref, m_scr, acc_scr, fcol_scr, *, tq):
    bi = pl.program_id(0)
    h = pl.program_id(1)
    qi = pl.program_id(2)
    nblk = tq // LANES

    @pl.when(qi == 0)
    def _():
        for c in range(f_ref.shape[2] * nblk):
            row = f_ref[0, 0, c // nblk][:, (c % nblk) * LANES:(c % nblk + 1) * LANES]
            fcol_scr[c * LANES:(c + 1) * LANES, :] = jnp.broadcast_to(row, (LANES, LANES)).T

    q = q_ref[0]
    fref = fref_ref[(bi * pl.num_programs(1) + h) * pl.num_programs(2) + qi]
    _softmax_init(m_scr, acc_scr)

    ngrp = tq // ATTN_GROUP

    def scores(j, causal):
        start = pl.multiple_of(j * tq, tq)
        k = k_ref[0, pl.ds(start, tq), :]
        decay = (fref - fcol_scr[pl.ds(start, tq), :]) * LOG2E
        decay = jnp.concatenate([decay] * (ATTN_GROUP // LANES), axis=1)

        def group(i):
            s = lax.dot_general(k, q[i * ATTN_GROUP:(i + 1) * ATTN_GROUP], _NT, preferred_element_type=F32)
            s = s + decay
            if causal:
                key = lax.broadcasted_iota(jnp.int32, s.shape, 0)
                qry = lax.broadcasted_iota(jnp.int32, s.shape, 1) + i * ATTN_GROUP
                s = jnp.where(key <= qry, s, NEG_INF)
            return _with_max(s)
        return group

    @pl.when(qi >= 1)
    def _():
        first = scores(0, False)
        cur = tuple(x for i in range(ngrp) for x in first(i))

        def far(j, cur):
            return _consume_tile(cur, vt_ref[0, 0, j], m_scr, acc_scr, scores(j + 1, False))

        cur = lax.fori_loop(0, qi - 1, far, cur)
        cur = _consume_tile(cur, vt_ref[0, 0, qi - 1], m_scr, acc_scr, scores(qi, True))
        _consume_tile(cur, vt_ref[0, 0, qi], m_scr, acc_scr)

    @pl.when(qi == 0)
    def _():
        diag = scores(qi, True)
        _consume_tile(tuple(x for i in range(ngrp) for x in diag(i)), vt_ref[0, 0, qi], m_scr, acc_scr)

    o_ref[0] = _normalised_t(acc_scr[...]).astype(BF16)


def _fox_prompt(fref, qkv, vt, fcum, n_heads, col0, vt_head0):
    b, t, _ = qkv.shape
    tq = vt.shape[-1]
    nq = t // tq
    return pl.pallas_call(
        functools.partial(_fox_p_kernel, tq=tq),
        grid=(b, n_heads, nq),
        in_specs=[
            pl.BlockSpec(memory_space=pltpu.SMEM),
            pl.BlockSpec((1, tq, HEAD_DIM), lambda bi, h, qi: (bi, qi, col0 + h)),
            pl.BlockSpec((1, t, HEAD_DIM), lambda bi, h, qi: (bi, 0, col0 + n_heads + h)),
            pl.BlockSpec((1, 1, nq, VT_ROWS, tq), lambda bi, h, qi: (bi, vt_head0 + h, 0, 0, 0)),
            pl.BlockSpec((1, 1, nq, 1, tq), lambda bi, h, qi: (bi, h, 0, 0, 0)),
        ],
        out_specs=pl.BlockSpec((1, tq, HEAD_DIM), lambda bi, h, qi: (bi, qi, h)),
        out_shape=jax.ShapeDtypeStruct((b, t, n_heads * HEAD_DIM), BF16),
        scratch_shapes=[pltpu.VMEM((1, tq), F32), pltpu.VMEM((VT_ROWS, tq), F32),
                        pltpu.VMEM((t, LANES), F32)],
        compiler_params=_params("arbitrary", "arbitrary", "arbitrary"),
        name="fox_attn_prompt",
    )(fref, qkv, qkv, vt, fcum)


def _joint_softmax_pv(s_c, s_n, vc, vn):
    m = jnp.maximum(jnp.max(s_c, axis=1, keepdims=True), jnp.max(s_n, axis=1, keepdims=True))
    p_c = jnp.exp2(s_c - m)
    p_n = jnp.exp2(s_n - m)
    l = jnp.sum(p_c, axis=1, keepdims=True) + jnp.sum(p_n, axis=1, keepdims=True)
    acc = jnp.dot(p_c.astype(BF16), vc, preferred_element_type=F32)
    acc = acc + jnp.dot(p_n.astype(BF16), vn, preferred_element_type=F32)
    return acc / l


def _diff_s_kernel(lam_ref, q_ref, kn_ref, vn_ref, kc_ref, vc_ref, bnear_ref, bnew_ref, g_ref, o_ref, *, near, out_scale):
    qs = _stack_diff_queries(q_ref[0])
    kc = kc_ref[0].astype(BF16)
    vc = vc_ref[0].astype(BF16)
    p = kc.shape[0]
    s_c = lax.dot_general(qs, kc, _NT, preferred_element_type=F32)
    bnear = bnear_ref[0]
    s_c = jnp.concatenate([s_c[:, :p - near], s_c[:, p - near:] + jnp.concatenate([bnear, bnear], axis=0)], axis=1)
    bnew = bnew_ref[0]
    s_n = lax.dot_general(qs, kn_ref[0], _NT, preferred_element_type=F32) + jnp.concatenate([bnew, bnew], axis=0)
    o = _joint_softmax_pv(s_c, s_n, vc, vn_ref[0])
    o_ref[0] = _diff_finalize(o, lam_ref[0], g_ref[...], out_scale).astype(BF16)


def _diff_sample(lam, qkv, kc, vc, bnear, bnew, g, n_heads, out_scale):
    b, t, _ = qkv.shape
    p = kc.shape[1]
    near = bnear.shape[-1]
    hd = HEAD_DIM
    return pl.pallas_call(
        functools.partial(_diff_s_kernel, near=near, out_scale=out_scale),
        grid=(b, n_heads),
        in_specs=[
            pl.BlockSpec(memory_space=pltpu.SMEM),
            pl.BlockSpec((1, t, hd), lambda bi, h: (bi, 0, h)),
            pl.BlockSpec((1, t, hd), lambda bi, h: (bi, 0, n_heads + h)),
            pl.BlockSpec((1, t, hd), lambda bi, h: (bi, 0, 2 * n_heads + h)),
            pl.BlockSpec((1, p, hd), lambda bi, h: (bi, 0, h)),
            pl.BlockSpec((1, p, hd), lambda bi, h: (bi, 0, h)),
            pl.BlockSpec((1, t, near), lambda bi, h: (h, 0, 0)),
            pl.BlockSpec((1, t, t), lambda bi, h: (h, 0, 0)),
            pl.BlockSpec((1, hd), lambda bi, h: (0, 0)),
        ],
        out_specs=pl.BlockSpec((1, t, hd), lambda bi, h: (bi, 0, h)),
        out_shape=jax.ShapeDtypeStruct((b, t, n_heads * hd), BF16),
        compiler_params=_params("parallel", "parallel"),
        name="diff_attn_sample",
    )(lam, qkv, qkv, qkv, kc, vc, bnear, bnew, g)


def _fox_s_kernel(fref_ref, q_ref, kn_ref, vn_ref, kc_ref, vc_ref, fc_ref, fn_ref, o_ref):
    bi = pl.program_id(0)
    h = pl.program_id(1)
    q = q_ref[0]
    t = q.shape[0]
    fref = fref_ref[bi * pl.num_programs(1) + h]
    s_c = lax.dot_general(q, kc_ref[0].astype(BF16), _NT, preferred_element_type=F32) + (fref - fc_ref[0, 0]) * LOG2E
    s_n = lax.dot_general(q, kn_ref[0], _NT, preferred_element_type=F32) + (fref - fn_ref[0, 0][:, :t]) * LOG2E
    row = lax.broadcasted_iota(jnp.int32, s_n.shape, 0)
    col = lax.broadcasted_iota(jnp.int32, s_n.shape, 1)
    s_n = jnp.where(col <= row, s_n, NEG_INF)
    o_ref[0] = _joint_softmax_pv(s_c, s_n, vc_ref[0].astype(BF16), vn_ref[0]).astype(BF16)


def _fox_sample(fref, qkv, kc, vc, fc, fn, n_heads, col0):
    b, t, _ = qkv.shape
    p = kc.shape[1]
    hd = HEAD_DIM
    return pl.pallas_call(
        _fox_s_kernel,
        grid=(b, n_heads),
        in_specs=[
            pl.BlockSpec(memory_space=pltpu.SMEM),
            pl.BlockSpec((1, t, hd), lambda bi, h: (bi, 0, col0 + h)),
            pl.BlockSpec((1, t, hd), lambda bi, h: (bi, 0, col0 + n_heads + h)),
            pl.BlockSpec((1, t, hd), lambda bi, h: (bi, 0, col0 + 2 * n_heads + h)),
            pl.BlockSpec((1, p, hd), lambda bi, h: (bi, 0, h)),
            pl.BlockSpec((1, p, hd), lambda bi, h: (bi, 0, h)),
            pl.BlockSpec((1, 1, 1, p), lambda bi, h: (bi, h, 0, 0)),
            pl.BlockSpec((1, 1, 1, fn.shape[-1]), lambda bi, h: (bi, h, 0, 0)),
        ],
        out_specs=pl.BlockSpec((1, t, hd), lambda bi, h: (bi, 0, h)),
        out_shape=jax.ShapeDtypeStruct((b, t, n_heads * hd), BF16),
        compiler_params=_params("parallel", "parallel"),
        name="fox_attn_sample",
    )(fref, qkv, qkv, qkv, kc, vc, fc, fn)


def _band_p_kernel(tbl_ref, q_ref, k_ref, vt_ref, idx_ref, base_ref, o_ref, bm_scr, *, tq, nkb, nh, hps):
    hg = pl.program_id(0)
    bi = pl.program_id(1)
    i = pl.program_id(2)
    nblk = tq // LANES
    qoff = (nkb - 1) * nblk

    @pl.when(jnp.logical_and(bi == 0, i == 0))
    def _():
        for e in range(hps):
            t = _toeplitz_values(tbl_ref, idx_ref[...], 2 * REL_CLIP + 1, nh, hg * hps + e, 0)
            bm_scr[e] = base_ref[...]
            for bq in range(nblk):
                kd = qoff + bq
                bm_scr[e, kd * LANES:(kd + 1) * LANES, bq * LANES:(bq + 1) * LANES] += t[0] * LOG2E
                bm_scr[e, (kd - 1) * LANES:kd * LANES, bq * LANES:(bq + 1) * LANES] += t[1] * LOG2E

    def head_scores(e):
        hs = slice(e * HEAD_DIM, (e + 1) * HEAD_DIM)
        q = q_ref[0, :, hs]
        ss = []
        for blk in range(nkb):
            jb = i - (nkb - 1) + blk
            jc = jnp.maximum(jb, 0)
            k = k_ref[0, pl.ds(pl.multiple_of(jc * tq, tq), tq), hs]
            s = lax.dot_general(k, q, _NT, preferred_element_type=F32) + bm_scr[e, blk * tq:(blk + 1) * tq, :]
            if blk < nkb - 1:
                s = s + jnp.where(jb < 0, NEG_INF, 0.0)
            ss.append(s)
        return ss, functools.reduce(jnp.maximum, [jnp.max(s, axis=0, keepdims=True) for s in ss])

    outs = []
    nxt = head_scores(0)
    for e in range(hps):
        ss, m = nxt
        if e + 1 < hps:
            nxt = head_scores(e + 1)
        acc = None
        for blk, s in enumerate(ss):
            jc = jnp.maximum(i - (nkb - 1) + blk, 0)
            a = jnp.dot(vt_ref[0, e, jc], jnp.exp2(s - m).astype(BF16), preferred_element_type=F32)
            acc = a if acc is None else acc + a
        outs.append(_normalised_t(acc).astype(BF16))
    o_ref[0] = jnp.concatenate(outs, axis=1)


def _band_prompt(tbl, qkv, vt, n_heads):
    b, t, _ = qkv.shape
    tq = vt.shape[-1]
    nkb = C_BAND // tq + 1
    assert (nkb - 1) * tq == C_BAND and tq % LANES == 0 and LANES >= REL_CLIP
    pos = np.arange(LANES)
    rel = pos[:, None] - pos[None, :]
    idx = (np.stack([np.clip(rel, -REL_CLIP, REL_CLIP), np.clip(rel - LANES, -REL_CLIP, REL_CLIP)]) + REL_CLIP).astype(np.int32)
    k_pos = np.arange(nkb * tq)[:, None]
    q_pos = C_BAND + np.arange(tq)[None, :]
    kc, qc = k_pos // CHUNK, q_pos // CHUNK
    base = np.where((kc <= qc) & (qc - kc <= C_PREV_CHUNKS), 0.0, NEG_INF).astype(np.float32)
    hps = BAND_HEADS
    ngrp = n_heads // hps
    assert ngrp * hps == n_heads
    wd = hps * HEAD_DIM
    return pl.pallas_call(
        functools.partial(_band_p_kernel, tq=tq, nkb=nkb, nh=n_heads, hps=hps),
        grid=(ngrp, b, t // tq),
        in_specs=[
            pl.BlockSpec(memory_space=pltpu.SMEM),
            pl.BlockSpec((1, tq, wd), lambda h, bi, i: (bi, i, h)),
            pl.BlockSpec((1, t, wd), lambda h, bi, i: (bi, 0, ngrp + h)),
            pl.BlockSpec((1, hps, t // tq, VT_ROWS, tq), lambda h, bi, i: (bi, h, 0, 0, 0)),
            pl.BlockSpec(idx.shape, lambda h, bi, i: (0, 0, 0)),
            pl.BlockSpec(base.shape, lambda h, bi, i: (0, 0)),
        ],
        out_specs=pl.BlockSpec((1, tq, wd), lambda h, bi, i: (bi, i, h)),
        out_shape=jax.ShapeDtypeStruct((b, t, n_heads * HEAD_DIM), BF16),
        scratch_shapes=[pltpu.VMEM((hps, nkb * tq, tq), F32)],
        compiler_params=_params("arbitrary", "arbitrary", "arbitrary"),
        name="band_attn_prompt",
    )(tbl, qkv, qkv, vt, jnp.asarray(idx), jnp.asarray(base))


def _band_s_kernel(q_ref, kn_ref, vn_ref, kc_ref, vc_ref, bmc_ref, bmn_ref, o_ref, *, n_heads):
    outs = []
    for h in range(n_heads):
        hs = slice(h * HEAD_DIM, (h + 1) * HEAD_DIM)
        q = q_ref[0, :, hs]
        kc = kc_ref[0, :, h, :].astype(BF16)
        vc = vc_ref[0, :, h, :].astype(BF16)
        s_c = lax.dot_general(q, kc, _NT, preferred_element_type=F32) + bmc_ref[h]
        s_n = lax.dot_general(q, kn_ref[0, :, hs], _NT, preferred_element_type=F32) + bmn_ref[h]
        outs.append(_joint_softmax_pv(s_c, s_n, vc, vn_ref[0, :, hs]).astype(BF16))
    o_ref[0] = jnp.concatenate(outs, axis=1)


def _band_sample(qkv, kc, vc, bmc, bmn, n_heads):
    b, t, _ = qkv.shape
    lc = kc.shape[1]
    cw = n_heads * HEAD_DIM
    return pl.pallas_call(
        functools.partial(_band_s_kernel, n_heads=n_heads),
        grid=(b,),
        in_specs=[
            pl.BlockSpec((1, t, cw), lambda bi: (bi, 0, 0)),
            pl.BlockSpec((1, t, cw), lambda bi: (bi, 0, 1)),
            pl.BlockSpec((1, t, cw), lambda bi: (bi, 0, 2)),
            pl.BlockSpec((1, lc, n_heads, HEAD_DIM), lambda bi: (bi, 0, 0, 0)),
            pl.BlockSpec((1, lc, n_heads, HEAD_DIM), lambda bi: (bi, 0, 0, 0)),
            pl.BlockSpec(bmc.shape, lambda bi: (0, 0, 0)),
            pl.BlockSpec(bmn.shape, lambda bi: (0, 0, 0)),
        ],
        out_specs=pl.BlockSpec((1, t, cw), lambda bi: (bi, 0, 0)),
        out_shape=jax.ShapeDtypeStruct((b, t, cw), BF16),
        compiler_params=_params("parallel"),
        name="band_attn_sample",
    )(qkv, qkv, qkv, kc, vc, bmc, bmn)


def _t5_bucket_np(rel):
    nb = N_T5_BUCKETS // 2
    max_exact = nb // 2
    n = np.abs(rel)
    nf = np.maximum(n, 1).astype(np.float64)
    large = max_exact + (np.log(nf / max_exact) / math.log(T5_MAX_DIST / max_exact) * (nb - max_exact)).astype(np.int64)
    large = np.minimum(large, nb - 1)
    return np.where(rel > 0, nb, 0) + np.where(n < max_exact, n, large)


def _toeplitz(table, index_of_rel, q_pos, k_pos):
    r, c = len(q_pos), len(k_pos)
    assert (np.diff(q_pos) == 1).all() and (np.diff(k_pos) == 1).all()
    rels = (k_pos[0] - q_pos[0]) - (r - 1) + np.arange(r + c - 1)
    v = jnp.pad(table.astype(F32)[index_of_rel(rels)].T, ((0, 0), (0, 1)))
    x = jnp.tile(v, (1, r))[:, :r * (r + c - 1)].reshape(v.shape[0], r, r + c - 1)
    return x[:, :, r - 1:r - 1 + c]


def _t5_bias_tiles(t5_table, q_pos, k_pos, far_bucket):
    mask = (k_pos[None, :] // CHUNK) <= (q_pos[:, None] // CHUNK)
    bias = _toeplitz(t5_table, _t5_bucket_np, q_pos, k_pos)
    bias = (bias - t5_table.astype(F32)[far_bucket][:, None, None]) * LOG2E
    return jnp.where(mask[None], bias, NEG_INF)


def _band_bias_tiles(rel_table, q_pos, k_pos):
    qc = q_pos[:, None] // CHUNK
    kc = k_pos[None, :] // CHUNK
    mask = (kc <= qc) & (qc - kc <= C_PREV_CHUNKS) & (k_pos[None, :] >= 0)
    bias = _toeplitz(rel_table, lambda rel: np.clip(rel, -REL_CLIP, REL_CLIP) + REL_CLIP, q_pos, k_pos) * LOG2E
    return jnp.where(mask[None], bias, NEG_INF)


def kernel(x_prompt, x_sample, cache_a_k, cache_a_v, cache_b_k, cache_b_v, cache_b_logf, cache_c_k, cache_c_v, c_prompt, c_sample, w_ada, b_ada, norm_g, w_ffn_in, w_ffn_out, w_in_ab, b_forget, w_out_ab, lambda_q1, lambda_k1, lambda_q2, lambda_k2, subln_g, t5_table, w_in_c, w_out_c, c_rel_bias, final_g):
    depth = w_ada.shape[0]
    bsz, seq, d = x_prompt.shape
    dbsz, dseq, _ = x_sample.shape
    past = cache_b_logf.shape[2]
    h_a = cache_a_k.shape[3]
    h_b = cache_b_k.shape[3]
    h_c = cache_c_k.shape[3]
    wa, wb, cw = h_a * HEAD_DIM, h_b * HEAD_DIM, h_c * HEAD_DIM
    assert bsz == dbsz and wa == wb and 3 * wa + 3 * wb == 3 * cw
    assert LANES >= T5_MAX_DIST and SAMPLE_NEAR >= T5_MAX_DIST and past >= SAMPLE_NEAR
    far_bucket = (N_T5_BUCKETS // 2) - 1

    mods = _ada_mods(jnp.concatenate([c_prompt, c_sample], axis=0), w_ada, b_ada)
    mods = mods.reshape(depth, 2, bsz, 9, d).transpose(0, 3, 1, 2, 4).reshape(depth * 9 * 2, bsz, 1, d)

    def slot(l, k, grp):
        return (l * 9 + k) * 2 + grp

    xs = [x_prompt, x_sample]
    ab_states = [[], []]
    c_states = [[], []]
    for l in range(depth):
        i = l // 2
        last = l == depth - 1
        w1 = _prep_ffn_weights(w_ffn_in[l, 0], w_ffn_out[l, 0])
        w2 = _prep_ffn_weights(w_ffn_in[l, 1], w_ffn_out[l, 1])
        for grp in range(2):
            xs[grp] = _ffn(xs[grp], mods, [slot(l, k, grp) for k in range(3)], norm_g[l, 0], *w1)

        if l % 2 == 0:
            lam_init = 0.8 - 0.6 * math.exp(-0.3 * l)
            lam = (jnp.exp(jnp.sum(lambda_q1[i].astype(F32) * lambda_k1[i].astype(F32)))
                   - jnp.exp(jnp.sum(lambda_q2[i].astype(F32) * lambda_k2[i].astype(F32))) + lam_init).reshape(1)
            n_main = 3 * wa + 3 * wb
            w_main = w_in_ab[i][:, :n_main].astype(BF16)
            wf = jnp.pad(w_in_ab[i][:, n_main:].T.astype(BF16), ((0, 16 - h_b), (0, 0)))
            bfo = b_forget[i].astype(F32).reshape(h_b, 1)
            w_out = w_out_ab[i].astype(BF16)
            g_sub = subln_g[i].astype(F32).reshape(1, HEAD_DIM)
            tn = min(PROJ_COL_TILE, wa)
            taps = _seg_taps(tn, wa, (1, 2, 4, 5), False)
            cs = np.ones((n_main,), np.float32)
            cs[:wa] = DA_HALF ** -0.5 * LOG2E
            cs[3 * wa:3 * wa + wb] = HEAD_DIM ** -0.5 * LOG2E
            for grp in range(2):
                x = xs[grp]
                b, t, _ = x.shape
                tap_shapes = [((b, t, wa), True)] * 4
                if grp == 0:
                    qkv, ka, va, kb, vb, vt, logf_t = _proj(
                        x, mods, [slot(l, 3, grp), slot(l, 4, grp)], norm_g[l, 1], w_main, cs, tn, tap_shapes, taps,
                        wf, bfo, vt_cols=_vt_cols(tn, wa, (2, 5)), vt_tile=ROW_TILE)
                else:
                    qkv, ka, va, kb, vb, logf_t = _proj(
                        x, mods, [slot(l, 3, grp), slot(l, 4, grp)], norm_g[l, 1], w_main, cs, tn, tap_shapes, taps,
                        wf, bfo)
                logf = logf_t.reshape(h_b, b, t).transpose(1, 2, 0)
                ab_states[grp].append((ka.reshape(b, t, h_a, HEAD_DIM), va.reshape(b, t, h_a, HEAD_DIM),
                                       kb.reshape(b, t, h_b, HEAD_DIM), vb.reshape(b, t, h_b, HEAD_DIM), logf))
                logf_bh = logf_t.reshape(h_b, b, t).transpose(1, 0, 2)
                if grp == 0:
                    o_a = _diff_prompt(lam, t5_table.astype(F32).reshape(-1), qkv, vt, g_sub, h_a, 1.0 - lam_init)
                    fcum = _cumsum(logf_bh, jnp.zeros((b, h_b, LANES), F32))
                    nq = t // ROW_TILE
                    fref = fcum[:, :, ::ROW_TILE].reshape(-1)
                    o_b = _fox_prompt(fref, qkv, vt, fcum.reshape(b, h_b, nq, 1, ROW_TILE), h_b, 3 * h_a, h_a)
                else:
                    q_pos = past + np.arange(t)
                    bnear = _t5_bias_tiles(t5_table, q_pos, past - SAMPLE_NEAR + np.arange(SAMPLE_NEAR), far_bucket)
                    bnew = _t5_bias_tiles(t5_table, q_pos, q_pos, far_bucket)
                    kc = cache_a_k[i].reshape(b, past, wa)
                    vc = cache_a_v[i].reshape(b, past, wa)
                    o_a = _diff_sample(lam, qkv, kc, vc, bnear, bnew, g_sub, h_a, 1.0 - lam_init)
                    fc = _cumsum(cache_b_logf[i].astype(F32).transpose(0, 2, 1), jnp.zeros((b, h_b, LANES), F32))
                    carry = fc[:, :, past - 1:past]
                    fn = _cumsum(jnp.pad(logf_bh, ((0, 0), (0, 0), (0, LANES - t))),
                                 jnp.broadcast_to(carry, (b, h_b, LANES)))
                    o_b = _fox_sample(carry.reshape(-1), qkv, cache_b_k[i].reshape(b, past, wb),
                                      cache_b_v[i].reshape(b, past, wb), fc.reshape(b, h_b, 1, past),
                                      fn.reshape(b, h_b, 1, LANES), h_b, 3 * h_a)
                xs[grp] = _outproj([o_a, o_b], w_out, x, mods, slot(l, 5, grp))
        else:
            w_in = w_in_c[i].astype(BF16)
            w_out = w_out_c[i].astype(BF16)
            tn = min(PROJ_COL_TILE, cw)
            taps = _seg_taps(tn, cw, (1, 2), True)
            cs = np.ones((3 * cw,), np.float32)
            cs[:cw] = HEAD_DIM ** -0.5 * LOG2E
            for grp in range(2):
                x = xs[grp]
                b, t, _ = x.shape
                keep = min(C_BAND, t)
                assert keep == min(t, ROW_TILE)
                tap_shapes = [((b, keep, cw), False)] * 2
                if grp == 0:
                    qkv, k_new, v_new, vt = _proj(
                        x, mods, [slot(l, 3, grp), slot(l, 4, grp)], norm_g[l, 1], w_in, cs, tn, tap_shapes, taps,
                        vt_cols=_vt_cols(tn, cw, (2,)), vt_tile=BAND_TILE)
                else:
                    qkv, k_new, v_new = _proj(
                        x, mods, [slot(l, 3, grp), slot(l, 4, grp)], norm_g[l, 1], w_in, cs, tn, tap_shapes, taps)
                c_states[grp].append((k_new.reshape(b, keep, h_c, HEAD_DIM), v_new.reshape(b, keep, h_c, HEAD_DIM)))
                if grp == 0:
                    o_c = _band_prompt(c_rel_bias[i].astype(F32).reshape(-1), qkv, vt, h_c)
                else:
                    lc = cache_c_k.shape[2]
                    q_pos = past + np.arange(t)
                    bmc = _band_bias_tiles(c_rel_bias[i], q_pos, past - lc + np.arange(lc))
                    bmn = _band_bias_tiles(c_rel_bias[i], q_pos, q_pos)
                    o_c = _band_sample(qkv, cache_c_k[i], cache_c_v[i], bmc, bmn, h_c)
                xs[grp] = _outproj([o_c], w_out, x, mods, slot(l, 5, grp))

        for grp in range(2):
            xs[grp] = _ffn(xs[grp], mods, [slot(l, k, grp) for k in (6, 7, 8)], norm_g[l, 2], *w2,
                           final_g=final_g if last else None)

    outs = [xs[0], xs[1]]
    for grp in range(2):
        st = ab_states[grp]
        outs += [jnp.stack([s[k] for s in st]) for k in range(5)]
        st = c_states[grp]
        outs += [jnp.stack([s[k] for s in st]) for k in range(2)]
    return tuple(outs)
```

```python
import functools
import math

import numpy as np
import jax
import jax.numpy as jnp
from jax import lax
from jax.experimental import pallas as pl
from jax.experimental.pallas import tpu as pltpu

F32 = jnp.float32
BF16 = jnp.bfloat16

CHUNK = 64
HEAD_DIM = 128
DA_HALF = HEAD_DIM // 2
N_T5_BUCKETS = 32
T5_MAX_DIST = 128
C_PREV_CHUNKS = 8
C_BAND = C_PREV_CHUNKS * CHUNK
REL_CLIP = 128
EPS = 1e-6
NEG_INF = -1e30
LOG2E = math.log2(math.e)
VT_ROWS = HEAD_DIM + 16

VMEM_LIMIT_BYTES = 56 * 1024 * 1024
LANES = 128

ROW_TILE = 512
FF_TILE = 512
FFN_OUT_TILE = 512
FFN_ROW_SPLIT = 2
PROJ_ROW_SPLIT = 2
PROJ_COL_TILE = 1024
ADA_COL_TILE = 1024
BAND_TILE = 256
BAND_HEADS = 4
ATTN_GROUP = 512
SAMPLE_NEAR = 256

_NT = (((1,), (1,)), ((), ()))


def _params(*sem, flags=None):
    return pltpu.CompilerParams(dimension_semantics=sem, vmem_limit_bytes=VMEM_LIMIT_BYTES, flags=flags)


def _modulated_norm(x, g, shift, scale):
    y = x * lax.rsqrt(jnp.mean(x * x, axis=-1, keepdims=True) + EPS) * g
    return y * (1.0 + scale) + shift


def _normed_rows(x_ref, shift_ref, scale_ref, g_ref, h_scr, n_split, r):
    nb, tt, d = x_ref.shape
    rows = nb * tt // n_split
    if nb == 1:
        x, sh, sc = x_ref[:, r * rows:(r + 1) * rows, :], shift_ref[...], scale_ref[...]
    else:
        seqs = slice(r * (nb // n_split), (r + 1) * (nb // n_split))
        x, sh, sc = x_ref[seqs], shift_ref[seqs], scale_ref[seqs]
    hr = _modulated_norm(x, g_ref[...], sh, sc).reshape(rows, d).astype(BF16)
    h_scr[r * rows:(r + 1) * rows, :] = hr
    return hr


def _ada_kernel(c_ref, w_ref, b_ref, o_ref):
    c = c_ref[...]
    a = (c * jax.nn.sigmoid(c)).astype(BF16)
    o_ref[0] = jnp.dot(a, w_ref[0].astype(BF16), preferred_element_type=F32) + b_ref[0]


def _ada_mods(c_all, w_ada, b_ada):
    depth, d, n = w_ada.shape
    r = c_all.shape[0]
    tn = math.gcd(n, ADA_COL_TILE)
    return pl.pallas_call(
        _ada_kernel,
        grid=(depth, n // tn),
        in_specs=[
            pl.BlockSpec((r, d), lambda l, j: (0, 0)),
            pl.BlockSpec((1, d, tn), lambda l, j: (l, 0, j)),
            pl.BlockSpec((1, 1, tn), lambda l, j: (l, 0, j)),
        ],
        out_specs=pl.BlockSpec((1, r, tn), lambda l, j: (l, 0, j)),
        out_shape=jax.ShapeDtypeStruct((depth, r, n), F32),
        compiler_params=_params("arbitrary", "arbitrary"),
        name="ada_mods",
    )(c_all, w_ada, b_ada.reshape(depth, 1, n))


def _row_blocking(x):
    b, t, _ = x.shape
    if t >= ROW_TILE:
        assert t % ROW_TILE == 0
        return 1, ROW_TILE, t // ROW_TILE, b * (t // ROW_TILE)
    assert (b * t) % 8 == 0
    return b, t, 1, 1


def _x_index(nb, tps):
    if nb == 1:
        return lambda i, j: (i // tps, i % tps, 0)
    return lambda i, j: (0, 0, 0)


def _mod_spec(nb, tps, d, slot):
    if nb == 1:
        return pl.BlockSpec((None, 1, 1, d), lambda i, j: (slot, i // tps, 0, 0))
    return pl.BlockSpec((None, nb, 1, d), lambda i, j: (slot, 0, 0, 0))


def _ffn_kernel(*refs, nb, tt, n_ff, n_out, final):
    if final:
        (x_ref, shift_ref, scale_ref, gate_ref, g_ref, wg_ref, wu_ref, wo_ref, fg_ref,
         o_ref, h_scr, a_scr) = refs
    else:
        (x_ref, shift_ref, scale_ref, gate_ref, g_ref, wg_ref, wu_ref, wo_ref,
         o_ref, h_scr, a_scr) = refs
    j = pl.program_id(1)
    d = x_ref.shape[-1]
    tm = nb * tt
    tf = wg_ref.shape[1]
    tn = wo_ref.shape[1]

    rows = tm // FFN_ROW_SPLIT
    normed_rows = functools.partial(_normed_rows, x_ref, shift_ref, scale_ref, g_ref, h_scr, FFN_ROW_SPLIT)

    def hidden_chunk(rows_of):
        def products(r):
            hr = rows_of(r)
            return (jnp.dot(hr, wg_ref[...], preferred_element_type=F32),
                    jnp.dot(hr, wu_ref[...], preferred_element_type=F32))

        parts = []
        cur = products(0)
        for r in range(FFN_ROW_SPLIT):
            nxt = products(r + 1) if r + 1 < FFN_ROW_SPLIT else None
            gg, uu = cur
            parts.append((gg * jax.nn.sigmoid(gg) * uu).astype(BF16))
            cur = nxt
        a_scr[j] = jnp.concatenate(parts, axis=0)

    @pl.when(j == 0)
    def _():
        hidden_chunk(normed_rows)

    @pl.when(jnp.logical_and(j > 0, j < n_ff))
    def _():
        hidden_chunk(lambda r: h_scr[r * rows:(r + 1) * rows, :])

    @pl.when(j >= n_ff)
    def _():
        a = jnp.concatenate([a_scr[f] for f in range(n_ff)], axis=1)
        acc = jnp.dot(a, wo_ref[...], preferred_element_type=F32).reshape(nb, tt, tn)
        for n in range(n_out):
            @pl.when(j == n_ff + n)
            def _(n=n):
                cs = slice(n * tn, (n + 1) * tn)
                o_ref[:, :, cs] = x_ref[:, :, cs] + 0.5 * gate_ref[:, :, cs] * acc

    if final:
        @pl.when(j == n_ff + n_out - 1)
        def _():
            y = o_ref[...]
            o_ref[...] = y * lax.rsqrt(jnp.mean(y * y, axis=-1, keepdims=True) + EPS) * fg_ref[...]


def _ffn(x, mods, slots, g, wg_p, wu_p, w_out_p, final_g=None):
    b, t, d = x.shape
    nb, tt, tps, nrows = _row_blocking(x)
    ffp = w_out_p.shape[0]
    n_ff = ffp // FF_TILE
    tn = math.gcd(d, FFN_OUT_TILE)
    n_out = d // tn
    final = final_g is not None
    in_specs = [
        pl.BlockSpec((nb, tt, d), _x_index(nb, tps)),
        _mod_spec(nb, tps, d, slots[0]),
        _mod_spec(nb, tps, d, slots[1]),
        _mod_spec(nb, tps, d, slots[2]),
        pl.BlockSpec((1, d), lambda i, j: (0, 0)),
        pl.BlockSpec((d, FF_TILE), lambda i, j: (0, jnp.minimum(j, n_ff - 1))),
        pl.BlockSpec((d, FF_TILE), lambda i, j: (0, jnp.minimum(j, n_ff - 1))),
        pl.BlockSpec((ffp, tn), lambda i, j: (0, jnp.maximum(j - n_ff, 0))),
    ]
    args = [x, mods, mods, mods, g.reshape(1, d), wg_p, wu_p, w_out_p]
    if final:
        in_specs.append(pl.BlockSpec((1, d), lambda i, j: (0, 0)))
        args.append(final_g.reshape(1, d))
    return pl.pallas_call(
        functools.partial(_ffn_kernel, nb=nb, tt=tt, n_ff=n_ff, n_out=n_out, final=final),
        grid=(nrows, n_ff + n_out),
        in_specs=in_specs,
        out_specs=pl.BlockSpec((nb, tt, d), _x_index(nb, tps)),
        out_shape=jax.ShapeDtypeStruct((b, t, d), F32),
        scratch_shapes=[pltpu.VMEM((nb * tt, d), BF16), pltpu.VMEM((n_ff, nb * tt, FF_TILE), BF16)],
        compiler_params=_params("parallel", "arbitrary"),
        name="ffn",
    )(*args)


def _prep_ffn_weights(w_in, w_out):
    ff = w_out.shape[0]
    ffp = -(-ff // FF_TILE) * FF_TILE
    pad = ffp - ff
    wg = jnp.pad(w_in[:, :ff].astype(BF16), ((0, 0), (0, pad)))
    wu = jnp.pad(w_in[:, ff:].astype(BF16), ((0, 0), (0, pad)))
    return wg, wu, jnp.pad(w_out.astype(BF16), ((0, pad), (0, 0)))


def _log_sigmoid(x):
    return jnp.minimum(x, 0.0) - jnp.log1p(jnp.exp(-jnp.abs(x)))


def _proj_kernel(*refs, nb, tt, tn, tps, taps, has_f, vt_cols, vt_tile):
    x_ref, shift_ref, scale_ref, g_ref, w_ref, cs_ref = refs[:6]
    pos = 6
    if has_f:
        wf_ref, bf_ref = refs[6:8]
        pos = 8
    obf_ref = refs[pos]
    tap_refs = refs[pos + 1:pos + 1 + len({tp[0] for tp in taps})]
    pos = pos + 1 + len(tap_refs)
    if vt_cols:
        vt_ref = refs[pos]
        pos += 1
    if has_f:
        logf_ref = refs[pos]
        pos += 1
    h_scr = refs[pos]
    i = pl.program_id(0)
    j = pl.program_id(1)
    d = x_ref.shape[-1]

    def store_attention_copy(y):
        obf_ref[...] = (y * cs_ref[...]).astype(BF16).reshape(nb, tt, tn)

    assert all(tp[1] > 0 for tp in taps) and all(vc[0] > 0 for vc in vt_cols)

    @pl.when(j == 0)
    def _():
        hs = []
        ys = []
        for r in range(PROJ_ROW_SPLIT):
            hs.append(_normed_rows(x_ref, shift_ref, scale_ref, g_ref, h_scr, PROJ_ROW_SPLIT, r))
            ys.append(jnp.dot(hs[-1], w_ref[...], preferred_element_type=F32))
        store_attention_copy(jnp.concatenate(ys, axis=0))
        if has_f:
            fr = lax.dot_general(wf_ref[...], jnp.concatenate(hs, axis=0), _NT, preferred_element_type=F32)
            logf_ref[...] = _log_sigmoid(fr[:logf_ref.shape[0]] + bf_ref[...])

    @pl.when(j > 0)
    def _():
        y = jnp.dot(h_scr[...], w_ref[...], preferred_element_type=F32)
        store_attention_copy(y)
        _proj_taps(y, i, j, tap_refs, vt_ref if vt_cols else None, nb=nb, tt=tt, tn=tn, tps=tps, taps=taps,
                   vt_cols=vt_cols, vt_tile=vt_tile)


def _proj_taps(y, i, j, tap_refs, vt_ref, *, nb, tt, tn, tps, taps, vt_cols, vt_tile):
    for out_idx, jval, col0, last_rows_only in taps:
        cond = j == jval
        if last_rows_only and tps > 1:
            cond = jnp.logical_and(cond, i % tps == tps - 1)

        @pl.when(cond)
        def _(out_idx=out_idx, col0=col0):
            tap_refs[out_idx][:, :, col0:col0 + tn] = y.reshape(nb, tt, tn)

    for jval, head0 in vt_cols:
        @pl.when(j == jval)
        def _(head0=head0):
            for hh in range(tn // HEAD_DIM):
                yt = y[:, hh * HEAD_DIM:(hh + 1) * HEAD_DIM].T.astype(BF16)
                yt = jnp.concatenate([yt, jnp.ones((VT_ROWS - HEAD_DIM, yt.shape[1]), BF16)], axis=0)
                for s in range(tt // vt_tile):
                    vt_ref[0, head0 + hh, s] = yt[:, s * vt_tile:(s + 1) * vt_tile]


def _seg_taps(tn, width, segments, last_rows_only):
    per = width // tn
    assert per * tn == width
    return [(o, sidx * per + s, s * tn, last_rows_only) for o, sidx in enumerate(segments) for s in range(per)]


def _vt_cols(tn, width, segments):
    per = width // tn
    hpb = tn // HEAD_DIM
    return [(sidx * per + s, (o * per + s) * hpb) for o, sidx in enumerate(segments) for s in range(per)]


def _proj(x, mods, slots, g, w_bf, col_scale, tn, tap_shapes, taps, wf=None, bf=None, vt_cols=(), vt_tile=None):
    b, t, d = x.shape
    n = w_bf.shape[1]
    nb, tt, tps, nrows = _row_blocking(x)
    has_f = wf is not None
    assert not vt_cols or nb == 1
    in_specs = [
        pl.BlockSpec((nb, tt, d), _x_index(nb, tps)),
        _mod_spec(nb, tps, d, slots[0]),
        _mod_spec(nb, tps, d, slots[1]),
        pl.BlockSpec((1, d), lambda i, j: (0, 0)),
        pl.BlockSpec((d, tn), lambda i, j: (0, j)),
        pl.BlockSpec((1, tn), lambda i, j: (0, j)),
    ]
    args = [x, mods, mods, g.reshape(1, d), w_bf, jnp.asarray(col_scale, F32).reshape(1, n)]
    if has_f:
        in_specs += [pl.BlockSpec(wf.shape, lambda i, j: (0, 0)), pl.BlockSpec(bf.shape, lambda i, j: (0, 0))]
        args += [wf, bf]
    xi = _x_index(nb, tps)
    out_specs = [pl.BlockSpec((nb, tt, tn), lambda i, j: xi(i, j)[:2] + (j,))]
    out_shape = [jax.ShapeDtypeStruct((b, t, n), BF16)]
    for shp, follows_rows in tap_shapes:
        if follows_rows:
            out_specs.append(pl.BlockSpec((nb, tt, shp[2]), xi))
        elif nb == 1:
            out_specs.append(pl.BlockSpec((1, shp[1], shp[2]), lambda i, j: (i // tps, 0, 0)))
        else:
            out_specs.append(pl.BlockSpec(shp, lambda i, j: (0, 0, 0)))
        out_shape.append(jax.ShapeDtypeStruct(shp, F32))
    if vt_cols:
        n_vh = len(vt_cols) * (tn // HEAD_DIM)
        out_specs.append(pl.BlockSpec((1, n_vh, tt // vt_tile, VT_ROWS, vt_tile), lambda i, j: (i // tps, 0, i % tps, 0, 0)))
        out_shape.append(jax.ShapeDtypeStruct((b, n_vh, t // vt_tile, VT_ROWS, vt_tile), BF16))
    if has_f:
        nf = bf.shape[0]
        out_specs.append(pl.BlockSpec((nf, nb * tt), lambda i, j: (0, i)))
        out_shape.append(jax.ShapeDtypeStruct((nf, b * t), F32))
    return pl.pallas_call(
        functools.partial(_proj_kernel, nb=nb, tt=tt, tn=tn, tps=tps, taps=tuple(taps), has_f=has_f,
                          vt_cols=tuple(vt_cols), vt_tile=vt_tile),
        grid=(nrows, n // tn),
        in_specs=in_specs,
        out_specs=out_specs,
        out_shape=out_shape,
        scratch_shapes=[pltpu.VMEM((nb * tt, d), BF16)],
        compiler_params=_params("arbitrary", "arbitrary"),
        name="mixer_in_proj",
    )(*args)


def _outproj_kernel(*refs, nb, tt, n_in):
    o_refs = refs[:n_in]
    w_ref, x_ref, gate_ref, out_ref = refs[n_in:]
    d = x_ref.shape[-1]
    acc = None
    row0 = 0
    for o_ref in o_refs:
        wd = o_ref.shape[-1]
        part = jnp.dot(o_ref[...].reshape(nb * tt, wd), w_ref[row0:row0 + wd, :], preferred_element_type=F32)
        acc = part if acc is None else acc + part
        row0 += wd
    out_ref[...] = x_ref[...] + gate_ref[...] * acc.reshape(nb, tt, d)


def _outproj(o_list, w_bf, x, mods, gate_slot):
    b, t, d = x.shape
    nb, tt, tps, nrows = _row_blocking(x)
    xi = _x_index(nb, tps)
    in_specs = [pl.BlockSpec((nb, tt, o.shape[-1]), xi) for o in o_list]
    in_specs += [
        pl.BlockSpec(w_bf.shape, lambda i, j: (0, 0)),
        pl.BlockSpec((nb, tt, d), xi),
        _mod_spec(nb, tps, d, gate_slot),
    ]
    return pl.pallas_call(
        functools.partial(_outproj_kernel, nb=nb, tt=tt, n_in=len(o_list)),
        grid=(nrows, 1),
        in_specs=in_specs,
        out_specs=pl.BlockSpec((nb, tt, d), xi),
        out_shape=jax.ShapeDtypeStruct((b, t, d), F32),
        compiler_params=_params("parallel", "arbitrary"),
        name="mixer_out_proj",
    )(*o_list, w_bf, x, mods)


def _cumsum_kernel(x_ref, c_ref, o_ref):
    x = x_ref[0]
    n = x.shape[1]
    lane = lax.broadcasted_iota(jnp.int32, x.shape, 1)
    s = 1
    while s < n:
        x = x + jnp.where(lane >= s, pltpu.roll(x, s, axis=1), 0.0)
        s *= 2
    o_ref[0] = x + c_ref[0][:, :1]


def _cumsum(x, carry):
    g, h, n = x.shape
    return pl.pallas_call(
        _cumsum_kernel,
        grid=(g,),
        in_specs=[pl.BlockSpec((1, h, n), lambda i: (i, 0, 0)), pl.BlockSpec((1, h, LANES), lambda i: (i, 0, 0))],
        out_specs=pl.BlockSpec((1, h, n), lambda i: (i, 0, 0)),
        out_shape=jax.ShapeDtypeStruct((g, h, n), F32),
        compiler_params=_params("arbitrary"),
        name="logf_cumsum",
    )(x, carry)


def _softmax_init(m_scr, acc_scr):
    m_scr[...] = jnp.full(m_scr.shape, NEG_INF, F32)
    acc_scr[...] = jnp.zeros(acc_scr.shape, F32)


def _softmax_update_t(s, s_max, vt, m_prev, acc_prev):
    m_new = jnp.maximum(m_prev, s_max)
    alpha = jnp.exp2(m_prev - m_new)
    p = jnp.exp2(s - m_new)
    acc_new = alpha * acc_prev + jnp.dot(vt, p.astype(BF16), preferred_element_type=F32)
    return m_new, acc_new


def _normalised_t(acc):
    return (acc[:HEAD_DIM] / acc[HEAD_DIM:HEAD_DIM + 1]).T


def _tile_stage(slot, vt, m_scr, acc_scr, s_scr, smax_scr, n_groups, next_scores=None):
    g = m_scr.shape[1] // n_groups
    sls = [slice(i * g, (i + 1) * g) for i in range(n_groups)]
    state = [(m_scr[:, sl], acc_scr[:, sl]) for sl in sls]
    new = []
    for i, sl in enumerate(sls):
        if next_scores is not None:
            s = next_scores(i)
            s_scr[1 - slot, :, sl] = s
            smax_scr[1 - slot, :, sl] = jnp.max(s, axis=0, keepdims=True)
        new.append(_softmax_update_t(s_scr[slot, :, sl], smax_scr[slot, :, sl], vt, *state[i]))
    for sl, (m_new, acc_new) in zip(sls, new):
        m_scr[:, sl] = m_new
        acc_scr[:, sl] = acc_new


def _first_tile(scores, s_scr, smax_scr, n_groups):
    g = s_scr.shape[2] // n_groups
    for i in range(n_groups):
        sl = slice(i * g, (i + 1) * g)
        s = scores(i)
        s_scr[0, :, sl] = s
        smax_scr[0, :, sl] = jnp.max(s, axis=0, keepdims=True)


def _pipelined_tiles(last, n_special, produce_first, stage):
    def kind(dist):
        return dist if dist < n_special else "far"

    for k in range(n_special):
        @pl.when(last == k)
        def _(k=k):
            produce_first(kind(k))

    @pl.when(last >= n_special)
    def _():
        produce_first("far")

    n_pairs = jnp.maximum(last - n_special, 0) // 2

    def pair(i, carry):
        stage(2 * i, 0, "far")
        stage(2 * i + 1, 1, "far")
        return carry

    lax.fori_loop(0, n_pairs, pair, 0)
    t0 = 2 * n_pairs
    remaining = last - t0 + 1
    for r in range(1, n_special + 3):
        @pl.when(remaining == r)
        def _(r=r):
            for u in range(r):
                stage(t0 + u, u % 2, kind(r - 2 - u) if u + 1 < r else None)


def _toeplitz_values(tbl_ref, idx, n_vals, nh, h, ref_row):
    ref_val = tbl_ref[ref_row * nh + h]

    def body(v, acc):
        return jnp.where(idx == v, tbl_ref[v * nh + h] - ref_val, acc)

    return lax.fori_loop(0, n_vals, body, jnp.zeros(idx.shape, F32))


def _stack_diff_queries(q):
    lane = lax.broadcasted_iota(jnp.int32, q.shape, 1)
    zero = jnp.zeros_like(q)
    qs = jnp.concatenate([jnp.where(lane < DA_HALF, q, zero), jnp.where(lane >= DA_HALF, q, zero)], axis=0)
    return qs


def _diff_finalize(o, lam, g, out_scale):
    tq = o.shape[0] // 2
    od = o[:tq] - lam * o[tq:]
    return od * lax.rsqrt(jnp.mean(od * od, axis=-1, keepdims=True) + EPS) * g * out_scale


def _diff_p_kernel(lam_ref, tbl_ref, q_ref, k_ref, vt_ref, idx_ref, base_ref, g_ref, o_ref,
                   m_scr, acc_scr, bias_scr, s_scr, smax_scr, *, tq, nh, far_bucket, out_scale):
    h = pl.program_id(0)
    bi = pl.program_id(1)
    qi = pl.program_id(2)
    nblk = tq // LANES

    @pl.when(jnp.logical_and(bi == 0, qi == 0))
    def _():
        t = _toeplitz_values(tbl_ref, idx_ref[...], N_T5_BUCKETS, nh, h, far_bucket) * LOG2E
        bias_scr[...] = base_ref[...]
        for blk in range(nblk):
            sl = slice(blk * LANES, (blk + 1) * LANES)
            bias_scr[0, sl, sl] += t[0]
            if blk >= 1:
                bias_scr[0, (blk - 1) * LANES:blk * LANES, sl] += t[1]
        bias_scr[1, (nblk - 1) * LANES:, :LANES] += t[1]

    qs = _stack_diff_queries(q_ref[0])
    _softmax_init(m_scr, acc_scr)
    ngrp = 2 * tq // ATTN_GROUP

    def scores(j, kind):
        k = k_ref[0, pl.ds(pl.multiple_of(j * tq, tq), tq), :]

        def group(i):
            s = lax.dot_general(k, qs[i * ATTN_GROUP:(i + 1) * ATTN_GROUP], _NT, preferred_element_type=F32)
            if kind != "far":
                q0 = (i * ATTN_GROUP) % tq
                s = s + bias_scr[kind, :, q0:q0 + ATTN_GROUP]
            return s
        return group

    def stage(t, slot, next_kind):
        nxt = None if next_kind is None else scores(t + 1, next_kind)
        _tile_stage(slot, vt_ref[0, 0, t], m_scr, acc_scr, s_scr, smax_scr, ngrp, nxt)

    _pipelined_tiles(qi, 2, lambda kind: _first_tile(scores(0, kind), s_scr, smax_scr, ngrp), stage)
    o = _normalised_t(acc_scr[...])
    o_ref[0] = _diff_finalize(o, lam_ref[0], g_ref[...], out_scale).astype(BF16)


def _diff_prompt(lam, tbl, qkv, vt, g, n_heads, out_scale):
    b, t, _ = qkv.shape
    tq = vt.shape[-1]
    far_bucket = N_T5_BUCKETS // 2 - 1
    pos = np.arange(LANES)
    idx = np.stack([_t5_bucket_np(pos[:, None] - pos[None, :]),
                    _t5_bucket_np(pos[:, None] - pos[None, :] - LANES)]).astype(np.int32)
    kq = np.arange(tq)
    base = np.zeros((2, tq, tq), np.float32)
    base[0] = np.where((kq[:, None] // CHUNK) <= (kq[None, :] // CHUNK), 0.0, NEG_INF)
    return pl.pallas_call(
        functools.partial(_diff_p_kernel, tq=tq, nh=n_heads, far_bucket=far_bucket, out_scale=out_scale),
        grid=(n_heads, b, t // tq),
        in_specs=[
            pl.BlockSpec(memory_space=pltpu.SMEM),
            pl.BlockSpec(memory_space=pltpu.SMEM),
            pl.BlockSpec((1, tq, HEAD_DIM), lambda h, bi, qi: (bi, qi, h)),
            pl.BlockSpec((1, t, HEAD_DIM), lambda h, bi, qi: (bi, 0, n_heads + h)),
            pl.BlockSpec((1, 1, t // tq, VT_ROWS, tq), lambda h, bi, qi: (bi, h, 0, 0, 0)),
            pl.BlockSpec(idx.shape, lambda h, bi, qi: (0, 0, 0)),
            pl.BlockSpec(base.shape, lambda h, bi, qi: (0, 0, 0)),
            pl.BlockSpec((1, HEAD_DIM), lambda h, bi, qi: (0, 0)),
        ],
        out_specs=pl.BlockSpec((1, tq, HEAD_DIM), lambda h, bi, qi: (bi, qi, h)),
        out_shape=jax.ShapeDtypeStruct((b, t, n_heads * HEAD_DIM), BF16),
        scratch_shapes=[pltpu.VMEM((1, 2 * tq), F32), pltpu.VMEM((VT_ROWS, 2 * tq), F32),
                        pltpu.VMEM((2, tq, tq), F32), pltpu.VMEM((2, tq, 2 * tq), F32),
                        pltpu.VMEM((2, 1, 2 * tq), F32)],
        compiler_params=_params("arbitrary", "arbitrary", "arbitrary"),
        name="diff_attn_prompt",
    )(lam, tbl, qkv, qkv, vt, jnp.asarray(idx), jnp.asarray(base), g)


def _fox_p_kernel(fref_ref, q_ref, k_ref, vt_ref, f_ref, o_ref, m_scr, acc_scr, fcol_scr, s_scr, smax_scr, *, tq):
    bi = pl.program_id(0)
    h = pl.program_id(1)
    qi = pl.program_id(2)
    nblk = tq // LANES

    @pl.when(qi == 0)
    def _():
        for c in range(f_ref.shape[2] * nblk):
            row = f_ref[0, 0, c // nblk][:, (c % nblk) * LANES:(c % nblk + 1) * LANES]
            fcol_scr[c * LANES:(c + 1) * LANES, :] = jnp.broadcast_to(row, (LANES, LANES)).T

    q = q_ref[0]
    fref = fref_ref[(bi * pl.num_programs(1) + h) * pl.num_programs(2) + qi]
    _softmax_init(m_scr, acc_scr)

    ngrp = tq // ATTN_GROUP

    def scores(j, kind):
        start = pl.multiple_of(j * tq, tq)
        k = k_ref[0, pl.ds(start, tq), :]
        decay = (fref - fcol_scr[pl.ds(start, tq), :]) * LOG2E
        decay = jnp.concatenate([decay] * (ATTN_GROUP // LANES), axis=1)

        def group(i):
            s = lax.dot_general(k, q[i * ATTN_GROUP:(i + 1) * ATTN_GROUP], _NT, preferred_element_type=F32)
            s = s + decay
            if kind != "far":
                key = lax.broadcasted_iota(jnp.int32, s.shape, 0)
                qry = lax.broadcasted_iota(jnp.int32, s.shape, 1) + i * ATTN_GROUP
                s = jnp.where(key <= qry, s, NEG_INF)
            return s
        return group

    def stage(t, slot, next_kind):
        nxt = None if next_kind is None else scores(t + 1, next_kind)
        _tile_stage(slot, vt_ref[0, 0, t], m_scr, acc_scr, s_scr, smax_scr, ngrp, nxt)

    _pipelined_tiles(qi, 1, lambda kind: _first_tile(scores(0, kind), s_scr, smax_scr, ngrp), stage)


    o_ref[0] = _normalised_t(acc_scr[...]).astype(BF16)


def _fox_prompt(fref, qkv, vt, fcum, n_heads, col0, vt_head0):
    b, t, _ = qkv.shape
    tq = vt.shape[-1]
    nq = t // tq
    return pl.pallas_call(
        functools.partial(_fox_p_kernel, tq=tq),
        grid=(b, n_heads, nq),
        in_specs=[
            pl.BlockSpec(memory_space=pltpu.SMEM),
            pl.BlockSpec((1, tq, HEAD_DIM), lambda bi, h, qi: (bi, qi, col0 + h)),
            pl.BlockSpec((1, t, HEAD_DIM), lambda bi, h, qi: (bi, 0, col0 + n_heads + h)),
            pl.BlockSpec((1, 1, nq, VT_ROWS, tq), lambda bi, h, qi: (bi, vt_head0 + h, 0, 0, 0)),
            pl.BlockSpec((1, 1, nq, 1, tq), lambda bi, h, qi: (bi, h, 0, 0, 0)),
        ],
        out_specs=pl.BlockSpec((1, tq, HEAD_DIM), lambda bi, h, qi: (bi, qi, h)),
        out_shape=jax.ShapeDtypeStruct((b, t, n_heads * HEAD_DIM), BF16),
        scratch_shapes=[pltpu.VMEM((1, tq), F32), pltpu.VMEM((VT_ROWS, tq), F32),
                        pltpu.VMEM((t, LANES), F32), pltpu.VMEM((2, tq, tq), F32), pltpu.VMEM((2, 1, tq), F32)],
        compiler_params=_params("arbitrary", "arbitrary", "arbitrary"),
        name="fox_attn_prompt",
    )(fref, qkv, qkv, vt, fcum)


def _joint_softmax_pv(s_c, s_n, vc, vn):
    m = jnp.maximum(jnp.max(s_c, axis=1, keepdims=True), jnp.max(s_n, axis=1, keepdims=True))
    p_c = jnp.exp2(s_c - m)
    p_n = jnp.exp2(s_n - m)
    l = jnp.sum(p_c, axis=1, keepdims=True) + jnp.sum(p_n, axis=1, keepdims=True)
    acc = jnp.dot(p_c.astype(BF16), vc, preferred_element_type=F32)
    acc = acc + jnp.dot(p_n.astype(BF16), vn, preferred_element_type=F32)
    return acc / l


def _diff_s_kernel(lam_ref, q_ref, kn_ref, vn_ref, kc_ref, vc_ref, bnear_ref, bnew_ref, g_ref, o_ref, *, near, out_scale):
    qs = _stack_diff_queries(q_ref[0])
    kc = kc_ref[0].astype(BF16)
    vc = vc_ref[0].astype(BF16)
    p = kc.shape[0]
    s_c = lax.dot_general(qs, kc, _NT, preferred_element_type=F32)
    bnear = bnear_ref[0]
    s_c = jnp.concatenate([s_c[:, :p - near], s_c[:, p - near:] + jnp.concatenate([bnear, bnear], axis=0)], axis=1)
    bnew = bnew_ref[0]
    s_n = lax.dot_general(qs, kn_ref[0], _NT, preferred_element_type=F32) + jnp.concatenate([bnew, bnew], axis=0)
    o = _joint_softmax_pv(s_c, s_n, vc, vn_ref[0])
    o_ref[0] = _diff_finalize(o, lam_ref[0], g_ref[...], out_scale).astype(BF16)


def _diff_sample(lam, qkv, kc, vc, bnear, bnew, g, n_heads, out_scale):
    b, t, _ = qkv.shape
    p = kc.shape[1]
    near = bnear.shape[-1]
    hd = HEAD_DIM
    return pl.pallas_call(
        functools.partial(_diff_s_kernel, near=near, out_scale=out_scale),
        grid=(b, n_heads),
        in_specs=[
            pl.BlockSpec(memory_space=pltpu.SMEM),
            pl.BlockSpec((1, t, hd), lambda bi, h: (bi, 0, h)),
            pl.BlockSpec((1, t, hd), lambda bi, h: (bi, 0, n_heads + h)),
            pl.BlockSpec((1, t, hd), lambda bi, h: (bi, 0, 2 * n_heads + h)),
            pl.BlockSpec((1, p, hd), lambda bi, h: (bi, 0, h)),
            pl.BlockSpec((1, p, hd), lambda bi, h: (bi, 0, h)),
            pl.BlockSpec((1, t, near), lambda bi, h: (h, 0, 0)),
            pl.BlockSpec((1, t, t), lambda bi, h: (h, 0, 0)),
            pl.BlockSpec((1, hd), lambda bi, h: (0, 0)),
        ],
        out_specs=pl.BlockSpec((1, t, hd), lambda bi, h: (bi, 0, h)),
        out_shape=jax.ShapeDtypeStruct((b, t, n_heads * hd), BF16),
        compiler_params=_params("parallel", "parallel"),
        name="diff_attn_sample",
    )(lam, qkv, qkv, qkv, kc, vc, bnear, bnew, g)


def _fox_s_kernel(fref_ref, q_ref, kn_ref, vn_ref, kc_ref, vc_ref, fc_ref, fn_ref, o_ref):
    bi = pl.program_id(0)
    h = pl.program_id(1)
    q = q_ref[0]
    t = q.shape[0]
    fref = fref_ref[bi * pl.num_programs(1) + h]
    s_c = lax.dot_general(q, kc_ref[0].astype(BF16), _NT, preferred_element_type=F32) + (fref - fc_ref[0, 0]) * LOG2E
    s_n = lax.dot_general(q, kn_ref[0], _NT, preferred_element_type=F32) + (fref - fn_ref[0, 0][:, :t]) * LOG2E
    row = lax.broadcasted_iota(jnp.int32, s_n.shape, 0)
    col = lax.broadcasted_iota(jnp.int32, s_n.shape, 1)
    s_n = jnp.where(col <= row, s_n, NEG_INF)
    o_ref[0] = _joint_softmax_pv(s_c, s_n, vc_ref[0].astype(BF16), vn_ref[0]).astype(BF16)


def _fox_sample(fref, qkv, kc, vc, fc, fn, n_heads, col0):
    b, t, _ = qkv.shape
    p = kc.shape[1]
    hd = HEAD_DIM
    return pl.pallas_call(
        _fox_s_kernel,
        grid=(b, n_heads),
        in_specs=[
            pl.BlockSpec(memory_space=pltpu.SMEM),
            pl.BlockSpec((1, t, hd), lambda bi, h: (bi, 0, col0 + h)),
            pl.BlockSpec((1, t, hd), lambda bi, h: (bi, 0, col0 + n_heads + h)),
            pl.BlockSpec((1, t, hd), lambda bi, h: (bi, 0, col0 + 2 * n_heads + h)),
            pl.BlockSpec((1, p, hd), lambda bi, h: (bi, 0, h)),
            pl.BlockSpec((1, p, hd), lambda bi, h: (bi, 0, h)),
            pl.BlockSpec((1, 1, 1, p), lambda bi, h: (bi, h, 0, 0)),
            pl.BlockSpec((1, 1, 1, fn.shape[-1]), lambda bi, h: (bi, h, 0, 0)),
        ],
        out_specs=pl.BlockSpec((1, t, hd), lambda bi, h: (bi, 0, h)),
        out_shape=jax.ShapeDtypeStruct((b, t, n_heads * hd), BF16),
        compiler_params=_params("parallel", "parallel"),
        name="fox_attn_sample",
    )(fref, qkv, qkv, qkv, kc, vc, fc, fn)


def _band_p_kernel(tbl_ref, q_ref, k_ref, vt_ref, idx_ref, base_ref, o_ref, bm_scr, *, tq, nkb, nh, hps):
    hg = pl.program_id(0)
    bi = pl.program_id(1)
    i = pl.program_id(2)
    nblk = tq // LANES
    qoff = (nkb - 1) * nblk

    @pl.when(jnp.logical_and(bi == 0, i == 0))
    def _():
        for e in range(hps):
            t = _toeplitz_values(tbl_ref, idx_ref[...], 2 * REL_CLIP + 1, nh, hg * hps + e, 0)
            bm_scr[e] = base_ref[...]
            for bq in range(nblk):
                kd = qoff + bq
                bm_scr[e, kd * LANES:(kd + 1) * LANES, bq * LANES:(bq + 1) * LANES] += t[0] * LOG2E
                bm_scr[e, (kd - 1) * LANES:kd * LANES, bq * LANES:(bq + 1) * LANES] += t[1] * LOG2E

    def head_scores(e):
        hs = slice(e * HEAD_DIM, (e + 1) * HEAD_DIM)
        q = q_ref[0, :, hs]
        ss = []
        for blk in range(nkb):
            jb = i - (nkb - 1) + blk
            jc = jnp.maximum(jb, 0)
            k = k_ref[0, pl.ds(pl.multiple_of(jc * tq, tq), tq), hs]
            s = lax.dot_general(k, q, _NT, preferred_element_type=F32) + bm_scr[e, blk * tq:(blk + 1) * tq, :]
            if blk < nkb - 1:
                s = s + jnp.where(jb < 0, NEG_INF, 0.0)
            ss.append(s)
        return ss, functools.reduce(jnp.maximum, [jnp.max(s, axis=0, keepdims=True) for s in ss])

    outs = []
    nxt = head_scores(0)
    for e in range(hps):
        ss, m = nxt
        if e + 1 < hps:
            nxt = head_scores(e + 1)
        acc = None
        for blk, s in enumerate(ss):
            jc = jnp.maximum(i - (nkb - 1) + blk, 0)
            a = jnp.dot(vt_ref[0, e, jc], jnp.exp2(s - m).astype(BF16), preferred_element_type=F32)
            acc = a if acc is None else acc + a
        outs.append(_normalised_t(acc).astype(BF16))
    o_ref[0] = jnp.concatenate(outs, axis=1)


def _band_prompt(tbl, qkv, vt, n_heads):
    b, t, _ = qkv.shape
    tq = vt.shape[-1]
    nkb = C_BAND // tq + 1
    assert (nkb - 1) * tq == C_BAND and tq % LANES == 0 and LANES >= REL_CLIP
    pos = np.arange(LANES)
    rel = pos[:, None] - pos[None, :]
    idx = (np.stack([np.clip(rel, -REL_CLIP, REL_CLIP), np.clip(rel - LANES, -REL_CLIP, REL_CLIP)]) + REL_CLIP).astype(np.int32)
    k_pos = np.arange(nkb * tq)[:, None]
    q_pos = C_BAND + np.arange(tq)[None, :]
    kc, qc = k_pos // CHUNK, q_pos // CHUNK
    base = np.where((kc <= qc) & (qc - kc <= C_PREV_CHUNKS), 0.0, NEG_INF).astype(np.float32)
    hps = BAND_HEADS
    ngrp = n_heads // hps
    assert ngrp * hps == n_heads
    wd = hps * HEAD_DIM
    return pl.pallas_call(
        functools.partial(_band_p_kernel, tq=tq, nkb=nkb, nh=n_heads, hps=hps),
        grid=(ngrp, b, t // tq),
        in_specs=[
            pl.BlockSpec(memory_space=pltpu.SMEM),
            pl.BlockSpec((1, tq, wd), lambda h, bi, i: (bi, i, h)),
            pl.BlockSpec((1, t, wd), lambda h, bi, i: (bi, 0, ngrp + h)),
            pl.BlockSpec((1, hps, t // tq, VT_ROWS, tq), lambda h, bi, i: (bi, h, 0, 0, 0)),
            pl.BlockSpec(idx.shape, lambda h, bi, i: (0, 0, 0)),
            pl.BlockSpec(base.shape, lambda h, bi, i: (0, 0)),
        ],
        out_specs=pl.BlockSpec((1, tq, wd), lambda h, bi, i: (bi, i, h)),
        out_shape=jax.ShapeDtypeStruct((b, t, n_heads * HEAD_DIM), BF16),
        scratch_shapes=[pltpu.VMEM((hps, nkb * tq, tq), F32)],
        compiler_params=_params("arbitrary", "arbitrary", "arbitrary"),
        name="band_attn_prompt",
    )(tbl, qkv, qkv, vt, jnp.asarray(idx), jnp.asarray(base))


def _band_s_kernel(q_ref, kn_ref, vn_ref, kc_ref, vc_ref, bmc_ref, bmn_ref, o_ref, *, n_heads):
    outs = []
    for h in range(n_heads):
        hs = slice(h * HEAD_DIM, (h + 1) * HEAD_DIM)
        q = q_ref[0, :, hs]
        kc = kc_ref[0, :, h, :].astype(BF16)
        vc = vc_ref[0, :, h, :].astype(BF16)
        s_c = lax.dot_general(q, kc, _NT, preferred_element_type=F32) + bmc_ref[h]
        s_n = lax.dot_general(q, kn_ref[0, :, hs], _NT, preferred_element_type=F32) + bmn_ref[h]
        outs.append(_joint_softmax_pv(s_c, s_n, vc, vn_ref[0, :, hs]).astype(BF16))
    o_ref[0] = jnp.concatenate(outs, axis=1)


def _band_sample(qkv, kc, vc, bmc, bmn, n_heads):
    b, t, _ = qkv.shape
    lc = kc.shape[1]
    cw = n_heads * HEAD_DIM
    return pl.pallas_call(
        functools.partial(_band_s_kernel, n_heads=n_heads),
        grid=(b,),
        in_specs=[
            pl.BlockSpec((1, t, cw), lambda bi: (bi, 0, 0)),
            pl.BlockSpec((1, t, cw), lambda bi: (bi, 0, 1)),
            pl.BlockSpec((1, t, cw), lambda bi: (bi, 0, 2)),
            pl.BlockSpec((1, lc, n_heads, HEAD_DIM), lambda bi: (bi, 0, 0, 0)),
            pl.BlockSpec((1, lc, n_heads, HEAD_DIM), lambda bi: (bi, 0, 0, 0)),
            pl.BlockSpec(bmc.shape, lambda bi: (0, 0, 0)),
            pl.BlockSpec(bmn.shape, lambda bi: (0, 0, 0)),
        ],
        out_specs=pl.BlockSpec((1, t, cw), lambda bi: (bi, 0, 0)),
        out_shape=jax.ShapeDtypeStruct((b, t, cw), BF16),
        compiler_params=_params("parallel"),
        name="band_attn_sample",
    )(qkv, qkv, qkv, kc, vc, bmc, bmn)


def _t5_bucket_np(rel):
    nb = N_T5_BUCKETS // 2
    max_exact = nb // 2
    n = np.abs(rel)
    nf = np.maximum(n, 1).astype(np.float64)
    large = max_exact + (np.log(nf / max_exact) / math.log(T5_MAX_DIST / max_exact) * (nb - max_exact)).astype(np.int64)
    large = np.minimum(large, nb - 1)
    return np.where(rel > 0, nb, 0) + np.where(n < max_exact, n, large)


def _toeplitz(table, index_of_rel, q_pos, k_pos):
    r, c = len(q_pos), len(k_pos)
    assert (np.diff(q_pos) == 1).all() and (np.diff(k_pos) == 1).all()
    rels = (k_pos[0] - q_pos[0]) - (r - 1) + np.arange(r + c - 1)
    v = jnp.pad(table.astype(F32)[index_of_rel(rels)].T, ((0, 0), (0, 1)))
    x = jnp.tile(v, (1, r))[:, :r * (r + c - 1)].reshape(v.shape[0], r, r + c - 1)
    return x[:, :, r - 1:r - 1 + c]


def _t5_bias_tiles(t5_table, q_pos, k_pos, far_bucket):
    mask = (k_pos[None, :] // CHUNK) <= (q_pos[:, None] // CHUNK)
    bias = _toeplitz(t5_table, _t5_bucket_np, q_pos, k_pos)
    bias = (bias - t5_table.astype(F32)[far_bucket][:, None, None]) * LOG2E
    return jnp.where(mask[None], bias, NEG_INF)


def _band_bias_tiles(rel_table, q_pos, k_pos):
    qc = q_pos[:, None] // CHUNK
    kc = k_pos[None, :] // CHUNK
    mask = (kc <= qc) & (qc - kc <= C_PREV_CHUNKS) & (k_pos[None, :] >= 0)
    bias = _toeplitz(rel_table, lambda rel: np.clip(rel, -REL_CLIP, REL_CLIP) + REL_CLIP, q_pos, k_pos) * LOG2E
    return jnp.where(mask[None], bias, NEG_INF)


def kernel(x_prompt, x_sample, cache_a_k, cache_a_v, cache_b_k, cache_b_v, cache_b_logf, cache_c_k, cache_c_v, c_prompt, c_sample, w_ada, b_ada, norm_g, w_ffn_in, w_ffn_out, w_in_ab, b_forget, w_out_ab, lambda_q1, lambda_k1, lambda_q2, lambda_k2, subln_g, t5_table, w_in_c, w_out_c, c_rel_bias, final_g):
    depth = w_ada.shape[0]
    bsz, seq, d = x_prompt.shape
    dbsz, dseq, _ = x_sample.shape
    past = cache_b_logf.shape[2]
    h_a = cache_a_k.shape[3]
    h_b = cache_b_k.shape[3]
    h_c = cache_c_k.shape[3]
    wa, wb, cw = h_a * HEAD_DIM, h_b * HEAD_DIM, h_c * HEAD_DIM
    assert bsz == dbsz and wa == wb and 3 * wa + 3 * wb == 3 * cw
    assert LANES >= T5_MAX_DIST and SAMPLE_NEAR >= T5_MAX_DIST and past >= SAMPLE_NEAR
    far_bucket = (N_T5_BUCKETS // 2) - 1

    mods = _ada_mods(jnp.concatenate([c_prompt, c_sample], axis=0), w_ada, b_ada)
    mods = mods.reshape(depth, 2, bsz, 9, d).transpose(0, 3, 1, 2, 4).reshape(depth * 9 * 2, bsz, 1, d)

    def slot(l, k, grp):
        return (l * 9 + k) * 2 + grp

    xs = [x_prompt, x_sample]
    ab_states = [[], []]
    c_states = [[], []]
    for l in range(depth):
        i = l // 2
        last = l == depth - 1
        w1 = _prep_ffn_weights(w_ffn_in[l, 0], w_ffn_out[l, 0])
        w2 = _prep_ffn_weights(w_ffn_in[l, 1], w_ffn_out[l, 1])
        for grp in range(2):
            xs[grp] = _ffn(xs[grp], mods, [slot(l, k, grp) for k in range(3)], norm_g[l, 0], *w1)

        if l % 2 == 0:
            lam_init = 0.8 - 0.6 * math.exp(-0.3 * l)
            lam = (jnp.exp(jnp.sum(lambda_q1[i].astype(F32) * lambda_k1[i].astype(F32)))
                   - jnp.exp(jnp.sum(lambda_q2[i].astype(F32) * lambda_k2[i].astype(F32))) + lam_init).reshape(1)
            n_main = 3 * wa + 3 * wb
            w_main = w_in_ab[i][:, :n_main].astype(BF16)
            wf = jnp.pad(w_in_ab[i][:, n_main:].T.astype(BF16), ((0, 16 - h_b), (0, 0)))
            bfo = b_forget[i].astype(F32).reshape(h_b, 1)
            w_out = w_out_ab[i].astype(BF16)
            g_sub = subln_g[i].astype(F32).reshape(1, HEAD_DIM)
            tn = min(PROJ_COL_TILE, wa)
            taps = _seg_taps(tn, wa, (1, 2, 4, 5), False)
            cs = np.ones((n_main,), np.float32)
            cs[:wa] = DA_HALF ** -0.5 * LOG2E
            cs[3 * wa:3 * wa + wb] = HEAD_DIM ** -0.5 * LOG2E
            for grp in range(2):
                x = xs[grp]
                b, t, _ = x.shape
                tap_shapes = [((b, t, wa), True)] * 4
                if grp == 0:
                    qkv, ka, va, kb, vb, vt, logf_t = _proj(
                        x, mods, [slot(l, 3, grp), slot(l, 4, grp)], norm_g[l, 1], w_main, cs, tn, tap_shapes, taps,
                        wf, bfo, vt_cols=_vt_cols(tn, wa, (2, 5)), vt_tile=ROW_TILE)
                else:
                    qkv, ka, va, kb, vb, logf_t = _proj(
                        x, mods, [slot(l, 3, grp), slot(l, 4, grp)], norm_g[l, 1], w_main, cs, tn, tap_shapes, taps,
                        wf, bfo)
                logf = logf_t.reshape(h_b, b, t).transpose(1, 2, 0)
                ab_states[grp].append((ka.reshape(b, t, h_a, HEAD_DIM), va.reshape(b, t, h_a, HEAD_DIM),
                                       kb.reshape(b, t, h_b, HEAD_DIM), vb.reshape(b, t, h_b, HEAD_DIM), logf))
                logf_bh = logf_t.reshape(h_b, b, t).transpose(1, 0, 2)
                if grp == 0:
                    o_a = _diff_prompt(lam, t5_table.astype(F32).reshape(-1), qkv, vt, g_sub, h_a, 1.0 - lam_init)
                    fcum = _cumsum(logf_bh, jnp.zeros((b, h_b, LANES), F32))
                    nq = t // ROW_TILE
                    fref = fcum[:, :, ::ROW_TILE].reshape(-1)
                    o_b = _fox_prompt(fref, qkv, vt, fcum.reshape(b, h_b, nq, 1, ROW_TILE), h_b, 3 * h_a, h_a)
                else:
                    q_pos = past + np.arange(t)
                    bnear = _t5_bias_tiles(t5_table, q_pos, past - SAMPLE_NEAR + np.arange(SAMPLE_NEAR), far_bucket)
                    bnew = _t5_bias_tiles(t5_table, q_pos, q_pos, far_bucket)
                    kc = cache_a_k[i].reshape(b, past, wa)
                    vc = cache_a_v[i].reshape(b, past, wa)
                    o_a = _diff_sample(lam, qkv, kc, vc, bnear, bnew, g_sub, h_a, 1.0 - lam_init)
                    fc = _cumsum(cache_b_logf[i].astype(F32).transpose(0, 2, 1), jnp.zeros((b, h_b, LANES), F32))
                    carry = fc[:, :, past - 1:past]
                    fn = _cumsum(jnp.pad(logf_bh, ((0, 0), (0, 0), (0, LANES - t))),
                                 jnp.broadcast_to(carry, (b, h_b, LANES)))
                    o_b = _fox_sample(carry.reshape(-1), qkv, cache_b_k[i].reshape(b, past, wb),
                                      cache_b_v[i].reshape(b, past, wb), fc.reshape(b, h_b, 1, past),
                                      fn.reshape(b, h_b, 1, LANES), h_b, 3 * h_a)
                xs[grp] = _outproj([o_a, o_b], w_out, x, mods, slot(l, 5, grp))
        else:
            w_in = w_in_c[i].astype(BF16)
            w_out = w_out_c[i].astype(BF16)
            tn = min(PROJ_COL_TILE, cw)
            taps = _seg_taps(tn, cw, (1, 2), True)
            cs = np.ones((3 * cw,), np.float32)
            cs[:cw] = HEAD_DIM ** -0.5 * LOG2E
            for grp in range(2):
                x = xs[grp]
                b, t, _ = x.shape
                keep = min(C_BAND, t)
                assert keep == min(t, ROW_TILE)
                tap_shapes = [((b, keep, cw), False)] * 2
                if grp == 0:
                    qkv, k_new, v_new, vt = _proj(
                        x, mods, [slot(l, 3, grp), slot(l, 4, grp)], norm_g[l, 1], w_in, cs, tn, tap_shapes, taps,
                        vt_cols=_vt_cols(tn, cw, (2,)), vt_tile=BAND_TILE)
                else:
                    qkv, k_new, v_new = _proj(
                        x, mods, [slot(l, 3, grp), slot(l, 4, grp)], norm_g[l, 1], w_in, cs, tn, tap_shapes, taps)
                c_states[grp].append((k_new.reshape(b, keep, h_c, HEAD_DIM), v_new.reshape(b, keep, h_c, HEAD_DIM)))
                if grp == 0:
                    o_c = _band_prompt(c_rel_bias[i].astype(F32).reshape(-1), qkv, vt, h_c)
                else:
                    lc = cache_c_k.shape[2]
                    q_pos = past + np.arange(t)
                    bmc = _band_bias_tiles(c_rel_bias[i], q_pos, past - lc + np.arange(lc))
                    bmn = _band_bias_tiles(c_rel_bias[i], q_pos, q_pos)
                    o_c = _band_sample(qkv, cache_c_k[i], cache_c_v[i], bmc, bmn, h_c)
                xs[grp] = _outproj([o_c], w_out, x, mods, slot(l, 5, grp))

        for grp in range(2):
            xs[grp] = _ffn(xs[grp], mods, [slot(l, k, grp) for k in (6, 7, 8)], norm_g[l, 2], *w2,
                           final_g=final_g if last else None)

    outs = [xs[0], xs[1]]
    for grp in range(2):
        st = ab_states[grp]
        outs += [jnp.stack([s[k] for s in st]) for k in range(5)]
        st = c_states[grp]
        outs += [jnp.stack([s[k] for s in st]) for k in range(2)]
    return tuple(outs)
```

```python
import functools
import math

import numpy as np
import jax
import jax.numpy as jnp
from jax import lax
from jax.experimental import pallas as pl
from jax.experimental.pallas import tpu as pltpu

F32 = jnp.float32
BF16 = jnp.bfloat16

CHUNK = 64
HEAD_DIM = 128
DA_HALF = HEAD_DIM // 2
N_T5_BUCKETS = 32
T5_MAX_DIST = 128
C_PREV_CHUNKS = 8
C_BAND = C_PREV_CHUNKS * CHUNK
REL_CLIP = 128
EPS = 1e-6
NEG_INF = -1e30
LOG2E = math.log2(math.e)
VT_ROWS = HEAD_DIM + 16

VMEM_LIMIT_BYTES = 56 * 1024 * 1024
LANES = 128

ROW_TILE = 512
FF_TILE = 512
FFN_OUT_TILE = 512
FFN_ROW_SPLIT = 2
PROJ_ROW_SPLIT = 2
PROJ_COL_TILE = 1024
ADA_COL_TILE = 1024
BAND_TILE = 256
BAND_HEADS = 4
ATTN_HEADS = 2
ATTN_GROUP = 512
SAMPLE_NEAR = 256

_NT = (((1,), (1,)), ((), ()))


def _params(*sem, flags=None):
    return pltpu.CompilerParams(dimension_semantics=sem, vmem_limit_bytes=VMEM_LIMIT_BYTES, flags=flags)


def _modulated_norm(x, g, shift, scale):
    y = x * lax.rsqrt(jnp.mean(x * x, axis=-1, keepdims=True) + EPS) * g
    return y * (1.0 + scale) + shift


def _normed_rows(x_ref, shift_ref, scale_ref, g_ref, h_scr, n_split, r):
    nb, tt, d = x_ref.shape
    rows = nb * tt // n_split
    if nb == 1:
        x, sh, sc = x_ref[:, r * rows:(r + 1) * rows, :], shift_ref[...], scale_ref[...]
    else:
        seqs = slice(r * (nb // n_split), (r + 1) * (nb // n_split))
        x, sh, sc = x_ref[seqs], shift_ref[seqs], scale_ref[seqs]
    hr = _modulated_norm(x, g_ref[...], sh, sc).reshape(rows, d).astype(BF16)
    h_scr[r * rows:(r + 1) * rows, :] = hr
    return hr


def _ada_kernel(c_ref, w_ref, b_ref, o_ref):
    c = c_ref[...]
    a = (c * jax.nn.sigmoid(c)).astype(BF16)
    o_ref[0] = jnp.dot(a, w_ref[0].astype(BF16), preferred_element_type=F32) + b_ref[0]


def _ada_mods(c_all, w_ada, b_ada):
    depth, d, n = w_ada.shape
    r = c_all.shape[0]
    tn = math.gcd(n, ADA_COL_TILE)
    return pl.pallas_call(
        _ada_kernel,
        grid=(depth, n // tn),
        in_specs=[
            pl.BlockSpec((r, d), lambda l, j: (0, 0)),
            pl.BlockSpec((1, d, tn), lambda l, j: (l, 0, j)),
            pl.BlockSpec((1, 1, tn), lambda l, j: (l, 0, j)),
        ],
        out_specs=pl.BlockSpec((1, r, tn), lambda l, j: (l, 0, j)),
        out_shape=jax.ShapeDtypeStruct((depth, r, n), F32),
        compiler_params=_params("arbitrary", "arbitrary"),
        name="ada_mods",
    )(c_all, w_ada, b_ada.reshape(depth, 1, n))


def _row_blocking(x):
    b, t, _ = x.shape
    if t >= ROW_TILE:
        assert t % ROW_TILE == 0
        return 1, ROW_TILE, t // ROW_TILE, b * (t // ROW_TILE)
    assert (b * t) % 8 == 0
    return b, t, 1, 1


def _x_index(nb, tps):
    if nb == 1:
        return lambda i, j: (i // tps, i % tps, 0)
    return lambda i, j: (0, 0, 0)


def _mod_spec(nb, tps, d, slot):
    if nb == 1:
        return pl.BlockSpec((None, 1, 1, d), lambda i, j: (slot, i // tps, 0, 0))
    return pl.BlockSpec((None, nb, 1, d), lambda i, j: (slot, 0, 0, 0))


def _ffn_kernel(*refs, nb, tt, n_ff, last_cols, n_out, final):
    if final:
        (x_ref, shift_ref, scale_ref, gate_ref, g_ref, wg_ref, wu_ref, wo_ref, fg_ref,
         o_ref, h_scr, a_scr) = refs
    else:
        (x_ref, shift_ref, scale_ref, gate_ref, g_ref, wg_ref, wu_ref, wo_ref,
         o_ref, h_scr, a_scr) = refs
    j = pl.program_id(1)
    d = x_ref.shape[-1]
    tm = nb * tt
    tf = wg_ref.shape[1]
    tn = wo_ref.shape[1]

    rows = tm // FFN_ROW_SPLIT
    normed_rows = functools.partial(_normed_rows, x_ref, shift_ref, scale_ref, g_ref, h_scr, FFN_ROW_SPLIT)

    def hidden_chunk(rows_of, cols=tf):
        def products(r):
            hr = rows_of(r)
            return (jnp.dot(hr, wg_ref[:, :cols], preferred_element_type=F32),
                    jnp.dot(hr, wu_ref[:, :cols], preferred_element_type=F32))

        parts = []
        cur = products(0)
        for r in range(FFN_ROW_SPLIT):
            nxt = products(r + 1) if r + 1 < FFN_ROW_SPLIT else None
            gg, uu = cur
            parts.append((gg * jax.nn.sigmoid(gg) * uu).astype(BF16))
            cur = nxt
        a_scr[j, :, :cols] = jnp.concatenate(parts, axis=0)

    kept_rows = lambda r: h_scr[r * rows:(r + 1) * rows, :]
    n_full = n_ff if last_cols == tf else n_ff - 1
    assert n_full >= 1

    @pl.when(j == 0)
    def _():
        hidden_chunk(normed_rows)

    @pl.when(jnp.logical_and(j > 0, j < n_full))
    def _():
        hidden_chunk(kept_rows)

    if n_full < n_ff:
        @pl.when(j == n_ff - 1)
        def _():
            hidden_chunk(kept_rows, last_cols)

    @pl.when(j >= n_ff)
    def _():
        a = jnp.concatenate([a_scr[f] for f in range(n_full)] + [a_scr[f, :, :last_cols] for f in range(n_full, n_ff)],
                            axis=1)
        acc = jnp.dot(a, wo_ref[...], preferred_element_type=F32).reshape(nb, tt, tn)
        for n in range(n_out):
            @pl.when(j == n_ff + n)
            def _(n=n):
                cs = slice(n * tn, (n + 1) * tn)
                o_ref[:, :, cs] = x_ref[:, :, cs] + 0.5 * gate_ref[:, :, cs] * acc

    if final:
        @pl.when(j == n_ff + n_out - 1)
        def _():
            y = o_ref[...]
            o_ref[...] = y * lax.rsqrt(jnp.mean(y * y, axis=-1, keepdims=True) + EPS) * fg_ref[...]


def _ffn(x, mods, slots, g, wg_p, wu_p, w_out_p, final_g=None):
    b, t, d = x.shape
    nb, tt, tps, nrows = _row_blocking(x)
    ff = w_out_p.shape[0]
    n_ff = -(-ff // FF_TILE)
    last_cols = ff - (n_ff - 1) * FF_TILE
    tn = math.gcd(d, FFN_OUT_TILE)
    n_out = d // tn
    final = final_g is not None
    in_specs = [
        pl.BlockSpec((nb, tt, d), _x_index(nb, tps)),
        _mod_spec(nb, tps, d, slots[0]),
        _mod_spec(nb, tps, d, slots[1]),
        _mod_spec(nb, tps, d, slots[2]),
        pl.BlockSpec((1, d), lambda i, j: (0, 0)),
        pl.BlockSpec((d, FF_TILE), lambda i, j: (0, jnp.minimum(j, n_ff - 1))),
        pl.BlockSpec((d, FF_TILE), lambda i, j: (0, jnp.minimum(j, n_ff - 1))),
        pl.BlockSpec((ff, tn), lambda i, j: (0, jnp.maximum(j - n_ff, 0))),
    ]
    args = [x, mods, mods, mods, g.reshape(1, d), wg_p, wu_p, w_out_p]
    if final:
        in_specs.append(pl.BlockSpec((1, d), lambda i, j: (0, 0)))
        args.append(final_g.reshape(1, d))
    return pl.pallas_call(
        functools.partial(_ffn_kernel, nb=nb, tt=tt, n_ff=n_ff, last_cols=last_cols, n_out=n_out, final=final),
        grid=(nrows, n_ff + n_out),
        in_specs=in_specs,
        out_specs=pl.BlockSpec((nb, tt, d), _x_index(nb, tps)),
        out_shape=jax.ShapeDtypeStruct((b, t, d), F32),
        scratch_shapes=[pltpu.VMEM((nb * tt, d), BF16), pltpu.VMEM((n_ff, nb * tt, FF_TILE), BF16)],
        compiler_params=_params("parallel", "arbitrary"),
        name="ffn",
    )(*args)


def _prep_ffn_weights(w_in, w_out):
    ff = w_out.shape[0]
    return w_in[:, :ff].astype(BF16), w_in[:, ff:].astype(BF16), w_out.astype(BF16)


def _log_sigmoid(x):
    return jnp.minimum(x, 0.0) - jnp.log1p(jnp.exp(-jnp.abs(x)))


def _proj_kernel(*refs, nb, tt, tn, tps, taps, has_f, vt_cols, vt_tile):
    x_ref, shift_ref, scale_ref, g_ref, w_ref, cs_ref = refs[:6]
    pos = 6
    if has_f:
        wf_ref, bf_ref = refs[6:8]
        pos = 8
    obf_ref = refs[pos]
    tap_refs = refs[pos + 1:pos + 1 + len({tp[0] for tp in taps})]
    pos = pos + 1 + len(tap_refs)
    if vt_cols:
        vt_ref = refs[pos]
        pos += 1
    if has_f:
        logf_ref = refs[pos]
        pos += 1
    h_scr = refs[pos]
    i = pl.program_id(0)
    j = pl.program_id(1)
    d = x_ref.shape[-1]

    def store_attention_copy(y):
        obf_ref[...] = (y * cs_ref[...]).astype(BF16).reshape(nb, tt, tn)

    assert all(tp[1] > 0 for tp in taps) and all(vc[0] > 0 for vc in vt_cols)

    @pl.when(j == 0)
    def _():
        hs = []
        ys = []
        for r in range(PROJ_ROW_SPLIT):
            hs.append(_normed_rows(x_ref, shift_ref, scale_ref, g_ref, h_scr, PROJ_ROW_SPLIT, r))
            ys.append(jnp.dot(hs[-1], w_ref[...], preferred_element_type=F32))
        store_attention_copy(jnp.concatenate(ys, axis=0))
        if has_f:
            fr = lax.dot_general(wf_ref[...], jnp.concatenate(hs, axis=0), _NT, preferred_element_type=F32)
            logf_ref[...] = _log_sigmoid(fr[:logf_ref.shape[0]] + bf_ref[...])

    @pl.when(j > 0)
    def _():
        y = jnp.dot(h_scr[...], w_ref[...], preferred_element_type=F32)
        store_attention_copy(y)
        _proj_taps(y, i, j, tap_refs, vt_ref if vt_cols else None, nb=nb, tt=tt, tn=tn, tps=tps, taps=taps,
                   vt_cols=vt_cols, vt_tile=vt_tile)


def _proj_taps(y, i, j, tap_refs, vt_ref, *, nb, tt, tn, tps, taps, vt_cols, vt_tile):
    for out_idx, jval, col0, last_rows_only in taps:
        cond = j == jval
        if last_rows_only and tps > 1:
            cond = jnp.logical_and(cond, i % tps == tps - 1)

        @pl.when(cond)
        def _(out_idx=out_idx, col0=col0):
            tap_refs[out_idx][:, :, col0:col0 + tn] = y.reshape(nb, tt, tn)

    for jval, head0 in vt_cols:
        @pl.when(j == jval)
        def _(head0=head0):
            for hh in range(tn // HEAD_DIM):
                yt = y[:, hh * HEAD_DIM:(hh + 1) * HEAD_DIM].T.astype(BF16)
                yt = jnp.concatenate([yt, jnp.ones((VT_ROWS - HEAD_DIM, yt.shape[1]), BF16)], axis=0)
                for s in range(tt // vt_tile):
                    vt_ref[0, head0 + hh, s] = yt[:, s * vt_tile:(s + 1) * vt_tile]


def _seg_taps(tn, width, segments, last_rows_only):
    per = width // tn
    assert per * tn == width
    return [(o, sidx * per + s, s * tn, last_rows_only) for o, sidx in enumerate(segments) for s in range(per)]


def _vt_cols(tn, width, segments):
    per = width // tn
    hpb = tn // HEAD_DIM
    return [(sidx * per + s, (o * per + s) * hpb) for o, sidx in enumerate(segments) for s in range(per)]


def _proj(x, mods, slots, g, w_bf, col_scale, tn, tap_shapes, taps, wf=None, bf=None, vt_cols=(), vt_tile=None):
    b, t, d = x.shape
    n = w_bf.shape[1]
    nb, tt, tps, nrows = _row_blocking(x)
    has_f = wf is not None
    assert not vt_cols or nb == 1
    in_specs = [
        pl.BlockSpec((nb, tt, d), _x_index(nb, tps)),
        _mod_spec(nb, tps, d, slots[0]),
        _mod_spec(nb, tps, d, slots[1]),
        pl.BlockSpec((1, d), lambda i, j: (0, 0)),
        pl.BlockSpec((d, tn), lambda i, j: (0, j)),
        pl.BlockSpec((1, tn), lambda i, j: (0, j)),
    ]
    args = [x, mods, mods, g.reshape(1, d), w_bf, jnp.asarray(col_scale, F32).reshape(1, n)]
    if has_f:
        in_specs += [pl.BlockSpec(wf.shape, lambda i, j: (0, 0)), pl.BlockSpec(bf.shape, lambda i, j: (0, 0))]
        args += [wf, bf]
    xi = _x_index(nb, tps)
    out_specs = [pl.BlockSpec((nb, tt, tn), lambda i, j: xi(i, j)[:2] + (j,))]
    out_shape = [jax.ShapeDtypeStruct((b, t, n), BF16)]
    for shp, follows_rows in tap_shapes:
        if follows_rows:
            out_specs.append(pl.BlockSpec((nb, tt, shp[2]), xi))
        elif nb == 1:
            out_specs.append(pl.BlockSpec((1, shp[1], shp[2]), lambda i, j: (i // tps, 0, 0)))
        else:
            out_specs.append(pl.BlockSpec(shp, lambda i, j: (0, 0, 0)))
        out_shape.append(jax.ShapeDtypeStruct(shp, F32))
    if vt_cols:
        n_vh = len(vt_cols) * (tn // HEAD_DIM)
        out_specs.append(pl.BlockSpec((1, n_vh, tt // vt_tile, VT_ROWS, vt_tile), lambda i, j: (i // tps, 0, i % tps, 0, 0)))
        out_shape.append(jax.ShapeDtypeStruct((b, n_vh, t // vt_tile, VT_ROWS, vt_tile), BF16))
    if has_f:
        nf = bf.shape[0]
        out_specs.append(pl.BlockSpec((nf, nb * tt), lambda i, j: (0, i)))
        out_shape.append(jax.ShapeDtypeStruct((nf, b * t), F32))
    return pl.pallas_call(
        functools.partial(_proj_kernel, nb=nb, tt=tt, tn=tn, tps=tps, taps=tuple(taps), has_f=has_f,
                          vt_cols=tuple(vt_cols), vt_tile=vt_tile),
        grid=(nrows, n // tn),
        in_specs=in_specs,
        out_specs=out_specs,
        out_shape=out_shape,
        scratch_shapes=[pltpu.VMEM((nb * tt, d), BF16)],
        compiler_params=_params("arbitrary", "arbitrary"),
        name="mixer_in_proj",
    )(*args)


def _outproj_kernel(*refs, nb, tt, n_in):
    o_refs = refs[:n_in]
    w_ref, x_ref, gate_ref, out_ref = refs[n_in:]
    d = x_ref.shape[-1]
    acc = None
    row0 = 0
    for o_ref in o_refs:
        wd = o_ref.shape[-1]
        part = jnp.dot(o_ref[...].reshape(nb * tt, wd), w_ref[row0:row0 + wd, :], preferred_element_type=F32)
        acc = part if acc is None else acc + part
        row0 += wd
    out_ref[...] = x_ref[...] + gate_ref[...] * acc.reshape(nb, tt, d)


def _outproj(o_list, w_bf, x, mods, gate_slot):
    b, t, d = x.shape
    nb, tt, tps, nrows = _row_blocking(x)
    xi = _x_index(nb, tps)
    in_specs = [pl.BlockSpec((nb, tt, o.shape[-1]), xi) for o in o_list]
    in_specs += [
        pl.BlockSpec(w_bf.shape, lambda i, j: (0, 0)),
        pl.BlockSpec((nb, tt, d), xi),
        _mod_spec(nb, tps, d, gate_slot),
    ]
    return pl.pallas_call(
        functools.partial(_outproj_kernel, nb=nb, tt=tt, n_in=len(o_list)),
        grid=(nrows, 1),
        in_specs=in_specs,
        out_specs=pl.BlockSpec((nb, tt, d), xi),
        out_shape=jax.ShapeDtypeStruct((b, t, d), F32),
        compiler_params=_params("parallel", "arbitrary"),
        name="mixer_out_proj",
    )(*o_list, w_bf, x, mods)


def _cumsum_kernel(x_ref, c_ref, o_ref):
    x = x_ref[0]
    n = x.shape[1]
    lane = lax.broadcasted_iota(jnp.int32, x.shape, 1)
    s = 1
    while s < n:
        x = x + jnp.where(lane >= s, pltpu.roll(x, s, axis=1), 0.0)
        s *= 2
    o_ref[0] = x + c_ref[0][:, :1]


def _cumsum(x, carry):
    g, h, n = x.shape
    return pl.pallas_call(
        _cumsum_kernel,
        grid=(g,),
        in_specs=[pl.BlockSpec((1, h, n), lambda i: (i, 0, 0)), pl.BlockSpec((1, h, LANES), lambda i: (i, 0, 0))],
        out_specs=pl.BlockSpec((1, h, n), lambda i: (i, 0, 0)),
        out_shape=jax.ShapeDtypeStruct((g, h, n), F32),
        compiler_params=_params("arbitrary"),
        name="logf_cumsum",
    )(x, carry)


def _softmax_init(m_scr, acc_scr):
    m_scr[...] = jnp.full(m_scr.shape, NEG_INF, F32)
    acc_scr[...] = jnp.zeros(acc_scr.shape, F32)


def _softmax_update_t(s, s_max, vt, m_prev, acc_prev):
    m_new = jnp.maximum(m_prev, s_max)
    alpha = jnp.exp2(m_prev - m_new)
    p = jnp.exp2(s - m_new)
    acc_new = alpha * acc_prev + jnp.dot(vt, p.astype(BF16), preferred_element_type=F32)
    return m_new, acc_new


def _normalised_t(acc):
    return (acc[:HEAD_DIM] / acc[HEAD_DIM:HEAD_DIM + 1]).T


def _tile_stage(slot, vt_of, m_scr, acc_scr, s_scr, smax_scr, n_groups, next_scores=None):
    g = m_scr.shape[1] // n_groups
    sls = [slice(i * g, (i + 1) * g) for i in range(n_groups)]
    state = [(m_scr[:, sl], acc_scr[:, sl]) for sl in sls]
    new = []
    for i, sl in enumerate(sls):
        if next_scores is not None:
            s = next_scores(i)
            s_scr[1 - slot, :, sl] = s
            smax_scr[1 - slot, :, sl] = jnp.max(s, axis=0, keepdims=True)
        new.append(_softmax_update_t(s_scr[slot, :, sl], smax_scr[slot, :, sl], vt_of(i), *state[i]))
    for sl, (m_new, acc_new) in zip(sls, new):
        m_scr[:, sl] = m_new
        acc_scr[:, sl] = acc_new


def _first_tile(scores, s_scr, smax_scr, n_groups):
    g = s_scr.shape[2] // n_groups
    for i in range(n_groups):
        sl = slice(i * g, (i + 1) * g)
        s = scores(i)
        s_scr[0, :, sl] = s
        smax_scr[0, :, sl] = jnp.max(s, axis=0, keepdims=True)


def _pipelined_tiles(last, n_special, produce_first, stage):
    def kind(dist):
        return dist if dist < n_special else "far"

    for k in range(n_special):
        @pl.when(last == k)
        def _(k=k):
            produce_first(kind(k))

    @pl.when(last >= n_special)
    def _():
        produce_first("far")

    n_pairs = jnp.maximum(last - n_special, 0) // 2

    def pair(i, carry):
        stage(2 * i, 0, "far")
        stage(2 * i + 1, 1, "far")
        return carry

    lax.fori_loop(0, n_pairs, pair, 0)
    t0 = 2 * n_pairs
    remaining = last - t0 + 1
    for r in range(1, n_special + 3):
        @pl.when(remaining == r)
        def _(r=r):
            for u in range(r):
                stage(t0 + u, u % 2, kind(r - 2 - u) if u + 1 < r else None)


def _toeplitz_values(tbl_ref, idx, n_vals, nh, h, ref_row):
    ref_val = tbl_ref[ref_row * nh + h]

    def body(v, acc):
        return jnp.where(idx == v, tbl_ref[v * nh + h] - ref_val, acc)

    return lax.fori_loop(0, n_vals, body, jnp.zeros(idx.shape, F32))


def _stack_diff_queries(q):
    lane = lax.broadcasted_iota(jnp.int32, q.shape, 1)
    zero = jnp.zeros_like(q)
    qs = jnp.concatenate([jnp.where(lane < DA_HALF, q, zero), jnp.where(lane >= DA_HALF, q, zero)], axis=0)
    return qs


def _diff_finalize(o, lam, g, out_scale):
    tq = o.shape[0] // 2
    od = o[:tq] - lam * o[tq:]
    return od * lax.rsqrt(jnp.mean(od * od, axis=-1, keepdims=True) + EPS) * g * out_scale


def _diff_p_kernel(lam_ref, tbl_ref, q_ref, k_ref, vt_ref, idx_ref, base_ref, g_ref, o_ref,
                   m_scr, acc_scr, bias_scr, s_scr, smax_scr, *, tq, nh, hps, far_bucket, out_scale):
    hg = pl.program_id(0)
    bi = pl.program_id(1)
    qi = pl.program_id(2)
    nblk = tq // LANES

    @pl.when(jnp.logical_and(bi == 0, qi == 0))
    def _():
        for e in range(hps):
            t = _toeplitz_values(tbl_ref, idx_ref[...], N_T5_BUCKETS, nh, hg * hps + e, far_bucket) * LOG2E
            bias_scr[e] = base_ref[...]
            for blk in range(nblk):
                sl = slice(blk * LANES, (blk + 1) * LANES)
                bias_scr[e, 0, sl, sl] += t[0]
                if blk >= 1:
                    bias_scr[e, 0, (blk - 1) * LANES:blk * LANES, sl] += t[1]
            bias_scr[e, 1, (nblk - 1) * LANES:, :LANES] += t[1]

    qs = [_stack_diff_queries(q_ref[0, :, e * HEAD_DIM:(e + 1) * HEAD_DIM]) for e in range(hps)]
    _softmax_init(m_scr, acc_scr)
    gph = 2 * tq // ATTN_GROUP
    ngrp = hps * gph

    def scores(j, kind):
        start = pl.multiple_of(j * tq, tq)

        def group(i):
            e, gi = divmod(i, gph)
            k = k_ref[0, pl.ds(start, tq), e * HEAD_DIM:(e + 1) * HEAD_DIM]
            s = lax.dot_general(k, qs[e][gi * ATTN_GROUP:(gi + 1) * ATTN_GROUP], _NT, preferred_element_type=F32)
            if kind != "far":
                q0 = (gi * ATTN_GROUP) % tq
                s = s + bias_scr[e, kind, :, q0:q0 + ATTN_GROUP]
            return s
        return group

    def stage(t, slot, next_kind):
        nxt = None if next_kind is None else scores(t + 1, next_kind)
        _tile_stage(slot, lambda i: vt_ref[0, i // gph, t], m_scr, acc_scr, s_scr, smax_scr, ngrp, nxt)

    _pipelined_tiles(qi, 2, lambda kind: _first_tile(scores(0, kind), s_scr, smax_scr, ngrp), stage)
    outs = []
    for e in range(hps):
        o = _normalised_t(acc_scr[:, e * 2 * tq:(e + 1) * 2 * tq])
        outs.append(_diff_finalize(o, lam_ref[0], g_ref[...], out_scale).astype(BF16))
    o_ref[0] = jnp.concatenate(outs, axis=1)


def _diff_prompt(lam, tbl, qkv, vt, g, n_heads, out_scale):
    b, t, _ = qkv.shape
    tq = vt.shape[-1]
    far_bucket = N_T5_BUCKETS // 2 - 1
    pos = np.arange(LANES)
    idx = np.stack([_t5_bucket_np(pos[:, None] - pos[None, :]),
                    _t5_bucket_np(pos[:, None] - pos[None, :] - LANES)]).astype(np.int32)
    kq = np.arange(tq)
    base = np.zeros((2, tq, tq), np.float32)
    base[0] = np.where((kq[:, None] // CHUNK) <= (kq[None, :] // CHUNK), 0.0, NEG_INF)
    hps = ATTN_HEADS
    ngrp = n_heads // hps
    assert ngrp * hps == n_heads
    wd = hps * HEAD_DIM
    lanes = hps * 2 * tq
    return pl.pallas_call(
        functools.partial(_diff_p_kernel, tq=tq, nh=n_heads, hps=hps, far_bucket=far_bucket, out_scale=out_scale),
        grid=(ngrp, b, t // tq),
        in_specs=[
            pl.BlockSpec(memory_space=pltpu.SMEM),
            pl.BlockSpec(memory_space=pltpu.SMEM),
            pl.BlockSpec((1, tq, wd), lambda h, bi, qi: (bi, qi, h)),
            pl.BlockSpec((1, t, wd), lambda h, bi, qi: (bi, 0, ngrp + h)),
            pl.BlockSpec((1, hps, t // tq, VT_ROWS, tq), lambda h, bi, qi: (bi, h, 0, 0, 0)),
            pl.BlockSpec(idx.shape, lambda h, bi, qi: (0, 0, 0)),
            pl.BlockSpec(base.shape, lambda h, bi, qi: (0, 0, 0)),
            pl.BlockSpec((1, HEAD_DIM), lambda h, bi, qi: (0, 0)),
        ],
        out_specs=pl.BlockSpec((1, tq, wd), lambda h, bi, qi: (bi, qi, h)),
        out_shape=jax.ShapeDtypeStruct((b, t, n_heads * HEAD_DIM), BF16),
        scratch_shapes=[pltpu.VMEM((1, lanes), F32), pltpu.VMEM((VT_ROWS, lanes), F32),
                        pltpu.VMEM((hps, 2, tq, tq), F32), pltpu.VMEM((2, tq, lanes), F32),
                        pltpu.VMEM((2, 1, lanes), F32)],
        compiler_params=_params("arbitrary", "arbitrary", "arbitrary"),
        name="diff_attn_prompt",
    )(lam, tbl, qkv, qkv, vt, jnp.asarray(idx), jnp.asarray(base), g)


def _fox_p_kernel(fref_ref, q_ref, k_ref, vt_ref, f_ref, o_ref, m_scr, acc_scr, fcol_scr, s_scr, smax_scr,
                  *, tq, hps):
    bi = pl.program_id(0)
    hg = pl.program_id(1)
    qi = pl.program_id(2)
    nblk = tq // LANES

    @pl.when(qi == 0)
    def _():
        for e in range(hps):
            for c in range(f_ref.shape[2] * nblk):
                row = f_ref[0, e, c // nblk][:, (c % nblk) * LANES:(c % nblk + 1) * LANES]
                fcol_scr[e, c * LANES:(c + 1) * LANES, :] = jnp.broadcast_to(row, (LANES, LANES)).T

    n_heads = pl.num_programs(1) * hps
    frefs = [fref_ref[(bi * n_heads + hg * hps + e) * pl.num_programs(2) + qi] for e in range(hps)]
    _softmax_init(m_scr, acc_scr)

    gph = tq // ATTN_GROUP
    ngrp = hps * gph

    def scores(j, kind):
        start = pl.multiple_of(j * tq, tq)

        def group(i):
            e, gi = divmod(i, gph)
            hs = slice(e * HEAD_DIM, (e + 1) * HEAD_DIM)
            decay = (frefs[e] - fcol_scr[e, pl.ds(start, tq), :]) * LOG2E
            decay = jnp.concatenate([decay] * (ATTN_GROUP // LANES), axis=1)
            s = lax.dot_general(k_ref[0, pl.ds(start, tq), hs], q_ref[0, gi * ATTN_GROUP:(gi + 1) * ATTN_GROUP, hs],
                                _NT, preferred_element_type=F32)
            s = s + decay
            if kind != "far":
                key = lax.broadcasted_iota(jnp.int32, s.shape, 0)
                qry = lax.broadcasted_iota(jnp.int32, s.shape, 1) + gi * ATTN_GROUP
                s = jnp.where(key <= qry, s, NEG_INF)
            return s
        return group

    def stage(t, slot, next_kind):
        nxt = None if next_kind is None else scores(t + 1, next_kind)
        _tile_stage(slot, lambda i: vt_ref[0, i // gph, t], m_scr, acc_scr, s_scr, smax_scr, ngrp, nxt)

    _pipelined_tiles(qi, 1, lambda kind: _first_tile(scores(0, kind), s_scr, smax_scr, ngrp), stage)
    o_ref[0] = jnp.concatenate(
        [_normalised_t(acc_scr[:, e * tq:(e + 1) * tq]).astype(BF16) for e in range(hps)], axis=1)


def _fox_prompt(fref, qkv, vt, fcum, n_heads, col0, vt_head0):
    b, t, _ = qkv.shape
    tq = vt.shape[-1]
    nq = t // tq
    hps = ATTN_HEADS
    ngrp = n_heads // hps
    assert ngrp * hps == n_heads and col0 % hps == 0 and vt_head0 % hps == 0
    wd = hps * HEAD_DIM
    lanes = hps * tq
    return pl.pallas_call(
        functools.partial(_fox_p_kernel, tq=tq, hps=hps),
        grid=(b, ngrp, nq),
        in_specs=[
            pl.BlockSpec(memory_space=pltpu.SMEM),
            pl.BlockSpec((1, tq, wd), lambda bi, h, qi: (bi, qi, col0 // hps + h)),
            pl.BlockSpec((1, t, wd), lambda bi, h, qi: (bi, 0, (col0 + n_heads) // hps + h)),
            pl.BlockSpec((1, hps, nq, VT_ROWS, tq), lambda bi, h, qi: (bi, vt_head0 // hps + h, 0, 0, 0)),
            pl.BlockSpec((1, hps, nq, 1, tq), lambda bi, h, qi: (bi, h, 0, 0, 0)),
        ],
        out_specs=pl.BlockSpec((1, tq, wd), lambda bi, h, qi: (bi, qi, h)),
        out_shape=jax.ShapeDtypeStruct((b, t, n_heads * HEAD_DIM), BF16),
        scratch_shapes=[pltpu.VMEM((1, lanes), F32), pltpu.VMEM((VT_ROWS, lanes), F32),
                        pltpu.VMEM((hps, t, LANES), F32), pltpu.VMEM((2, tq, lanes), F32),
                        pltpu.VMEM((2, 1, lanes), F32)],
        compiler_params=_params("arbitrary", "arbitrary", "arbitrary"),
        name="fox_attn_prompt",
    )(fref, qkv, qkv, vt, fcum)


def _joint_softmax_pv(s_c, s_n, vc, vn):
    m = jnp.maximum(jnp.max(s_c, axis=1, keepdims=True), jnp.max(s_n, axis=1, keepdims=True))
    p_c = jnp.exp2(s_c - m)
    p_n = jnp.exp2(s_n - m)
    l = jnp.sum(p_c, axis=1, keepdims=True) + jnp.sum(p_n, axis=1, keepdims=True)
    acc = jnp.dot(p_c.astype(BF16), vc, preferred_element_type=F32)
    acc = acc + jnp.dot(p_n.astype(BF16), vn, preferred_element_type=F32)
    return acc / l


def _diff_s_kernel(lam_ref, q_ref, kn_ref, vn_ref, kc_ref, vc_ref, bnear_ref, bnew_ref, g_ref, o_ref, *, near, out_scale):
    qs = _stack_diff_queries(q_ref[0])
    kc = kc_ref[0].astype(BF16)
    vc = vc_ref[0].astype(BF16)
    p = kc.shape[0]
    s_c = lax.dot_general(qs, kc, _NT, preferred_element_type=F32)
    bnear = bnear_ref[0]
    s_c = jnp.concatenate([s_c[:, :p - near], s_c[:, p - near:] + jnp.concatenate([bnear, bnear], axis=0)], axis=1)
    bnew = bnew_ref[0]
    s_n = lax.dot_general(qs, kn_ref[0], _NT, preferred_element_type=F32) + jnp.concatenate([bnew, bnew], axis=0)
    o = _joint_softmax_pv(s_c, s_n, vc, vn_ref[0])
    o_ref[0] = _diff_finalize(o, lam_ref[0], g_ref[...], out_scale).astype(BF16)


def _diff_sample(lam, qkv, kc, vc, bnear, bnew, g, n_heads, out_scale):
    b, t, _ = qkv.shape
    p = kc.shape[1]
    near = bnear.shape[-1]
    hd = HEAD_DIM
    return pl.pallas_call(
        functools.partial(_diff_s_kernel, near=near, out_scale=out_scale),
        grid=(b, n_heads),
        in_specs=[
            pl.BlockSpec(memory_space=pltpu.SMEM),
            pl.BlockSpec((1, t, hd), lambda bi, h: (bi, 0, h)),
            pl.BlockSpec((1, t, hd), lambda bi, h: (bi, 0, n_heads + h)),
            pl.BlockSpec((1, t, hd), lambda bi, h: (bi, 0, 2 * n_heads + h)),
            pl.BlockSpec((1, p, hd), lambda bi, h: (bi, 0, h)),
            pl.BlockSpec((1, p, hd), lambda bi, h: (bi, 0, h)),
            pl.BlockSpec((1, t, near), lambda bi, h: (h, 0, 0)),
            pl.BlockSpec((1, t, t), lambda bi, h: (h, 0, 0)),
            pl.BlockSpec((1, hd), lambda bi, h: (0, 0)),
        ],
        out_specs=pl.BlockSpec((1, t, hd), lambda bi, h: (bi, 0, h)),
        out_shape=jax.ShapeDtypeStruct((b, t, n_heads * hd), BF16),
        compiler_params=_params("parallel", "parallel"),
        name="diff_attn_sample",
    )(lam, qkv, qkv, qkv, kc, vc, bnear, bnew, g)


def _fox_s_kernel(fref_ref, q_ref, kn_ref, vn_ref, kc_ref, vc_ref, fc_ref, fn_ref, o_ref):
    bi = pl.program_id(0)
    h = pl.program_id(1)
    q = q_ref[0]
    t = q.shape[0]
    fref = fref_ref[bi * pl.num_programs(1) + h]
    s_c = lax.dot_general(q, kc_ref[0].astype(BF16), _NT, preferred_element_type=F32) + (fref - fc_ref[0, 0]) * LOG2E
    s_n = lax.dot_general(q, kn_ref[0], _NT, preferred_element_type=F32) + (fref - fn_ref[0, 0][:, :t]) * LOG2E
    row = lax.broadcasted_iota(jnp.int32, s_n.shape, 0)
    col = lax.broadcasted_iota(jnp.int32, s_n.shape, 1)
    s_n = jnp.where(col <= row, s_n, NEG_INF)
    o_ref[0] = _joint_softmax_pv(s_c, s_n, vc_ref[0].astype(BF16), vn_ref[0]).astype(BF16)


def _fox_sample(fref, qkv, kc, vc, fc, fn, n_heads, col0):
    b, t, _ = qkv.shape
    p = kc.shape[1]
    hd = HEAD_DIM
    return pl.pallas_call(
        _fox_s_kernel,
        grid=(b, n_heads),
        in_specs=[
            pl.BlockSpec(memory_space=pltpu.SMEM),
            pl.BlockSpec((1, t, hd), lambda bi, h: (bi, 0, col0 + h)),
            pl.BlockSpec((1, t, hd), lambda bi, h: (bi, 0, col0 + n_heads + h)),
            pl.BlockSpec((1, t, hd), lambda bi, h: (bi, 0, col0 + 2 * n_heads + h)),
            pl.BlockSpec((1, p, hd), lambda bi, h: (bi, 0, h)),
            pl.BlockSpec((1, p, hd), lambda bi, h: (bi, 0, h)),
            pl.BlockSpec((1, 1, 1, p), lambda bi, h: (bi, h, 0, 0)),
            pl.BlockSpec((1, 1, 1, fn.shape[-1]), lambda bi, h: (bi, h, 0, 0)),
        ],
        out_specs=pl.BlockSpec((1, t, hd), lambda bi, h: (bi, 0, h)),
        out_shape=jax.ShapeDtypeStruct((b, t, n_heads * hd), BF16),
        compiler_params=_params("parallel", "parallel"),
        name="fox_attn_sample",
    )(fref, qkv, qkv, qkv, kc, vc, fc, fn)


def _band_p_kernel(tbl_ref, q_ref, k_ref, vt_ref, idx_ref, base_ref, o_ref, bm_scr, *, tq, nkb, nh, hps):
    hg = pl.program_id(0)
    bi = pl.program_id(1)
    i = pl.program_id(2)
    nblk = tq // LANES
    qoff = (nkb - 1) * nblk

    @pl.when(jnp.logical_and(bi == 0, i == 0))
    def _():
        for e in range(hps):
            t = _toeplitz_values(tbl_ref, idx_ref[...], 2 * REL_CLIP + 1, nh, hg * hps + e, 0)
            bm_scr[e] = base_ref[...]
            for bq in range(nblk):
                kd = qoff + bq
                bm_scr[e, kd * LANES:(kd + 1) * LANES, bq * LANES:(bq + 1) * LANES] += t[0] * LOG2E
                bm_scr[e, (kd - 1) * LANES:kd * LANES, bq * LANES:(bq + 1) * LANES] += t[1] * LOG2E

    def head_scores(e):
        hs = slice(e * HEAD_DIM, (e + 1) * HEAD_DIM)
        q = q_ref[0, :, hs]
        ss = []
        for blk in range(nkb):
            jb = i - (nkb - 1) + blk
            jc = jnp.maximum(jb, 0)
            k = k_ref[0, pl.ds(pl.multiple_of(jc * tq, tq), tq), hs]
            s = lax.dot_general(k, q, _NT, preferred_element_type=F32) + bm_scr[e, blk * tq:(blk + 1) * tq, :]
            if blk < nkb - 1:
                s = s + jnp.where(jb < 0, NEG_INF, 0.0)
            ss.append(s)
        return ss, functools.reduce(jnp.maximum, [jnp.max(s, axis=0, keepdims=True) for s in ss])

    outs = []
    nxt = head_scores(0)
    for e in range(hps):
        ss, m = nxt
        if e + 1 < hps:
            nxt = head_scores(e + 1)
        acc = None
        for blk, s in enumerate(ss):
            jc = jnp.maximum(i - (nkb - 1) + blk, 0)
            a = jnp.dot(vt_ref[0, e, jc], jnp.exp2(s - m).astype(BF16), preferred_element_type=F32)
            acc = a if acc is None else acc + a
        outs.append(_normalised_t(acc).astype(BF16))
    o_ref[0] = jnp.concatenate(outs, axis=1)


def _band_prompt(tbl, qkv, vt, n_heads):
    b, t, _ = qkv.shape
    tq = vt.shape[-1]
    nkb = C_BAND // tq + 1
    assert (nkb - 1) * tq == C_BAND and tq % LANES == 0 and LANES >= REL_CLIP
    pos = np.arange(LANES)
    rel = pos[:, None] - pos[None, :]
    idx = (np.stack([np.clip(rel, -REL_CLIP, REL_CLIP), np.clip(rel - LANES, -REL_CLIP, REL_CLIP)]) + REL_CLIP).astype(np.int32)
    k_pos = np.arange(nkb * tq)[:, None]
    q_pos = C_BAND + np.arange(tq)[None, :]
    kc, qc = k_pos // CHUNK, q_pos // CHUNK
    base = np.where((kc <= qc) & (qc - kc <= C_PREV_CHUNKS), 0.0, NEG_INF).astype(np.float32)
    hps = BAND_HEADS
    ngrp = n_heads // hps
    assert ngrp * hps == n_heads
    wd = hps * HEAD_DIM
    return pl.pallas_call(
        functools.partial(_band_p_kernel, tq=tq, nkb=nkb, nh=n_heads, hps=hps),
        grid=(ngrp, b, t // tq),
        in_specs=[
            pl.BlockSpec(memory_space=pltpu.SMEM),
            pl.BlockSpec((1, tq, wd), lambda h, bi, i: (bi, i, h)),
            pl.BlockSpec((1, t, wd), lambda h, bi, i: (bi, 0, ngrp + h)),
            pl.BlockSpec((1, hps, t // tq, VT_ROWS, tq), lambda h, bi, i: (bi, h, 0, 0, 0)),
            pl.BlockSpec(idx.shape, lambda h, bi, i: (0, 0, 0)),
            pl.BlockSpec(base.shape, lambda h, bi, i: (0, 0)),
        ],
        out_specs=pl.BlockSpec((1, tq, wd), lambda h, bi, i: (bi, i, h)),
        out_shape=jax.ShapeDtypeStruct((b, t, n_heads * HEAD_DIM), BF16),
        scratch_shapes=[pltpu.VMEM((hps, nkb * tq, tq), F32)],
        compiler_params=_params("arbitrary", "arbitrary", "arbitrary"),
        name="band_attn_prompt",
    )(tbl, qkv, qkv, vt, jnp.asarray(idx), jnp.asarray(base))


def _band_s_kernel(q_ref, kn_ref, vn_ref, kc_ref, vc_ref, bmc_ref, bmn_ref, o_ref, *, n_heads):
    outs = []
    for h in range(n_heads):
        hs = slice(h * HEAD_DIM, (h + 1) * HEAD_DIM)
        q = q_ref[0, :, hs]
        kc = kc_ref[0, :, h, :].astype(BF16)
        vc = vc_ref[0, :, h, :].astype(BF16)
        s_c = lax.dot_general(q, kc, _NT, preferred_element_type=F32) + bmc_ref[h]
        s_n = lax.dot_general(q, kn_ref[0, :, hs], _NT, preferred_element_type=F32) + bmn_ref[h]
        outs.append(_joint_softmax_pv(s_c, s_n, vc, vn_ref[0, :, hs]).astype(BF16))
    o_ref[0] = jnp.concatenate(outs, axis=1)


def _band_sample(qkv, kc, vc, bmc, bmn, n_heads):
    b, t, _ = qkv.shape
    lc = kc.shape[1]
    cw = n_heads * HEAD_DIM
    return pl.pallas_call(
        functools.partial(_band_s_kernel, n_heads=n_heads),
        grid=(b,),
        in_specs=[
            pl.BlockSpec((1, t, cw), lambda bi: (bi, 0, 0)),
            pl.BlockSpec((1, t, cw), lambda bi: (bi, 0, 1)),
            pl.BlockSpec((1, t, cw), lambda bi: (bi, 0, 2)),
            pl.BlockSpec((1, lc, n_heads, HEAD_DIM), lambda bi: (bi, 0, 0, 0)),
            pl.BlockSpec((1, lc, n_heads, HEAD_DIM), lambda bi: (bi, 0, 0, 0)),
            pl.BlockSpec(bmc.shape, lambda bi: (0, 0, 0)),
            pl.BlockSpec(bmn.shape, lambda bi: (0, 0, 0)),
        ],
        out_specs=pl.BlockSpec((1, t, cw), lambda bi: (bi, 0, 0)),
        out_shape=jax.ShapeDtypeStruct((b, t, cw), BF16),
        compiler_params=_params("parallel"),
        name="band_attn_sample",
    )(qkv, qkv, qkv, kc, vc, bmc, bmn)


def _t5_bucket_np(rel):
    nb = N_T5_BUCKETS // 2
    max_exact = nb // 2
    n = np.abs(rel)
    nf = np.maximum(n, 1).astype(np.float64)
    large = max_exact + (np.log(nf / max_exact) / math.log(T5_MAX_DIST / max_exact) * (nb - max_exact)).astype(np.int64)
    large = np.minimum(large, nb - 1)
    return np.where(rel > 0, nb, 0) + np.where(n < max_exact, n, large)


def _toeplitz(table, index_of_rel, q_pos, k_pos):
    r, c = len(q_pos), len(k_pos)
    assert (np.diff(q_pos) == 1).all() and (np.diff(k_pos) == 1).all()
    rels = (k_pos[0] - q_pos[0]) - (r - 1) + np.arange(r + c - 1)
    v = jnp.pad(table.astype(F32)[index_of_rel(rels)].T, ((0, 0), (0, 1)))
    x = jnp.tile(v, (1, r))[:, :r * (r + c - 1)].reshape(v.shape[0], r, r + c - 1)
    return x[:, :, r - 1:r - 1 + c]


def _t5_bias_tiles(t5_table, q_pos, k_pos, far_bucket):
    mask = (k_pos[None, :] // CHUNK) <= (q_pos[:, None] // CHUNK)
    bias = _toeplitz(t5_table, _t5_bucket_np, q_pos, k_pos)
    bias = (bias - t5_table.astype(F32)[far_bucket][:, None, None]) * LOG2E
    return jnp.where(mask[None], bias, NEG_INF)


def _band_bias_tiles(rel_table, q_pos, k_pos):
    qc = q_pos[:, None] // CHUNK
    kc = k_pos[None, :] // CHUNK
    mask = (kc <= qc) & (qc - kc <= C_PREV_CHUNKS) & (k_pos[None, :] >= 0)
    bias = _toeplitz(rel_table, lambda rel: np.clip(rel, -REL_CLIP, REL_CLIP) + REL_CLIP, q_pos, k_pos) * LOG2E
    return jnp.where(mask[None], bias, NEG_INF)


def kernel(x_prompt, x_sample, cache_a_k, cache_a_v, cache_b_k, cache_b_v, cache_b_logf, cache_c_k, cache_c_v, c_prompt, c_sample, w_ada, b_ada, norm_g, w_ffn_in, w_ffn_out, w_in_ab, b_forget, w_out_ab, lambda_q1, lambda_k1, lambda_q2, lambda_k2, subln_g, t5_table, w_in_c, w_out_c, c_rel_bias, final_g):
    depth = w_ada.shape[0]
    bsz, seq, d = x_prompt.shape
    dbsz, dseq, _ = x_sample.shape
    past = cache_b_logf.shape[2]
    h_a = cache_a_k.shape[3]
    h_b = cache_b_k.shape[3]
    h_c = cache_c_k.shape[3]
    wa, wb, cw = h_a * HEAD_DIM, h_b * HEAD_DIM, h_c * HEAD_DIM
    assert bsz == dbsz and wa == wb and 3 * wa + 3 * wb == 3 * cw
    assert LANES >= T5_MAX_DIST and SAMPLE_NEAR >= T5_MAX_DIST and past >= SAMPLE_NEAR
    far_bucket = (N_T5_BUCKETS // 2) - 1

    mods = _ada_mods(jnp.concatenate([c_prompt, c_sample], axis=0), w_ada, b_ada)
    mods = mods.reshape(depth, 2, bsz, 9, d).transpose(0, 3, 1, 2, 4).reshape(depth * 9 * 2, bsz, 1, d)

    def slot(l, k, grp):
        return (l * 9 + k) * 2 + grp

    xs = [x_prompt, x_sample]
    ab_states = [[], []]
    c_states = [[], []]
    for l in range(depth):
        i = l // 2
        last = l == depth - 1
        w1 = _prep_ffn_weights(w_ffn_in[l, 0], w_ffn_out[l, 0])
        w2 = _prep_ffn_weights(w_ffn_in[l, 1], w_ffn_out[l, 1])
        for grp in range(2):
            xs[grp] = _ffn(xs[grp], mods, [slot(l, k, grp) for k in range(3)], norm_g[l, 0], *w1)

        if l % 2 == 0:
            lam_init = 0.8 - 0.6 * math.exp(-0.3 * l)
            lam = (jnp.exp(jnp.sum(lambda_q1[i].astype(F32) * lambda_k1[i].astype(F32)))
                   - jnp.exp(jnp.sum(lambda_q2[i].astype(F32) * lambda_k2[i].astype(F32))) + lam_init).reshape(1)
            n_main = 3 * wa + 3 * wb
            w_main = w_in_ab[i][:, :n_main].astype(BF16)
            wf = jnp.pad(w_in_ab[i][:, n_main:].T.astype(BF16), ((0, 16 - h_b), (0, 0)))
            bfo = b_forget[i].astype(F32).reshape(h_b, 1)
            w_out = w_out_ab[i].astype(BF16)
            g_sub = subln_g[i].astype(F32).reshape(1, HEAD_DIM)
            tn = min(PROJ_COL_TILE, wa)
            taps = _seg_taps(tn, wa, (1, 2, 4, 5), False)
            cs = np.ones((n_main,), np.float32)
            cs[:wa] = DA_HALF ** -0.5 * LOG2E
            cs[3 * wa:3 * wa + wb] = HEAD_DIM ** -0.5 * LOG2E
            for grp in range(2):
                x = xs[grp]
                b, t, _ = x.shape
                tap_shapes = [((b, t, wa), True)] * 4
                if grp == 0:
                    qkv, ka, va, kb, vb, vt, logf_t = _proj(
                        x, mods, [slot(l, 3, grp), slot(l, 4, grp)], norm_g[l, 1], w_main, cs, tn, tap_shapes, taps,
                        wf, bfo, vt_cols=_vt_cols(tn, wa, (2, 5)), vt_tile=ROW_TILE)
                else:
                    qkv, ka, va, kb, vb, logf_t = _proj(
                        x, mods, [slot(l, 3, grp), slot(l, 4, grp)], norm_g[l, 1], w_main, cs, tn, tap_shapes, taps,
                        wf, bfo)
                logf = logf_t.reshape(h_b, b, t).transpose(1, 2, 0)
                ab_states[grp].append((ka.reshape(b, t, h_a, HEAD_DIM), va.reshape(b, t, h_a, HEAD_DIM),
                                       kb.reshape(b, t, h_b, HEAD_DIM), vb.reshape(b, t, h_b, HEAD_DIM), logf))
                logf_bh = logf_t.reshape(h_b, b, t).transpose(1, 0, 2)
                if grp == 0:
                    o_a = _diff_prompt(lam, t5_table.astype(F32).reshape(-1), qkv, vt, g_sub, h_a, 1.0 - lam_init)
                    fcum = _cumsum(logf_bh, jnp.zeros((b, h_b, LANES), F32))
                    nq = t // ROW_TILE
                    fref = fcum[:, :, ::ROW_TILE].reshape(-1)
                    o_b = _fox_prompt(fref, qkv, vt, fcum.reshape(b, h_b, nq, 1, ROW_TILE), h_b, 3 * h_a, h_a)
                else:
                    q_pos = past + np.arange(t)
                    bnear = _t5_bias_tiles(t5_table, q_pos, past - SAMPLE_NEAR + np.arange(SAMPLE_NEAR), far_bucket)
                    bnew = _t5_bias_tiles(t5_table, q_pos, q_pos, far_bucket)
                    kc = cache_a_k[i].reshape(b, past, wa)
                    vc = cache_a_v[i].reshape(b, past, wa)
                    o_a = _diff_sample(lam, qkv, kc, vc, bnear, bnew, g_sub, h_a, 1.0 - lam_init)
                    fc = _cumsum(cache_b_logf[i].astype(F32).transpose(0, 2, 1), jnp.zeros((b, h_b, LANES), F32))
                    carry = fc[:, :, past - 1:past]
                    fn = _cumsum(jnp.pad(logf_bh, ((0, 0), (0, 0), (0, LANES - t))),
                                 jnp.broadcast_to(carry, (b, h_b, LANES)))
                    o_b = _fox_sample(carry.reshape(-1), qkv, cache_b_k[i].reshape(b, past, wb),
                                      cache_b_v[i].reshape(b, past, wb), fc.reshape(b, h_b, 1, past),
                                      fn.reshape(b, h_b, 1, LANES), h_b, 3 * h_a)
                xs[grp] = _outproj([o_a, o_b], w_out, x, mods, slot(l, 5, grp))
        else:
            w_in = w_in_c[i].astype(BF16)
            w_out = w_out_c[i].astype(BF16)
            tn = min(PROJ_COL_TILE, cw)
            taps = _seg_taps(tn, cw, (1, 2), True)
            cs = np.ones((3 * cw,), np.float32)
            cs[:cw] = HEAD_DIM ** -0.5 * LOG2E
            for grp in range(2):
                x = xs[grp]
                b, t, _ = x.shape
                keep = min(C_BAND, t)
                assert keep == min(t, ROW_TILE)
                tap_shapes = [((b, keep, cw), False)] * 2
                if grp == 0:
                    qkv, k_new, v_new, vt = _proj(
                        x, mods, [slot(l, 3, grp), slot(l, 4, grp)], norm_g[l, 1], w_in, cs, tn, tap_shapes, taps,
                        vt_cols=_vt_cols(tn, cw, (2,)), vt_tile=BAND_TILE)
                else:
                    qkv, k_new, v_new = _proj(
                        x, mods, [slot(l, 3, grp), slot(l, 4, grp)], norm_g[l, 1], w_in, cs, tn, tap_shapes, taps)
                c_states[grp].append((k_new.reshape(b, keep, h_c, HEAD_DIM), v_new.reshape(b, keep, h_c, HEAD_DIM)))
                if grp == 0:
                    o_c = _band_prompt(c_rel_bias[i].astype(F32).reshape(-1), qkv, vt, h_c)
                else:
                    lc = cache_c_k.shape[2]
                    q_pos = past + np.arange(t)
                    bmc = _band_bias_tiles(c_rel_bias[i], q_pos, past - lc + np.arange(lc))
                    bmn = _band_bias_tiles(c_rel_bias[i], q_pos, q_pos)
                    o_c = _band_sample(qkv, cache_c_k[i], cache_c_v[i], bmc, bmn, h_c)
                xs[grp] = _outproj([o_c], w_out, x, mods, slot(l, 5, grp))

        for grp in range(2):
            xs[grp] = _ffn(xs[grp], mods, [slot(l, k, grp) for k in (6, 7, 8)], norm_g[l, 2], *w2,
                           final_g=final_g if last else None)

    outs = [xs[0], xs[1]]
    for grp in range(2):
        st = ab_states[grp]
        outs += [jnp.stack([s[k] for s in st]) for k in range(5)]
        st = c_states[grp]
        outs += [jnp.stack([s[k] for s in st]) for k in range(2)]
    return tuple(outs)
```

```python
import functools
import math

import numpy as np
import jax
import jax.numpy as jnp
from jax import lax
from jax.experimental import pallas as pl
from jax.experimental.pallas import tpu as pltpu

F32 = jnp.float32
BF16 = jnp.bfloat16

CHUNK = 64
HEAD_DIM = 128
DA_HALF = HEAD_DIM // 2
N_T5_BUCKETS = 32
T5_MAX_DIST = 128
C_PREV_CHUNKS = 8
C_BAND = C_PREV_CHUNKS * CHUNK
REL_CLIP = 128
EPS = 1e-6
NEG_INF = -1e30
LOG2E = math.log2(math.e)
VT_ROWS = HEAD_DIM + 16

VMEM_LIMIT_BYTES = 56 * 1024 * 1024
LANES = 128

ROW_TILE = 512
FF_TILE = 512
FFN_OUT_TILE = 512
FFN_ROW_SPLIT = 2
PROJ_ROW_SPLIT = 2
PROJ_COL_TILE = 1024
ADA_COL_TILE = 1024
BAND_TILE = 256
BAND_HEADS = 4
BAND_LOOKAHEAD = 3
ATTN_HEADS = 2
ATTN_GROUP = 512
SAMPLE_NEAR = 256

_NT = (((1,), (1,)), ((), ()))


def _params(*sem, flags=None):
    return pltpu.CompilerParams(dimension_semantics=sem, vmem_limit_bytes=VMEM_LIMIT_BYTES, flags=flags)


def _modulated_norm(x, g, shift, scale):
    y = x * lax.rsqrt(jnp.mean(x * x, axis=-1, keepdims=True) + EPS) * g
    return y * (1.0 + scale) + shift


def _normed_rows(x_ref, shift_ref, scale_ref, g_ref, h_scr, n_split, r):
    nb, tt, d = x_ref.shape
    rows = nb * tt // n_split
    if nb == 1:
        x, sh, sc = x_ref[:, r * rows:(r + 1) * rows, :], shift_ref[...], scale_ref[...]
    else:
        seqs = slice(r * (nb // n_split), (r + 1) * (nb // n_split))
        x, sh, sc = x_ref[seqs], shift_ref[seqs], scale_ref[seqs]
    hr = _modulated_norm(x, g_ref[...], sh, sc).reshape(rows, d).astype(BF16)
    h_scr[r * rows:(r + 1) * rows, :] = hr
    return hr


def _ada_kernel(c_ref, w_ref, b_ref, o_ref):
    c = c_ref[...]
    a = (c * jax.nn.sigmoid(c)).astype(BF16)
    o_ref[0] = jnp.dot(a, w_ref[0].astype(BF16), preferred_element_type=F32) + b_ref[0]


def _ada_mods(c_all, w_ada, b_ada):
    depth, d, n = w_ada.shape
    r = c_all.shape[0]
    tn = math.gcd(n, ADA_COL_TILE)
    return pl.pallas_call(
        _ada_kernel,
        grid=(depth, n // tn),
        in_specs=[
            pl.BlockSpec((r, d), lambda l, j: (0, 0)),
            pl.BlockSpec((1, d, tn), lambda l, j: (l, 0, j)),
            pl.BlockSpec((1, 1, tn), lambda l, j: (l, 0, j)),
        ],
        out_specs=pl.BlockSpec((1, r, tn), lambda l, j: (l, 0, j)),
        out_shape=jax.ShapeDtypeStruct((depth, r, n), F32),
        compiler_params=_params("arbitrary", "arbitrary"),
        name="ada_mods",
    )(c_all, w_ada, b_ada.reshape(depth, 1, n))


def _row_blocking(x):
    b, t, _ = x.shape
    if t >= ROW_TILE:
        assert t % ROW_TILE == 0
        return 1, ROW_TILE, t // ROW_TILE, b * (t // ROW_TILE)
    assert (b * t) % 8 == 0
    return b, t, 1, 1


def _x_index(nb, tps):
    if nb == 1:
        return lambda i, j: (i // tps, i % tps, 0)
    return lambda i, j: (0, 0, 0)


def _mod_spec(nb, tps, d, slot):
    if nb == 1:
        return pl.BlockSpec((None, 1, 1, d), lambda i, j: (slot, i // tps, 0, 0))
    return pl.BlockSpec((None, nb, 1, d), lambda i, j: (slot, 0, 0, 0))


def _ffn_kernel(*refs, nb, tt, n_ff, last_cols, n_out, final):
    if final:
        (x_ref, shift_ref, scale_ref, gate_ref, g_ref, wg_ref, wu_ref, wo_ref, fg_ref,
         o_ref, h_scr, a_scr) = refs
    else:
        (x_ref, shift_ref, scale_ref, gate_ref, g_ref, wg_ref, wu_ref, wo_ref,
         o_ref, h_scr, a_scr) = refs
    j = pl.program_id(1)
    d = x_ref.shape[-1]
    tm = nb * tt
    tf = wg_ref.shape[1]
    tn = wo_ref.shape[1]

    rows = tm // FFN_ROW_SPLIT
    normed_rows = functools.partial(_normed_rows, x_ref, shift_ref, scale_ref, g_ref, h_scr, FFN_ROW_SPLIT)

    def hidden_chunk(rows_of, cols=tf):
        def products(r):
            hr = rows_of(r)
            return (jnp.dot(hr, wg_ref[:, :cols], preferred_element_type=F32),
                    jnp.dot(hr, wu_ref[:, :cols], preferred_element_type=F32))

        parts = []
        cur = products(0)
        for r in range(FFN_ROW_SPLIT):
            nxt = products(r + 1) if r + 1 < FFN_ROW_SPLIT else None
            gg, uu = cur
            parts.append((gg * jax.nn.sigmoid(gg) * uu).astype(BF16))
            cur = nxt
        a_scr[j, :, :cols] = jnp.concatenate(parts, axis=0)

    kept_rows = lambda r: h_scr[r * rows:(r + 1) * rows, :]
    n_full = n_ff if last_cols == tf else n_ff - 1
    assert n_full >= 1

    @pl.when(j == 0)
    def _():
        hidden_chunk(normed_rows)

    @pl.when(jnp.logical_and(j > 0, j < n_full))
    def _():
        hidden_chunk(kept_rows)

    if n_full < n_ff:
        @pl.when(j == n_ff - 1)
        def _():
            hidden_chunk(kept_rows, last_cols)

    @pl.when(j >= n_ff)
    def _():
        a = jnp.concatenate([a_scr[f] for f in range(n_full)] + [a_scr[f, :, :last_cols] for f in range(n_full, n_ff)],
                            axis=1)
        acc = jnp.dot(a, wo_ref[...], preferred_element_type=F32).reshape(nb, tt, tn)
        for n in range(n_out):
            @pl.when(j == n_ff + n)
            def _(n=n):
                cs = slice(n * tn, (n + 1) * tn)
                y = x_ref[:, :, cs] + 0.5 * gate_ref[:, :, cs] * acc
                if final:
                    o_ref[:, :, cs] = y
                else:
                    o_ref[...] = y

    if final:
        @pl.when(j == n_ff + n_out - 1)
        def _():
            y = o_ref[...]
            o_ref[...] = y * lax.rsqrt(jnp.mean(y * y, axis=-1, keepdims=True) + EPS) * fg_ref[...]


def _ffn(x, mods, slots, g, wg_p, wu_p, w_out_p, final_g=None):
    b, t, d = x.shape
    nb, tt, tps, nrows = _row_blocking(x)
    ff = w_out_p.shape[0]
    n_ff = -(-ff // FF_TILE)
    last_cols = ff - (n_ff - 1) * FF_TILE
    tn = math.gcd(d, FFN_OUT_TILE)
    n_out = d // tn
    final = final_g is not None
    xi = _x_index(nb, tps)
    in_specs = [
        pl.BlockSpec((nb, tt, d), xi),
        _mod_spec(nb, tps, d, slots[0]),
        _mod_spec(nb, tps, d, slots[1]),
        _mod_spec(nb, tps, d, slots[2]),
        pl.BlockSpec((1, d), lambda i, j: (0, 0)),
        pl.BlockSpec((d, FF_TILE), lambda i, j: (0, jnp.minimum(j, n_ff - 1))),
        pl.BlockSpec((d, FF_TILE), lambda i, j: (0, jnp.minimum(j, n_ff - 1))),
        pl.BlockSpec((ff, tn), lambda i, j: (0, jnp.maximum(j - n_ff, 0))),
    ]
    args = [x, mods, mods, mods, g.reshape(1, d), wg_p, wu_p, w_out_p]
    if final:
        in_specs.append(pl.BlockSpec((1, d), lambda i, j: (0, 0)))
        args.append(final_g.reshape(1, d))
    return pl.pallas_call(
        functools.partial(_ffn_kernel, nb=nb, tt=tt, n_ff=n_ff, last_cols=last_cols, n_out=n_out, final=final),
        grid=(nrows, n_ff + n_out),
        in_specs=in_specs,
        out_specs=(pl.BlockSpec((nb, tt, d), xi) if final else
                   pl.BlockSpec((nb, tt, tn), lambda i, j: xi(i, j)[:2] + (jnp.maximum(j - n_ff, 0),))),
        out_shape=jax.ShapeDtypeStruct((b, t, d), F32),
        scratch_shapes=[pltpu.VMEM((nb * tt, d), BF16), pltpu.VMEM((n_ff, nb * tt, FF_TILE), BF16)],
        compiler_params=_params("parallel", "arbitrary"),
        name="ffn",
    )(*args)


def _prep_ffn_weights(w_in, w_out):
    ff = w_out.shape[0]
    return w_in[:, :ff].astype(BF16), w_in[:, ff:].astype(BF16), w_out.astype(BF16)


def _log_sigmoid(x):
    return jnp.minimum(x, 0.0) - jnp.log1p(jnp.exp(-jnp.abs(x)))


def _proj_kernel(*refs, nb, tt, tn, tps, taps, has_f, vt_cols, vt_tile):
    x_ref, shift_ref, scale_ref, g_ref, w_ref, cs_ref = refs[:6]
    pos = 6
    if has_f:
        wf_ref, bf_ref = refs[6:8]
        pos = 8
    obf_ref = refs[pos]
    tap_refs = refs[pos + 1:pos + 1 + len({tp[0] for tp in taps})]
    pos = pos + 1 + len(tap_refs)
    if vt_cols:
        vt_ref = refs[pos]
        pos += 1
    if has_f:
        logf_ref = refs[pos]
        pos += 1
    h_scr = refs[pos]
    i = pl.program_id(0)
    j = pl.program_id(1)
    d = x_ref.shape[-1]

    def store_attention_copy(y):
        obf_ref[...] = (y * cs_ref[...]).astype(BF16).reshape(nb, tt, tn)

    assert all(tp[1] > 0 for tp in taps) and all(vc[0] > 0 for vc in vt_cols)

    @pl.when(j == 0)
    def _():
        hs = []
        ys = []
        for r in range(PROJ_ROW_SPLIT):
            hs.append(_normed_rows(x_ref, shift_ref, scale_ref, g_ref, h_scr, PROJ_ROW_SPLIT, r))
            ys.append(jnp.dot(hs[-1], w_ref[...], preferred_element_type=F32))
        store_attention_copy(jnp.concatenate(ys, axis=0))
        if has_f:
            fr = lax.dot_general(wf_ref[...], jnp.concatenate(hs, axis=0), _NT, preferred_element_type=F32)
            logf_ref[...] = _log_sigmoid(fr[:logf_ref.shape[0]] + bf_ref[...])

    @pl.when(j > 0)
    def _():
        y = jnp.dot(h_scr[...], w_ref[...], preferred_element_type=F32)
        store_attention_copy(y)
        _proj_taps(y, i, j, tap_refs, vt_ref if vt_cols else None, nb=nb, tt=tt, tn=tn, tps=tps, taps=taps,
                   vt_cols=vt_cols, vt_tile=vt_tile)


def _proj_taps(y, i, j, tap_refs, vt_ref, *, nb, tt, tn, tps, taps, vt_cols, vt_tile):
    for out_idx, jval, col0, last_rows_only in taps:
        cond = j == jval
        if last_rows_only and tps > 1:
            cond = jnp.logical_and(cond, i % tps == tps - 1)

        @pl.when(cond)
        def _(out_idx=out_idx, col0=col0):
            tap_refs[out_idx][:, :, col0:col0 + tn] = y.reshape(nb, tt, tn)

    for jval, head0 in vt_cols:
        @pl.when(j == jval)
        def _(head0=head0):
            for hh in range(tn // HEAD_DIM):
                yt = y[:, hh * HEAD_DIM:(hh + 1) * HEAD_DIM].T.astype(BF16)
                yt = jnp.concatenate([yt, jnp.ones((VT_ROWS - HEAD_DIM, yt.shape[1]), BF16)], axis=0)
                for s in range(tt // vt_tile):
                    vt_ref[0, head0 + hh, s] = yt[:, s * vt_tile:(s + 1) * vt_tile]


def _seg_taps(tn, width, segments, last_rows_only):
    per = width // tn
    assert per * tn == width
    return [(o, sidx * per + s, s * tn, last_rows_only) for o, sidx in enumerate(segments) for s in range(per)]


def _vt_cols(tn, width, segments):
    per = width // tn
    hpb = tn // HEAD_DIM
    return [(sidx * per + s, (o * per + s) * hpb) for o, sidx in enumerate(segments) for s in range(per)]


def _proj(x, mods, slots, g, w_bf, col_scale, tn, tap_shapes, taps, wf=None, bf=None, vt_cols=(), vt_tile=None):
    b, t, d = x.shape
    n = w_bf.shape[1]
    nb, tt, tps, nrows = _row_blocking(x)
    has_f = wf is not None
    assert not vt_cols or nb == 1
    in_specs = [
        pl.BlockSpec((nb, tt, d), _x_index(nb, tps)),
        _mod_spec(nb, tps, d, slots[0]),
        _mod_spec(nb, tps, d, slots[1]),
        pl.BlockSpec((1, d), lambda i, j: (0, 0)),
        pl.BlockSpec((d, tn), lambda i, j: (0, j)),
        pl.BlockSpec((1, tn), lambda i, j: (0, j)),
    ]
    args = [x, mods, mods, g.reshape(1, d), w_bf, jnp.asarray(col_scale, F32).reshape(1, n)]
    if has_f:
        in_specs += [pl.BlockSpec(wf.shape, lambda i, j: (0, 0)), pl.BlockSpec(bf.shape, lambda i, j: (0, 0))]
        args += [wf, bf]
    xi = _x_index(nb, tps)
    out_specs = [pl.BlockSpec((nb, tt, tn), lambda i, j: xi(i, j)[:2] + (j,))]
    out_shape = [jax.ShapeDtypeStruct((b, t, n), BF16)]
    for shp, follows_rows in tap_shapes:
        if follows_rows:
            out_specs.append(pl.BlockSpec((nb, tt, shp[2]), xi))
        elif nb == 1:
            out_specs.append(pl.BlockSpec((1, shp[1], shp[2]), lambda i, j: (i // tps, 0, 0)))
        else:
            out_specs.append(pl.BlockSpec(shp, lambda i, j: (0, 0, 0)))
        out_shape.append(jax.ShapeDtypeStruct(shp, F32))
    if vt_cols:
        n_vh = len(vt_cols) * (tn // HEAD_DIM)
        out_specs.append(pl.BlockSpec((1, n_vh, tt // vt_tile, VT_ROWS, vt_tile), lambda i, j: (i // tps, 0, i % tps, 0, 0)))
        out_shape.append(jax.ShapeDtypeStruct((b, n_vh, t // vt_tile, VT_ROWS, vt_tile), BF16))
    if has_f:
        nf = bf.shape[0]
        out_specs.append(pl.BlockSpec((nf, nb * tt), lambda i, j: (0, i)))
        out_shape.append(jax.ShapeDtypeStruct((nf, b * t), F32))
    return pl.pallas_call(
        functools.partial(_proj_kernel, nb=nb, tt=tt, tn=tn, tps=tps, taps=tuple(taps), has_f=has_f,
                          vt_cols=tuple(vt_cols), vt_tile=vt_tile),
        grid=(nrows, n // tn),
        in_specs=in_specs,
        out_specs=out_specs,
        out_shape=out_shape,
        scratch_shapes=[pltpu.VMEM((nb * tt, d), BF16)],
        compiler_params=_params("arbitrary", "arbitrary"),
        name="mixer_in_proj",
    )(*args)


def _outproj_kernel(*refs, nb, tt, n_in):
    o_refs = refs[:n_in]
    w_ref, x_ref, gate_ref, out_ref = refs[n_in:]
    d = x_ref.shape[-1]
    acc = None
    row0 = 0
    for o_ref in o_refs:
        wd = o_ref.shape[-1]
        part = jnp.dot(o_ref[...].reshape(nb * tt, wd), w_ref[row0:row0 + wd, :], preferred_element_type=F32)
        acc = part if acc is None else acc + part
        row0 += wd
    out_ref[...] = x_ref[...] + gate_ref[...] * acc.reshape(nb, tt, d)


def _outproj(o_list, w_bf, x, mods, gate_slot):
    b, t, d = x.shape
    nb, tt, tps, nrows = _row_blocking(x)
    xi = _x_index(nb, tps)
    in_specs = [pl.BlockSpec((nb, tt, o.shape[-1]), xi) for o in o_list]
    in_specs += [
        pl.BlockSpec(w_bf.shape, lambda i, j: (0, 0)),
        pl.BlockSpec((nb, tt, d), xi),
        _mod_spec(nb, tps, d, gate_slot),
    ]
    return pl.pallas_call(
        functools.partial(_outproj_kernel, nb=nb, tt=tt, n_in=len(o_list)),
        grid=(nrows, 1),
        in_specs=in_specs,
        out_specs=pl.BlockSpec((nb, tt, d), xi),
        out_shape=jax.ShapeDtypeStruct((b, t, d), F32),
        compiler_params=_params("parallel", "arbitrary"),
        name="mixer_out_proj",
    )(*o_list, w_bf, x, mods)


def _cumsum_kernel(x_ref, c_ref, o_ref):
    x = x_ref[0]
    n = x.shape[1]
    lane = lax.broadcasted_iota(jnp.int32, x.shape, 1)
    s = 1
    while s < n:
        x = x + jnp.where(lane >= s, pltpu.roll(x, s, axis=1), 0.0)
        s *= 2
    o_ref[0] = x + c_ref[0][:, :1]


def _cumsum(x, carry):
    g, h, n = x.shape
    return pl.pallas_call(
        _cumsum_kernel,
        grid=(g,),
        in_specs=[pl.BlockSpec((1, h, n), lambda i: (i, 0, 0)), pl.BlockSpec((1, h, LANES), lambda i: (i, 0, 0))],
        out_specs=pl.BlockSpec((1, h, n), lambda i: (i, 0, 0)),
        out_shape=jax.ShapeDtypeStruct((g, h, n), F32),
        compiler_params=_params("arbitrary"),
        name="logf_cumsum",
    )(x, carry)


def _softmax_init(m_scr, acc_scr):
    m_scr[...] = jnp.full(m_scr.shape, NEG_INF, F32)
    acc_scr[...] = jnp.zeros(acc_scr.shape, F32)


def _softmax_update_t(s, s_max, vt, m_prev, acc_prev):
    m_new = jnp.maximum(m_prev, s_max)
    alpha = jnp.exp2(m_prev - m_new)
    p = jnp.exp2(s - m_new)
    acc_new = alpha * acc_prev + jnp.dot(vt, p.astype(BF16), preferred_element_type=F32)
    return m_new, acc_new


def _normalised_t(acc):
    return (acc[:HEAD_DIM] / acc[HEAD_DIM:HEAD_DIM + 1]).T


def _tile_stage(slot, vt_of, m_scr, acc_scr, s_scr, smax_scr, n_groups, next_scores=None):
    g = m_scr.shape[1] // n_groups
    sls = [slice(i * g, (i + 1) * g) for i in range(n_groups)]
    state = [(m_scr[:, sl], acc_scr[:, sl]) for sl in sls]
    new = []
    for i, sl in enumerate(sls):
        if next_scores is not None:
            s = next_scores(i)
            s_scr[1 - slot, :, sl] = s
            smax_scr[1 - slot, :, sl] = jnp.max(s, axis=0, keepdims=True)
        new.append(_softmax_update_t(s_scr[slot, :, sl], smax_scr[slot, :, sl], vt_of(i), *state[i]))
    for sl, (m_new, acc_new) in zip(sls, new):
        m_scr[:, sl] = m_new
        acc_scr[:, sl] = acc_new


def _first_tile(scores, s_scr, smax_scr, n_groups):
    g = s_scr.shape[2] // n_groups
    for i in range(n_groups):
        sl = slice(i * g, (i + 1) * g)
        s = scores(i)
        s_scr[0, :, sl] = s
        smax_scr[0, :, sl] = jnp.max(s, axis=0, keepdims=True)


def _pipelined_tiles(last, n_special, produce_first, stage):
    def kind(dist):
        return dist if dist < n_special else "far"

    for k in range(n_special):
        @pl.when(last == k)
        def _(k=k):
            produce_first(kind(k))

    @pl.when(last >= n_special)
    def _():
        produce_first("far")

    n_pairs = jnp.maximum(last - n_special, 0) // 2

    def pair(i, carry):
        stage(2 * i, 0, "far")
        stage(2 * i + 1, 1, "far")
        return carry

    lax.fori_loop(0, n_pairs, pair, 0)
    t0 = 2 * n_pairs
    remaining = last - t0 + 1
    for r in range(1, n_special + 3):
        @pl.when(remaining == r)
        def _(r=r):
            for u in range(r):
                stage(t0 + u, u % 2, kind(r - 2 - u) if u + 1 < r else None)


def _toeplitz_values(tbl_ref, idx, n_vals, nh, h, ref_row):
    ref_val = tbl_ref[ref_row * nh + h]

    def body(v, acc):
        return jnp.where(idx == v, tbl_ref[v * nh + h] - ref_val, acc)

    return lax.fori_loop(0, n_vals, body, jnp.zeros(idx.shape, F32))


def _stack_diff_queries(q):
    lane = lax.broadcasted_iota(jnp.int32, q.shape, 1)
    zero = jnp.zeros_like(q)
    qs = jnp.concatenate([jnp.where(lane < DA_HALF, q, zero), jnp.where(lane >= DA_HALF, q, zero)], axis=0)
    return qs


def _diff_finalize(o, lam, g, out_scale):
    tq = o.shape[0] // 2
    od = o[:tq] - lam * o[tq:]
    return od * lax.rsqrt(jnp.mean(od * od, axis=-1, keepdims=True) + EPS) * g * out_scale


def _diff_p_kernel(lam_ref, tbl_ref, q_ref, k_ref, vt_ref, idx_ref, base_ref, g_ref, o_ref,
                   m_scr, acc_scr, bias_scr, s_scr, smax_scr, *, tq, nh, hps, far_bucket, out_scale):
    hg = pl.program_id(0)
    bi = pl.program_id(1)
    qi = pl.program_id(2)
    nblk = tq // LANES

    @pl.when(jnp.logical_and(bi == 0, qi == 0))
    def _():
        for e in range(hps):
            t = _toeplitz_values(tbl_ref, idx_ref[...], N_T5_BUCKETS, nh, hg * hps + e, far_bucket) * LOG2E
            bias_scr[e] = base_ref[...]
            for blk in range(nblk):
                sl = slice(blk * LANES, (blk + 1) * LANES)
                bias_scr[e, 0, sl, sl] += t[0]
                if blk >= 1:
                    bias_scr[e, 0, (blk - 1) * LANES:blk * LANES, sl] += t[1]
            bias_scr[e, 1, (nblk - 1) * LANES:, :LANES] += t[1]

    qs = [_stack_diff_queries(q_ref[0, :, e * HEAD_DIM:(e + 1) * HEAD_DIM]) for e in range(hps)]
    _softmax_init(m_scr, acc_scr)
    gph = 2 * tq // ATTN_GROUP
    ngrp = hps * gph

    def scores(j, kind):
        start = pl.multiple_of(j * tq, tq)

        def group(i):
            e, gi = divmod(i, gph)
            k = k_ref[0, pl.ds(start, tq), e * HEAD_DIM:(e + 1) * HEAD_DIM]
            s = lax.dot_general(k, qs[e][gi * ATTN_GROUP:(gi + 1) * ATTN_GROUP], _NT, preferred_element_type=F32)
            if kind != "far":
                q0 = (gi * ATTN_GROUP) % tq
                s = s + bias_scr[e, kind, :, q0:q0 + ATTN_GROUP]
            return s
        return group

    def stage(t, slot, next_kind):
        nxt = None if next_kind is None else scores(t + 1, next_kind)
        _tile_stage(slot, lambda i: vt_ref[0, i // gph, t], m_scr, acc_scr, s_scr, smax_scr, ngrp, nxt)

    _pipelined_tiles(qi, 2, lambda kind: _first_tile(scores(0, kind), s_scr, smax_scr, ngrp), stage)
    outs = []
    for e in range(hps):
        o = _normalised_t(acc_scr[:, e * 2 * tq:(e + 1) * 2 * tq])
        outs.append(_diff_finalize(o, lam_ref[0], g_ref[...], out_scale).astype(BF16))
    o_ref[0] = jnp.concatenate(outs, axis=1)


def _diff_prompt(lam, tbl, qkv, vt, g, n_heads, out_scale):
    b, t, _ = qkv.shape
    tq = vt.shape[-1]
    far_bucket = N_T5_BUCKETS // 2 - 1
    pos = np.arange(LANES)
    idx = np.stack([_t5_bucket_np(pos[:, None] - pos[None, :]),
                    _t5_bucket_np(pos[:, None] - pos[None, :] - LANES)]).astype(np.int32)
    kq = np.arange(tq)
    base = np.zeros((2, tq, tq), np.float32)
    base[0] = np.where((kq[:, None] // CHUNK) <= (kq[None, :] // CHUNK), 0.0, NEG_INF)
    hps = ATTN_HEADS
    ngrp = n_heads // hps
    assert ngrp * hps == n_heads
    wd = hps * HEAD_DIM
    lanes = hps * 2 * tq
    return pl.pallas_call(
        functools.partial(_diff_p_kernel, tq=tq, nh=n_heads, hps=hps, far_bucket=far_bucket, out_scale=out_scale),
        grid=(ngrp, b, t // tq),
        in_specs=[
            pl.BlockSpec(memory_space=pltpu.SMEM),
            pl.BlockSpec(memory_space=pltpu.SMEM),
            pl.BlockSpec((1, tq, wd), lambda h, bi, qi: (bi, qi, h)),
            pl.BlockSpec((1, t, wd), lambda h, bi, qi: (bi, 0, ngrp + h)),
            pl.BlockSpec((1, hps, t // tq, VT_ROWS, tq), lambda h, bi, qi: (bi, h, 0, 0, 0)),
            pl.BlockSpec(idx.shape, lambda h, bi, qi: (0, 0, 0)),
            pl.BlockSpec(base.shape, lambda h, bi, qi: (0, 0, 0)),
            pl.BlockSpec((1, HEAD_DIM), lambda h, bi, qi: (0, 0)),
        ],
        out_specs=pl.BlockSpec((1, tq, wd), lambda h, bi, qi: (bi, qi, h)),
        out_shape=jax.ShapeDtypeStruct((b, t, n_heads * HEAD_DIM), BF16),
        scratch_shapes=[pltpu.VMEM((1, lanes), F32), pltpu.VMEM((VT_ROWS, lanes), F32),
                        pltpu.VMEM((hps, 2, tq, tq), F32), pltpu.VMEM((2, tq, lanes), F32),
                        pltpu.VMEM((2, 1, lanes), F32)],
        compiler_params=_params("arbitrary", "arbitrary", "arbitrary"),
        name="diff_attn_prompt",
    )(lam, tbl, qkv, qkv, vt, jnp.asarray(idx), jnp.asarray(base), g)


def _fox_p_kernel(fref_ref, q_ref, k_ref, vt_ref, f_ref, o_ref, m_scr, acc_scr, fcol_scr, s_scr, smax_scr,
                  *, tq, hps):
    bi = pl.program_id(0)
    hg = pl.program_id(1)
    qi = pl.program_id(2)
    nblk = tq // LANES

    @pl.when(qi == 0)
    def _():
        for e in range(hps):
            for c in range(f_ref.shape[2] * nblk):
                row = f_ref[0, e, c // nblk][:, (c % nblk) * LANES:(c % nblk + 1) * LANES]
                fcol_scr[e, c * LANES:(c + 1) * LANES, :] = jnp.broadcast_to(row, (LANES, LANES)).T

    n_heads = pl.num_programs(1) * hps
    frefs = [fref_ref[(bi * n_heads + hg * hps + e) * pl.num_programs(2) + qi] for e in range(hps)]
    _softmax_init(m_scr, acc_scr)

    gph = tq // ATTN_GROUP
    ngrp = hps * gph

    def scores(j, kind):
        start = pl.multiple_of(j * tq, tq)

        def group(i):
            e, gi = divmod(i, gph)
            hs = slice(e * HEAD_DIM, (e + 1) * HEAD_DIM)
            decay = (frefs[e] - fcol_scr[e, pl.ds(start, tq), :]) * LOG2E
            decay = jnp.concatenate([decay] * (ATTN_GROUP // LANES), axis=1)
            s = lax.dot_general(k_ref[0, pl.ds(start, tq), hs], q_ref[0, gi * ATTN_GROUP:(gi + 1) * ATTN_GROUP, hs],
                                _NT, preferred_element_type=F32)
            s = s + decay
            if kind != "far":
                key = lax.broadcasted_iota(jnp.int32, s.shape, 0)
                qry = lax.broadcasted_iota(jnp.int32, s.shape, 1) + gi * ATTN_GROUP
                s = jnp.where(key <= qry, s, NEG_INF)
            return s
        return group

    def stage(t, slot, next_kind):
        nxt = None if next_kind is None else scores(t + 1, next_kind)
        _tile_stage(slot, lambda i: vt_ref[0, i // gph, t], m_scr, acc_scr, s_scr, smax_scr, ngrp, nxt)

    _pipelined_tiles(qi, 1, lambda kind: _first_tile(scores(0, kind), s_scr, smax_scr, ngrp), stage)
    o_ref[0] = jnp.concatenate(
        [_normalised_t(acc_scr[:, e * tq:(e + 1) * tq]).astype(BF16) for e in range(hps)], axis=1)


def _fox_prompt(fref, qkv, vt, fcum, n_heads, col0, vt_head0):
    b, t, _ = qkv.shape
    tq = vt.shape[-1]
    nq = t // tq
    hps = ATTN_HEADS
    ngrp = n_heads // hps
    assert ngrp * hps == n_heads and col0 % hps == 0 and vt_head0 % hps == 0
    wd = hps * HEAD_DIM
    lanes = hps * tq
    return pl.pallas_call(
        functools.partial(_fox_p_kernel, tq=tq, hps=hps),
        grid=(b, ngrp, nq),
        in_specs=[
            pl.BlockSpec(memory_space=pltpu.SMEM),
            pl.BlockSpec((1, tq, wd), lambda bi, h, qi: (bi, qi, col0 // hps + h)),
            pl.BlockSpec((1, t, wd), lambda bi, h, qi: (bi, 0, (col0 + n_heads) // hps + h)),
            pl.BlockSpec((1, hps, nq, VT_ROWS, tq), lambda bi, h, qi: (bi, vt_head0 // hps + h, 0, 0, 0)),
            pl.BlockSpec((1, hps, nq, 1, tq), lambda bi, h, qi: (bi, h, 0, 0, 0)),
        ],
        out_specs=pl.BlockSpec((1, tq, wd), lambda bi, h, qi: (bi, qi, h)),
        out_shape=jax.ShapeDtypeStruct((b, t, n_heads * HEAD_DIM), BF16),
        scratch_shapes=[pltpu.VMEM((1, lanes), F32), pltpu.VMEM((VT_ROWS, lanes), F32),
                        pltpu.VMEM((hps, t, LANES), F32), pltpu.VMEM((2, tq, lanes), F32),
                        pltpu.VMEM((2, 1, lanes), F32)],
        compiler_params=_params("arbitrary", "arbitrary", "arbitrary"),
        name="fox_attn_prompt",
    )(fref, qkv, qkv, vt, fcum)


def _joint_softmax_pv(s_c, s_n, vc, vn):
    m = jnp.maximum(jnp.max(s_c, axis=1, keepdims=True), jnp.max(s_n, axis=1, keepdims=True))
    p_c = jnp.exp2(s_c - m)
    p_n = jnp.exp2(s_n - m)
    l = jnp.sum(p_c, axis=1, keepdims=True) + jnp.sum(p_n, axis=1, keepdims=True)
    acc = jnp.dot(p_c.astype(BF16), vc, preferred_element_type=F32)
    acc = acc + jnp.dot(p_n.astype(BF16), vn, preferred_element_type=F32)
    return acc / l


def _diff_s_kernel(lam_ref, q_ref, kn_ref, vn_ref, kc_ref, vc_ref, bnear_ref, bnew_ref, g_ref, o_ref, *, near, out_scale):
    qs = _stack_diff_queries(q_ref[0])
    kc = kc_ref[0].astype(BF16)
    vc = vc_ref[0].astype(BF16)
    p = kc.shape[0]
    s_c = lax.dot_general(qs, kc, _NT, preferred_element_type=F32)
    bnear = bnear_ref[0]
    s_c = jnp.concatenate([s_c[:, :p - near], s_c[:, p - near:] + jnp.concatenate([bnear, bnear], axis=0)], axis=1)
    bnew = bnew_ref[0]
    s_n = lax.dot_general(qs, kn_ref[0], _NT, preferred_element_type=F32) + jnp.concatenate([bnew, bnew], axis=0)
    o = _joint_softmax_pv(s_c, s_n, vc, vn_ref[0])
    o_ref[0] = _diff_finalize(o, lam_ref[0], g_ref[...], out_scale).astype(BF16)


def _diff_sample(lam, qkv, kc, vc, bnear, bnew, g, n_heads, out_scale):
    b, t, _ = qkv.shape
    p = kc.shape[1]
    near = bnear.shape[-1]
    hd = HEAD_DIM
    return pl.pallas_call(
        functools.partial(_diff_s_kernel, near=near, out_scale=out_scale),
        grid=(b, n_heads),
        in_specs=[
            pl.BlockSpec(memory_space=pltpu.SMEM),
            pl.BlockSpec((1, t, hd), lambda bi, h: (bi, 0, h)),
            pl.BlockSpec((1, t, hd), lambda bi, h: (bi, 0, n_heads + h)),
            pl.BlockSpec((1, t, hd), lambda bi, h: (bi, 0, 2 * n_heads + h)),
            pl.BlockSpec((1, p, hd), lambda bi, h: (bi, 0, h)),
            pl.BlockSpec((1, p, hd), lambda bi, h: (bi, 0, h)),
            pl.BlockSpec((1, t, near), lambda bi, h: (h, 0, 0)),
            pl.BlockSpec((1, t, t), lambda bi, h: (h, 0, 0)),
            pl.BlockSpec((1, hd), lambda bi, h: (0, 0)),
        ],
        out_specs=pl.BlockSpec((1, t, hd), lambda bi, h: (bi, 0, h)),
        out_shape=jax.ShapeDtypeStruct((b, t, n_heads * hd), BF16),
        compiler_params=_params("parallel", "parallel"),
        name="diff_attn_sample",
    )(lam, qkv, qkv, qkv, kc, vc, bnear, bnew, g)


def _fox_s_kernel(fref_ref, q_ref, kn_ref, vn_ref, kc_ref, vc_ref, fc_ref, fn_ref, o_ref):
    bi = pl.program_id(0)
    h = pl.program_id(1)
    q = q_ref[0]
    t = q.shape[0]
    fref = fref_ref[bi * pl.num_programs(1) + h]
    s_c = lax.dot_general(q, kc_ref[0].astype(BF16), _NT, preferred_element_type=F32) + (fref - fc_ref[0, 0]) * LOG2E
    s_n = lax.dot_general(q, kn_ref[0], _NT, preferred_element_type=F32) + (fref - fn_ref[0, 0][:, :t]) * LOG2E
    row = lax.broadcasted_iota(jnp.int32, s_n.shape, 0)
    col = lax.broadcasted_iota(jnp.int32, s_n.shape, 1)
    s_n = jnp.where(col <= row, s_n, NEG_INF)
    o_ref[0] = _joint_softmax_pv(s_c, s_n, vc_ref[0].astype(BF16), vn_ref[0]).astype(BF16)


def _fox_sample(fref, qkv, kc, vc, fc, fn, n_heads, col0):
    b, t, _ = qkv.shape
    p = kc.shape[1]
    hd = HEAD_DIM
    return pl.pallas_call(
        _fox_s_kernel,
        grid=(b, n_heads),
        in_specs=[
            pl.BlockSpec(memory_space=pltpu.SMEM),
            pl.BlockSpec((1, t, hd), lambda bi, h: (bi, 0, col0 + h)),
            pl.BlockSpec((1, t, hd), lambda bi, h: (bi, 0, col0 + n_heads + h)),
            pl.BlockSpec((1, t, hd), lambda bi, h: (bi, 0, col0 + 2 * n_heads + h)),
            pl.BlockSpec((1, p, hd), lambda bi, h: (bi, 0, h)),
            pl.BlockSpec((1, p, hd), lambda bi, h: (bi, 0, h)),
            pl.BlockSpec((1, 1, 1, p), lambda bi, h: (bi, h, 0, 0)),
            pl.BlockSpec((1, 1, 1, fn.shape[-1]), lambda bi, h: (bi, h, 0, 0)),
        ],
        out_specs=pl.BlockSpec((1, t, hd), lambda bi, h: (bi, 0, h)),
        out_shape=jax.ShapeDtypeStruct((b, t, n_heads * hd), BF16),
        compiler_params=_params("parallel", "parallel"),
        name="fox_attn_sample",
    )(fref, qkv, qkv, qkv, kc, vc, fc, fn)


def _band_p_kernel(tbl_ref, q_ref, k_ref, vt_ref, idx_ref, base_ref, o_ref, bm_scr, *, tq, nkb, nh, hps):
    hg = pl.program_id(0)
    bi = pl.program_id(1)
    i = pl.program_id(2)
    nblk = tq // LANES
    qoff = (nkb - 1) * nblk

    @pl.when(jnp.logical_and(bi == 0, i == 0))
    def _():
        for e in range(hps):
            t = _toeplitz_values(tbl_ref, idx_ref[...], 2 * REL_CLIP + 1, nh, hg * hps + e, 0)
            bm_scr[e] = base_ref[...]
            for bq in range(nblk):
                kd = qoff + bq
                bm_scr[e, kd * LANES:(kd + 1) * LANES, bq * LANES:(bq + 1) * LANES] += t[0] * LOG2E
                bm_scr[e, (kd - 1) * LANES:kd * LANES, bq * LANES:(bq + 1) * LANES] += t[1] * LOG2E

    def head_scores(e):
        hs = slice(e * HEAD_DIM, (e + 1) * HEAD_DIM)
        q = q_ref[0, :, hs]
        ss = []
        for blk in range(nkb):
            jb = i - (nkb - 1) + blk
            jc = jnp.maximum(jb, 0)
            k = k_ref[0, pl.ds(pl.multiple_of(jc * tq, tq), tq), hs]
            s = lax.dot_general(k, q, _NT, preferred_element_type=F32) + bm_scr[e, blk * tq:(blk + 1) * tq, :]
            if blk < nkb - 1:
                s = s + jnp.where(jb < 0, NEG_INF, 0.0)
            ss.append(s)
        return ss, functools.reduce(jnp.maximum, [jnp.max(s, axis=0, keepdims=True) for s in ss])

    outs = []
    pending = [head_scores(e) for e in range(min(BAND_LOOKAHEAD, hps))]
    for e in range(hps):
        ss, m = pending.pop(0)
        if e + BAND_LOOKAHEAD < hps:
            pending.append(head_scores(e + BAND_LOOKAHEAD))
        acc = None
        for blk, s in enumerate(ss):
            jc = jnp.maximum(i - (nkb - 1) + blk, 0)
            a = jnp.dot(vt_ref[0, e, jc], jnp.exp2(s - m).astype(BF16), preferred_element_type=F32)
            acc = a if acc is None else acc + a
        outs.append(_normalised_t(acc).astype(BF16))
    o_ref[0] = jnp.concatenate(outs, axis=1)


def _band_prompt(tbl, qkv, vt, n_heads):
    b, t, _ = qkv.shape
    tq = vt.shape[-1]
    nkb = C_BAND // tq + 1
    assert (nkb - 1) * tq == C_BAND and tq % LANES == 0 and LANES >= REL_CLIP
    pos = np.arange(LANES)
    rel = pos[:, None] - pos[None, :]
    idx = (np.stack([np.clip(rel, -REL_CLIP, REL_CLIP), np.clip(rel - LANES, -REL_CLIP, REL_CLIP)]) + REL_CLIP).astype(np.int32)
    k_pos = np.arange(nkb * tq)[:, None]
    q_pos = C_BAND + np.arange(tq)[None, :]
    kc, qc = k_pos // CHUNK, q_pos // CHUNK
    base = np.where((kc <= qc) & (qc - kc <= C_PREV_CHUNKS), 0.0, NEG_INF).astype(np.float32)
    hps = BAND_HEADS
    ngrp = n_heads // hps
    assert ngrp * hps == n_heads
    wd = hps * HEAD_DIM
    return pl.pallas_call(
        functools.partial(_band_p_kernel, tq=tq, nkb=nkb, nh=n_heads, hps=hps),
        grid=(ngrp, b, t // tq),
        in_specs=[
            pl.BlockSpec(memory_space=pltpu.SMEM),
            pl.BlockSpec((1, tq, wd), lambda h, bi, i: (bi, i, h)),
            pl.BlockSpec((1, t, wd), lambda h, bi, i: (bi, 0, ngrp + h)),
            pl.BlockSpec((1, hps, t // tq, VT_ROWS, tq), lambda h, bi, i: (bi, h, 0, 0, 0)),
            pl.BlockSpec(idx.shape, lambda h, bi, i: (0, 0, 0)),
            pl.BlockSpec(base.shape, lambda h, bi, i: (0, 0)),
        ],
        out_specs=pl.BlockSpec((1, tq, wd), lambda h, bi, i: (bi, i, h)),
        out_shape=jax.ShapeDtypeStruct((b, t, n_heads * HEAD_DIM), BF16),
        scratch_shapes=[pltpu.VMEM((hps, nkb * tq, tq), F32)],
        compiler_params=_params("arbitrary", "arbitrary", "arbitrary"),
        name="band_attn_prompt",
    )(tbl, qkv, qkv, vt, jnp.asarray(idx), jnp.asarray(base))


def _band_s_kernel(q_ref, kn_ref, vn_ref, kc_ref, vc_ref, bmc_ref, bmn_ref, o_ref, *, n_heads):
    outs = []
    for h in range(n_heads):
        hs = slice(h * HEAD_DIM, (h + 1) * HEAD_DIM)
        q = q_ref[0, :, hs]
        kc = kc_ref[0, :, h, :].astype(BF16)
        vc = vc_ref[0, :, h, :].astype(BF16)
        s_c = lax.dot_general(q, kc, _NT, preferred_element_type=F32) + bmc_ref[h]
        s_n = lax.dot_general(q, kn_ref[0, :, hs], _NT, preferred_element_type=F32) + bmn_ref[h]
        outs.append(_joint_softmax_pv(s_c, s_n, vc, vn_ref[0, :, hs]).astype(BF16))
    o_ref[0] = jnp.concatenate(outs, axis=1)


def _band_sample(qkv, kc, vc, bmc, bmn, n_heads):
    b, t, _ = qkv.shape
    lc = kc.shape[1]
    cw = n_heads * HEAD_DIM
    return pl.pallas_call(
        functools.partial(_band_s_kernel, n_heads=n_heads),
        grid=(b,),
        in_specs=[
            pl.BlockSpec((1, t, cw), lambda bi: (bi, 0, 0)),
            pl.BlockSpec((1, t, cw), lambda bi: (bi, 0, 1)),
            pl.BlockSpec((1, t, cw), lambda bi: (bi, 0, 2)),
            pl.BlockSpec((1, lc, n_heads, HEAD_DIM), lambda bi: (bi, 0, 0, 0)),
            pl.BlockSpec((1, lc, n_heads, HEAD_DIM), lambda bi: (bi, 0, 0, 0)),
            pl.BlockSpec(bmc.shape, lambda bi: (0, 0, 0)),
            pl.BlockSpec(bmn.shape, lambda bi: (0, 0, 0)),
        ],
        out_specs=pl.BlockSpec((1, t, cw), lambda bi: (bi, 0, 0)),
        out_shape=jax.ShapeDtypeStruct((b, t, cw), BF16),
        compiler_params=_params("parallel"),
        name="band_attn_sample",
    )(qkv, qkv, qkv, kc, vc, bmc, bmn)


def _t5_bucket_np(rel):
    nb = N_T5_BUCKETS // 2
    max_exact = nb // 2
    n = np.abs(rel)
    nf = np.maximum(n, 1).astype(np.float64)
    large = max_exact + (np.log(nf / max_exact) / math.log(T5_MAX_DIST / max_exact) * (nb - max_exact)).astype(np.int64)
    large = np.minimum(large, nb - 1)
    return np.where(rel > 0, nb, 0) + np.where(n < max_exact, n, large)


def _toeplitz(table, index_of_rel, q_pos, k_pos):
    r, c = len(q_pos), len(k_pos)
    assert (np.diff(q_pos) == 1).all() and (np.diff(k_pos) == 1).all()
    rels = (k_pos[0] - q_pos[0]) - (r - 1) + np.arange(r + c - 1)
    v = jnp.pad(table.astype(F32)[index_of_rel(rels)].T, ((0, 0), (0, 1)))
    x = jnp.tile(v, (1, r))[:, :r * (r + c - 1)].reshape(v.shape[0], r, r + c - 1)
    return x[:, :, r - 1:r - 1 + c]


def _t5_bias_tiles(t5_table, q_pos, k_pos, far_bucket):
    mask = (k_pos[None, :] // CHUNK) <= (q_pos[:, None] // CHUNK)
    bias = _toeplitz(t5_table, _t5_bucket_np, q_pos, k_pos)
    bias = (bias - t5_table.astype(F32)[far_bucket][:, None, None]) * LOG2E
    return jnp.where(mask[None], bias, NEG_INF)


def _band_bias_tiles(rel_table, q_pos, k_pos):
    qc = q_pos[:, None] // CHUNK
    kc = k_pos[None, :] // CHUNK
    mask = (kc <= qc) & (qc - kc <= C_PREV_CHUNKS) & (k_pos[None, :] >= 0)
    bias = _toeplitz(rel_table, lambda rel: np.clip(rel, -REL_CLIP, REL_CLIP) + REL_CLIP, q_pos, k_pos) * LOG2E
    return jnp.where(mask[None], bias, NEG_INF)


def kernel(x_prompt, x_sample, cache_a_k, cache_a_v, cache_b_k, cache_b_v, cache_b_logf, cache_c_k, cache_c_v, c_prompt, c_sample, w_ada, b_ada, norm_g, w_ffn_in, w_ffn_out, w_in_ab, b_forget, w_out_ab, lambda_q1, lambda_k1, lambda_q2, lambda_k2, subln_g, t5_table, w_in_c, w_out_c, c_rel_bias, final_g):
    depth = w_ada.shape[0]
    bsz, seq, d = x_prompt.shape
    dbsz, dseq, _ = x_sample.shape
    past = cache_b_logf.shape[2]
    h_a = cache_a_k.shape[3]
    h_b = cache_b_k.shape[3]
    h_c = cache_c_k.shape[3]
    wa, wb, cw = h_a * HEAD_DIM, h_b * HEAD_DIM, h_c * HEAD_DIM
    assert bsz == dbsz and wa == wb and 3 * wa + 3 * wb == 3 * cw
    assert LANES >= T5_MAX_DIST and SAMPLE_NEAR >= T5_MAX_DIST and past >= SAMPLE_NEAR
    far_bucket = (N_T5_BUCKETS // 2) - 1

    mods = _ada_mods(jnp.concatenate([c_prompt, c_sample], axis=0), w_ada, b_ada)
    mods = mods.reshape(depth, 2, bsz, 9, d).transpose(0, 3, 1, 2, 4).reshape(depth * 9 * 2, bsz, 1, d)

    def slot(l, k, grp):
        return (l * 9 + k) * 2 + grp

    xs = [x_prompt, x_sample]
    ab_states = [[], []]
    c_states = [[], []]
    for l in range(depth):
        i = l // 2
        last = l == depth - 1
        w1 = _prep_ffn_weights(w_ffn_in[l, 0], w_ffn_out[l, 0])
        w2 = _prep_ffn_weights(w_ffn_in[l, 1], w_ffn_out[l, 1])
        for grp in range(2):
            xs[grp] = _ffn(xs[grp], mods, [slot(l, k, grp) for k in range(3)], norm_g[l, 0], *w1)

        if l % 2 == 0:
            lam_init = 0.8 - 0.6 * math.exp(-0.3 * l)
            lam = (jnp.exp(jnp.sum(lambda_q1[i].astype(F32) * lambda_k1[i].astype(F32)))
                   - jnp.exp(jnp.sum(lambda_q2[i].astype(F32) * lambda_k2[i].astype(F32))) + lam_init).reshape(1)
            n_main = 3 * wa + 3 * wb
            w_main = w_in_ab[i][:, :n_main].astype(BF16)
            wf = jnp.pad(w_in_ab[i][:, n_main:].T.astype(BF16), ((0, 16 - h_b), (0, 0)))
            bfo = b_forget[i].astype(F32).reshape(h_b, 1)
            w_out = w_out_ab[i].astype(BF16)
            g_sub = subln_g[i].astype(F32).reshape(1, HEAD_DIM)
            tn = min(PROJ_COL_TILE, wa)
            taps = _seg_taps(tn, wa, (1, 2, 4, 5), False)
            cs = np.ones((n_main,), np.float32)
            cs[:wa] = DA_HALF ** -0.5 * LOG2E
            cs[3 * wa:3 * wa + wb] = HEAD_DIM ** -0.5 * LOG2E
            for grp in range(2):
                x = xs[grp]
                b, t, _ = x.shape
                tap_shapes = [((b, t, wa), True)] * 4
                if grp == 0:
                    qkv, ka, va, kb, vb, vt, logf_t = _proj(
                        x, mods, [slot(l, 3, grp), slot(l, 4, grp)], norm_g[l, 1], w_main, cs, tn, tap_shapes, taps,
                        wf, bfo, vt_cols=_vt_cols(tn, wa, (2, 5)), vt_tile=ROW_TILE)
                else:
                    qkv, ka, va, kb, vb, logf_t = _proj(
                        x, mods, [slot(l, 3, grp), slot(l, 4, grp)], norm_g[l, 1], w_main, cs, tn, tap_shapes, taps,
                        wf, bfo)
                logf = logf_t.reshape(h_b, b, t).transpose(1, 2, 0)
                ab_states[grp].append((ka.reshape(b, t, h_a, HEAD_DIM), va.reshape(b, t, h_a, HEAD_DIM),
                                       kb.reshape(b, t, h_b, HEAD_DIM), vb.reshape(b, t, h_b, HEAD_DIM), logf))
                logf_bh = logf_t.reshape(h_b, b, t).transpose(1, 0, 2)
                if grp == 0:
                    o_a = _diff_prompt(lam, t5_table.astype(F32).reshape(-1), qkv, vt, g_sub, h_a, 1.0 - lam_init)
                    fcum = _cumsum(logf_bh, jnp.zeros((b, h_b, LANES), F32))
                    nq = t // ROW_TILE
                    fref = fcum[:, :, ::ROW_TILE].reshape(-1)
                    o_b = _fox_prompt(fref, qkv, vt, fcum.reshape(b, h_b, nq, 1, ROW_TILE), h_b, 3 * h_a, h_a)
                else:
                    q_pos = past + np.arange(t)
                    bnear = _t5_bias_tiles(t5_table, q_pos, past - SAMPLE_NEAR + np.arange(SAMPLE_NEAR), far_bucket)
                    bnew = _t5_bias_tiles(t5_table, q_pos, q_pos, far_bucket)
                    kc = cache_a_k[i].reshape(b, past, wa)
                    vc = cache_a_v[i].reshape(b, past, wa)
                    o_a = _diff_sample(lam, qkv, kc, vc, bnear, bnew, g_sub, h_a, 1.0 - lam_init)
                    fc = _cumsum(cache_b_logf[i].astype(F32).transpose(0, 2, 1), jnp.zeros((b, h_b, LANES), F32))
                    carry = fc[:, :, past - 1:past]
                    fn = _cumsum(jnp.pad(logf_bh, ((0, 0), (0, 0), (0, LANES - t))),
                                 jnp.broadcast_to(carry, (b, h_b, LANES)))
                    o_b = _fox_sample(carry.reshape(-1), qkv, cache_b_k[i].reshape(b, past, wb),
                                      cache_b_v[i].reshape(b, past, wb), fc.reshape(b, h_b, 1, past),
                                      fn.reshape(b, h_b, 1, LANES), h_b, 3 * h_a)
                xs[grp] = _outproj([o_a, o_b], w_out, x, mods, slot(l, 5, grp))
        else:
            w_in = w_in_c[i].astype(BF16)
            w_out = w_out_c[i].astype(BF16)
            tn = min(PROJ_COL_TILE, cw)
            taps = _seg_taps(tn, cw, (1, 2), True)
            cs = np.ones((3 * cw,), np.float32)
            cs[:cw] = HEAD_DIM ** -0.5 * LOG2E
            for grp in range(2):
                x = xs[grp]
                b, t, _ = x.shape
                keep = min(C_BAND, t)
                assert keep == min(t, ROW_TILE)
                tap_shapes = [((b, keep, cw), False)] * 2
                if grp == 0:
                    qkv, k_new, v_new, vt = _proj(
                        x, mods, [slot(l, 3, grp), slot(l, 4, grp)], norm_g[l, 1], w_in, cs, tn, tap_shapes, taps,
                        vt_cols=_vt_cols(tn, cw, (2,)), vt_tile=BAND_TILE)
                else:
                    qkv, k_new, v_new = _proj(
                        x, mods, [slot(l, 3, grp), slot(l, 4, grp)], norm_g[l, 1], w_in, cs, tn, tap_shapes, taps)
                c_states[grp].append((k_new.reshape(b, keep, h_c, HEAD_DIM), v_new.reshape(b, keep, h_c, HEAD_DIM)))
                if grp == 0:
                    o_c = _band_prompt(c_rel_bias[i].astype(F32).reshape(-1), qkv, vt, h_c)
                else:
                    lc = cache_c_k.shape[2]
                    q_pos = past + np.arange(t)
                    bmc = _band_bias_tiles(c_rel_bias[i], q_pos, past - lc + np.arange(lc))
                    bmn = _band_bias_tiles(c_rel_bias[i], q_pos, q_pos)
                    o_c = _band_sample(qkv, cache_c_k[i], cache_c_v[i], bmc, bmn, h_c)
                xs[grp] = _outproj([o_c], w_out, x, mods, slot(l, 5, grp))

        for grp in range(2):
            xs[grp] = _ffn(xs[grp], mods, [slot(l, k, grp) for k in (6, 7, 8)], norm_g[l, 2], *w2,
                           final_g=final_g if last else None)

    outs = [xs[0], xs[1]]
    for grp in range(2):
        st = ab_states[grp]
        outs += [jnp.stack([s[k] for s in st]) for k in range(5)]
        st = c_states[grp]
        outs += [jnp.stack([s[k] for s in st]) for k in range(2)]
    return tuple(outs)
```

```python
import functools
import math

import numpy as np
import jax
import jax.numpy as jnp
from jax import lax
from jax.experimental import pallas as pl
from jax.experimental.pallas import tpu as pltpu

F32 = jnp.float32
BF16 = jnp.bfloat16

CHUNK = 64
HEAD_DIM = 128
DA_HALF = HEAD_DIM // 2
N_T5_BUCKETS = 32
T5_MAX_DIST = 128
C_PREV_CHUNKS = 8
C_BAND = C_PREV_CHUNKS * CHUNK
REL_CLIP = 128
EPS = 1e-6
NEG_INF = -1e30
LOG2E = math.log2(math.e)
VT_ROWS = HEAD_DIM + 16

VMEM_LIMIT_BYTES = 56 * 1024 * 1024
LANES = 128

ROW_TILE = 512
FF_TILE = 512
FFN_OUT_TILE = 512
FFN_ROW_SPLIT = 2
PROJ_ROW_SPLIT = 2
PROJ_COL_TILE = 1024
ADA_COL_TILE = 1024
BAND_TILE = 256
BAND_HEADS = 4
BAND_LOOKAHEAD = 3
DIFF_HEADS = 2
FOX_HEADS = 4
ATTN_GROUP = 512
SAMPLE_NEAR = 256

_NT = (((1,), (1,)), ((), ()))


def _params(*sem, flags=None):
    return pltpu.CompilerParams(dimension_semantics=sem, vmem_limit_bytes=VMEM_LIMIT_BYTES, flags=flags)


def _modulated_norm(x, g, shift, scale):
    y = x * lax.rsqrt(jnp.mean(x * x, axis=-1, keepdims=True) + EPS) * g
    return y * (1.0 + scale) + shift


def _normed_rows(x_ref, shift_ref, scale_ref, g_ref, h_scr, n_split, r):
    nb, tt, d = x_ref.shape
    rows = nb * tt // n_split
    if nb == 1:
        x, sh, sc = x_ref[:, r * rows:(r + 1) * rows, :], shift_ref[...], scale_ref[...]
    else:
        seqs = slice(r * (nb // n_split), (r + 1) * (nb // n_split))
        x, sh, sc = x_ref[seqs], shift_ref[seqs], scale_ref[seqs]
    hr = _modulated_norm(x, g_ref[...], sh, sc).reshape(rows, d).astype(BF16)
    h_scr[r * rows:(r + 1) * rows, :] = hr
    return hr


def _ada_kernel(c_ref, w_ref, b_ref, o_ref):
    c = c_ref[...]
    a = (c * jax.nn.sigmoid(c)).astype(BF16)
    o_ref[0] = jnp.dot(a, w_ref[0].astype(BF16), preferred_element_type=F32) + b_ref[0]


def _ada_mods(c_all, w_ada, b_ada):
    depth, d, n = w_ada.shape
    r = c_all.shape[0]
    tn = math.gcd(n, ADA_COL_TILE)
    return pl.pallas_call(
        _ada_kernel,
        grid=(depth, n // tn),
        in_specs=[
            pl.BlockSpec((r, d), lambda l, j: (0, 0)),
            pl.BlockSpec((1, d, tn), lambda l, j: (l, 0, j)),
            pl.BlockSpec((1, 1, tn), lambda l, j: (l, 0, j)),
        ],
        out_specs=pl.BlockSpec((1, r, tn), lambda l, j: (l, 0, j)),
        out_shape=jax.ShapeDtypeStruct((depth, r, n), F32),
        compiler_params=_params("arbitrary", "arbitrary"),
        name="ada_mods",
    )(c_all, w_ada, b_ada.reshape(depth, 1, n))


def _row_blocking(x):
    b, t, _ = x.shape
    if t >= ROW_TILE:
        assert t % ROW_TILE == 0
        return 1, ROW_TILE, t // ROW_TILE, b * (t // ROW_TILE)
    assert (b * t) % 8 == 0
    return b, t, 1, 1


def _x_index(nb, tps):
    if nb == 1:
        return lambda i, j: (i // tps, i % tps, 0)
    return lambda i, j: (0, 0, 0)


def _mod_spec(nb, tps, d, slot):
    if nb == 1:
        return pl.BlockSpec((None, 1, 1, d), lambda i, j: (slot, i // tps, 0, 0))
    return pl.BlockSpec((None, nb, 1, d), lambda i, j: (slot, 0, 0, 0))


def _ffn_kernel(*refs, nb, tt, n_ff, last_cols, n_out, final):
    if final:
        (x_ref, shift_ref, scale_ref, gate_ref, g_ref, wg_ref, wu_ref, wo_ref, fg_ref,
         o_ref, h_scr, a_scr) = refs
    else:
        (x_ref, shift_ref, scale_ref, gate_ref, g_ref, wg_ref, wu_ref, wo_ref,
         o_ref, h_scr, a_scr) = refs
    j = pl.program_id(1)
    d = x_ref.shape[-1]
    tm = nb * tt
    tf = wg_ref.shape[1]
    tn = wo_ref.shape[1]

    rows = tm // FFN_ROW_SPLIT
    normed_rows = functools.partial(_normed_rows, x_ref, shift_ref, scale_ref, g_ref, h_scr, FFN_ROW_SPLIT)

    def hidden_chunk(rows_of, cols=tf):
        def products(r):
            hr = rows_of(r)
            return (jnp.dot(hr, wg_ref[:, :cols], preferred_element_type=F32),
                    jnp.dot(hr, wu_ref[:, :cols], preferred_element_type=F32))

        parts = []
        cur = products(0)
        for r in range(FFN_ROW_SPLIT):
            nxt = products(r + 1) if r + 1 < FFN_ROW_SPLIT else None
            gg, uu = cur
            parts.append((gg * jax.nn.sigmoid(gg) * uu).astype(BF16))
            cur = nxt
        a_scr[j, :, :cols] = jnp.concatenate(parts, axis=0)

    kept_rows = lambda r: h_scr[r * rows:(r + 1) * rows, :]
    n_full = n_ff if last_cols == tf else n_ff - 1
    assert n_full >= 1

    @pl.when(j == 0)
    def _():
        hidden_chunk(normed_rows)

    @pl.when(jnp.logical_and(j > 0, j < n_full))
    def _():
        hidden_chunk(kept_rows)

    if n_full < n_ff:
        @pl.when(j == n_ff - 1)
        def _():
            hidden_chunk(kept_rows, last_cols)

    @pl.when(j >= n_ff)
    def _():
        a = jnp.concatenate([a_scr[f] for f in range(n_full)] + [a_scr[f, :, :last_cols] for f in range(n_full, n_ff)],
                            axis=1)
        acc = jnp.dot(a, wo_ref[...], preferred_element_type=F32).reshape(nb, tt, tn)
        for n in range(n_out):
            @pl.when(j == n_ff + n)
            def _(n=n):
                cs = slice(n * tn, (n + 1) * tn)
                y = x_ref[:, :, cs] + 0.5 * gate_ref[:, :, cs] * acc
                if final:
                    o_ref[:, :, cs] = y
                else:
                    o_ref[...] = y

    if final:
        @pl.when(j == n_ff + n_out - 1)
        def _():
            y = o_ref[...]
            o_ref[...] = y * lax.rsqrt(jnp.mean(y * y, axis=-1, keepdims=True) + EPS) * fg_ref[...]


def _ffn(x, mods, slots, g, wg_p, wu_p, w_out_p, final_g=None):
    b, t, d = x.shape
    nb, tt, tps, nrows = _row_blocking(x)
    ff = w_out_p.shape[0]
    n_ff = -(-ff // FF_TILE)
    last_cols = ff - (n_ff - 1) * FF_TILE
    tn = math.gcd(d, FFN_OUT_TILE)
    n_out = d // tn
    final = final_g is not None
    xi = _x_index(nb, tps)
    in_specs = [
        pl.BlockSpec((nb, tt, d), xi),
        _mod_spec(nb, tps, d, slots[0]),
        _mod_spec(nb, tps, d, slots[1]),
        _mod_spec(nb, tps, d, slots[2]),
        pl.BlockSpec((1, d), lambda i, j: (0, 0)),
        pl.BlockSpec((d, FF_TILE), lambda i, j: (0, jnp.minimum(j, n_ff - 1))),
        pl.BlockSpec((d, FF_TILE), lambda i, j: (0, jnp.minimum(j, n_ff - 1))),
        pl.BlockSpec((ff, tn), lambda i, j: (0, jnp.maximum(j - n_ff, 0))),
    ]
    args = [x, mods, mods, mods, g.reshape(1, d), wg_p, wu_p, w_out_p]
    if final:
        in_specs.append(pl.BlockSpec((1, d), lambda i, j: (0, 0)))
        args.append(final_g.reshape(1, d))
    return pl.pallas_call(
        functools.partial(_ffn_kernel, nb=nb, tt=tt, n_ff=n_ff, last_cols=last_cols, n_out=n_out, final=final),
        grid=(nrows, n_ff + n_out),
        in_specs=in_specs,
        out_specs=(pl.BlockSpec((nb, tt, d), xi) if final else
                   pl.BlockSpec((nb, tt, tn), lambda i, j: xi(i, j)[:2] + (jnp.maximum(j - n_ff, 0),))),
        out_shape=jax.ShapeDtypeStruct((b, t, d), F32),
        scratch_shapes=[pltpu.VMEM((nb * tt, d), BF16), pltpu.VMEM((n_ff, nb * tt, FF_TILE), BF16)],
        compiler_params=_params("parallel", "arbitrary"),
        name="ffn",
    )(*args)


def _prep_ffn_weights(w_in, w_out):
    ff = w_out.shape[0]
    return w_in[:, :ff].astype(BF16), w_in[:, ff:].astype(BF16), w_out.astype(BF16)


def _log_sigmoid(x):
    return jnp.minimum(x, 0.0) - jnp.log1p(jnp.exp(-jnp.abs(x)))


def _proj_kernel(*refs, nb, tt, tn, tps, taps, has_f, vt_cols, vt_tile):
    x_ref, shift_ref, scale_ref, g_ref, w_ref, cs_ref = refs[:6]
    pos = 6
    if has_f:
        wf_ref, bf_ref = refs[6:8]
        pos = 8
    obf_ref = refs[pos]
    tap_refs = refs[pos + 1:pos + 1 + len({tp[0] for tp in taps})]
    pos = pos + 1 + len(tap_refs)
    if vt_cols:
        vt_ref = refs[pos]
        pos += 1
    if has_f:
        logf_ref = refs[pos]
        pos += 1
    h_scr = refs[pos]
    i = pl.program_id(0)
    j = pl.program_id(1)
    d = x_ref.shape[-1]

    def store_attention_copy(y):
        obf_ref[...] = (y * cs_ref[...]).astype(BF16).reshape(nb, tt, tn)

    assert all(tp[1] > 0 for tp in taps) and all(vc[0] > 0 for vc in vt_cols)

    @pl.when(j == 0)
    def _():
        hs = []
        ys = []
        for r in range(PROJ_ROW_SPLIT):
            hs.append(_normed_rows(x_ref, shift_ref, scale_ref, g_ref, h_scr, PROJ_ROW_SPLIT, r))
            ys.append(jnp.dot(hs[-1], w_ref[...], preferred_element_type=F32))
        store_attention_copy(jnp.concatenate(ys, axis=0))
        if has_f:
            fr = lax.dot_general(wf_ref[...], jnp.concatenate(hs, axis=0), _NT, preferred_element_type=F32)
            logf_ref[...] = _log_sigmoid(fr[:logf_ref.shape[0]] + bf_ref[...])

    @pl.when(j > 0)
    def _():
        y = jnp.dot(h_scr[...], w_ref[...], preferred_element_type=F32)
        store_attention_copy(y)
        _proj_taps(y, i, j, tap_refs, vt_ref if vt_cols else None, nb=nb, tt=tt, tn=tn, tps=tps, taps=taps,
                   vt_cols=vt_cols, vt_tile=vt_tile)


def _proj_taps(y, i, j, tap_refs, vt_ref, *, nb, tt, tn, tps, taps, vt_cols, vt_tile):
    for out_idx, jval, col0, last_rows_only in taps:
        cond = j == jval
        if last_rows_only and tps > 1:
            cond = jnp.logical_and(cond, i % tps == tps - 1)

        @pl.when(cond)
        def _(out_idx=out_idx, col0=col0):
            tap_refs[out_idx][:, :, col0:col0 + tn] = y.reshape(nb, tt, tn)

    for jval, head0 in vt_cols:
        @pl.when(j == jval)
        def _(head0=head0):
            for hh in range(tn // HEAD_DIM):
                yt = y[:, hh * HEAD_DIM:(hh + 1) * HEAD_DIM].T.astype(BF16)
                yt = jnp.concatenate([yt, jnp.ones((VT_ROWS - HEAD_DIM, yt.shape[1]), BF16)], axis=0)
                for s in range(tt // vt_tile):
                    vt_ref[0, head0 + hh, s] = yt[:, s * vt_tile:(s + 1) * vt_tile]


def _seg_taps(tn, width, segments, last_rows_only):
    per = width // tn
    assert per * tn == width
    return [(o, sidx * per + s, s * tn, last_rows_only) for o, sidx in enumerate(segments) for s in range(per)]


def _vt_cols(tn, width, segments):
    per = width // tn
    hpb = tn // HEAD_DIM
    return [(sidx * per + s, (o * per + s) * hpb) for o, sidx in enumerate(segments) for s in range(per)]


def _proj(x, mods, slots, g, w_bf, col_scale, tn, tap_shapes, taps, wf=None, bf=None, vt_cols=(), vt_tile=None):
    b, t, d = x.shape
    n = w_bf.shape[1]
    nb, tt, tps, nrows = _row_blocking(x)
    has_f = wf is not None
    assert not vt_cols or nb == 1
    in_specs = [
        pl.BlockSpec((nb, tt, d), _x_index(nb, tps)),
        _mod_spec(nb, tps, d, slots[0]),
        _mod_spec(nb, tps, d, slots[1]),
        pl.BlockSpec((1, d), lambda i, j: (0, 0)),
        pl.BlockSpec((d, tn), lambda i, j: (0, j)),
        pl.BlockSpec((1, tn), lambda i, j: (0, j)),
    ]
    args = [x, mods, mods, g.reshape(1, d), w_bf, jnp.asarray(col_scale, F32).reshape(1, n)]
    if has_f:
        in_specs += [pl.BlockSpec(wf.shape, lambda i, j: (0, 0)), pl.BlockSpec(bf.shape, lambda i, j: (0, 0))]
        args += [wf, bf]
    xi = _x_index(nb, tps)
    out_specs = [pl.BlockSpec((nb, tt, tn), lambda i, j: xi(i, j)[:2] + (j,))]
    out_shape = [jax.ShapeDtypeStruct((b, t, n), BF16)]
    for shp, follows_rows in tap_shapes:
        if follows_rows:
            out_specs.append(pl.BlockSpec((nb, tt, shp[2]), xi))
        elif nb == 1:
            out_specs.append(pl.BlockSpec((1, shp[1], shp[2]), lambda i, j: (i // tps, 0, 0)))
        else:
            out_specs.append(pl.BlockSpec(shp, lambda i, j: (0, 0, 0)))
        out_shape.append(jax.ShapeDtypeStruct(shp, F32))
    if vt_cols:
        n_vh = len(vt_cols) * (tn // HEAD_DIM)
        out_specs.append(pl.BlockSpec((1, n_vh, tt // vt_tile, VT_ROWS, vt_tile), lambda i, j: (i // tps, 0, i % tps, 0, 0)))
        out_shape.append(jax.ShapeDtypeStruct((b, n_vh, t // vt_tile, VT_ROWS, vt_tile), BF16))
    if has_f:
        nf = bf.shape[0]
        out_specs.append(pl.BlockSpec((nf, nb * tt), lambda i, j: (0, i)))
        out_shape.append(jax.ShapeDtypeStruct((nf, b * t), F32))
    return pl.pallas_call(
        functools.partial(_proj_kernel, nb=nb, tt=tt, tn=tn, tps=tps, taps=tuple(taps), has_f=has_f,
                          vt_cols=tuple(vt_cols), vt_tile=vt_tile),
        grid=(nrows, n // tn),
        in_specs=in_specs,
        out_specs=out_specs,
        out_shape=out_shape,
        scratch_shapes=[pltpu.VMEM((nb * tt, d), BF16)],
        compiler_params=_params("arbitrary", "arbitrary"),
        name="mixer_in_proj",
    )(*args)


def _outproj_kernel(*refs, nb, tt, n_in):
    o_refs = refs[:n_in]
    w_ref, x_ref, gate_ref, out_ref = refs[n_in:]
    d = x_ref.shape[-1]
    acc = None
    row0 = 0
    for o_ref in o_refs:
        wd = o_ref.shape[-1]
        part = jnp.dot(o_ref[...].reshape(nb * tt, wd), w_ref[row0:row0 + wd, :], preferred_element_type=F32)
        acc = part if acc is None else acc + part
        row0 += wd
    out_ref[...] = x_ref[...] + gate_ref[...] * acc.reshape(nb, tt, d)


def _outproj(o_list, w_bf, x, mods, gate_slot):
    b, t, d = x.shape
    nb, tt, tps, nrows = _row_blocking(x)
    xi = _x_index(nb, tps)
    in_specs = [pl.BlockSpec((nb, tt, o.shape[-1]), xi) for o in o_list]
    in_specs += [
        pl.BlockSpec(w_bf.shape, lambda i, j: (0, 0)),
        pl.BlockSpec((nb, tt, d), xi),
        _mod_spec(nb, tps, d, gate_slot),
    ]
    return pl.pallas_call(
        functools.partial(_outproj_kernel, nb=nb, tt=tt, n_in=len(o_list)),
        grid=(nrows, 1),
        in_specs=in_specs,
        out_specs=pl.BlockSpec((nb, tt, d), xi),
        out_shape=jax.ShapeDtypeStruct((b, t, d), F32),
        compiler_params=_params("parallel", "arbitrary"),
        name="mixer_out_proj",
    )(*o_list, w_bf, x, mods)


def _cumsum_kernel(x_ref, c_ref, o_ref):
    x = x_ref[0]
    n = x.shape[1]
    lane = lax.broadcasted_iota(jnp.int32, x.shape, 1)
    s = 1
    while s < n:
        x = x + jnp.where(lane >= s, pltpu.roll(x, s, axis=1), 0.0)
        s *= 2
    o_ref[0] = x + c_ref[0][:, :1]


def _cumsum(x, carry):
    g, h, n = x.shape
    return pl.pallas_call(
        _cumsum_kernel,
        grid=(g,),
        in_specs=[pl.BlockSpec((1, h, n), lambda i: (i, 0, 0)), pl.BlockSpec((1, h, LANES), lambda i: (i, 0, 0))],
        out_specs=pl.BlockSpec((1, h, n), lambda i: (i, 0, 0)),
        out_shape=jax.ShapeDtypeStruct((g, h, n), F32),
        compiler_params=_params("arbitrary"),
        name="logf_cumsum",
    )(x, carry)


def _softmax_init(m_scr, acc_scr):
    m_scr[...] = jnp.full(m_scr.shape, NEG_INF, F32)
    acc_scr[...] = jnp.zeros(acc_scr.shape, F32)


def _softmax_update_t(s, s_max, vt, m_prev, acc_prev):
    m_new = jnp.maximum(m_prev, s_max)
    alpha = jnp.exp2(m_prev - m_new)
    p = jnp.exp2(s - m_new)
    acc_new = alpha * acc_prev + jnp.dot(vt, p.astype(BF16), preferred_element_type=F32)
    return m_new, acc_new


def _normalised_t(acc):
    return (acc[:HEAD_DIM] / acc[HEAD_DIM:HEAD_DIM + 1]).T


def _tile_stage(slot, vt_of, m_scr, acc_scr, s_scr, smax_scr, n_groups, next_scores=None):
    g = m_scr.shape[1] // n_groups
    sls = [slice(i * g, (i + 1) * g) for i in range(n_groups)]
    state = [(m_scr[:, sl], acc_scr[:, sl]) for sl in sls]
    new = []
    for i, sl in enumerate(sls):
        if next_scores is not None:
            s = next_scores(i)
            s_scr[1 - slot, :, sl] = s
            smax_scr[1 - slot, :, sl] = jnp.max(s, axis=0, keepdims=True)
        new.append(_softmax_update_t(s_scr[slot, :, sl], smax_scr[slot, :, sl], vt_of(i), *state[i]))
    for sl, (m_new, acc_new) in zip(sls, new):
        m_scr[:, sl] = m_new
        acc_scr[:, sl] = acc_new


def _first_tile(scores, s_scr, smax_scr, n_groups):
    g = s_scr.shape[2] // n_groups
    for i in range(n_groups):
        sl = slice(i * g, (i + 1) * g)
        s = scores(i)
        s_scr[0, :, sl] = s
        smax_scr[0, :, sl] = jnp.max(s, axis=0, keepdims=True)


def _pipelined_tiles(last, n_special, produce_first, stage):
    def kind(dist):
        return dist if dist < n_special else "far"

    for k in range(n_special):
        @pl.when(last == k)
        def _(k=k):
            produce_first(kind(k))

    @pl.when(last >= n_special)
    def _():
        produce_first("far")

    n_pairs = jnp.maximum(last - n_special, 0) // 2

    def pair(i, carry):
        stage(2 * i, 0, "far")
        stage(2 * i + 1, 1, "far")
        return carry

    lax.fori_loop(0, n_pairs, pair, 0)
    t0 = 2 * n_pairs
    remaining = last - t0 + 1
    for r in range(1, n_special + 3):
        @pl.when(remaining == r)
        def _(r=r):
            for u in range(r):
                stage(t0 + u, u % 2, kind(r - 2 - u) if u + 1 < r else None)


def _toeplitz_values(tbl_ref, idx, n_vals, nh, h, ref_row):
    ref_val = tbl_ref[ref_row * nh + h]

    def body(v, acc):
        return jnp.where(idx == v, tbl_ref[v * nh + h] - ref_val, acc)

    return lax.fori_loop(0, n_vals, body, jnp.zeros(idx.shape, F32))


def _stack_diff_queries(q):
    lane = lax.broadcasted_iota(jnp.int32, q.shape, 1)
    zero = jnp.zeros_like(q)
    qs = jnp.concatenate([jnp.where(lane < DA_HALF, q, zero), jnp.where(lane >= DA_HALF, q, zero)], axis=0)
    return qs


def _diff_finalize(o, lam, g, out_scale):
    tq = o.shape[0] // 2
    od = o[:tq] - lam * o[tq:]
    return od * lax.rsqrt(jnp.mean(od * od, axis=-1, keepdims=True) + EPS) * g * out_scale


def _diff_p_kernel(lam_ref, tbl_ref, q_ref, k_ref, vt_ref, idx_ref, base_ref, g_ref, o_ref,
                   m_scr, acc_scr, bias_scr, s_scr, smax_scr, *, tq, nh, hps, far_bucket, out_scale):
    hg = pl.program_id(0)
    bi = pl.program_id(1)
    qi = pl.program_id(2)
    nblk = tq // LANES

    @pl.when(jnp.logical_and(bi == 0, qi == 0))
    def _():
        for e in range(hps):
            t = _toeplitz_values(tbl_ref, idx_ref[...], N_T5_BUCKETS, nh, hg * hps + e, far_bucket) * LOG2E
            bias_scr[e] = base_ref[...]
            for blk in range(nblk):
                sl = slice(blk * LANES, (blk + 1) * LANES)
                bias_scr[e, 0, sl, sl] += t[0]
                if blk >= 1:
                    bias_scr[e, 0, (blk - 1) * LANES:blk * LANES, sl] += t[1]
            bias_scr[e, 1, (nblk - 1) * LANES:, :LANES] += t[1]

    qs = [_stack_diff_queries(q_ref[0, :, e * HEAD_DIM:(e + 1) * HEAD_DIM]) for e in range(hps)]
    _softmax_init(m_scr, acc_scr)
    gph = 2 * tq // ATTN_GROUP
    ngrp = hps * gph

    def scores(j, kind):
        start = pl.multiple_of(j * tq, tq)

        def group(i):
            e, gi = divmod(i, gph)
            k = k_ref[0, pl.ds(start, tq), e * HEAD_DIM:(e + 1) * HEAD_DIM]
            s = lax.dot_general(k, qs[e][gi * ATTN_GROUP:(gi + 1) * ATTN_GROUP], _NT, preferred_element_type=F32)
            if kind != "far":
                q0 = (gi * ATTN_GROUP) % tq
                s = s + bias_scr[e, kind, :, q0:q0 + ATTN_GROUP]
            return s
        return group

    def stage(t, slot, next_kind):
        nxt = None if next_kind is None else scores(t + 1, next_kind)
        _tile_stage(slot, lambda i: vt_ref[0, i // gph, t], m_scr, acc_scr, s_scr, smax_scr, ngrp, nxt)

    _pipelined_tiles(qi, 2, lambda kind: _first_tile(scores(0, kind), s_scr, smax_scr, ngrp), stage)
    outs = []
    for e in range(hps):
        o = _normalised_t(acc_scr[:, e * 2 * tq:(e + 1) * 2 * tq])
        outs.append(_diff_finalize(o, lam_ref[0], g_ref[...], out_scale).astype(BF16))
    o_ref[0] = jnp.concatenate(outs, axis=1)


def _diff_prompt(lam, tbl, qkv, vt, g, n_heads, out_scale):
    b, t, _ = qkv.shape
    tq = vt.shape[-1]
    far_bucket = N_T5_BUCKETS // 2 - 1
    pos = np.arange(LANES)
    idx = np.stack([_t5_bucket_np(pos[:, None] - pos[None, :]),
                    _t5_bucket_np(pos[:, None] - pos[None, :] - LANES)]).astype(np.int32)
    kq = np.arange(tq)
    base = np.zeros((2, tq, tq), np.float32)
    base[0] = np.where((kq[:, None] // CHUNK) <= (kq[None, :] // CHUNK), 0.0, NEG_INF)
    hps = math.gcd(n_heads, DIFF_HEADS)
    ngrp = n_heads // hps
    assert ngrp * hps == n_heads
    wd = hps * HEAD_DIM
    lanes = hps * 2 * tq
    return pl.pallas_call(
        functools.partial(_diff_p_kernel, tq=tq, nh=n_heads, hps=hps, far_bucket=far_bucket, out_scale=out_scale),
        grid=(ngrp, b, t // tq),
        in_specs=[
            pl.BlockSpec(memory_space=pltpu.SMEM),
            pl.BlockSpec(memory_space=pltpu.SMEM),
            pl.BlockSpec((1, tq, wd), lambda h, bi, qi: (bi, qi, h)),
            pl.BlockSpec((1, t, wd), lambda h, bi, qi: (bi, 0, ngrp + h)),
            pl.BlockSpec((1, hps, t // tq, VT_ROWS, tq), lambda h, bi, qi: (bi, h, 0, 0, 0)),
            pl.BlockSpec(idx.shape, lambda h, bi, qi: (0, 0, 0)),
            pl.BlockSpec(base.shape, lambda h, bi, qi: (0, 0, 0)),
            pl.BlockSpec((1, HEAD_DIM), lambda h, bi, qi: (0, 0)),
        ],
        out_specs=pl.BlockSpec((1, tq, wd), lambda h, bi, qi: (bi, qi, h)),
        out_shape=jax.ShapeDtypeStruct((b, t, n_heads * HEAD_DIM), BF16),
        scratch_shapes=[pltpu.VMEM((1, lanes), F32), pltpu.VMEM((VT_ROWS, lanes), F32),
                        pltpu.VMEM((hps, 2, tq, tq), F32), pltpu.VMEM((2, tq, lanes), F32),
                        pltpu.VMEM((2, 1, lanes), F32)],
        compiler_params=_params("arbitrary", "arbitrary", "arbitrary"),
        name="diff_attn_prompt",
    )(lam, tbl, qkv, qkv, vt, jnp.asarray(idx), jnp.asarray(base), g)


def _fox_p_kernel(fref_ref, q_ref, k_ref, vt_ref, f_ref, o_ref, m_scr, acc_scr, fcol_scr, s_scr, smax_scr,
                  *, tq, hps):
    bi = pl.program_id(0)
    hg = pl.program_id(1)
    qi = pl.program_id(2)
    nblk = tq // LANES

    @pl.when(qi == 0)
    def _():
        for e in range(hps):
            for c in range(f_ref.shape[2] * nblk):
                row = f_ref[0, e, c // nblk][:, (c % nblk) * LANES:(c % nblk + 1) * LANES]
                fcol_scr[e, c * LANES:(c + 1) * LANES, :] = jnp.broadcast_to(row, (LANES, LANES)).T

    n_heads = pl.num_programs(1) * hps
    frefs = [fref_ref[(bi * n_heads + hg * hps + e) * pl.num_programs(2) + qi] for e in range(hps)]
    _softmax_init(m_scr, acc_scr)

    gph = tq // ATTN_GROUP
    ngrp = hps * gph

    def scores(j, kind):
        start = pl.multiple_of(j * tq, tq)

        def group(i):
            e, gi = divmod(i, gph)
            hs = slice(e * HEAD_DIM, (e + 1) * HEAD_DIM)
            decay = (frefs[e] - fcol_scr[e, pl.ds(start, tq), :]) * LOG2E
            decay = jnp.concatenate([decay] * (ATTN_GROUP // LANES), axis=1)
            s = lax.dot_general(k_ref[0, pl.ds(start, tq), hs], q_ref[0, gi * ATTN_GROUP:(gi + 1) * ATTN_GROUP, hs],
                                _NT, preferred_element_type=F32)
            s = s + decay
            if kind != "far":
                key = lax.broadcasted_iota(jnp.int32, s.shape, 0)
                qry = lax.broadcasted_iota(jnp.int32, s.shape, 1) + gi * ATTN_GROUP
                s = jnp.where(key <= qry, s, NEG_INF)
            return s
        return group

    def stage(t, slot, next_kind):
        nxt = None if next_kind is None else scores(t + 1, next_kind)
        _tile_stage(slot, lambda i: vt_ref[0, i // gph, t], m_scr, acc_scr, s_scr, smax_scr, ngrp, nxt)

    _pipelined_tiles(qi, 1, lambda kind: _first_tile(scores(0, kind), s_scr, smax_scr, ngrp), stage)
    o_ref[0] = jnp.concatenate(
        [_normalised_t(acc_scr[:, e * tq:(e + 1) * tq]).astype(BF16) for e in range(hps)], axis=1)


def _fox_prompt(fref, qkv, vt, fcum, n_heads, col0, vt_head0):
    b, t, _ = qkv.shape
    tq = vt.shape[-1]
    nq = t // tq
    hps = math.gcd(math.gcd(n_heads, FOX_HEADS), math.gcd(col0, vt_head0))
    ngrp = n_heads // hps
    assert ngrp * hps == n_heads and col0 % hps == 0 and vt_head0 % hps == 0
    wd = hps * HEAD_DIM
    lanes = hps * tq
    return pl.pallas_call(
        functools.partial(_fox_p_kernel, tq=tq, hps=hps),
        grid=(b, ngrp, nq),
        in_specs=[
            pl.BlockSpec(memory_space=pltpu.SMEM),
            pl.BlockSpec((1, tq, wd), lambda bi, h, qi: (bi, qi, col0 // hps + h)),
            pl.BlockSpec((1, t, wd), lambda bi, h, qi: (bi, 0, (col0 + n_heads) // hps + h)),
            pl.BlockSpec((1, hps, nq, VT_ROWS, tq), lambda bi, h, qi: (bi, vt_head0 // hps + h, 0, 0, 0)),
            pl.BlockSpec((1, hps, nq, 1, tq), lambda bi, h, qi: (bi, h, 0, 0, 0)),
        ],
        out_specs=pl.BlockSpec((1, tq, wd), lambda bi, h, qi: (bi, qi, h)),
        out_shape=jax.ShapeDtypeStruct((b, t, n_heads * HEAD_DIM), BF16),
        scratch_shapes=[pltpu.VMEM((1, lanes), F32), pltpu.VMEM((VT_ROWS, lanes), F32),
                        pltpu.VMEM((hps, t, LANES), F32), pltpu.VMEM((2, tq, lanes), F32),
                        pltpu.VMEM((2, 1, lanes), F32)],
        compiler_params=_params("arbitrary", "arbitrary", "arbitrary"),
        name="fox_attn_prompt",
    )(fref, qkv, qkv, vt, fcum)


def _joint_softmax_pv(s_c, s_n, vc, vn):
    m = jnp.maximum(jnp.max(s_c, axis=1, keepdims=True), jnp.max(s_n, axis=1, keepdims=True))
    p_c = jnp.exp2(s_c - m)
    p_n = jnp.exp2(s_n - m)
    l = jnp.sum(p_c, axis=1, keepdims=True) + jnp.sum(p_n, axis=1, keepdims=True)
    acc = jnp.dot(p_c.astype(BF16), vc, preferred_element_type=F32)
    acc = acc + jnp.dot(p_n.astype(BF16), vn, preferred_element_type=F32)
    return acc / l


def _diff_s_kernel(lam_ref, q_ref, kn_ref, vn_ref, kc_ref, vc_ref, bnear_ref, bnew_ref, g_ref, o_ref, *, near, out_scale):
    qs = _stack_diff_queries(q_ref[0])
    kc = kc_ref[0, 0].astype(BF16)
    vc = vc_ref[0, 0].astype(BF16)
    p = kc.shape[0]
    s_c = lax.dot_general(qs, kc, _NT, preferred_element_type=F32)
    bnear = bnear_ref[0]
    s_c = jnp.concatenate([s_c[:, :p - near], s_c[:, p - near:] + jnp.concatenate([bnear, bnear], axis=0)], axis=1)
    bnew = bnew_ref[0]
    s_n = lax.dot_general(qs, kn_ref[0], _NT, preferred_element_type=F32) + jnp.concatenate([bnew, bnew], axis=0)
    o = _joint_softmax_pv(s_c, s_n, vc, vn_ref[0])
    o_ref[0] = _diff_finalize(o, lam_ref[0], g_ref[...], out_scale).astype(BF16)


def _diff_sample(lam, qkv, kc, vc, bnear, bnew, g, n_heads, out_scale):
    b, t, _ = qkv.shape
    p = kc.shape[2]
    near = bnear.shape[-1]
    hd = HEAD_DIM
    return pl.pallas_call(
        functools.partial(_diff_s_kernel, near=near, out_scale=out_scale),
        grid=(b, n_heads),
        in_specs=[
            pl.BlockSpec(memory_space=pltpu.SMEM),
            pl.BlockSpec((1, t, hd), lambda bi, h: (bi, 0, h)),
            pl.BlockSpec((1, t, hd), lambda bi, h: (bi, 0, n_heads + h)),
            pl.BlockSpec((1, t, hd), lambda bi, h: (bi, 0, 2 * n_heads + h)),
            pl.BlockSpec((1, 1, p, hd), lambda bi, h: (bi, h, 0, 0)),
            pl.BlockSpec((1, 1, p, hd), lambda bi, h: (bi, h, 0, 0)),
            pl.BlockSpec((1, t, near), lambda bi, h: (h, 0, 0)),
            pl.BlockSpec((1, t, t), lambda bi, h: (h, 0, 0)),
            pl.BlockSpec((1, hd), lambda bi, h: (0, 0)),
        ],
        out_specs=pl.BlockSpec((1, t, hd), lambda bi, h: (bi, 0, h)),
        out_shape=jax.ShapeDtypeStruct((b, t, n_heads * hd), BF16),
        compiler_params=_params("parallel", "parallel"),
        name="diff_attn_sample",
    )(lam, qkv, qkv, qkv, kc, vc, bnear, bnew, g)


def _fox_s_kernel(fref_ref, q_ref, kn_ref, vn_ref, kc_ref, vc_ref, fc_ref, fn_ref, o_ref):
    bi = pl.program_id(0)
    h = pl.program_id(1)
    q = q_ref[0]
    t = q.shape[0]
    fref = fref_ref[bi * pl.num_programs(1) + h]
    s_c = lax.dot_general(q, kc_ref[0, 0].astype(BF16), _NT, preferred_element_type=F32) + (fref - fc_ref[0, 0]) * LOG2E
    s_n = lax.dot_general(q, kn_ref[0], _NT, preferred_element_type=F32) + (fref - fn_ref[0, 0][:, :t]) * LOG2E
    row = lax.broadcasted_iota(jnp.int32, s_n.shape, 0)
    col = lax.broadcasted_iota(jnp.int32, s_n.shape, 1)
    s_n = jnp.where(col <= row, s_n, NEG_INF)
    o_ref[0] = _joint_softmax_pv(s_c, s_n, vc_ref[0, 0].astype(BF16), vn_ref[0]).astype(BF16)


def _fox_sample(fref, qkv, kc, vc, fc, fn, n_heads, col0):
    b, t, _ = qkv.shape
    p = kc.shape[2]
    hd = HEAD_DIM
    return pl.pallas_call(
        _fox_s_kernel,
        grid=(b, n_heads),
        in_specs=[
            pl.BlockSpec(memory_space=pltpu.SMEM),
            pl.BlockSpec((1, t, hd), lambda bi, h: (bi, 0, col0 + h)),
            pl.BlockSpec((1, t, hd), lambda bi, h: (bi, 0, col0 + n_heads + h)),
            pl.BlockSpec((1, t, hd), lambda bi, h: (bi, 0, col0 + 2 * n_heads + h)),
            pl.BlockSpec((1, 1, p, hd), lambda bi, h: (bi, h, 0, 0)),
            pl.BlockSpec((1, 1, p, hd), lambda bi, h: (bi, h, 0, 0)),
            pl.BlockSpec((1, 1, 1, p), lambda bi, h: (bi, h, 0, 0)),
            pl.BlockSpec((1, 1, 1, fn.shape[-1]), lambda bi, h: (bi, h, 0, 0)),
        ],
        out_specs=pl.BlockSpec((1, t, hd), lambda bi, h: (bi, 0, h)),
        out_shape=jax.ShapeDtypeStruct((b, t, n_heads * hd), BF16),
        compiler_params=_params("parallel", "parallel"),
        name="fox_attn_sample",
    )(fref, qkv, qkv, qkv, kc, vc, fc, fn)


def _band_p_kernel(tbl_ref, q_ref, k_ref, vt_ref, idx_ref, base_ref, o_ref, bm_scr, *, tq, nkb, nh, hps):
    hg = pl.program_id(0)
    bi = pl.program_id(1)
    i = pl.program_id(2)
    nblk = tq // LANES
    qoff = (nkb - 1) * nblk

    @pl.when(jnp.logical_and(bi == 0, i == 0))
    def _():
        for e in range(hps):
            t = _toeplitz_values(tbl_ref, idx_ref[...], 2 * REL_CLIP + 1, nh, hg * hps + e, 0)
            bm_scr[e] = base_ref[...]
            for bq in range(nblk):
                kd = qoff + bq
                bm_scr[e, kd * LANES:(kd + 1) * LANES, bq * LANES:(bq + 1) * LANES] += t[0] * LOG2E
                bm_scr[e, (kd - 1) * LANES:kd * LANES, bq * LANES:(bq + 1) * LANES] += t[1] * LOG2E

    def head_scores(e):
        hs = slice(e * HEAD_DIM, (e + 1) * HEAD_DIM)
        q = q_ref[0, :, hs]
        ss = []
        for blk in range(nkb):
            jb = i - (nkb - 1) + blk
            jc = jnp.maximum(jb, 0)
            k = k_ref[0, pl.ds(pl.multiple_of(jc * tq, tq), tq), hs]
            s = lax.dot_general(k, q, _NT, preferred_element_type=F32) + bm_scr[e, blk * tq:(blk + 1) * tq, :]
            if blk < nkb - 1:
                s = s + jnp.where(jb < 0, NEG_INF, 0.0)
            ss.append(s)
        return ss, functools.reduce(jnp.maximum, [jnp.max(s, axis=0, keepdims=True) for s in ss])

    outs = []
    pending = [head_scores(e) for e in range(min(BAND_LOOKAHEAD, hps))]
    for e in range(hps):
        ss, m = pending.pop(0)
        if e + BAND_LOOKAHEAD < hps:
            pending.append(head_scores(e + BAND_LOOKAHEAD))
        acc = None
        for blk, s in enumerate(ss):
            jc = jnp.maximum(i - (nkb - 1) + blk, 0)
            a = jnp.dot(vt_ref[0, e, jc], jnp.exp2(s - m).astype(BF16), preferred_element_type=F32)
            acc = a if acc is None else acc + a
        outs.append(_normalised_t(acc).astype(BF16))
    o_ref[0] = jnp.concatenate(outs, axis=1)


def _band_prompt(tbl, qkv, vt, n_heads):
    b, t, _ = qkv.shape
    tq = vt.shape[-1]
    nkb = C_BAND // tq + 1
    assert (nkb - 1) * tq == C_BAND and tq % LANES == 0 and LANES >= REL_CLIP
    pos = np.arange(LANES)
    rel = pos[:, None] - pos[None, :]
    idx = (np.stack([np.clip(rel, -REL_CLIP, REL_CLIP), np.clip(rel - LANES, -REL_CLIP, REL_CLIP)]) + REL_CLIP).astype(np.int32)
    k_pos = np.arange(nkb * tq)[:, None]
    q_pos = C_BAND + np.arange(tq)[None, :]
    kc, qc = k_pos // CHUNK, q_pos // CHUNK
    base = np.where((kc <= qc) & (qc - kc <= C_PREV_CHUNKS), 0.0, NEG_INF).astype(np.float32)
    hps = BAND_HEADS
    ngrp = n_heads // hps
    assert ngrp * hps == n_heads
    wd = hps * HEAD_DIM
    return pl.pallas_call(
        functools.partial(_band_p_kernel, tq=tq, nkb=nkb, nh=n_heads, hps=hps),
        grid=(ngrp, b, t // tq),
        in_specs=[
            pl.BlockSpec(memory_space=pltpu.SMEM),
            pl.BlockSpec((1, tq, wd), lambda h, bi, i: (bi, i, h)),
            pl.BlockSpec((1, t, wd), lambda h, bi, i: (bi, 0, ngrp + h)),
            pl.BlockSpec((1, hps, t // tq, VT_ROWS, tq), lambda h, bi, i: (bi, h, 0, 0, 0)),
            pl.BlockSpec(idx.shape, lambda h, bi, i: (0, 0, 0)),
            pl.BlockSpec(base.shape, lambda h, bi, i: (0, 0)),
        ],
        out_specs=pl.BlockSpec((1, tq, wd), lambda h, bi, i: (bi, i, h)),
        out_shape=jax.ShapeDtypeStruct((b, t, n_heads * HEAD_DIM), BF16),
        scratch_shapes=[pltpu.VMEM((hps, nkb * tq, tq), F32)],
        compiler_params=_params("arbitrary", "arbitrary", "arbitrary"),
        name="band_attn_prompt",
    )(tbl, qkv, qkv, vt, jnp.asarray(idx), jnp.asarray(base))


def _band_s_kernel(q_ref, kn_ref, vn_ref, kc_ref, vc_ref, bmc_ref, bmn_ref, o_ref, *, n_heads):
    outs = []
    for h in range(n_heads):
        hs = slice(h * HEAD_DIM, (h + 1) * HEAD_DIM)
        q = q_ref[0, :, hs]
        kc = kc_ref[0, :, h, :].astype(BF16)
        vc = vc_ref[0, :, h, :].astype(BF16)
        s_c = lax.dot_general(q, kc, _NT, preferred_element_type=F32) + bmc_ref[h]
        s_n = lax.dot_general(q, kn_ref[0, :, hs], _NT, preferred_element_type=F32) + bmn_ref[h]
        outs.append(_joint_softmax_pv(s_c, s_n, vc, vn_ref[0, :, hs]).astype(BF16))
    o_ref[0] = jnp.concatenate(outs, axis=1)


def _band_sample(qkv, kc, vc, bmc, bmn, n_heads):
    b, t, _ = qkv.shape
    lc = kc.shape[1]
    cw = n_heads * HEAD_DIM
    return pl.pallas_call(
        functools.partial(_band_s_kernel, n_heads=n_heads),
        grid=(b,),
        in_specs=[
            pl.BlockSpec((1, t, cw), lambda bi: (bi, 0, 0)),
            pl.BlockSpec((1, t, cw), lambda bi: (bi, 0, 1)),
            pl.BlockSpec((1, t, cw), lambda bi: (bi, 0, 2)),
            pl.BlockSpec((1, lc, n_heads, HEAD_DIM), lambda bi: (bi, 0, 0, 0)),
            pl.BlockSpec((1, lc, n_heads, HEAD_DIM), lambda bi: (bi, 0, 0, 0)),
            pl.BlockSpec(bmc.shape, lambda bi: (0, 0, 0)),
            pl.BlockSpec(bmn.shape, lambda bi: (0, 0, 0)),
        ],
        out_specs=pl.BlockSpec((1, t, cw), lambda bi: (bi, 0, 0)),
        out_shape=jax.ShapeDtypeStruct((b, t, cw), BF16),
        compiler_params=_params("parallel"),
        name="band_attn_sample",
    )(qkv, qkv, qkv, kc, vc, bmc, bmn)


def _t5_bucket_np(rel):
    nb = N_T5_BUCKETS // 2
    max_exact = nb // 2
    n = np.abs(rel)
    nf = np.maximum(n, 1).astype(np.float64)
    large = max_exact + (np.log(nf / max_exact) / math.log(T5_MAX_DIST / max_exact) * (nb - max_exact)).astype(np.int64)
    large = np.minimum(large, nb - 1)
    return np.where(rel > 0, nb, 0) + np.where(n < max_exact, n, large)


def _toeplitz(table, index_of_rel, q_pos, k_pos):
    r, c = len(q_pos), len(k_pos)
    assert (np.diff(q_pos) == 1).all() and (np.diff(k_pos) == 1).all()
    rels = (k_pos[0] - q_pos[0]) - (r - 1) + np.arange(r + c - 1)
    v = jnp.pad(table.astype(F32)[index_of_rel(rels)].T, ((0, 0), (0, 1)))
    x = jnp.tile(v, (1, r))[:, :r * (r + c - 1)].reshape(v.shape[0], r, r + c - 1)
    return x[:, :, r - 1:r - 1 + c]


def _t5_bias_tiles(t5_table, q_pos, k_pos, far_bucket):
    mask = (k_pos[None, :] // CHUNK) <= (q_pos[:, None] // CHUNK)
    bias = _toeplitz(t5_table, _t5_bucket_np, q_pos, k_pos)
    bias = (bias - t5_table.astype(F32)[far_bucket][:, None, None]) * LOG2E
    return jnp.where(mask[None], bias, NEG_INF)


def _band_bias_tiles(rel_table, q_pos, k_pos):
    qc = q_pos[:, None] // CHUNK
    kc = k_pos[None, :] // CHUNK
    mask = (kc <= qc) & (qc - kc <= C_PREV_CHUNKS) & (k_pos[None, :] >= 0)
    bias = _toeplitz(rel_table, lambda rel: np.clip(rel, -REL_CLIP, REL_CLIP) + REL_CLIP, q_pos, k_pos) * LOG2E
    return jnp.where(mask[None], bias, NEG_INF)


def kernel(x_prompt, x_sample, cache_a_k, cache_a_v, cache_b_k, cache_b_v, cache_b_logf, cache_c_k, cache_c_v, c_prompt, c_sample, w_ada, b_ada, norm_g, w_ffn_in, w_ffn_out, w_in_ab, b_forget, w_out_ab, lambda_q1, lambda_k1, lambda_q2, lambda_k2, subln_g, t5_table, w_in_c, w_out_c, c_rel_bias, final_g):
    depth = w_ada.shape[0]
    bsz, seq, d = x_prompt.shape
    dbsz, dseq, _ = x_sample.shape
    past = cache_b_logf.shape[2]
    h_a = cache_a_k.shape[3]
    h_b = cache_b_k.shape[3]
    h_c = cache_c_k.shape[3]
    wa, wb, cw = h_a * HEAD_DIM, h_b * HEAD_DIM, h_c * HEAD_DIM
    assert bsz == dbsz and wa == wb and 3 * wa + 3 * wb == 3 * cw
    assert LANES >= T5_MAX_DIST and SAMPLE_NEAR >= T5_MAX_DIST and past >= SAMPLE_NEAR
    far_bucket = (N_T5_BUCKETS // 2) - 1

    mods = _ada_mods(jnp.concatenate([c_prompt, c_sample], axis=0), w_ada, b_ada)
    mods = mods.reshape(depth, 2, bsz, 9, d).transpose(0, 3, 1, 2, 4).reshape(depth * 9 * 2, bsz, 1, d)

    def slot(l, k, grp):
        return (l * 9 + k) * 2 + grp

    xs = [x_prompt, x_sample]
    ab_states = [[], []]
    c_states = [[], []]
    for l in range(depth):
        i = l // 2
        last = l == depth - 1
        w1 = _prep_ffn_weights(w_ffn_in[l, 0], w_ffn_out[l, 0])
        w2 = _prep_ffn_weights(w_ffn_in[l, 1], w_ffn_out[l, 1])
        for grp in range(2):
            xs[grp] = _ffn(xs[grp], mods, [slot(l, k, grp) for k in range(3)], norm_g[l, 0], *w1)

        if l % 2 == 0:
            lam_init = 0.8 - 0.6 * math.exp(-0.3 * l)
            lam = (jnp.exp(jnp.sum(lambda_q1[i].astype(F32) * lambda_k1[i].astype(F32)))
                   - jnp.exp(jnp.sum(lambda_q2[i].astype(F32) * lambda_k2[i].astype(F32))) + lam_init).reshape(1)
            n_main = 3 * wa + 3 * wb
            w_main = w_in_ab[i][:, :n_main].astype(BF16)
            wf = jnp.pad(w_in_ab[i][:, n_main:].T.astype(BF16), ((0, 16 - h_b), (0, 0)))
            bfo = b_forget[i].astype(F32).reshape(h_b, 1)
            w_out = w_out_ab[i].astype(BF16)
            g_sub = subln_g[i].astype(F32).reshape(1, HEAD_DIM)
            tn = min(PROJ_COL_TILE, wa)
            taps = _seg_taps(tn, wa, (1, 2, 4, 5), False)
            cs = np.ones((n_main,), np.float32)
            cs[:wa] = DA_HALF ** -0.5 * LOG2E
            cs[3 * wa:3 * wa + wb] = HEAD_DIM ** -0.5 * LOG2E
            for grp in range(2):
                x = xs[grp]
                b, t, _ = x.shape
                tap_shapes = [((b, t, wa), True)] * 4
                if grp == 0:
                    qkv, ka, va, kb, vb, vt, logf_t = _proj(
                        x, mods, [slot(l, 3, grp), slot(l, 4, grp)], norm_g[l, 1], w_main, cs, tn, tap_shapes, taps,
                        wf, bfo, vt_cols=_vt_cols(tn, wa, (2, 5)), vt_tile=ROW_TILE)
                else:
                    qkv, ka, va, kb, vb, logf_t = _proj(
                        x, mods, [slot(l, 3, grp), slot(l, 4, grp)], norm_g[l, 1], w_main, cs, tn, tap_shapes, taps,
                        wf, bfo)
                logf = logf_t.reshape(h_b, b, t).transpose(1, 2, 0)
                ab_states[grp].append((ka.reshape(b, t, h_a, HEAD_DIM), va.reshape(b, t, h_a, HEAD_DIM),
                                       kb.reshape(b, t, h_b, HEAD_DIM), vb.reshape(b, t, h_b, HEAD_DIM), logf))
                logf_bh = logf_t.reshape(h_b, b, t).transpose(1, 0, 2)
                if grp == 0:
                    o_a = _diff_prompt(lam, t5_table.astype(F32).reshape(-1), qkv, vt, g_sub, h_a, 1.0 - lam_init)
                    fcum = _cumsum(logf_bh, jnp.zeros((b, h_b, LANES), F32))
                    nq = t // ROW_TILE
                    fref = fcum[:, :, ::ROW_TILE].reshape(-1)
                    o_b = _fox_prompt(fref, qkv, vt, fcum.reshape(b, h_b, nq, 1, ROW_TILE), h_b, 3 * h_a, h_a)
                else:
                    q_pos = past + np.arange(t)
                    bnear = _t5_bias_tiles(t5_table, q_pos, past - SAMPLE_NEAR + np.arange(SAMPLE_NEAR), far_bucket)
                    bnew = _t5_bias_tiles(t5_table, q_pos, q_pos, far_bucket)
                    kc = cache_a_k[i].transpose(0, 2, 1, 3)
                    vc = cache_a_v[i].transpose(0, 2, 1, 3)
                    o_a = _diff_sample(lam, qkv, kc, vc, bnear, bnew, g_sub, h_a, 1.0 - lam_init)
                    fc = _cumsum(cache_b_logf[i].astype(F32).transpose(0, 2, 1), jnp.zeros((b, h_b, LANES), F32))
                    carry = fc[:, :, past - 1:past]
                    fn = _cumsum(jnp.pad(logf_bh, ((0, 0), (0, 0), (0, LANES - t))),
                                 jnp.broadcast_to(carry, (b, h_b, LANES)))
                    o_b = _fox_sample(carry.reshape(-1), qkv, cache_b_k[i].transpose(0, 2, 1, 3),
                                      cache_b_v[i].transpose(0, 2, 1, 3), fc.reshape(b, h_b, 1, past),
                                      fn.reshape(b, h_b, 1, LANES), h_b, 3 * h_a)
                xs[grp] = _outproj([o_a, o_b], w_out, x, mods, slot(l, 5, grp))
        else:
            w_in = w_in_c[i].astype(BF16)
            w_out = w_out_c[i].astype(BF16)
            tn = min(PROJ_COL_TILE, cw)
            taps = _seg_taps(tn, cw, (1, 2), True)
            cs = np.ones((3 * cw,), np.float32)
            cs[:cw] = HEAD_DIM ** -0.5 * LOG2E
            for grp in range(2):
                x = xs[grp]
                b, t, _ = x.shape
                keep = min(C_BAND, t)
                assert keep == min(t, ROW_TILE)
                tap_shapes = [((b, keep, cw), False)] * 2
                if grp == 0:
                    qkv, k_new, v_new, vt = _proj(
                        x, mods, [slot(l, 3, grp), slot(l, 4, grp)], norm_g[l, 1], w_in, cs, tn, tap_shapes, taps,
                        vt_cols=_vt_cols(tn, cw, (2,)), vt_tile=BAND_TILE)
                else:
                    qkv, k_new, v_new = _proj(
                        x, mods, [slot(l, 3, grp), slot(l, 4, grp)], norm_g[l, 1], w_in, cs, tn, tap_shapes, taps)
                c_states[grp].append((k_new.reshape(b, keep, h_c, HEAD_DIM), v_new.reshape(b, keep, h_c, HEAD_DIM)))
                if grp == 0:
                    o_c = _band_prompt(c_rel_bias[i].astype(F32).reshape(-1), qkv, vt, h_c)
                else:
                    lc = cache_c_k.shape[2]
                    q_pos = past + np.arange(t)
                    bmc = _band_bias_tiles(c_rel_bias[i], q_pos, past - lc + np.arange(lc))
                    bmn = _band_bias_tiles(c_rel_bias[i], q_pos, q_pos)
                    o_c = _band_sample(qkv, cache_c_k[i], cache_c_v[i], bmc, bmn, h_c)
                xs[grp] = _outproj([o_c], w_out, x, mods, slot(l, 5, grp))

        for grp in range(2):
            xs[grp] = _ffn(xs[grp], mods, [slot(l, k, grp) for k in (6, 7, 8)], norm_g[l, 2], *w2,
                           final_g=final_g if last else None)

    outs = [xs[0], xs[1]]
    for grp in range(2):
        st = ab_states[grp]
        outs += [jnp.stack([s[k] for s in st]) for k in range(5)]
        st = c_states[grp]
        outs += [jnp.stack([s[k] for s in st]) for k in range(2)]
    return tuple(outs)
```

```python
import functools
import math

import numpy as np
import jax
import jax.numpy as jnp
from jax import lax
from jax.experimental import pallas as pl
from jax.experimental.pallas import tpu as pltpu

F32 = jnp.float32
BF16 = jnp.bfloat16

CHUNK = 64
HEAD_DIM = 128
DA_HALF = HEAD_DIM // 2
N_T5_BUCKETS = 32
T5_MAX_DIST = 128
C_PREV_CHUNKS = 8
C_BAND = C_PREV_CHUNKS * CHUNK
REL_CLIP = 128
EPS = 1e-6
NEG_INF = -1e30
LOG2E = math.log2(math.e)
VT_ROWS = HEAD_DIM + 16

VMEM_LIMIT_BYTES = 56 * 1024 * 1024
LANES = 128

ROW_TILE = 512
FF_TILE = 512
FFN_OUT_TILE = 512
FFN_ROW_SPLIT = 2
PROJ_ROW_SPLIT = 2
PROJ_COL_TILE = 1024
ADA_COL_TILE = 1024
BAND_TILE = 256
BAND_HEADS = 8
BAND_LOOKAHEAD = 3
DIFF_HEADS = 4
FOX_HEADS = 4
ATTN_GROUP = 512
SAMPLE_NEAR = 256

_NT = (((1,), (1,)), ((), ()))


def _params(*sem, flags=None):
    return pltpu.CompilerParams(dimension_semantics=sem, vmem_limit_bytes=VMEM_LIMIT_BYTES, flags=flags)


def _modulated_norm(x, g, shift, scale):
    y = x * lax.rsqrt(jnp.mean(x * x, axis=-1, keepdims=True) + EPS) * g
    return y * (1.0 + scale) + shift


def _normed_rows(x_ref, shift_ref, scale_ref, g_ref, h_scr, n_split, r):
    nb, tt, d = x_ref.shape
    rows = nb * tt // n_split
    if nb == 1:
        x, sh, sc = x_ref[:, r * rows:(r + 1) * rows, :], shift_ref[...], scale_ref[...]
    else:
        seqs = slice(r * (nb // n_split), (r + 1) * (nb // n_split))
        x, sh, sc = x_ref[seqs], shift_ref[seqs], scale_ref[seqs]
    hr = _modulated_norm(x, g_ref[...], sh, sc).reshape(rows, d).astype(BF16)
    h_scr[r * rows:(r + 1) * rows, :] = hr
    return hr


def _ada_kernel(c_ref, w_ref, b_ref, o_ref):
    c = c_ref[...]
    a = (c * jax.nn.sigmoid(c)).astype(BF16)
    o_ref[0] = jnp.dot(a, w_ref[0].astype(BF16), preferred_element_type=F32) + b_ref[0]


def _ada_mods(c_all, w_ada, b_ada):
    depth, d, n = w_ada.shape
    r = c_all.shape[0]
    tn = math.gcd(n, ADA_COL_TILE)
    return pl.pallas_call(
        _ada_kernel,
        grid=(depth, n // tn),
        in_specs=[
            pl.BlockSpec((r, d), lambda l, j: (0, 0)),
            pl.BlockSpec((1, d, tn), lambda l, j: (l, 0, j)),
            pl.BlockSpec((1, 1, tn), lambda l, j: (l, 0, j)),
        ],
        out_specs=pl.BlockSpec((1, r, tn), lambda l, j: (l, 0, j)),
        out_shape=jax.ShapeDtypeStruct((depth, r, n), F32),
        compiler_params=_params("arbitrary", "arbitrary"),
        name="ada_mods",
    )(c_all, w_ada, b_ada.reshape(depth, 1, n))


def _row_blocking(x):
    b, t, _ = x.shape
    if t >= ROW_TILE:
        assert t % ROW_TILE == 0
        return 1, ROW_TILE, t // ROW_TILE, b * (t // ROW_TILE)
    assert (b * t) % 8 == 0
    return b, t, 1, 1


def _x_index(nb, tps):
    if nb == 1:
        return lambda i, j: (i // tps, i % tps, 0)
    return lambda i, j: (0, 0, 0)


def _mod_spec(nb, tps, d, slot):
    if nb == 1:
        return pl.BlockSpec((None, 1, 1, d), lambda i, j: (slot, i // tps, 0, 0))
    return pl.BlockSpec((None, nb, 1, d), lambda i, j: (slot, 0, 0, 0))


def _ffn_kernel(*refs, nb, tt, n_ff, last_cols, n_out, final):
    if final:
        (x_ref, shift_ref, scale_ref, gate_ref, g_ref, wg_ref, wu_ref, wo_ref, fg_ref,
         o_ref, h_scr, a_scr) = refs
    else:
        (x_ref, shift_ref, scale_ref, gate_ref, g_ref, wg_ref, wu_ref, wo_ref,
         o_ref, h_scr, a_scr) = refs
    j = pl.program_id(1)
    d = x_ref.shape[-1]
    tm = nb * tt
    tf = wg_ref.shape[1]
    tn = wo_ref.shape[1]

    rows = tm // FFN_ROW_SPLIT
    normed_rows = functools.partial(_normed_rows, x_ref, shift_ref, scale_ref, g_ref, h_scr, FFN_ROW_SPLIT)

    def hidden_chunk(rows_of, cols=tf):
        def products(r):
            hr = rows_of(r)
            return (jnp.dot(hr, wg_ref[:, :cols], preferred_element_type=F32),
                    jnp.dot(hr, wu_ref[:, :cols], preferred_element_type=F32))

        parts = []
        cur = products(0)
        for r in range(FFN_ROW_SPLIT):
            nxt = products(r + 1) if r + 1 < FFN_ROW_SPLIT else None
            gg, uu = cur
            parts.append((gg * jax.nn.sigmoid(gg) * uu).astype(BF16))
            cur = nxt
        a_scr[j, :, :cols] = jnp.concatenate(parts, axis=0)

    kept_rows = lambda r: h_scr[r * rows:(r + 1) * rows, :]
    n_full = n_ff if last_cols == tf else n_ff - 1
    assert n_full >= 1

    @pl.when(j == 0)
    def _():
        hidden_chunk(normed_rows)

    @pl.when(jnp.logical_and(j > 0, j < n_full))
    def _():
        hidden_chunk(kept_rows)

    if n_full < n_ff:
        @pl.when(j == n_ff - 1)
        def _():
            hidden_chunk(kept_rows, last_cols)

    @pl.when(j >= n_ff)
    def _():
        a = jnp.concatenate([a_scr[f] for f in range(n_full)] + [a_scr[f, :, :last_cols] for f in range(n_full, n_ff)],
                            axis=1)
        acc = jnp.dot(a, wo_ref[...], preferred_element_type=F32).reshape(nb, tt, tn)
        for n in range(n_out):
            @pl.when(j == n_ff + n)
            def _(n=n):
                cs = slice(n * tn, (n + 1) * tn)
                y = x_ref[:, :, cs] + 0.5 * gate_ref[:, :, cs] * acc
                if final:
                    o_ref[:, :, cs] = y
                else:
                    o_ref[...] = y

    if final:
        @pl.when(j == n_ff + n_out - 1)
        def _():
            y = o_ref[...]
            o_ref[...] = y * lax.rsqrt(jnp.mean(y * y, axis=-1, keepdims=True) + EPS) * fg_ref[...]


def _ffn(x, mods, slots, g, wg_p, wu_p, w_out_p, final_g=None):
    b, t, d = x.shape
    nb, tt, tps, nrows = _row_blocking(x)
    ff = w_out_p.shape[0]
    n_ff = -(-ff // FF_TILE)
    last_cols = ff - (n_ff - 1) * FF_TILE
    tn = math.gcd(d, FFN_OUT_TILE)
    n_out = d // tn
    final = final_g is not None
    xi = _x_index(nb, tps)
    in_specs = [
        pl.BlockSpec((nb, tt, d), xi),
        _mod_spec(nb, tps, d, slots[0]),
        _mod_spec(nb, tps, d, slots[1]),
        _mod_spec(nb, tps, d, slots[2]),
        pl.BlockSpec((1, d), lambda i, j: (0, 0)),
        pl.BlockSpec((d, FF_TILE), lambda i, j: (0, jnp.minimum(j, n_ff - 1))),
        pl.BlockSpec((d, FF_TILE), lambda i, j: (0, jnp.minimum(j, n_ff - 1))),
        pl.BlockSpec((ff, tn), lambda i, j: (0, jnp.maximum(j - n_ff, 0))),
    ]
    args = [x, mods, mods, mods, g.reshape(1, d), wg_p, wu_p, w_out_p]
    if final:
        in_specs.append(pl.BlockSpec((1, d), lambda i, j: (0, 0)))
        args.append(final_g.reshape(1, d))
    return pl.pallas_call(
        functools.partial(_ffn_kernel, nb=nb, tt=tt, n_ff=n_ff, last_cols=last_cols, n_out=n_out, final=final),
        grid=(nrows, n_ff + n_out),
        in_specs=in_specs,
        out_specs=(pl.BlockSpec((nb, tt, d), xi) if final else
                   pl.BlockSpec((nb, tt, tn), lambda i, j: xi(i, j)[:2] + (jnp.maximum(j - n_ff, 0),))),
        out_shape=jax.ShapeDtypeStruct((b, t, d), F32),
        scratch_shapes=[pltpu.VMEM((nb * tt, d), BF16), pltpu.VMEM((n_ff, nb * tt, FF_TILE), BF16)],
        compiler_params=_params("parallel", "arbitrary"),
        name="ffn",
    )(*args)


def _prep_ffn_weights(w_in, w_out):
    ff = w_out.shape[0]
    return w_in[:, :ff].astype(BF16), w_in[:, ff:].astype(BF16), w_out.astype(BF16)


def _log_sigmoid(x):
    return jnp.minimum(x, 0.0) - jnp.log1p(jnp.exp(-jnp.abs(x)))


def _proj_kernel(*refs, nb, tt, tn, tps, taps, has_f, vt_cols, vt_tile):
    x_ref, shift_ref, scale_ref, g_ref, w_ref, cs_ref = refs[:6]
    pos = 6
    if has_f:
        wf_ref, bf_ref = refs[6:8]
        pos = 8
    obf_ref = refs[pos]
    tap_refs = refs[pos + 1:pos + 1 + len({tp[0] for tp in taps})]
    pos = pos + 1 + len(tap_refs)
    if vt_cols:
        vt_ref = refs[pos]
        pos += 1
    if has_f:
        logf_ref = refs[pos]
        pos += 1
    h_scr = refs[pos]
    i = pl.program_id(0)
    j = pl.program_id(1)
    d = x_ref.shape[-1]

    def store_attention_copy(y):
        obf_ref[...] = (y * cs_ref[...]).astype(BF16).reshape(nb, tt, tn)

    assert all(tp[1] > 0 for tp in taps) and all(vc[0] > 0 for vc in vt_cols)

    @pl.when(j == 0)
    def _():
        hs = []
        ys = []
        for r in range(PROJ_ROW_SPLIT):
            hs.append(_normed_rows(x_ref, shift_ref, scale_ref, g_ref, h_scr, PROJ_ROW_SPLIT, r))
            ys.append(jnp.dot(hs[-1], w_ref[...], preferred_element_type=F32))
        store_attention_copy(jnp.concatenate(ys, axis=0))
        if has_f:
            fr = lax.dot_general(wf_ref[...], jnp.concatenate(hs, axis=0), _NT, preferred_element_type=F32)
            logf_ref[...] = _log_sigmoid(fr[:logf_ref.shape[0]] + bf_ref[...])

    @pl.when(j > 0)
    def _():
        y = jnp.dot(h_scr[...], w_ref[...], preferred_element_type=F32)
        store_attention_copy(y)
        _proj_taps(y, i, j, tap_refs, vt_ref if vt_cols else None, nb=nb, tt=tt, tn=tn, tps=tps, taps=taps,
                   vt_cols=vt_cols, vt_tile=vt_tile)


def _proj_taps(y, i, j, tap_refs, vt_ref, *, nb, tt, tn, tps, taps, vt_cols, vt_tile):
    for out_idx, jval, col0, last_rows_only in taps:
        cond = j == jval
        if last_rows_only and tps > 1:
            cond = jnp.logical_and(cond, i % tps == tps - 1)

        @pl.when(cond)
        def _(out_idx=out_idx, col0=col0):
            tap_refs[out_idx][:, :, col0:col0 + tn] = y.reshape(nb, tt, tn)

    for jval, head0 in vt_cols:
        @pl.when(j == jval)
        def _(head0=head0):
            for hh in range(tn // HEAD_DIM):
                yt = y[:, hh * HEAD_DIM:(hh + 1) * HEAD_DIM].T.astype(BF16)
                yt = jnp.concatenate([yt, jnp.ones((VT_ROWS - HEAD_DIM, yt.shape[1]), BF16)], axis=0)
                for s in range(tt // vt_tile):
                    vt_ref[0, head0 + hh, s] = yt[:, s * vt_tile:(s + 1) * vt_tile]


def _seg_taps(tn, width, segments, last_rows_only):
    per = width // tn
    assert per * tn == width
    return [(o, sidx * per + s, s * tn, last_rows_only) for o, sidx in enumerate(segments) for s in range(per)]


def _vt_cols(tn, width, segments):
    per = width // tn
    hpb = tn // HEAD_DIM
    return [(sidx * per + s, (o * per + s) * hpb) for o, sidx in enumerate(segments) for s in range(per)]


def _proj(x, mods, slots, g, w_bf, col_scale, tn, tap_shapes, taps, wf=None, bf=None, vt_cols=(), vt_tile=None):
    b, t, d = x.shape
    n = w_bf.shape[1]
    nb, tt, tps, nrows = _row_blocking(x)
    has_f = wf is not None
    assert not vt_cols or nb == 1
    in_specs = [
        pl.BlockSpec((nb, tt, d), _x_index(nb, tps)),
        _mod_spec(nb, tps, d, slots[0]),
        _mod_spec(nb, tps, d, slots[1]),
        pl.BlockSpec((1, d), lambda i, j: (0, 0)),
        pl.BlockSpec((d, tn), lambda i, j: (0, j)),
        pl.BlockSpec((1, tn), lambda i, j: (0, j)),
    ]
    args = [x, mods, mods, g.reshape(1, d), w_bf, jnp.asarray(col_scale, F32).reshape(1, n)]
    if has_f:
        in_specs += [pl.BlockSpec(wf.shape, lambda i, j: (0, 0)), pl.BlockSpec(bf.shape, lambda i, j: (0, 0))]
        args += [wf, bf]
    xi = _x_index(nb, tps)
    out_specs = [pl.BlockSpec((nb, tt, tn), lambda i, j: xi(i, j)[:2] + (j,))]
    out_shape = [jax.ShapeDtypeStruct((b, t, n), BF16)]
    for shp, follows_rows in tap_shapes:
        if follows_rows:
            out_specs.append(pl.BlockSpec((nb, tt, shp[2]), xi))
        elif nb == 1:
            out_specs.append(pl.BlockSpec((1, shp[1], shp[2]), lambda i, j: (i // tps, 0, 0)))
        else:
            out_specs.append(pl.BlockSpec(shp, lambda i, j: (0, 0, 0)))
        out_shape.append(jax.ShapeDtypeStruct(shp, F32))
    if vt_cols:
        n_vh = len(vt_cols) * (tn // HEAD_DIM)
        out_specs.append(pl.BlockSpec((1, n_vh, tt // vt_tile, VT_ROWS, vt_tile), lambda i, j: (i // tps, 0, i % tps, 0, 0)))
        out_shape.append(jax.ShapeDtypeStruct((b, n_vh, t // vt_tile, VT_ROWS, vt_tile), BF16))
    if has_f:
        nf = bf.shape[0]
        out_specs.append(pl.BlockSpec((nf, nb * tt), lambda i, j: (0, i)))
        out_shape.append(jax.ShapeDtypeStruct((nf, b * t), F32))
    return pl.pallas_call(
        functools.partial(_proj_kernel, nb=nb, tt=tt, tn=tn, tps=tps, taps=tuple(taps), has_f=has_f,
                          vt_cols=tuple(vt_cols), vt_tile=vt_tile),
        grid=(nrows, n // tn),
        in_specs=in_specs,
        out_specs=out_specs,
        out_shape=out_shape,
        scratch_shapes=[pltpu.VMEM((nb * tt, d), BF16)],
        compiler_params=_params("arbitrary", "arbitrary"),
        name="mixer_in_proj",
    )(*args)


def _outproj_kernel(*refs, nb, tt, n_in):
    o_refs = refs[:n_in]
    w_ref, x_ref, gate_ref, out_ref = refs[n_in:]
    d = x_ref.shape[-1]
    acc = None
    row0 = 0
    for o_ref in o_refs:
        wd = o_ref.shape[-1]
        part = jnp.dot(o_ref[...].reshape(nb * tt, wd), w_ref[row0:row0 + wd, :], preferred_element_type=F32)
        acc = part if acc is None else acc + part
        row0 += wd
    out_ref[...] = x_ref[...] + gate_ref[...] * acc.reshape(nb, tt, d)


def _outproj(o_list, w_bf, x, mods, gate_slot):
    b, t, d = x.shape
    nb, tt, tps, nrows = _row_blocking(x)
    xi = _x_index(nb, tps)
    in_specs = [pl.BlockSpec((nb, tt, o.shape[-1]), xi) for o in o_list]
    in_specs += [
        pl.BlockSpec(w_bf.shape, lambda i, j: (0, 0)),
        pl.BlockSpec((nb, tt, d), xi),
        _mod_spec(nb, tps, d, gate_slot),
    ]
    return pl.pallas_call(
        functools.partial(_outproj_kernel, nb=nb, tt=tt, n_in=len(o_list)),
        grid=(nrows, 1),
        in_specs=in_specs,
        out_specs=pl.BlockSpec((nb, tt, d), xi),
        out_shape=jax.ShapeDtypeStruct((b, t, d), F32),
        compiler_params=_params("parallel", "arbitrary"),
        name="mixer_out_proj",
    )(*o_list, w_bf, x, mods)


def _cumsum_kernel(x_ref, c_ref, o_ref):
    x = x_ref[0]
    n = x.shape[1]
    lane = lax.broadcasted_iota(jnp.int32, x.shape, 1)
    s = 1
    while s < n:
        x = x + jnp.where(lane >= s, pltpu.roll(x, s, axis=1), 0.0)
        s *= 2
    o_ref[0] = x + c_ref[0][:, :1]


def _cumsum(x, carry):
    g, h, n = x.shape
    return pl.pallas_call(
        _cumsum_kernel,
        grid=(g,),
        in_specs=[pl.BlockSpec((1, h, n), lambda i: (i, 0, 0)), pl.BlockSpec((1, h, LANES), lambda i: (i, 0, 0))],
        out_specs=pl.BlockSpec((1, h, n), lambda i: (i, 0, 0)),
        out_shape=jax.ShapeDtypeStruct((g, h, n), F32),
        compiler_params=_params("arbitrary"),
        name="logf_cumsum",
    )(x, carry)


def _softmax_init(m_scr, acc_scr):
    m_scr[...] = jnp.full(m_scr.shape, NEG_INF, F32)
    acc_scr[...] = jnp.zeros(acc_scr.shape, F32)


def _softmax_update_t(s, s_max, vt, m_prev, acc_prev):
    m_new = jnp.maximum(m_prev, s_max)
    alpha = jnp.exp2(m_prev - m_new)
    p = jnp.exp2(s - m_new)
    acc_new = alpha * acc_prev + jnp.dot(vt, p.astype(BF16), preferred_element_type=F32)
    return m_new, acc_new


def _normalised_t(acc):
    return (acc[:HEAD_DIM] / acc[HEAD_DIM:HEAD_DIM + 1]).T


def _tile_stage(slot, vt_of, m_scr, acc_scr, s_scr, smax_scr, n_groups, next_scores=None):
    g = m_scr.shape[1] // n_groups
    sls = [slice(i * g, (i + 1) * g) for i in range(n_groups)]
    state = [(m_scr[:, sl], acc_scr[:, sl]) for sl in sls]
    new = []
    for i, sl in enumerate(sls):
        if next_scores is not None:
            s = next_scores(i)
            s_scr[1 - slot, :, sl] = s
            smax_scr[1 - slot, :, sl] = jnp.max(s, axis=0, keepdims=True)
        new.append(_softmax_update_t(s_scr[slot, :, sl], smax_scr[slot, :, sl], vt_of(i), *state[i]))
    for sl, (m_new, acc_new) in zip(sls, new):
        m_scr[:, sl] = m_new
        acc_scr[:, sl] = acc_new


def _first_tile(scores, s_scr, smax_scr, n_groups):
    g = s_scr.shape[2] // n_groups
    for i in range(n_groups):
        sl = slice(i * g, (i + 1) * g)
        s = scores(i)
        s_scr[0, :, sl] = s
        smax_scr[0, :, sl] = jnp.max(s, axis=0, keepdims=True)


def _pipelined_tiles(last, n_special, produce_first, stage):
    def kind(dist):
        return dist if dist < n_special else "far"

    for k in range(n_special):
        @pl.when(last == k)
        def _(k=k):
            produce_first(kind(k))

    @pl.when(last >= n_special)
    def _():
        produce_first("far")

    n_pairs = jnp.maximum(last - n_special, 0) // 2

    def pair(i, carry):
        stage(2 * i, 0, "far")
        stage(2 * i + 1, 1, "far")
        return carry

    lax.fori_loop(0, n_pairs, pair, 0)
    t0 = 2 * n_pairs
    remaining = last - t0 + 1
    for r in range(1, n_special + 3):
        @pl.when(remaining == r)
        def _(r=r):
            for u in range(r):
                stage(t0 + u, u % 2, kind(r - 2 - u) if u + 1 < r else None)


def _toeplitz_values(tbl_ref, idx, n_vals, nh, h, ref_row):
    ref_val = tbl_ref[ref_row * nh + h]

    def body(v, acc):
        return jnp.where(idx == v, tbl_ref[v * nh + h] - ref_val, acc)

    return lax.fori_loop(0, n_vals, body, jnp.zeros(idx.shape, F32))


def _stack_diff_queries(q):
    lane = lax.broadcasted_iota(jnp.int32, q.shape, 1)
    zero = jnp.zeros_like(q)
    qs = jnp.concatenate([jnp.where(lane < DA_HALF, q, zero), jnp.where(lane >= DA_HALF, q, zero)], axis=0)
    return qs


def _diff_finalize(o, lam, g, out_scale):
    tq = o.shape[0] // 2
    od = o[:tq] - lam * o[tq:]
    return od * lax.rsqrt(jnp.mean(od * od, axis=-1, keepdims=True) + EPS) * g * out_scale


def _diff_p_kernel(lam_ref, tbl_ref, q_ref, k_ref, vt_ref, idx_ref, base_ref, g_ref, o_ref,
                   m_scr, acc_scr, bias_scr, s_scr, smax_scr, *, tq, nh, hps, far_bucket, out_scale):
    hg = pl.program_id(0)
    bi = pl.program_id(1)
    qi = pl.program_id(2)
    nblk = tq // LANES

    @pl.when(jnp.logical_and(bi == 0, qi == 0))
    def _():
        for e in range(hps):
            t = _toeplitz_values(tbl_ref, idx_ref[...], N_T5_BUCKETS, nh, hg * hps + e, far_bucket) * LOG2E
            bias_scr[e] = base_ref[...]
            for blk in range(nblk):
                sl = slice(blk * LANES, (blk + 1) * LANES)
                bias_scr[e, 0, sl, sl] += t[0]
                if blk >= 1:
                    bias_scr[e, 0, (blk - 1) * LANES:blk * LANES, sl] += t[1]
            bias_scr[e, 1, (nblk - 1) * LANES:, :LANES] += t[1]

    qs = [_stack_diff_queries(q_ref[0, :, e * HEAD_DIM:(e + 1) * HEAD_DIM]) for e in range(hps)]
    _softmax_init(m_scr, acc_scr)
    gph = 2 * tq // ATTN_GROUP
    ngrp = hps * gph

    def scores(j, kind):
        start = pl.multiple_of(j * tq, tq)

        def group(i):
            e, gi = divmod(i, gph)
            k = k_ref[0, pl.ds(start, tq), e * HEAD_DIM:(e + 1) * HEAD_DIM]
            s = lax.dot_general(k, qs[e][gi * ATTN_GROUP:(gi + 1) * ATTN_GROUP], _NT, preferred_element_type=F32)
            if kind != "far":
                q0 = (gi * ATTN_GROUP) % tq
                s = s + bias_scr[e, kind, :, q0:q0 + ATTN_GROUP]
            return s
        return group

    def stage(t, slot, next_kind):
        nxt = None if next_kind is None else scores(t + 1, next_kind)
        _tile_stage(slot, lambda i: vt_ref[0, i // gph, t], m_scr, acc_scr, s_scr, smax_scr, ngrp, nxt)

    _pipelined_tiles(qi, 2, lambda kind: _first_tile(scores(0, kind), s_scr, smax_scr, ngrp), stage)
    outs = []
    for e in range(hps):
        o = _normalised_t(acc_scr[:, e * 2 * tq:(e + 1) * 2 * tq])
        outs.append(_diff_finalize(o, lam_ref[0], g_ref[...], out_scale).astype(BF16))
    o_ref[0] = jnp.concatenate(outs, axis=1)


def _diff_prompt(lam, tbl, qkv, vt, g, n_heads, out_scale):
    b, t, _ = qkv.shape
    tq = vt.shape[-1]
    far_bucket = N_T5_BUCKETS // 2 - 1
    pos = np.arange(LANES)
    idx = np.stack([_t5_bucket_np(pos[:, None] - pos[None, :]),
                    _t5_bucket_np(pos[:, None] - pos[None, :] - LANES)]).astype(np.int32)
    kq = np.arange(tq)
    base = np.zeros((2, tq, tq), np.float32)
    base[0] = np.where((kq[:, None] // CHUNK) <= (kq[None, :] // CHUNK), 0.0, NEG_INF)
    hps = math.gcd(n_heads, DIFF_HEADS)
    ngrp = n_heads // hps
    assert ngrp * hps == n_heads
    wd = hps * HEAD_DIM
    lanes = hps * 2 * tq
    return pl.pallas_call(
        functools.partial(_diff_p_kernel, tq=tq, nh=n_heads, hps=hps, far_bucket=far_bucket, out_scale=out_scale),
        grid=(ngrp, b, t // tq),
        in_specs=[
            pl.BlockSpec(memory_space=pltpu.SMEM),
            pl.BlockSpec(memory_space=pltpu.SMEM),
            pl.BlockSpec((1, tq, wd), lambda h, bi, qi: (bi, qi, h)),
            pl.BlockSpec((1, t, wd), lambda h, bi, qi: (bi, 0, ngrp + h)),
            pl.BlockSpec((1, hps, t // tq, VT_ROWS, tq), lambda h, bi, qi: (bi, h, 0, 0, 0)),
            pl.BlockSpec(idx.shape, lambda h, bi, qi: (0, 0, 0)),
            pl.BlockSpec(base.shape, lambda h, bi, qi: (0, 0, 0)),
            pl.BlockSpec((1, HEAD_DIM), lambda h, bi, qi: (0, 0)),
        ],
        out_specs=pl.BlockSpec((1, tq, wd), lambda h, bi, qi: (bi, qi, h)),
        out_shape=jax.ShapeDtypeStruct((b, t, n_heads * HEAD_DIM), BF16),
        scratch_shapes=[pltpu.VMEM((1, lanes), F32), pltpu.VMEM((VT_ROWS, lanes), F32),
                        pltpu.VMEM((hps, 2, tq, tq), F32), pltpu.VMEM((2, tq, lanes), F32),
                        pltpu.VMEM((2, 1, lanes), F32)],
        compiler_params=_params("arbitrary", "arbitrary", "arbitrary"),
        name="diff_attn_prompt",
    )(lam, tbl, qkv, qkv, vt, jnp.asarray(idx), jnp.asarray(base), g)


def _fox_p_kernel(fref_ref, q_ref, k_ref, vt_ref, f_ref, o_ref, m_scr, acc_scr, fcol_scr, s_scr, smax_scr,
                  *, tq, hps):
    bi = pl.program_id(0)
    hg = pl.program_id(1)
    qi = pl.program_id(2)
    nblk = tq // LANES

    @pl.when(qi == 0)
    def _():
        for e in range(hps):
            for c in range(f_ref.shape[2] * nblk):
                row = f_ref[0, e, c // nblk][:, (c % nblk) * LANES:(c % nblk + 1) * LANES]
                fcol_scr[e, c * LANES:(c + 1) * LANES, :] = jnp.broadcast_to(row, (LANES, LANES)).T

    n_heads = pl.num_programs(1) * hps
    frefs = [fref_ref[(bi * n_heads + hg * hps + e) * pl.num_programs(2) + qi] for e in range(hps)]
    _softmax_init(m_scr, acc_scr)

    gph = tq // ATTN_GROUP
    ngrp = hps * gph

    def scores(j, kind):
        start = pl.multiple_of(j * tq, tq)

        def group(i):
            e, gi = divmod(i, gph)
            hs = slice(e * HEAD_DIM, (e + 1) * HEAD_DIM)
            decay = (frefs[e] - fcol_scr[e, pl.ds(start, tq), :]) * LOG2E
            decay = jnp.concatenate([decay] * (ATTN_GROUP // LANES), axis=1)
            s = lax.dot_general(k_ref[0, pl.ds(start, tq), hs], q_ref[0, gi * ATTN_GROUP:(gi + 1) * ATTN_GROUP, hs],
                                _NT, preferred_element_type=F32)
            s = s + decay
            if kind != "far":
                key = lax.broadcasted_iota(jnp.int32, s.shape, 0)
                qry = lax.broadcasted_iota(jnp.int32, s.shape, 1) + gi * ATTN_GROUP
                s = jnp.where(key <= qry, s, NEG_INF)
            return s
        return group

    def stage(t, slot, next_kind):
        nxt = None if next_kind is None else scores(t + 1, next_kind)
        _tile_stage(slot, lambda i: vt_ref[0, i // gph, t], m_scr, acc_scr, s_scr, smax_scr, ngrp, nxt)

    _pipelined_tiles(qi, 1, lambda kind: _first_tile(scores(0, kind), s_scr, smax_scr, ngrp), stage)
    o_ref[0] = jnp.concatenate(
        [_normalised_t(acc_scr[:, e * tq:(e + 1) * tq]).astype(BF16) for e in range(hps)], axis=1)


def _fox_prompt(fref, qkv, vt, fcum, n_heads, col0, vt_head0):
    b, t, _ = qkv.shape
    tq = vt.shape[-1]
    nq = t // tq
    hps = math.gcd(math.gcd(n_heads, FOX_HEADS), math.gcd(col0, vt_head0))
    ngrp = n_heads // hps
    assert ngrp * hps == n_heads and col0 % hps == 0 and vt_head0 % hps == 0
    wd = hps * HEAD_DIM
    lanes = hps * tq
    return pl.pallas_call(
        functools.partial(_fox_p_kernel, tq=tq, hps=hps),
        grid=(b, ngrp, nq),
        in_specs=[
            pl.BlockSpec(memory_space=pltpu.SMEM),
            pl.BlockSpec((1, tq, wd), lambda bi, h, qi: (bi, qi, col0 // hps + h)),
            pl.BlockSpec((1, t, wd), lambda bi, h, qi: (bi, 0, (col0 + n_heads) // hps + h)),
            pl.BlockSpec((1, hps, nq, VT_ROWS, tq), lambda bi, h, qi: (bi, vt_head0 // hps + h, 0, 0, 0)),
            pl.BlockSpec((1, hps, nq, 1, tq), lambda bi, h, qi: (bi, h, 0, 0, 0)),
        ],
        out_specs=pl.BlockSpec((1, tq, wd), lambda bi, h, qi: (bi, qi, h)),
        out_shape=jax.ShapeDtypeStruct((b, t, n_heads * HEAD_DIM), BF16),
        scratch_shapes=[pltpu.VMEM((1, lanes), F32), pltpu.VMEM((VT_ROWS, lanes), F32),
                        pltpu.VMEM((hps, t, LANES), F32), pltpu.VMEM((2, tq, lanes), F32),
                        pltpu.VMEM((2, 1, lanes), F32)],
        compiler_params=_params("arbitrary", "arbitrary", "arbitrary"),
        name="fox_attn_prompt",
    )(fref, qkv, qkv, vt, fcum)


def _joint_softmax_pv(s_c, s_n, vc, vn):
    m = jnp.maximum(jnp.max(s_c, axis=1, keepdims=True), jnp.max(s_n, axis=1, keepdims=True))
    p_c = jnp.exp2(s_c - m)
    p_n = jnp.exp2(s_n - m)
    l = jnp.sum(p_c, axis=1, keepdims=True) + jnp.sum(p_n, axis=1, keepdims=True)
    acc = jnp.dot(p_c.astype(BF16), vc, preferred_element_type=F32)
    acc = acc + jnp.dot(p_n.astype(BF16), vn, preferred_element_type=F32)
    return acc / l


def _diff_s_kernel(lam_ref, q_ref, kn_ref, vn_ref, kc_ref, vc_ref, bnear_ref, bnew_ref, g_ref, o_ref, *, near, out_scale):
    qs = _stack_diff_queries(q_ref[0])
    kc = kc_ref[0, 0].astype(BF16)
    vc = vc_ref[0, 0].astype(BF16)
    p = kc.shape[0]
    s_c = lax.dot_general(qs, kc, _NT, preferred_element_type=F32)
    bnear = bnear_ref[0]
    s_c = jnp.concatenate([s_c[:, :p - near], s_c[:, p - near:] + jnp.concatenate([bnear, bnear], axis=0)], axis=1)
    bnew = bnew_ref[0]
    s_n = lax.dot_general(qs, kn_ref[0], _NT, preferred_element_type=F32) + jnp.concatenate([bnew, bnew], axis=0)
    o = _joint_softmax_pv(s_c, s_n, vc, vn_ref[0])
    o_ref[0] = _diff_finalize(o, lam_ref[0], g_ref[...], out_scale).astype(BF16)


def _diff_sample(lam, qkv, kc, vc, bnear, bnew, g, n_heads, out_scale):
    b, t, _ = qkv.shape
    p = kc.shape[2]
    near = bnear.shape[-1]
    hd = HEAD_DIM
    return pl.pallas_call(
        functools.partial(_diff_s_kernel, near=near, out_scale=out_scale),
        grid=(b, n_heads),
        in_specs=[
            pl.BlockSpec(memory_space=pltpu.SMEM),
            pl.BlockSpec((1, t, hd), lambda bi, h: (bi, 0, h)),
            pl.BlockSpec((1, t, hd), lambda bi, h: (bi, 0, n_heads + h)),
            pl.BlockSpec((1, t, hd), lambda bi, h: (bi, 0, 2 * n_heads + h)),
            pl.BlockSpec((1, 1, p, hd), lambda bi, h: (bi, h, 0, 0)),
            pl.BlockSpec((1, 1, p, hd), lambda bi, h: (bi, h, 0, 0)),
            pl.BlockSpec((1, t, near), lambda bi, h: (h, 0, 0)),
            pl.BlockSpec((1, t, t), lambda bi, h: (h, 0, 0)),
            pl.BlockSpec((1, hd), lambda bi, h: (0, 0)),
        ],
        out_specs=pl.BlockSpec((1, t, hd), lambda bi, h: (bi, 0, h)),
        out_shape=jax.ShapeDtypeStruct((b, t, n_heads * hd), BF16),
        compiler_params=_params("parallel", "parallel"),
        name="diff_attn_sample",
    )(lam, qkv, qkv, qkv, kc, vc, bnear, bnew, g)


def _fox_s_kernel(fref_ref, q_ref, kn_ref, vn_ref, kc_ref, vc_ref, fc_ref, fn_ref, o_ref):
    bi = pl.program_id(0)
    h = pl.program_id(1)
    q = q_ref[0]
    t = q.shape[0]
    fref = fref_ref[bi * pl.num_programs(1) + h]
    s_c = lax.dot_general(q, kc_ref[0, 0].astype(BF16), _NT, preferred_element_type=F32) + (fref - fc_ref[0, 0]) * LOG2E
    s_n = lax.dot_general(q, kn_ref[0], _NT, preferred_element_type=F32) + (fref - fn_ref[0, 0][:, :t]) * LOG2E
    row = lax.broadcasted_iota(jnp.int32, s_n.shape, 0)
    col = lax.broadcasted_iota(jnp.int32, s_n.shape, 1)
    s_n = jnp.where(col <= row, s_n, NEG_INF)
    o_ref[0] = _joint_softmax_pv(s_c, s_n, vc_ref[0, 0].astype(BF16), vn_ref[0]).astype(BF16)


def _fox_sample(fref, qkv, kc, vc, fc, fn, n_heads, col0):
    b, t, _ = qkv.shape
    p = kc.shape[2]
    hd = HEAD_DIM
    return pl.pallas_call(
        _fox_s_kernel,
        grid=(b, n_heads),
        in_specs=[
            pl.BlockSpec(memory_space=pltpu.SMEM),
            pl.BlockSpec((1, t, hd), lambda bi, h: (bi, 0, col0 + h)),
            pl.BlockSpec((1, t, hd), lambda bi, h: (bi, 0, col0 + n_heads + h)),
            pl.BlockSpec((1, t, hd), lambda bi, h: (bi, 0, col0 + 2 * n_heads + h)),
            pl.BlockSpec((1, 1, p, hd), lambda bi, h: (bi, h, 0, 0)),
            pl.BlockSpec((1, 1, p, hd), lambda bi, h: (bi, h, 0, 0)),
            pl.BlockSpec((1, 1, 1, p), lambda bi, h: (bi, h, 0, 0)),
            pl.BlockSpec((1, 1, 1, fn.shape[-1]), lambda bi, h: (bi, h, 0, 0)),
        ],
        out_specs=pl.BlockSpec((1, t, hd), lambda bi, h: (bi, 0, h)),
        out_shape=jax.ShapeDtypeStruct((b, t, n_heads * hd), BF16),
        compiler_params=_params("parallel", "parallel"),
        name="fox_attn_sample",
    )(fref, qkv, qkv, qkv, kc, vc, fc, fn)


def _band_p_kernel(tbl_ref, q_ref, k_ref, vt_ref, idx_ref, base_ref, o_ref, bm_scr, *, tq, nkb, nh, hps):
    hg = pl.program_id(0)
    bi = pl.program_id(1)
    i = pl.program_id(2)
    nblk = tq // LANES
    qoff = (nkb - 1) * nblk

    @pl.when(jnp.logical_and(bi == 0, i == 0))
    def _():
        for e in range(hps):
            t = _toeplitz_values(tbl_ref, idx_ref[...], 2 * REL_CLIP + 1, nh, hg * hps + e, 0)
            bm_scr[e] = base_ref[...]
            for bq in range(nblk):
                kd = qoff + bq
                bm_scr[e, kd * LANES:(kd + 1) * LANES, bq * LANES:(bq + 1) * LANES] += t[0] * LOG2E
                bm_scr[e, (kd - 1) * LANES:kd * LANES, bq * LANES:(bq + 1) * LANES] += t[1] * LOG2E

    def head_scores(e):
        hs = slice(e * HEAD_DIM, (e + 1) * HEAD_DIM)
        q = q_ref[0, :, hs]
        ss = []
        for blk in range(nkb):
            jb = i - (nkb - 1) + blk
            jc = jnp.maximum(jb, 0)
            k = k_ref[0, pl.ds(pl.multiple_of(jc * tq, tq), tq), hs]
            s = lax.dot_general(k, q, _NT, preferred_element_type=F32) + bm_scr[e, blk * tq:(blk + 1) * tq, :]
            if blk < nkb - 1:
                s = s + jnp.where(jb < 0, NEG_INF, 0.0)
            ss.append(s)
        return ss, functools.reduce(jnp.maximum, [jnp.max(s, axis=0, keepdims=True) for s in ss])

    outs = []
    pending = [head_scores(e) for e in range(min(BAND_LOOKAHEAD, hps))]
    for e in range(hps):
        ss, m = pending.pop(0)
        if e + BAND_LOOKAHEAD < hps:
            pending.append(head_scores(e + BAND_LOOKAHEAD))
        acc = None
        for blk, s in enumerate(ss):
            jc = jnp.maximum(i - (nkb - 1) + blk, 0)
            a = jnp.dot(vt_ref[0, e, jc], jnp.exp2(s - m).astype(BF16), preferred_element_type=F32)
            acc = a if acc is None else acc + a
        outs.append(_normalised_t(acc).astype(BF16))
    o_ref[0] = jnp.concatenate(outs, axis=1)


def _band_prompt(tbl, qkv, vt, n_heads):
    b, t, _ = qkv.shape
    tq = vt.shape[-1]
    nkb = C_BAND // tq + 1
    assert (nkb - 1) * tq == C_BAND and tq % LANES == 0 and LANES >= REL_CLIP
    pos = np.arange(LANES)
    rel = pos[:, None] - pos[None, :]
    idx = (np.stack([np.clip(rel, -REL_CLIP, REL_CLIP), np.clip(rel - LANES, -REL_CLIP, REL_CLIP)]) + REL_CLIP).astype(np.int32)
    k_pos = np.arange(nkb * tq)[:, None]
    q_pos = C_BAND + np.arange(tq)[None, :]
    kc, qc = k_pos // CHUNK, q_pos // CHUNK
    base = np.where((kc <= qc) & (qc - kc <= C_PREV_CHUNKS), 0.0, NEG_INF).astype(np.float32)
    hps = math.gcd(n_heads, BAND_HEADS)
    ngrp = n_heads // hps
    assert ngrp * hps == n_heads
    wd = hps * HEAD_DIM
    return pl.pallas_call(
        functools.partial(_band_p_kernel, tq=tq, nkb=nkb, nh=n_heads, hps=hps),
        grid=(ngrp, b, t // tq),
        in_specs=[
            pl.BlockSpec(memory_space=pltpu.SMEM),
            pl.BlockSpec((1, tq, wd), lambda h, bi, i: (bi, i, h)),
            pl.BlockSpec((1, t, wd), lambda h, bi, i: (bi, 0, ngrp + h)),
            pl.BlockSpec((1, hps, t // tq, VT_ROWS, tq), lambda h, bi, i: (bi, h, 0, 0, 0)),
            pl.BlockSpec(idx.shape, lambda h, bi, i: (0, 0, 0)),
            pl.BlockSpec(base.shape, lambda h, bi, i: (0, 0)),
        ],
        out_specs=pl.BlockSpec((1, tq, wd), lambda h, bi, i: (bi, i, h)),
        out_shape=jax.ShapeDtypeStruct((b, t, n_heads * HEAD_DIM), BF16),
        scratch_shapes=[pltpu.VMEM((hps, nkb * tq, tq), F32)],
        compiler_params=_params("arbitrary", "arbitrary", "arbitrary"),
        name="band_attn_prompt",
    )(tbl, qkv, qkv, vt, jnp.asarray(idx), jnp.asarray(base))


def _band_s_kernel(q_ref, kn_ref, vn_ref, kc_ref, vc_ref, bmc_ref, bmn_ref, o_ref, *, n_heads):
    outs = []
    for h in range(n_heads):
        hs = slice(h * HEAD_DIM, (h + 1) * HEAD_DIM)
        q = q_ref[0, :, hs]
        kc = kc_ref[0, :, h, :].astype(BF16)
        vc = vc_ref[0, :, h, :].astype(BF16)
        s_c = lax.dot_general(q, kc, _NT, preferred_element_type=F32) + bmc_ref[h]
        s_n = lax.dot_general(q, kn_ref[0, :, hs], _NT, preferred_element_type=F32) + bmn_ref[h]
        outs.append(_joint_softmax_pv(s_c, s_n, vc, vn_ref[0, :, hs]).astype(BF16))
    o_ref[0] = jnp.concatenate(outs, axis=1)


def _band_sample(qkv, kc, vc, bmc, bmn, n_heads):
    b, t, _ = qkv.shape
    lc = kc.shape[1]
    cw = n_heads * HEAD_DIM
    return pl.pallas_call(
        functools.partial(_band_s_kernel, n_heads=n_heads),
        grid=(b,),
        in_specs=[
            pl.BlockSpec((1, t, cw), lambda bi: (bi, 0, 0)),
            pl.BlockSpec((1, t, cw), lambda bi: (bi, 0, 1)),
            pl.BlockSpec((1, t, cw), lambda bi: (bi, 0, 2)),
            pl.BlockSpec((1, lc, n_heads, HEAD_DIM), lambda bi: (bi, 0, 0, 0)),
            pl.BlockSpec((1, lc, n_heads, HEAD_DIM), lambda bi: (bi, 0, 0, 0)),
            pl.BlockSpec(bmc.shape, lambda bi: (0, 0, 0)),
            pl.BlockSpec(bmn.shape, lambda bi: (0, 0, 0)),
        ],
        out_specs=pl.BlockSpec((1, t, cw), lambda bi: (bi, 0, 0)),
        out_shape=jax.ShapeDtypeStruct((b, t, cw), BF16),
        compiler_params=_params("parallel"),
        name="band_attn_sample",
    )(qkv, qkv, qkv, kc, vc, bmc, bmn)


def _t5_bucket_np(rel):
    nb = N_T5_BUCKETS // 2
    max_exact = nb // 2
    n = np.abs(rel)
    nf = np.maximum(n, 1).astype(np.float64)
    large = max_exact + (np.log(nf / max_exact) / math.log(T5_MAX_DIST / max_exact) * (nb - max_exact)).astype(np.int64)
    large = np.minimum(large, nb - 1)
    return np.where(rel > 0, nb, 0) + np.where(n < max_exact, n, large)


def _toeplitz(table, index_of_rel, q_pos, k_pos):
    r, c = len(q_pos), len(k_pos)
    assert (np.diff(q_pos) == 1).all() and (np.diff(k_pos) == 1).all()
    rels = (k_pos[0] - q_pos[0]) - (r - 1) + np.arange(r + c - 1)
    v = jnp.pad(table.astype(F32)[index_of_rel(rels)].T, ((0, 0), (0, 1)))
    x = jnp.tile(v, (1, r))[:, :r * (r + c - 1)].reshape(v.shape[0], r, r + c - 1)
    return x[:, :, r - 1:r - 1 + c]


def _t5_bias_tiles(t5_table, q_pos, k_pos, far_bucket):
    mask = (k_pos[None, :] // CHUNK) <= (q_pos[:, None] // CHUNK)
    bias = _toeplitz(t5_table, _t5_bucket_np, q_pos, k_pos)
    bias = (bias - t5_table.astype(F32)[far_bucket][:, None, None]) * LOG2E
    return jnp.where(mask[None], bias, NEG_INF)


def _band_bias_tiles(rel_table, q_pos, k_pos):
    qc = q_pos[:, None] // CHUNK
    kc = k_pos[None, :] // CHUNK
    mask = (kc <= qc) & (qc - kc <= C_PREV_CHUNKS) & (k_pos[None, :] >= 0)
    bias = _toeplitz(rel_table, lambda rel: np.clip(rel, -REL_CLIP, REL_CLIP) + REL_CLIP, q_pos, k_pos) * LOG2E
    return jnp.where(mask[None], bias, NEG_INF)


def kernel(x_prompt, x_sample, cache_a_k, cache_a_v, cache_b_k, cache_b_v, cache_b_logf, cache_c_k, cache_c_v, c_prompt, c_sample, w_ada, b_ada, norm_g, w_ffn_in, w_ffn_out, w_in_ab, b_forget, w_out_ab, lambda_q1, lambda_k1, lambda_q2, lambda_k2, subln_g, t5_table, w_in_c, w_out_c, c_rel_bias, final_g):
    depth = w_ada.shape[0]
    bsz, seq, d = x_prompt.shape
    dbsz, dseq, _ = x_sample.shape
    past = cache_b_logf.shape[2]
    h_a = cache_a_k.shape[3]
    h_b = cache_b_k.shape[3]
    h_c = cache_c_k.shape[3]
    wa, wb, cw = h_a * HEAD_DIM, h_b * HEAD_DIM, h_c * HEAD_DIM
    assert bsz == dbsz and wa == wb and 3 * wa + 3 * wb == 3 * cw
    assert LANES >= T5_MAX_DIST and SAMPLE_NEAR >= T5_MAX_DIST and past >= SAMPLE_NEAR
    far_bucket = (N_T5_BUCKETS // 2) - 1

    mods = _ada_mods(jnp.concatenate([c_prompt, c_sample], axis=0), w_ada, b_ada)
    mods = mods.reshape(depth, 2, bsz, 9, d).transpose(0, 3, 1, 2, 4).reshape(depth * 9 * 2, bsz, 1, d)

    def slot(l, k, grp):
        return (l * 9 + k) * 2 + grp

    xs = [x_prompt, x_sample]
    ab_states = [[], []]
    c_states = [[], []]
    for l in range(depth):
        i = l // 2
        last = l == depth - 1
        w1 = _prep_ffn_weights(w_ffn_in[l, 0], w_ffn_out[l, 0])
        w2 = _prep_ffn_weights(w_ffn_in[l, 1], w_ffn_out[l, 1])
        for grp in range(2):
            xs[grp] = _ffn(xs[grp], mods, [slot(l, k, grp) for k in range(3)], norm_g[l, 0], *w1)

        if l % 2 == 0:
            lam_init = 0.8 - 0.6 * math.exp(-0.3 * l)
            lam = (jnp.exp(jnp.sum(lambda_q1[i].astype(F32) * lambda_k1[i].astype(F32)))
                   - jnp.exp(jnp.sum(lambda_q2[i].astype(F32) * lambda_k2[i].astype(F32))) + lam_init).reshape(1)
            n_main = 3 * wa + 3 * wb
            w_main = w_in_ab[i][:, :n_main].astype(BF16)
            wf = jnp.pad(w_in_ab[i][:, n_main:].T.astype(BF16), ((0, 16 - h_b), (0, 0)))
            bfo = b_forget[i].astype(F32).reshape(h_b, 1)
            w_out = w_out_ab[i].astype(BF16)
            g_sub = subln_g[i].astype(F32).reshape(1, HEAD_DIM)
            tn = min(PROJ_COL_TILE, wa)
            taps = _seg_taps(tn, wa, (1, 2, 4, 5), False)
            cs = np.ones((n_main,), np.float32)
            cs[:wa] = DA_HALF ** -0.5 * LOG2E
            cs[3 * wa:3 * wa + wb] = HEAD_DIM ** -0.5 * LOG2E
            for grp in range(2):
                x = xs[grp]
                b, t, _ = x.shape
                tap_shapes = [((b, t, wa), True)] * 4
                if grp == 0:
                    qkv, ka, va, kb, vb, vt, logf_t = _proj(
                        x, mods, [slot(l, 3, grp), slot(l, 4, grp)], norm_g[l, 1], w_main, cs, tn, tap_shapes, taps,
                        wf, bfo, vt_cols=_vt_cols(tn, wa, (2, 5)), vt_tile=ROW_TILE)
                else:
                    qkv, ka, va, kb, vb, logf_t = _proj(
                        x, mods, [slot(l, 3, grp), slot(l, 4, grp)], norm_g[l, 1], w_main, cs, tn, tap_shapes, taps,
                        wf, bfo)
                logf = logf_t.reshape(h_b, b, t).transpose(1, 2, 0)
                ab_states[grp].append((ka.reshape(b, t, h_a, HEAD_DIM), va.reshape(b, t, h_a, HEAD_DIM),
                                       kb.reshape(b, t, h_b, HEAD_DIM), vb.reshape(b, t, h_b, HEAD_DIM), logf))
                logf_bh = logf_t.reshape(h_b, b, t).transpose(1, 0, 2)
                if grp == 0:
                    o_a = _diff_prompt(lam, t5_table.astype(F32).reshape(-1), qkv, vt, g_sub, h_a, 1.0 - lam_init)
                    fcum = _cumsum(logf_bh, jnp.zeros((b, h_b, LANES), F32))
                    nq = t // ROW_TILE
                    fref = fcum[:, :, ::ROW_TILE].reshape(-1)
                    o_b = _fox_prompt(fref, qkv, vt, fcum.reshape(b, h_b, nq, 1, ROW_TILE), h_b, 3 * h_a, h_a)
                else:
                    q_pos = past + np.arange(t)
                    bnear = _t5_bias_tiles(t5_table, q_pos, past - SAMPLE_NEAR + np.arange(SAMPLE_NEAR), far_bucket)
                    bnew = _t5_bias_tiles(t5_table, q_pos, q_pos, far_bucket)
                    kc = cache_a_k[i].transpose(0, 2, 1, 3)
                    vc = cache_a_v[i].transpose(0, 2, 1, 3)
                    o_a = _diff_sample(lam, qkv, kc, vc, bnear, bnew, g_sub, h_a, 1.0 - lam_init)
                    fc = _cumsum(cache_b_logf[i].astype(F32).transpose(0, 2, 1), jnp.zeros((b, h_b, LANES), F32))
                    carry = fc[:, :, past - 1:past]
                    fn = _cumsum(jnp.pad(logf_bh, ((0, 0), (0, 0), (0, LANES - t))),
                                 jnp.broadcast_to(carry, (b, h_b, LANES)))
                    o_b = _fox_sample(carry.reshape(-1), qkv, cache_b_k[i].transpose(0, 2, 1, 3),
                                      cache_b_v[i].transpose(0, 2, 1, 3), fc.reshape(b, h_b, 1, past),
                                      fn.reshape(b, h_b, 1, LANES), h_b, 3 * h_a)
                xs[grp] = _outproj([o_a, o_b], w_out, x, mods, slot(l, 5, grp))
        else:
            w_in = w_in_c[i].astype(BF16)
            w_out = w_out_c[i].astype(BF16)
            tn = min(PROJ_COL_TILE, cw)
            taps = _seg_taps(tn, cw, (1, 2), True)
            cs = np.ones((3 * cw,), np.float32)
            cs[:cw] = HEAD_DIM ** -0.5 * LOG2E
            for grp in range(2):
                x = xs[grp]
                b, t, _ = x.shape
                keep = min(C_BAND, t)
                assert keep == min(t, ROW_TILE)
                tap_shapes = [((b, keep, cw), False)] * 2
                if grp == 0:
                    qkv, k_new, v_new, vt = _proj(
                        x, mods, [slot(l, 3, grp), slot(l, 4, grp)], norm_g[l, 1], w_in, cs, tn, tap_shapes, taps,
                        vt_cols=_vt_cols(tn, cw, (2,)), vt_tile=BAND_TILE)
                else:
                    qkv, k_new, v_new = _proj(
                        x, mods, [slot(l, 3, grp), slot(l, 4, grp)], norm_g[l, 1], w_in, cs, tn, tap_shapes, taps)
                c_states[grp].append((k_new.reshape(b, keep, h_c, HEAD_DIM), v_new.reshape(b, keep, h_c, HEAD_DIM)))
                if grp == 0:
                    o_c = _band_prompt(c_rel_bias[i].astype(F32).reshape(-1), qkv, vt, h_c)
                else:
                    lc = cache_c_k.shape[2]
                    q_pos = past + np.arange(t)
                    bmc = _band_bias_tiles(c_rel_bias[i], q_pos, past - lc + np.arange(lc))
                    bmn = _band_bias_tiles(c_rel_bias[i], q_pos, q_pos)
                    o_c = _band_sample(qkv, cache_c_k[i], cache_c_v[i], bmc, bmn, h_c)
                xs[grp] = _outproj([o_c], w_out, x, mods, slot(l, 5, grp))

        for grp in range(2):
            xs[grp] = _ffn(xs[grp], mods, [slot(l, k, grp) for k in (6, 7, 8)], norm_g[l, 2], *w2,
                           final_g=final_g if last else None)

    outs = [xs[0], xs[1]]
    for grp in range(2):
        st = ab_states[grp]
        outs += [jnp.stack([s[k] for s in st]) for k in range(5)]
        st = c_states[grp]
        outs += [jnp.stack([s[k] for s in st]) for k in range(2)]
    return tuple(outs)
```

```python
import functools
import math

import numpy as np
import jax
import jax.numpy as jnp
from jax import lax
from jax.experimental import pallas as pl
from jax.experimental.pallas import tpu as pltpu

F32 = jnp.float32
BF16 = jnp.bfloat16

CHUNK = 64
HEAD_DIM = 128
DA_HALF = HEAD_DIM // 2
N_T5_BUCKETS = 32
T5_MAX_DIST = 128
C_PREV_CHUNKS = 8
C_BAND = C_PREV_CHUNKS * CHUNK
REL_CLIP = 128
EPS = 1e-6
NEG_INF = -1e30
LOG2E = math.log2(math.e)
VT_ROWS = HEAD_DIM + 16

VMEM_LIMIT_BYTES = 56 * 1024 * 1024
LANES = 128

ROW_TILE = 512
FF_TILE = 512
FFN_OUT_TILE = 512
FFN_ROW_SPLIT = 2
PROJ_ROW_SPLIT = 2
PROJ_COL_TILE = 1024
ADA_COL_TILE = 1024
BAND_TILE = 256
BAND_HEADS = 8
BAND_LOOKAHEAD = 3
DIFF_HEADS = 4
FOX_HEADS = 4
ATTN_GROUP = 512
SAMPLE_NEAR = 256

_NT = (((1,), (1,)), ((), ()))


def _params(*sem, flags=None):
    return pltpu.CompilerParams(dimension_semantics=sem, vmem_limit_bytes=VMEM_LIMIT_BYTES, flags=flags)


def _modulated_norm(x, g, shift, scale):
    y = x * lax.rsqrt(jnp.mean(x * x, axis=-1, keepdims=True) + EPS) * g
    return y * (1.0 + scale) + shift


def _normed_rows(x_ref, shift_ref, scale_ref, g_ref, h_scr, n_split, r):
    nb, tt, d = x_ref.shape
    rows = nb * tt // n_split
    if nb == 1:
        x, sh, sc = x_ref[:, r * rows:(r + 1) * rows, :], shift_ref[...], scale_ref[...]
    else:
        seqs = slice(r * (nb // n_split), (r + 1) * (nb // n_split))
        x, sh, sc = x_ref[seqs], shift_ref[seqs], scale_ref[seqs]
    hr = _modulated_norm(x, g_ref[...], sh, sc).reshape(rows, d).astype(BF16)
    h_scr[r * rows:(r + 1) * rows, :] = hr
    return hr


def _ada_kernel(c_ref, w_ref, b_ref, o_ref):
    c = c_ref[...]
    a = (c * jax.nn.sigmoid(c)).astype(BF16)
    o_ref[0] = jnp.dot(a, w_ref[0].astype(BF16), preferred_element_type=F32) + b_ref[0]


def _ada_mods(c_all, w_ada, b_ada):
    depth, d, n = w_ada.shape
    r = c_all.shape[0]
    tn = math.gcd(n, ADA_COL_TILE)
    return pl.pallas_call(
        _ada_kernel,
        grid=(depth, n // tn),
        in_specs=[
            pl.BlockSpec((r, d), lambda l, j: (0, 0)),
            pl.BlockSpec((1, d, tn), lambda l, j: (l, 0, j)),
            pl.BlockSpec((1, 1, tn), lambda l, j: (l, 0, j)),
        ],
        out_specs=pl.BlockSpec((1, r, tn), lambda l, j: (l, 0, j)),
        out_shape=jax.ShapeDtypeStruct((depth, r, n), F32),
        compiler_params=_params("arbitrary", "arbitrary"),
        name="ada_mods",
    )(c_all, w_ada, b_ada.reshape(depth, 1, n))


def _row_blocking(x):
    b, t, _ = x.shape
    if t >= ROW_TILE:
        assert t % ROW_TILE == 0
        return 1, ROW_TILE, t // ROW_TILE, b * (t // ROW_TILE)
    assert (b * t) % 8 == 0
    return b, t, 1, 1


def _x_index(nb, tps):
    if nb == 1:
        return lambda i, j: (i // tps, i % tps, 0)
    return lambda i, j: (0, 0, 0)


def _mod_spec(nb, tps, d, slot):
    if nb == 1:
        return pl.BlockSpec((None, 1, 1, d), lambda i, j: (slot, i // tps, 0, 0))
    return pl.BlockSpec((None, nb, 1, d), lambda i, j: (slot, 0, 0, 0))


def _ffn_kernel(*refs, nb, tt, n_ff, last_cols, n_out, final):
    if final:
        (x_ref, shift_ref, scale_ref, gate_ref, g_ref, wg_ref, wu_ref, wo_ref, fg_ref,
         o_ref, h_scr, a_scr) = refs
    else:
        (x_ref, shift_ref, scale_ref, gate_ref, g_ref, wg_ref, wu_ref, wo_ref, xc_ref, gc_ref,
         o_ref, h_scr, a_scr) = refs
    j = pl.program_id(1)
    d = x_ref.shape[-1]
    tm = nb * tt
    tf = wg_ref.shape[1]
    tn = wo_ref.shape[1]

    rows = tm // FFN_ROW_SPLIT
    normed_rows = functools.partial(_normed_rows, x_ref, shift_ref, scale_ref, g_ref, h_scr, FFN_ROW_SPLIT)

    def hidden_chunk(rows_of, cols=tf):
        def products(r):
            hr = rows_of(r)
            return (jnp.dot(hr, wg_ref[:, :cols], preferred_element_type=F32),
                    jnp.dot(hr, wu_ref[:, :cols], preferred_element_type=F32))

        parts = []
        cur = products(0)
        for r in range(FFN_ROW_SPLIT):
            nxt = products(r + 1) if r + 1 < FFN_ROW_SPLIT else None
            gg, uu = cur
            parts.append((gg * jax.nn.sigmoid(gg) * uu).astype(BF16))
            cur = nxt
        a_scr[j, :, :cols] = jnp.concatenate(parts, axis=0)

    kept_rows = lambda r: h_scr[r * rows:(r + 1) * rows, :]
    n_full = n_ff if last_cols == tf else n_ff - 1
    assert n_full >= 1

    @pl.when(j == 0)
    def _():
        hidden_chunk(normed_rows)

    @pl.when(jnp.logical_and(j > 0, j < n_full))
    def _():
        hidden_chunk(kept_rows)

    if n_full < n_ff:
        @pl.when(j == n_ff - 1)
        def _():
            hidden_chunk(kept_rows, last_cols)

    @pl.when(j >= n_ff)
    def _():
        a = jnp.concatenate([a_scr[f] for f in range(n_full)] + [a_scr[f, :, :last_cols] for f in range(n_full, n_ff)],
                            axis=1)
        if final:
            acc = jnp.dot(a, wo_ref[...], preferred_element_type=F32).reshape(nb, tt, tn)
            for n in range(n_out):
                @pl.when(j == n_ff + n)
                def _(n=n):
                    cs = slice(n * tn, (n + 1) * tn)
                    o_ref[:, :, cs] = x_ref[:, :, cs] + 0.5 * gate_ref[:, :, cs] * acc
        else:
            acc = jnp.concatenate([jnp.dot(a[r * rows:(r + 1) * rows], wo_ref[...], preferred_element_type=F32)
                                   for r in range(FFN_ROW_SPLIT)], axis=0).reshape(nb, tt, tn)
            o_ref[...] = xc_ref[...] + 0.5 * gc_ref[...] * acc

    if final:
        @pl.when(j == n_ff + n_out - 1)
        def _():
            y = o_ref[...]
            o_ref[...] = y * lax.rsqrt(jnp.mean(y * y, axis=-1, keepdims=True) + EPS) * fg_ref[...]


def _ffn(x, mods, slots, g, wg_p, wu_p, w_out_p, final_g=None):
    b, t, d = x.shape
    nb, tt, tps, nrows = _row_blocking(x)
    ff = w_out_p.shape[0]
    n_ff = -(-ff // FF_TILE)
    last_cols = ff - (n_ff - 1) * FF_TILE
    tn = math.gcd(d, FFN_OUT_TILE)
    n_out = d // tn
    final = final_g is not None
    xi = _x_index(nb, tps)
    in_specs = [
        pl.BlockSpec((nb, tt, d), xi),
        _mod_spec(nb, tps, d, slots[0]),
        _mod_spec(nb, tps, d, slots[1]),
        _mod_spec(nb, tps, d, slots[2]),
        pl.BlockSpec((1, d), lambda i, j: (0, 0)),
        pl.BlockSpec((d, FF_TILE), lambda i, j: (0, jnp.minimum(j, n_ff - 1))),
        pl.BlockSpec((d, FF_TILE), lambda i, j: (0, jnp.minimum(j, n_ff - 1))),
        pl.BlockSpec((ff, tn), lambda i, j: (0, jnp.maximum(j - n_ff, 0))),
    ]
    args = [x, mods, mods, mods, g.reshape(1, d), wg_p, wu_p, w_out_p]
    if final:
        in_specs.append(pl.BlockSpec((1, d), lambda i, j: (0, 0)))
        args.append(final_g.reshape(1, d))
    else:
        col = lambda j: jnp.maximum(j - n_ff, 0)
        in_specs.append(pl.BlockSpec((nb, tt, tn), lambda i, j: xi(i, j)[:2] + (col(j),)))
        if nb == 1:
            in_specs.append(pl.BlockSpec((None, 1, 1, tn), lambda i, j: (slots[2], i // tps, 0, col(j))))
        else:
            in_specs.append(pl.BlockSpec((None, nb, 1, tn), lambda i, j: (slots[2], 0, 0, col(j))))
        args += [x, mods]
    return pl.pallas_call(
        functools.partial(_ffn_kernel, nb=nb, tt=tt, n_ff=n_ff, last_cols=last_cols, n_out=n_out, final=final),
        grid=(nrows, n_ff + n_out),
        in_specs=in_specs,
        out_specs=(pl.BlockSpec((nb, tt, d), xi) if final else
                   pl.BlockSpec((nb, tt, tn), lambda i, j: xi(i, j)[:2] + (jnp.maximum(j - n_ff, 0),))),
        out_shape=jax.ShapeDtypeStruct((b, t, d), F32),
        scratch_shapes=[pltpu.VMEM((nb * tt, d), BF16), pltpu.VMEM((n_ff, nb * tt, FF_TILE), BF16)],
        compiler_params=_params("parallel", "arbitrary"),
        name="ffn",
    )(*args)


def _prep_ffn_weights(w_in, w_out):
    ff = w_out.shape[0]
    return w_in[:, :ff].astype(BF16), w_in[:, ff:].astype(BF16), w_out.astype(BF16)


def _log_sigmoid(x):
    return jnp.minimum(x, 0.0) - jnp.log1p(jnp.exp(-jnp.abs(x)))


def _proj_kernel(*refs, nb, tt, tn, tps, taps, has_f, vt_cols, vt_tile):
    x_ref, shift_ref, scale_ref, g_ref, w_ref, cs_ref = refs[:6]
    pos = 6
    if has_f:
        wf_ref, bf_ref = refs[6:8]
        pos = 8
    obf_ref = refs[pos]
    tap_refs = refs[pos + 1:pos + 1 + len({tp[0] for tp in taps})]
    pos = pos + 1 + len(tap_refs)
    if vt_cols:
        vt_ref = refs[pos]
        pos += 1
    if has_f:
        logf_ref = refs[pos]
        pos += 1
    h_scr = refs[pos]
    i = pl.program_id(0)
    j = pl.program_id(1)
    d = x_ref.shape[-1]

    def store_attention_copy(y):
        obf_ref[...] = (y * cs_ref[...]).astype(BF16).reshape(nb, tt, tn)

    assert all(tp[1] > 0 for tp in taps) and all(vc[0] > 0 for vc in vt_cols)

    @pl.when(j == 0)
    def _():
        hs = []
        ys = []
        for r in range(PROJ_ROW_SPLIT):
            hs.append(_normed_rows(x_ref, shift_ref, scale_ref, g_ref, h_scr, PROJ_ROW_SPLIT, r))
            ys.append(jnp.dot(hs[-1], w_ref[...], preferred_element_type=F32))
        store_attention_copy(jnp.concatenate(ys, axis=0))
        if has_f:
            fr = lax.dot_general(wf_ref[...], jnp.concatenate(hs, axis=0), _NT, preferred_element_type=F32)
            logf_ref[...] = _log_sigmoid(fr[:logf_ref.shape[0]] + bf_ref[...])

    @pl.when(j > 0)
    def _():
        y = jnp.dot(h_scr[...], w_ref[...], preferred_element_type=F32)
        store_attention_copy(y)
        _proj_taps(y, i, j, tap_refs, vt_ref if vt_cols else None, nb=nb, tt=tt, tn=tn, tps=tps, taps=taps,
                   vt_cols=vt_cols, vt_tile=vt_tile)


def _proj_taps(y, i, j, tap_refs, vt_ref, *, nb, tt, tn, tps, taps, vt_cols, vt_tile):
    for out_idx, jval, col0, last_rows_only in taps:
        cond = j == jval
        if last_rows_only and tps > 1:
            cond = jnp.logical_and(cond, i % tps == tps - 1)

        @pl.when(cond)
        def _(out_idx=out_idx, col0=col0):
            tap_refs[out_idx][:, :, col0:col0 + tn] = y.reshape(nb, tt, tn)

    for jval, head0 in vt_cols:
        @pl.when(j == jval)
        def _(head0=head0):
            for hh in range(tn // HEAD_DIM):
                yt = y[:, hh * HEAD_DIM:(hh + 1) * HEAD_DIM].T.astype(BF16)
                yt = jnp.concatenate([yt, jnp.ones((VT_ROWS - HEAD_DIM, yt.shape[1]), BF16)], axis=0)
                for s in range(tt // vt_tile):
                    vt_ref[0, head0 + hh, s] = yt[:, s * vt_tile:(s + 1) * vt_tile]


def _seg_taps(tn, width, segments, last_rows_only):
    per = width // tn
    assert per * tn == width
    return [(o, sidx * per + s, s * tn, last_rows_only) for o, sidx in enumerate(segments) for s in range(per)]


def _vt_cols(tn, width, segments):
    per = width // tn
    hpb = tn // HEAD_DIM
    return [(sidx * per + s, (o * per + s) * hpb) for o, sidx in enumerate(segments) for s in range(per)]


def _proj(x, mods, slots, g, w_bf, col_scale, tn, tap_shapes, taps, wf=None, bf=None, vt_cols=(), vt_tile=None):
    b, t, d = x.shape
    n = w_bf.shape[1]
    nb, tt, tps, nrows = _row_blocking(x)
    has_f = wf is not None
    assert not vt_cols or nb == 1
    in_specs = [
        pl.BlockSpec((nb, tt, d), _x_index(nb, tps)),
        _mod_spec(nb, tps, d, slots[0]),
        _mod_spec(nb, tps, d, slots[1]),
        pl.BlockSpec((1, d), lambda i, j: (0, 0)),
        pl.BlockSpec((d, tn), lambda i, j: (0, j)),
        pl.BlockSpec((1, tn), lambda i, j: (0, j)),
    ]
    args = [x, mods, mods, g.reshape(1, d), w_bf, jnp.asarray(col_scale, F32).reshape(1, n)]
    if has_f:
        in_specs += [pl.BlockSpec(wf.shape, lambda i, j: (0, 0)), pl.BlockSpec(bf.shape, lambda i, j: (0, 0))]
        args += [wf, bf]
    xi = _x_index(nb, tps)
    out_specs = [pl.BlockSpec((nb, tt, tn), lambda i, j: xi(i, j)[:2] + (j,))]
    out_shape = [jax.ShapeDtypeStruct((b, t, n), BF16)]
    for shp, follows_rows in tap_shapes:
        if follows_rows:
            out_specs.append(pl.BlockSpec((nb, tt, shp[2]), xi))
        elif nb == 1:
            out_specs.append(pl.BlockSpec((1, shp[1], shp[2]), lambda i, j: (i // tps, 0, 0)))
        else:
            out_specs.append(pl.BlockSpec(shp, lambda i, j: (0, 0, 0)))
        out_shape.append(jax.ShapeDtypeStruct(shp, F32))
    if vt_cols:
        n_vh = len(vt_cols) * (tn // HEAD_DIM)
        out_specs.append(pl.BlockSpec((1, n_vh, tt // vt_tile, VT_ROWS, vt_tile), lambda i, j: (i // tps, 0, i % tps, 0, 0)))
        out_shape.append(jax.ShapeDtypeStruct((b, n_vh, t // vt_tile, VT_ROWS, vt_tile), BF16))
    if has_f:
        nf = bf.shape[0]
        out_specs.append(pl.BlockSpec((nf, nb * tt), lambda i, j: (0, i)))
        out_shape.append(jax.ShapeDtypeStruct((nf, b * t), F32))
    return pl.pallas_call(
        functools.partial(_proj_kernel, nb=nb, tt=tt, tn=tn, tps=tps, taps=tuple(taps), has_f=has_f,
                          vt_cols=tuple(vt_cols), vt_tile=vt_tile),
        grid=(nrows, n // tn),
        in_specs=in_specs,
        out_specs=out_specs,
        out_shape=out_shape,
        scratch_shapes=[pltpu.VMEM((nb * tt, d), BF16)],
        compiler_params=_params("arbitrary", "arbitrary"),
        name="mixer_in_proj",
    )(*args)


def _outproj_kernel(*refs, nb, tt, n_in):
    o_refs = refs[:n_in]
    w_ref, x_ref, gate_ref, out_ref = refs[n_in:]
    d = x_ref.shape[-1]
    acc = None
    row0 = 0
    for o_ref in o_refs:
        wd = o_ref.shape[-1]
        part = jnp.dot(o_ref[...].reshape(nb * tt, wd), w_ref[row0:row0 + wd, :], preferred_element_type=F32)
        acc = part if acc is None else acc + part
        row0 += wd
    out_ref[...] = x_ref[...] + gate_ref[...] * acc.reshape(nb, tt, d)


def _outproj(o_list, w_bf, x, mods, gate_slot):
    b, t, d = x.shape
    nb, tt, tps, nrows = _row_blocking(x)
    xi = _x_index(nb, tps)
    in_specs = [pl.BlockSpec((nb, tt, o.shape[-1]), xi) for o in o_list]
    in_specs += [
        pl.BlockSpec(w_bf.shape, lambda i, j: (0, 0)),
        pl.BlockSpec((nb, tt, d), xi),
        _mod_spec(nb, tps, d, gate_slot),
    ]
    return pl.pallas_call(
        functools.partial(_outproj_kernel, nb=nb, tt=tt, n_in=len(o_list)),
        grid=(nrows, 1),
        in_specs=in_specs,
        out_specs=pl.BlockSpec((nb, tt, d), xi),
        out_shape=jax.ShapeDtypeStruct((b, t, d), F32),
        compiler_params=_params("parallel", "arbitrary"),
        name="mixer_out_proj",
    )(*o_list, w_bf, x, mods)


def _cumsum_kernel(x_ref, c_ref, o_ref):
    x = x_ref[0]
    n = x.shape[1]
    lane = lax.broadcasted_iota(jnp.int32, x.shape, 1)
    s = 1
    while s < n:
        x = x + jnp.where(lane >= s, pltpu.roll(x, s, axis=1), 0.0)
        s *= 2
    o_ref[0] = x + c_ref[0][:, :1]


def _cumsum(x, carry):
    g, h, n = x.shape
    return pl.pallas_call(
        _cumsum_kernel,
        grid=(g,),
        in_specs=[pl.BlockSpec((1, h, n), lambda i: (i, 0, 0)), pl.BlockSpec((1, h, LANES), lambda i: (i, 0, 0))],
        out_specs=pl.BlockSpec((1, h, n), lambda i: (i, 0, 0)),
        out_shape=jax.ShapeDtypeStruct((g, h, n), F32),
        compiler_params=_params("arbitrary"),
        name="logf_cumsum",
    )(x, carry)


def _softmax_init(m_scr, acc_scr):
    m_scr[...] = jnp.full(m_scr.shape, NEG_INF, F32)
    acc_scr[...] = jnp.zeros(acc_scr.shape, F32)


def _softmax_update_t(s, s_max, vt, m_prev, acc_prev):
    m_new = jnp.maximum(m_prev, s_max)
    alpha = jnp.exp2(m_prev - m_new)
    p = jnp.exp2(s - m_new)
    acc_new = alpha * acc_prev + jnp.dot(vt, p.astype(BF16), preferred_element_type=F32)
    return m_new, acc_new


def _normalised_t(acc):
    return (acc[:HEAD_DIM] / acc[HEAD_DIM:HEAD_DIM + 1]).T


def _tile_stage(slot, vt_of, m_scr, acc_scr, s_scr, smax_scr, n_groups, next_scores=None):
    g = m_scr.shape[1] // n_groups
    sls = [slice(i * g, (i + 1) * g) for i in range(n_groups)]
    state = [(m_scr[:, sl], acc_scr[:, sl]) for sl in sls]
    new = []
    for i, sl in enumerate(sls):
        if next_scores is not None:
            s = next_scores(i)
            s_scr[1 - slot, :, sl] = s
            smax_scr[1 - slot, :, sl] = jnp.max(s, axis=0, keepdims=True)
        new.append(_softmax_update_t(s_scr[slot, :, sl], smax_scr[slot, :, sl], vt_of(i), *state[i]))
    for sl, (m_new, acc_new) in zip(sls, new):
        m_scr[:, sl] = m_new
        acc_scr[:, sl] = acc_new


def _first_tile(scores, s_scr, smax_scr, n_groups):
    g = s_scr.shape[2] // n_groups
    for i in range(n_groups):
        sl = slice(i * g, (i + 1) * g)
        s = scores(i)
        s_scr[0, :, sl] = s
        smax_scr[0, :, sl] = jnp.max(s, axis=0, keepdims=True)


def _pipelined_tiles(last, n_special, produce_first, stage):
    def kind(dist):
        return dist if dist < n_special else "far"

    for k in range(n_special):
        @pl.when(last == k)
        def _(k=k):
            produce_first(kind(k))

    @pl.when(last >= n_special)
    def _():
        produce_first("far")

    n_pairs = jnp.maximum(last - n_special, 0) // 2

    def pair(i, carry):
        stage(2 * i, 0, "far")
        stage(2 * i + 1, 1, "far")
        return carry

    lax.fori_loop(0, n_pairs, pair, 0)
    t0 = 2 * n_pairs
    remaining = last - t0 + 1
    for r in range(1, n_special + 3):
        @pl.when(remaining == r)
        def _(r=r):
            for u in range(r):
                stage(t0 + u, u % 2, kind(r - 2 - u) if u + 1 < r else None)


def _toeplitz_values(tbl_ref, idx, n_vals, nh, h, ref_row):
    ref_val = tbl_ref[ref_row * nh + h]

    def body(v, acc):
        return jnp.where(idx == v, tbl_ref[v * nh + h] - ref_val, acc)

    return lax.fori_loop(0, n_vals, body, jnp.zeros(idx.shape, F32))


def _stack_diff_queries(q):
    lane = lax.broadcasted_iota(jnp.int32, q.shape, 1)
    zero = jnp.zeros_like(q)
    qs = jnp.concatenate([jnp.where(lane < DA_HALF, q, zero), jnp.where(lane >= DA_HALF, q, zero)], axis=0)
    return qs


def _diff_finalize(o, lam, g, out_scale):
    tq = o.shape[0] // 2
    od = o[:tq] - lam * o[tq:]
    return od * lax.rsqrt(jnp.mean(od * od, axis=-1, keepdims=True) + EPS) * g * out_scale


def _diff_p_kernel(lam_ref, tbl_ref, q_ref, k_ref, vt_ref, idx_ref, base_ref, g_ref, o_ref,
                   m_scr, acc_scr, bias_scr, s_scr, smax_scr, *, tq, nh, hps, far_bucket, out_scale):
    hg = pl.program_id(0)
    bi = pl.program_id(1)
    qi = pl.program_id(2)
    nblk = tq // LANES

    @pl.when(jnp.logical_and(bi == 0, qi == 0))
    def _():
        for e in range(hps):
            t = _toeplitz_values(tbl_ref, idx_ref[...], N_T5_BUCKETS, nh, hg * hps + e, far_bucket) * LOG2E
            bias_scr[e] = base_ref[...]
            for blk in range(nblk):
                sl = slice(blk * LANES, (blk + 1) * LANES)
                bias_scr[e, 0, sl, sl] += t[0]
                if blk >= 1:
                    bias_scr[e, 0, (blk - 1) * LANES:blk * LANES, sl] += t[1]
            bias_scr[e, 1, (nblk - 1) * LANES:, :LANES] += t[1]

    qs = [_stack_diff_queries(q_ref[0, :, e * HEAD_DIM:(e + 1) * HEAD_DIM]) for e in range(hps)]
    _softmax_init(m_scr, acc_scr)
    gph = 2 * tq // ATTN_GROUP
    ngrp = hps * gph

    def scores(j, kind):
        start = pl.multiple_of(j * tq, tq)

        def group(i):
            e, gi = divmod(i, gph)
            k = k_ref[0, pl.ds(start, tq), e * HEAD_DIM:(e + 1) * HEAD_DIM]
            s = lax.dot_general(k, qs[e][gi * ATTN_GROUP:(gi + 1) * ATTN_GROUP], _NT, preferred_element_type=F32)
            if kind != "far":
                q0 = (gi * ATTN_GROUP) % tq
                s = s + bias_scr[e, kind, :, q0:q0 + ATTN_GROUP]
            return s
        return group

    def stage(t, slot, next_kind):
        nxt = None if next_kind is None else scores(t + 1, next_kind)
        _tile_stage(slot, lambda i: vt_ref[0, i // gph, t], m_scr, acc_scr, s_scr, smax_scr, ngrp, nxt)

    _pipelined_tiles(qi, 2, lambda kind: _first_tile(scores(0, kind), s_scr, smax_scr, ngrp), stage)
    outs = []
    for e in range(hps):
        o = _normalised_t(acc_scr[:, e * 2 * tq:(e + 1) * 2 * tq])
        outs.append(_diff_finalize(o, lam_ref[0], g_ref[...], out_scale).astype(BF16))
    o_ref[0] = jnp.concatenate(outs, axis=1)


def _diff_prompt(lam, tbl, qkv, vt, g, n_heads, out_scale):
    b, t, _ = qkv.shape
    tq = vt.shape[-1]
    far_bucket = N_T5_BUCKETS // 2 - 1
    pos = np.arange(LANES)
    idx = np.stack([_t5_bucket_np(pos[:, None] - pos[None, :]),
                    _t5_bucket_np(pos[:, None] - pos[None, :] - LANES)]).astype(np.int32)
    kq = np.arange(tq)
    base = np.zeros((2, tq, tq), np.float32)
    base[0] = np.where((kq[:, None] // CHUNK) <= (kq[None, :] // CHUNK), 0.0, NEG_INF)
    hps = math.gcd(n_heads, DIFF_HEADS)
    ngrp = n_heads // hps
    assert ngrp * hps == n_heads
    wd = hps * HEAD_DIM
    lanes = hps * 2 * tq
    return pl.pallas_call(
        functools.partial(_diff_p_kernel, tq=tq, nh=n_heads, hps=hps, far_bucket=far_bucket, out_scale=out_scale),
        grid=(ngrp, b, t // tq),
        in_specs=[
            pl.BlockSpec(memory_space=pltpu.SMEM),
            pl.BlockSpec(memory_space=pltpu.SMEM),
            pl.BlockSpec((1, tq, wd), lambda h, bi, qi: (bi, qi, h)),
            pl.BlockSpec((1, t, wd), lambda h, bi, qi: (bi, 0, ngrp + h)),
            pl.BlockSpec((1, hps, t // tq, VT_ROWS, tq), lambda h, bi, qi: (bi, h, 0, 0, 0)),
            pl.BlockSpec(idx.shape, lambda h, bi, qi: (0, 0, 0)),
            pl.BlockSpec(base.shape, lambda h, bi, qi: (0, 0, 0)),
            pl.BlockSpec((1, HEAD_DIM), lambda h, bi, qi: (0, 0)),
        ],
        out_specs=pl.BlockSpec((1, tq, wd), lambda h, bi, qi: (bi, qi, h)),
        out_shape=jax.ShapeDtypeStruct((b, t, n_heads * HEAD_DIM), BF16),
        scratch_shapes=[pltpu.VMEM((1, lanes), F32), pltpu.VMEM((VT_ROWS, lanes), F32),
                        pltpu.VMEM((hps, 2, tq, tq), F32), pltpu.VMEM((2, tq, lanes), F32),
                        pltpu.VMEM((2, 1, lanes), F32)],
        compiler_params=_params("arbitrary", "arbitrary", "arbitrary"),
        name="diff_attn_prompt",
    )(lam, tbl, qkv, qkv, vt, jnp.asarray(idx), jnp.asarray(base), g)


def _fox_p_kernel(fref_ref, q_ref, k_ref, vt_ref, f_ref, o_ref, m_scr, acc_scr, fcol_scr, s_scr, smax_scr,
                  *, tq, hps):
    bi = pl.program_id(0)
    hg = pl.program_id(1)
    qi = pl.program_id(2)
    nblk = tq // LANES

    @pl.when(qi == 0)
    def _():
        for e in range(hps):
            for c in range(f_ref.shape[2] * nblk):
                row = f_ref[0, e, c // nblk][:, (c % nblk) * LANES:(c % nblk + 1) * LANES]
                fcol_scr[e, c * LANES:(c + 1) * LANES, :] = jnp.broadcast_to(row, (LANES, LANES)).T

    n_heads = pl.num_programs(1) * hps
    frefs = [fref_ref[(bi * n_heads + hg * hps + e) * pl.num_programs(2) + qi] for e in range(hps)]
    _softmax_init(m_scr, acc_scr)

    gph = tq // ATTN_GROUP
    ngrp = hps * gph

    def scores(j, kind):
        start = pl.multiple_of(j * tq, tq)

        def group(i):
            e, gi = divmod(i, gph)
            hs = slice(e * HEAD_DIM, (e + 1) * HEAD_DIM)
            decay = (frefs[e] - fcol_scr[e, pl.ds(start, tq), :]) * LOG2E
            decay = jnp.concatenate([decay] * (ATTN_GROUP // LANES), axis=1)
            s = lax.dot_general(k_ref[0, pl.ds(start, tq), hs], q_ref[0, gi * ATTN_GROUP:(gi + 1) * ATTN_GROUP, hs],
                                _NT, preferred_element_type=F32)
            s = s + decay
            if kind != "far":
                key = lax.broadcasted_iota(jnp.int32, s.shape, 0)
                qry = lax.broadcasted_iota(jnp.int32, s.shape, 1) + gi * ATTN_GROUP
                s = jnp.where(key <= qry, s, NEG_INF)
            return s
        return group

    def stage(t, slot, next_kind):
        nxt = None if next_kind is None else scores(t + 1, next_kind)
        _tile_stage(slot, lambda i: vt_ref[0, i // gph, t], m_scr, acc_scr, s_scr, smax_scr, ngrp, nxt)

    _pipelined_tiles(qi, 1, lambda kind: _first_tile(scores(0, kind), s_scr, smax_scr, ngrp), stage)
    o_ref[0] = jnp.concatenate(
        [_normalised_t(acc_scr[:, e * tq:(e + 1) * tq]).astype(BF16) for e in range(hps)], axis=1)


def _fox_prompt(fref, qkv, vt, fcum, n_heads, col0, vt_head0):
    b, t, _ = qkv.shape
    tq = vt.shape[-1]
    nq = t // tq
    hps = math.gcd(math.gcd(n_heads, FOX_HEADS), math.gcd(col0, vt_head0))
    ngrp = n_heads // hps
    assert ngrp * hps == n_heads and col0 % hps == 0 and vt_head0 % hps == 0
    wd = hps * HEAD_DIM
    lanes = hps * tq
    return pl.pallas_call(
        functools.partial(_fox_p_kernel, tq=tq, hps=hps),
        grid=(b, ngrp, nq),
        in_specs=[
            pl.BlockSpec(memory_space=pltpu.SMEM),
            pl.BlockSpec((1, tq, wd), lambda bi, h, qi: (bi, qi, col0 // hps + h)),
            pl.BlockSpec((1, t, wd), lambda bi, h, qi: (bi, 0, (col0 + n_heads) // hps + h)),
            pl.BlockSpec((1, hps, nq, VT_ROWS, tq), lambda bi, h, qi: (bi, vt_head0 // hps + h, 0, 0, 0)),
            pl.BlockSpec((1, hps, nq, 1, tq), lambda bi, h, qi: (bi, h, 0, 0, 0)),
        ],
        out_specs=pl.BlockSpec((1, tq, wd), lambda bi, h, qi: (bi, qi, h)),
        out_shape=jax.ShapeDtypeStruct((b, t, n_heads * HEAD_DIM), BF16),
        scratch_shapes=[pltpu.VMEM((1, lanes), F32), pltpu.VMEM((VT_ROWS, lanes), F32),
                        pltpu.VMEM((hps, t, LANES), F32), pltpu.VMEM((2, tq, lanes), F32),
                        pltpu.VMEM((2, 1, lanes), F32)],
        compiler_params=_params("arbitrary", "arbitrary", "arbitrary"),
        name="fox_attn_prompt",
    )(fref, qkv, qkv, vt, fcum)


def _joint_softmax_pv(s_c, s_n, vc, vn):
    m = jnp.maximum(jnp.max(s_c, axis=1, keepdims=True), jnp.max(s_n, axis=1, keepdims=True))
    p_c = jnp.exp2(s_c - m)
    p_n = jnp.exp2(s_n - m)
    l = jnp.sum(p_c, axis=1, keepdims=True) + jnp.sum(p_n, axis=1, keepdims=True)
    acc = jnp.dot(p_c.astype(BF16), vc, preferred_element_type=F32)
    acc = acc + jnp.dot(p_n.astype(BF16), vn, preferred_element_type=F32)
    return acc / l


def _diff_s_kernel(lam_ref, q_ref, kn_ref, vn_ref, kc_ref, vc_ref, bnear_ref, bnew_ref, g_ref, o_ref, *, near, out_scale):
    qs = _stack_diff_queries(q_ref[0])
    kc = kc_ref[0, 0].astype(BF16)
    vc = vc_ref[0, 0].astype(BF16)
    p = kc.shape[0]
    s_c = lax.dot_general(qs, kc, _NT, preferred_element_type=F32)
    bnear = bnear_ref[0]
    s_c = jnp.concatenate([s_c[:, :p - near], s_c[:, p - near:] + jnp.concatenate([bnear, bnear], axis=0)], axis=1)
    bnew = bnew_ref[0]
    s_n = lax.dot_general(qs, kn_ref[0], _NT, preferred_element_type=F32) + jnp.concatenate([bnew, bnew], axis=0)
    o = _joint_softmax_pv(s_c, s_n, vc, vn_ref[0])
    o_ref[0] = _diff_finalize(o, lam_ref[0], g_ref[...], out_scale).astype(BF16)


def _diff_sample(lam, qkv, kc, vc, bnear, bnew, g, n_heads, out_scale):
    b, t, _ = qkv.shape
    p = kc.shape[2]
    near = bnear.shape[-1]
    hd = HEAD_DIM
    return pl.pallas_call(
        functools.partial(_diff_s_kernel, near=near, out_scale=out_scale),
        grid=(b, n_heads),
        in_specs=[
            pl.BlockSpec(memory_space=pltpu.SMEM),
            pl.BlockSpec((1, t, hd), lambda bi, h: (bi, 0, h)),
            pl.BlockSpec((1, t, hd), lambda bi, h: (bi, 0, n_heads + h)),
            pl.BlockSpec((1, t, hd), lambda bi, h: (bi, 0, 2 * n_heads + h)),
            pl.BlockSpec((1, 1, p, hd), lambda bi, h: (bi, h, 0, 0)),
            pl.BlockSpec((1, 1, p, hd), lambda bi, h: (bi, h, 0, 0)),
            pl.BlockSpec((1, t, near), lambda bi, h: (h, 0, 0)),
            pl.BlockSpec((1, t, t), lambda bi, h: (h, 0, 0)),
            pl.BlockSpec((1, hd), lambda bi, h: (0, 0)),
        ],
        out_specs=pl.BlockSpec((1, t, hd), lambda bi, h: (bi, 0, h)),
        out_shape=jax.ShapeDtypeStruct((b, t, n_heads * hd), BF16),
        compiler_params=_params("parallel", "parallel"),
        name="diff_attn_sample",
    )(lam, qkv, qkv, qkv, kc, vc, bnear, bnew, g)


def _fox_s_kernel(fref_ref, q_ref, kn_ref, vn_ref, kc_ref, vc_ref, fc_ref, fn_ref, o_ref):
    bi = pl.program_id(0)
    h = pl.program_id(1)
    q = q_ref[0]
    t = q.shape[0]
    fref = fref_ref[bi * pl.num_programs(1) + h]
    s_c = lax.dot_general(q, kc_ref[0, 0].astype(BF16), _NT, preferred_element_type=F32) + (fref - fc_ref[0, 0]) * LOG2E
    s_n = lax.dot_general(q, kn_ref[0], _NT, preferred_element_type=F32) + (fref - fn_ref[0, 0][:, :t]) * LOG2E
    row = lax.broadcasted_iota(jnp.int32, s_n.shape, 0)
    col = lax.broadcasted_iota(jnp.int32, s_n.shape, 1)
    s_n = jnp.where(col <= row, s_n, NEG_INF)
    o_ref[0] = _joint_softmax_pv(s_c, s_n, vc_ref[0, 0].astype(BF16), vn_ref[0]).astype(BF16)


def _fox_sample(fref, qkv, kc, vc, fc, fn, n_heads, col0):
    b, t, _ = qkv.shape
    p = kc.shape[2]
    hd = HEAD_DIM
    return pl.pallas_call(
        _fox_s_kernel,
        grid=(b, n_heads),
        in_specs=[
            pl.BlockSpec(memory_space=pltpu.SMEM),
            pl.BlockSpec((1, t, hd), lambda bi, h: (bi, 0, col0 + h)),
            pl.BlockSpec((1, t, hd), lambda bi, h: (bi, 0, col0 + n_heads + h)),
            pl.BlockSpec((1, t, hd), lambda bi, h: (bi, 0, col0 + 2 * n_heads + h)),
            pl.BlockSpec((1, 1, p, hd), lambda bi, h: (bi, h, 0, 0)),
            pl.BlockSpec((1, 1, p, hd), lambda bi, h: (bi, h, 0, 0)),
            pl.BlockSpec((1, 1, 1, p), lambda bi, h: (bi, h, 0, 0)),
            pl.BlockSpec((1, 1, 1, fn.shape[-1]), lambda bi, h: (bi, h, 0, 0)),
        ],
        out_specs=pl.BlockSpec((1, t, hd), lambda bi, h: (bi, 0, h)),
        out_shape=jax.ShapeDtypeStruct((b, t, n_heads * hd), BF16),
        compiler_params=_params("parallel", "parallel"),
        name="fox_attn_sample",
    )(fref, qkv, qkv, qkv, kc, vc, fc, fn)


def _band_p_kernel(tbl_ref, q_ref, k_ref, vt_ref, idx_ref, base_ref, o_ref, bm_scr, *, tq, nkb, nh, hps):
    hg = pl.program_id(0)
    bi = pl.program_id(1)
    i = pl.program_id(2)
    nblk = tq // LANES
    qoff = (nkb - 1) * nblk

    @pl.when(jnp.logical_and(bi == 0, i == 0))
    def _():
        for e in range(hps):
            t = _toeplitz_values(tbl_ref, idx_ref[...], 2 * REL_CLIP + 1, nh, hg * hps + e, 0)
            bm_scr[e] = base_ref[...]
            for bq in range(nblk):
                kd = qoff + bq
                bm_scr[e, kd * LANES:(kd + 1) * LANES, bq * LANES:(bq + 1) * LANES] += t[0] * LOG2E
                bm_scr[e, (kd - 1) * LANES:kd * LANES, bq * LANES:(bq + 1) * LANES] += t[1] * LOG2E

    def head_scores(e):
        hs = slice(e * HEAD_DIM, (e + 1) * HEAD_DIM)
        q = q_ref[0, :, hs]
        ss = []
        for blk in range(nkb):
            jb = i - (nkb - 1) + blk
            jc = jnp.maximum(jb, 0)
            k = k_ref[0, pl.ds(pl.multiple_of(jc * tq, tq), tq), hs]
            s = lax.dot_general(k, q, _NT, preferred_element_type=F32) + bm_scr[e, blk * tq:(blk + 1) * tq, :]
            if blk < nkb - 1:
                s = s + jnp.where(jb < 0, NEG_INF, 0.0)
            ss.append(s)
        return ss, functools.reduce(jnp.maximum, [jnp.max(s, axis=0, keepdims=True) for s in ss])

    outs = []
    pending = [head_scores(e) for e in range(min(BAND_LOOKAHEAD, hps))]
    for e in range(hps):
        ss, m = pending.pop(0)
        if e + BAND_LOOKAHEAD < hps:
            pending.append(head_scores(e + BAND_LOOKAHEAD))
        acc = None
        for blk, s in enumerate(ss):
            jc = jnp.maximum(i - (nkb - 1) + blk, 0)
            a = jnp.dot(vt_ref[0, e, jc], jnp.exp2(s - m).astype(BF16), preferred_element_type=F32)
            acc = a if acc is None else acc + a
        outs.append(_normalised_t(acc).astype(BF16))
    o_ref[0] = jnp.concatenate(outs, axis=1)


def _band_prompt(tbl, qkv, vt, n_heads):
    b, t, _ = qkv.shape
    tq = vt.shape[-1]
    nkb = C_BAND // tq + 1
    assert (nkb - 1) * tq == C_BAND and tq % LANES == 0 and LANES >= REL_CLIP
    pos = np.arange(LANES)
    rel = pos[:, None] - pos[None, :]
    idx = (np.stack([np.clip(rel, -REL_CLIP, REL_CLIP), np.clip(rel - LANES, -REL_CLIP, REL_CLIP)]) + REL_CLIP).astype(np.int32)
    k_pos = np.arange(nkb * tq)[:, None]
    q_pos = C_BAND + np.arange(tq)[None, :]
    kc, qc = k_pos // CHUNK, q_pos // CHUNK
    base = np.where((kc <= qc) & (qc - kc <= C_PREV_CHUNKS), 0.0, NEG_INF).astype(np.float32)
    hps = math.gcd(n_heads, BAND_HEADS)
    ngrp = n_heads // hps
    assert ngrp * hps == n_heads
    wd = hps * HEAD_DIM
    return pl.pallas_call(
        functools.partial(_band_p_kernel, tq=tq, nkb=nkb, nh=n_heads, hps=hps),
        grid=(ngrp, b, t // tq),
        in_specs=[
            pl.BlockSpec(memory_space=pltpu.SMEM),
            pl.BlockSpec((1, tq, wd), lambda h, bi, i: (bi, i, h)),
            pl.BlockSpec((1, t, wd), lambda h, bi, i: (bi, 0, ngrp + h)),
            pl.BlockSpec((1, hps, t // tq, VT_ROWS, tq), lambda h, bi, i: (bi, h, 0, 0, 0)),
            pl.BlockSpec(idx.shape, lambda h, bi, i: (0, 0, 0)),
            pl.BlockSpec(base.shape, lambda h, bi, i: (0, 0)),
        ],
        out_specs=pl.BlockSpec((1, tq, wd), lambda h, bi, i: (bi, i, h)),
        out_shape=jax.ShapeDtypeStruct((b, t, n_heads * HEAD_DIM), BF16),
        scratch_shapes=[pltpu.VMEM((hps, nkb * tq, tq), F32)],
        compiler_params=_params("arbitrary", "arbitrary", "arbitrary"),
        name="band_attn_prompt",
    )(tbl, qkv, qkv, vt, jnp.asarray(idx), jnp.asarray(base))


def _band_s_kernel(q_ref, kn_ref, vn_ref, kc_ref, vc_ref, bmc_ref, bmn_ref, o_ref, *, n_heads):
    outs = []
    for h in range(n_heads):
        hs = slice(h * HEAD_DIM, (h + 1) * HEAD_DIM)
        q = q_ref[0, :, hs]
        kc = kc_ref[0, :, h, :].astype(BF16)
        vc = vc_ref[0, :, h, :].astype(BF16)
        s_c = lax.dot_general(q, kc, _NT, preferred_element_type=F32) + bmc_ref[h]
        s_n = lax.dot_general(q, kn_ref[0, :, hs], _NT, preferred_element_type=F32) + bmn_ref[h]
        outs.append(_joint_softmax_pv(s_c, s_n, vc, vn_ref[0, :, hs]).astype(BF16))
    o_ref[0] = jnp.concatenate(outs, axis=1)


def _band_sample(qkv, kc, vc, bmc, bmn, n_heads):
    b, t, _ = qkv.shape
    lc = kc.shape[1]
    cw = n_heads * HEAD_DIM
    return pl.pallas_call(
        functools.partial(_band_s_kernel, n_heads=n_heads),
        grid=(b,),
        in_specs=[
            pl.BlockSpec((1, t, cw), lambda bi: (bi, 0, 0)),
            pl.BlockSpec((1, t, cw), lambda bi: (bi, 0, 1)),
            pl.BlockSpec((1, t, cw), lambda bi: (bi, 0, 2)),
            pl.BlockSpec((1, lc, n_heads, HEAD_DIM), lambda bi: (bi, 0, 0, 0)),
            pl.BlockSpec((1, lc, n_heads, HEAD_DIM), lambda bi: (bi, 0, 0, 0)),
            pl.BlockSpec(bmc.shape, lambda bi: (0, 0, 0)),
            pl.BlockSpec(bmn.shape, lambda bi: (0, 0, 0)),
        ],
        out_specs=pl.BlockSpec((1, t, cw), lambda bi: (bi, 0, 0)),
        out_shape=jax.ShapeDtypeStruct((b, t, cw), BF16),
        compiler_params=_params("parallel"),
        name="band_attn_sample",
    )(qkv, qkv, qkv, kc, vc, bmc, bmn)


def _t5_bucket_np(rel):
    nb = N_T5_BUCKETS // 2
    max_exact = nb // 2
    n = np.abs(rel)
    nf = np.maximum(n, 1).astype(np.float64)
    large = max_exact + (np.log(nf / max_exact) / math.log(T5_MAX_DIST / max_exact) * (nb - max_exact)).astype(np.int64)
    large = np.minimum(large, nb - 1)
    return np.where(rel > 0, nb, 0) + np.where(n < max_exact, n, large)


def _toeplitz(table, index_of_rel, q_pos, k_pos):
    r, c = len(q_pos), len(k_pos)
    assert (np.diff(q_pos) == 1).all() and (np.diff(k_pos) == 1).all()
    rels = (k_pos[0] - q_pos[0]) - (r - 1) + np.arange(r + c - 1)
    v = jnp.pad(table.astype(F32)[index_of_rel(rels)].T, ((0, 0), (0, 1)))
    x = jnp.tile(v, (1, r))[:, :r * (r + c - 1)].reshape(v.shape[0], r, r + c - 1)
    return x[:, :, r - 1:r - 1 + c]


def _t5_bias_tiles(t5_table, q_pos, k_pos, far_bucket):
    mask = (k_pos[None, :] // CHUNK) <= (q_pos[:, None] // CHUNK)
    bias = _toeplitz(t5_table, _t5_bucket_np, q_pos, k_pos)
    bias = (bias - t5_table.astype(F32)[far_bucket][:, None, None]) * LOG2E
    return jnp.where(mask[None], bias, NEG_INF)


def _band_bias_tiles(rel_table, q_pos, k_pos):
    qc = q_pos[:, None] // CHUNK
    kc = k_pos[None, :] // CHUNK
    mask = (kc <= qc) & (qc - kc <= C_PREV_CHUNKS) & (k_pos[None, :] >= 0)
    bias = _toeplitz(rel_table, lambda rel: np.clip(rel, -REL_CLIP, REL_CLIP) + REL_CLIP, q_pos, k_pos) * LOG2E
    return jnp.where(mask[None], bias, NEG_INF)


def kernel(x_prompt, x_sample, cache_a_k, cache_a_v, cache_b_k, cache_b_v, cache_b_logf, cache_c_k, cache_c_v, c_prompt, c_sample, w_ada, b_ada, norm_g, w_ffn_in, w_ffn_out, w_in_ab, b_forget, w_out_ab, lambda_q1, lambda_k1, lambda_q2, lambda_k2, subln_g, t5_table, w_in_c, w_out_c, c_rel_bias, final_g):
    depth = w_ada.shape[0]
    bsz, seq, d = x_prompt.shape
    dbsz, dseq, _ = x_sample.shape
    past = cache_b_logf.shape[2]
    h_a = cache_a_k.shape[3]
    h_b = cache_b_k.shape[3]
    h_c = cache_c_k.shape[3]
    wa, wb, cw = h_a * HEAD_DIM, h_b * HEAD_DIM, h_c * HEAD_DIM
    assert bsz == dbsz and wa == wb and 3 * wa + 3 * wb == 3 * cw
    assert LANES >= T5_MAX_DIST and SAMPLE_NEAR >= T5_MAX_DIST and past >= SAMPLE_NEAR
    far_bucket = (N_T5_BUCKETS // 2) - 1

    mods = _ada_mods(jnp.concatenate([c_prompt, c_sample], axis=0), w_ada, b_ada)
    mods = mods.reshape(depth, 2, bsz, 9, d).transpose(0, 3, 1, 2, 4).reshape(depth * 9 * 2, bsz, 1, d)

    def slot(l, k, grp):
        return (l * 9 + k) * 2 + grp

    xs = [x_prompt, x_sample]
    ab_states = [[], []]
    c_states = [[], []]
    for l in range(depth):
        i = l // 2
        last = l == depth - 1
        w1 = _prep_ffn_weights(w_ffn_in[l, 0], w_ffn_out[l, 0])
        w2 = _prep_ffn_weights(w_ffn_in[l, 1], w_ffn_out[l, 1])
        for grp in range(2):
            xs[grp] = _ffn(xs[grp], mods, [slot(l, k, grp) for k in range(3)], norm_g[l, 0], *w1)

        if l % 2 == 0:
            lam_init = 0.8 - 0.6 * math.exp(-0.3 * l)
            lam = (jnp.exp(jnp.sum(lambda_q1[i].astype(F32) * lambda_k1[i].astype(F32)))
                   - jnp.exp(jnp.sum(lambda_q2[i].astype(F32) * lambda_k2[i].astype(F32))) + lam_init).reshape(1)
            n_main = 3 * wa + 3 * wb
            w_main = w_in_ab[i][:, :n_main].astype(BF16)
            wf = jnp.pad(w_in_ab[i][:, n_main:].T.astype(BF16), ((0, 16 - h_b), (0, 0)))
            bfo = b_forget[i].astype(F32).reshape(h_b, 1)
            w_out = w_out_ab[i].astype(BF16)
            g_sub = subln_g[i].astype(F32).reshape(1, HEAD_DIM)
            tn = min(PROJ_COL_TILE, wa)
            taps = _seg_taps(tn, wa, (1, 2, 4, 5), False)
            cs = np.ones((n_main,), np.float32)
            cs[:wa] = DA_HALF ** -0.5 * LOG2E
            cs[3 * wa:3 * wa + wb] = HEAD_DIM ** -0.5 * LOG2E
            for grp in range(2):
                x = xs[grp]
                b, t, _ = x.shape
                tap_shapes = [((b, t, wa), True)] * 4
                if grp == 0:
                    qkv, ka, va, kb, vb, vt, logf_t = _proj(
                        x, mods, [slot(l, 3, grp), slot(l, 4, grp)], norm_g[l, 1], w_main, cs, tn, tap_shapes, taps,
                        wf, bfo, vt_cols=_vt_cols(tn, wa, (2, 5)), vt_tile=ROW_TILE)
                else:
                    qkv, ka, va, kb, vb, logf_t = _proj(
                        x, mods, [slot(l, 3, grp), slot(l, 4, grp)], norm_g[l, 1], w_main, cs, tn, tap_shapes, taps,
                        wf, bfo)
                logf = logf_t.reshape(h_b, b, t).transpose(1, 2, 0)
                ab_states[grp].append((ka.reshape(b, t, h_a, HEAD_DIM), va.reshape(b, t, h_a, HEAD_DIM),
                                       kb.reshape(b, t, h_b, HEAD_DIM), vb.reshape(b, t, h_b, HEAD_DIM), logf))
                logf_bh = logf_t.reshape(h_b, b, t).transpose(1, 0, 2)
                if grp == 0:
                    o_a = _diff_prompt(lam, t5_table.astype(F32).reshape(-1), qkv, vt, g_sub, h_a, 1.0 - lam_init)
                    fcum = _cumsum(logf_bh, jnp.zeros((b, h_b, LANES), F32))
                    nq = t // ROW_TILE
                    fref = fcum[:, :, ::ROW_TILE].reshape(-1)
                    o_b = _fox_prompt(fref, qkv, vt, fcum.reshape(b, h_b, nq, 1, ROW_TILE), h_b, 3 * h_a, h_a)
                else:
                    q_pos = past + np.arange(t)
                    bnear = _t5_bias_tiles(t5_table, q_pos, past - SAMPLE_NEAR + np.arange(SAMPLE_NEAR), far_bucket)
                    bnew = _t5_bias_tiles(t5_table, q_pos, q_pos, far_bucket)
                    kc = cache_a_k[i].transpose(0, 2, 1, 3)
                    vc = cache_a_v[i].transpose(0, 2, 1, 3)
                    o_a = _diff_sample(lam, qkv, kc, vc, bnear, bnew, g_sub, h_a, 1.0 - lam_init)
                    fc = _cumsum(cache_b_logf[i].astype(F32).transpose(0, 2, 1), jnp.zeros((b, h_b, LANES), F32))
                    carry = fc[:, :, past - 1:past]
                    fn = _cumsum(jnp.pad(logf_bh, ((0, 0), (0, 0), (0, LANES - t))),
                                 jnp.broadcast_to(carry, (b, h_b, LANES)))
                    o_b = _fox_sample(carry.reshape(-1), qkv, cache_b_k[i].transpose(0, 2, 1, 3),
                                      cache_b_v[i].transpose(0, 2, 1, 3), fc.reshape(b, h_b, 1, past),
                                      fn.reshape(b, h_b, 1, LANES), h_b, 3 * h_a)
                xs[grp] = _outproj([o_a, o_b], w_out, x, mods, slot(l, 5, grp))
        else:
            w_in = w_in_c[i].astype(BF16)
            w_out = w_out_c[i].astype(BF16)
            tn = min(PROJ_COL_TILE, cw)
            taps = _seg_taps(tn, cw, (1, 2), True)
            cs = np.ones((3 * cw,), np.float32)
            cs[:cw] = HEAD_DIM ** -0.5 * LOG2E
            for grp in range(2):
                x = xs[grp]
                b, t, _ = x.shape
                keep = min(C_BAND, t)
                assert keep == min(t, ROW_TILE)
                tap_shapes = [((b, keep, cw), False)] * 2
                if grp == 0:
                    qkv, k_new, v_new, vt = _proj(
                        x, mods, [slot(l, 3, grp), slot(l, 4, grp)], norm_g[l, 1], w_in, cs, tn, tap_shapes, taps,
                        vt_cols=_vt_cols(tn, cw, (2,)), vt_tile=BAND_TILE)
                else:
                    qkv, k_new, v_new = _proj(
                        x, mods, [slot(l, 3, grp), slot(l, 4, grp)], norm_g[l, 1], w_in, cs, tn, tap_shapes, taps)
                c_states[grp].append((k_new.reshape(b, keep, h_c, HEAD_DIM), v_new.reshape(b, keep, h_c, HEAD_DIM)))
                if grp == 0:
                    o_c = _band_prompt(c_rel_bias[i].astype(F32).reshape(-1), qkv, vt, h_c)
                else:
                    lc = cache_c_k.shape[2]
                    q_pos = past + np.arange(t)
                    bmc = _band_bias_tiles(c_rel_bias[i], q_pos, past - lc + np.arange(lc))
                    bmn = _band_bias_tiles(c_rel_bias[i], q_pos, q_pos)
                    o_c = _band_sample(qkv, cache_c_k[i], cache_c_v[i], bmc, bmn, h_c)
                xs[grp] = _outproj([o_c], w_out, x, mods, slot(l, 5, grp))

        for grp in range(2):
            xs[grp] = _ffn(xs[grp], mods, [slot(l, k, grp) for k in (6, 7, 8)], norm_g[l, 2], *w2,
                           final_g=final_g if last else None)

    outs = [xs[0], xs[1]]
    for grp in range(2):
        st = ab_states[grp]
        outs += [jnp.stack([s[k] for s in st]) for k in range(5)]
        st = c_states[grp]
        outs += [jnp.stack([s[k] for s in st]) for k in range(2)]
    return tuple(outs)
```

```python
import functools
import math

import numpy as np
import jax
import jax.numpy as jnp
from jax import lax
from jax.experimental import pallas as pl
from jax.experimental.pallas import tpu as pltpu

F32 = jnp.float32
BF16 = jnp.bfloat16

CHUNK = 64
HEAD_DIM = 128
DA_HALF = HEAD_DIM // 2
N_T5_BUCKETS = 32
T5_MAX_DIST = 128
C_PREV_CHUNKS = 8
C_BAND = C_PREV_CHUNKS * CHUNK
REL_CLIP = 128
EPS = 1e-6
NEG_INF = -1e30
LOG2E = math.log2(math.e)
VT_ROWS = HEAD_DIM + 16

VMEM_LIMIT_BYTES = 56 * 1024 * 1024
VMEM_TALL_LIMIT_BYTES = 60 * 1024 * 1024
LANES = 128

ROW_TILE = 512
FF_TILE = 512
FFN_OUT_TILE = 512
FFN_ROW_SPLIT = 2
FFN_TALL_ROW_TILE = 1024
PROJ_ROW_SPLIT = 2
PROJ_COL_TILE = 1024
ADA_COL_TILE = 1024
BAND_TILE = 256
BAND_HEADS = 8
BAND_LOOKAHEAD = 3
DIFF_HEADS = 4
FOX_HEADS = 4
ATTN_GROUP = 512
SAMPLE_NEAR = 256

_NT = (((1,), (1,)), ((), ()))


def _params(*sem, flags=None):
    return pltpu.CompilerParams(dimension_semantics=sem, vmem_limit_bytes=VMEM_LIMIT_BYTES, flags=flags)


def _modulated_norm(x, g, shift, scale):
    y = x * lax.rsqrt(jnp.mean(x * x, axis=-1, keepdims=True) + EPS) * g
    return y * (1.0 + scale) + shift


def _normed_rows(x_ref, shift_ref, scale_ref, g_ref, h_scr, n_split, r):
    nb, tt, d = x_ref.shape
    rows = nb * tt // n_split
    if nb == 1:
        x, sh, sc = x_ref[:, r * rows:(r + 1) * rows, :], shift_ref[...], scale_ref[...]
    else:
        seqs = slice(r * (nb // n_split), (r + 1) * (nb // n_split))
        x, sh, sc = x_ref[seqs], shift_ref[seqs], scale_ref[seqs]
    hr = _modulated_norm(x, g_ref[...], sh, sc).reshape(rows, d).astype(BF16)
    h_scr[r * rows:(r + 1) * rows, :] = hr
    return hr


def _ada_kernel(c_ref, w_ref, b_ref, o_ref):
    c = c_ref[...]
    a = (c * jax.nn.sigmoid(c)).astype(BF16)
    o_ref[0] = jnp.dot(a, w_ref[0].astype(BF16), preferred_element_type=F32) + b_ref[0]


def _ada_mods(c_all, w_ada, b_ada):
    depth, d, n = w_ada.shape
    r = c_all.shape[0]
    tn = math.gcd(n, ADA_COL_TILE)
    return pl.pallas_call(
        _ada_kernel,
        grid=(depth, n // tn),
        in_specs=[
            pl.BlockSpec((r, d), lambda l, j: (0, 0)),
            pl.BlockSpec((1, d, tn), lambda l, j: (l, 0, j)),
            pl.BlockSpec((1, 1, tn), lambda l, j: (l, 0, j)),
        ],
        out_specs=pl.BlockSpec((1, r, tn), lambda l, j: (l, 0, j)),
        out_shape=jax.ShapeDtypeStruct((depth, r, n), F32),
        compiler_params=_params("arbitrary", "arbitrary"),
        name="ada_mods",
    )(c_all, w_ada, b_ada.reshape(depth, 1, n))


def _row_blocking(x, row_tile=ROW_TILE):
    b, t, _ = x.shape
    if t >= row_tile:
        assert t % row_tile == 0
        return 1, row_tile, t // row_tile, b * (t // row_tile)
    assert (b * t) % 8 == 0
    return b, t, 1, 1


def _x_index(nb, tps):
    if nb == 1:
        return lambda i, j: (i // tps, i % tps, 0)
    return lambda i, j: (0, 0, 0)


def _mod_spec(nb, tps, d, slot):
    if nb == 1:
        return pl.BlockSpec((None, 1, 1, d), lambda i, j: (slot, i // tps, 0, 0))
    return pl.BlockSpec((None, nb, 1, d), lambda i, j: (slot, 0, 0, 0))


def _ffn_kernel(*refs, nb, tt, n_ff, last_cols, n_out, final):
    if final:
        (x_ref, shift_ref, scale_ref, gate_ref, g_ref, wg_ref, wu_ref, wo_ref, fg_ref,
         o_ref, h_scr, a_scr) = refs
    else:
        (x_ref, shift_ref, scale_ref, gate_ref, g_ref, wg_ref, wu_ref, wo_ref,
         o_ref, h_scr, a_scr) = refs
    j = pl.program_id(1)
    d = x_ref.shape[-1]
    tm = nb * tt
    tf = wg_ref.shape[1]
    tn = wo_ref.shape[1]

    rows = tm // FFN_ROW_SPLIT
    normed_rows = functools.partial(_normed_rows, x_ref, shift_ref, scale_ref, g_ref, h_scr, FFN_ROW_SPLIT)

    def hidden_chunk(rows_of, cols=tf):
        def products(r):
            hr = rows_of(r)
            return (jnp.dot(hr, wg_ref[:, :cols], preferred_element_type=F32),
                    jnp.dot(hr, wu_ref[:, :cols], preferred_element_type=F32))

        parts = []
        cur = products(0)
        for r in range(FFN_ROW_SPLIT):
            nxt = products(r + 1) if r + 1 < FFN_ROW_SPLIT else None
            gg, uu = cur
            parts.append((gg * jax.nn.sigmoid(gg) * uu).astype(BF16))
            cur = nxt
        a_scr[j, :, :cols] = jnp.concatenate(parts, axis=0)

    kept_rows = lambda r: h_scr[r * rows:(r + 1) * rows, :]
    n_full = n_ff if last_cols == tf else n_ff - 1
    assert n_full >= 1

    @pl.when(j == 0)
    def _():
        hidden_chunk(normed_rows)

    @pl.when(jnp.logical_and(j > 0, j < n_full))
    def _():
        hidden_chunk(kept_rows)

    if n_full < n_ff:
        @pl.when(j == n_ff - 1)
        def _():
            hidden_chunk(kept_rows, last_cols)

    @pl.when(j >= n_ff)
    def _():
        a = jnp.concatenate([a_scr[f] for f in range(n_full)] + [a_scr[f, :, :last_cols] for f in range(n_full, n_ff)],
                            axis=1)
        acc = jnp.dot(a, wo_ref[...], preferred_element_type=F32).reshape(nb, tt, tn)
        for n in range(n_out):
            @pl.when(j == n_ff + n)
            def _(n=n):
                cs = slice(n * tn, (n + 1) * tn)
                y = x_ref[:, :, cs] + 0.5 * gate_ref[:, :, cs] * acc
                if final:
                    o_ref[:, :, cs] = y
                else:
                    o_ref[...] = y

    if final:
        @pl.when(j == n_ff + n_out - 1)
        def _():
            y = o_ref[...]
            o_ref[...] = y * lax.rsqrt(jnp.mean(y * y, axis=-1, keepdims=True) + EPS) * fg_ref[...]


def _ffn(x, mods, slots, g, wg_p, wu_p, w_out_p, final_g=None):
    b, t, d = x.shape
    final = final_g is not None
    tall = not final and t >= FFN_TALL_ROW_TILE
    nb, tt, tps, nrows = _row_blocking(x, FFN_TALL_ROW_TILE if tall else ROW_TILE)
    ff = w_out_p.shape[0]
    n_ff = -(-ff // FF_TILE)
    last_cols = ff - (n_ff - 1) * FF_TILE
    tn = math.gcd(d, FFN_OUT_TILE)
    n_out = d // tn
    xi = _x_index(nb, tps)
    in_specs = [
        pl.BlockSpec((nb, tt, d), xi),
        _mod_spec(nb, tps, d, slots[0]),
        _mod_spec(nb, tps, d, slots[1]),
        _mod_spec(nb, tps, d, slots[2]),
        pl.BlockSpec((1, d), lambda i, j: (0, 0)),
        pl.BlockSpec((d, FF_TILE), lambda i, j: (0, jnp.minimum(j, n_ff - 1))),
        pl.BlockSpec((d, FF_TILE), lambda i, j: (0, jnp.minimum(j, n_ff - 1))),
        pl.BlockSpec((ff, tn), lambda i, j: (0, jnp.maximum(j - n_ff, 0))),
    ]
    args = [x, mods, mods, mods, g.reshape(1, d), wg_p, wu_p, w_out_p]
    if final:
        in_specs.append(pl.BlockSpec((1, d), lambda i, j: (0, 0)))
        args.append(final_g.reshape(1, d))
    return pl.pallas_call(
        functools.partial(_ffn_kernel, nb=nb, tt=tt, n_ff=n_ff, last_cols=last_cols, n_out=n_out, final=final),
        grid=(nrows, n_ff + n_out),
        in_specs=in_specs,
        out_specs=(pl.BlockSpec((nb, tt, d), xi) if final else
                   pl.BlockSpec((nb, tt, tn), lambda i, j: xi(i, j)[:2] + (jnp.maximum(j - n_ff, 0),))),
        out_shape=jax.ShapeDtypeStruct((b, t, d), F32),
        scratch_shapes=[pltpu.VMEM((nb * tt, d), BF16), pltpu.VMEM((n_ff, nb * tt, FF_TILE), BF16)],
        compiler_params=pltpu.CompilerParams(
            dimension_semantics=("parallel", "arbitrary"),
            vmem_limit_bytes=VMEM_TALL_LIMIT_BYTES if tall else VMEM_LIMIT_BYTES),
        name="ffn",
    )(*args)


def _prep_ffn_weights(w_in, w_out):
    ff = w_out.shape[0]
    return w_in[:, :ff].astype(BF16), w_in[:, ff:].astype(BF16), w_out.astype(BF16)


def _log_sigmoid(x):
    return jnp.minimum(x, 0.0) - jnp.log1p(jnp.exp(-jnp.abs(x)))


def _proj_kernel(*refs, nb, tt, tn, tps, taps, has_f, vt_cols, vt_tile):
    x_ref, shift_ref, scale_ref, g_ref, w_ref, cs_ref = refs[:6]
    pos = 6
    if has_f:
        wf_ref, bf_ref = refs[6:8]
        pos = 8
    obf_ref = refs[pos]
    tap_refs = refs[pos + 1:pos + 1 + len({tp[0] for tp in taps})]
    pos = pos + 1 + len(tap_refs)
    if vt_cols:
        vt_ref = refs[pos]
        pos += 1
    if has_f:
        logf_ref = refs[pos]
        pos += 1
    h_scr = refs[pos]
    i = pl.program_id(0)
    j = pl.program_id(1)
    d = x_ref.shape[-1]

    def store_attention_copy(y):
        obf_ref[...] = (y * cs_ref[...]).astype(BF16).reshape(nb, tt, tn)

    assert all(tp[1] > 0 for tp in taps) and all(vc[0] > 0 for vc in vt_cols)

    @pl.when(j == 0)
    def _():
        hs = []
        ys = []
        for r in range(PROJ_ROW_SPLIT):
            hs.append(_normed_rows(x_ref, shift_ref, scale_ref, g_ref, h_scr, PROJ_ROW_SPLIT, r))
            ys.append(jnp.dot(hs[-1], w_ref[...], preferred_element_type=F32))
        store_attention_copy(jnp.concatenate(ys, axis=0))
        if has_f:
            fr = lax.dot_general(wf_ref[...], jnp.concatenate(hs, axis=0), _NT, preferred_element_type=F32)
            logf_ref[...] = _log_sigmoid(fr[:logf_ref.shape[0]] + bf_ref[...])

    @pl.when(j > 0)
    def _():
        y = jnp.dot(h_scr[...], w_ref[...], preferred_element_type=F32)
        store_attention_copy(y)
        _proj_taps(y, i, j, tap_refs, vt_ref if vt_cols else None, nb=nb, tt=tt, tn=tn, tps=tps, taps=taps,
                   vt_cols=vt_cols, vt_tile=vt_tile)


def _proj_taps(y, i, j, tap_refs, vt_ref, *, nb, tt, tn, tps, taps, vt_cols, vt_tile):
    for out_idx, jval, col0, last_rows_only in taps:
        cond = j == jval
        if last_rows_only and tps > 1:
            cond = jnp.logical_and(cond, i % tps == tps - 1)

        @pl.when(cond)
        def _(out_idx=out_idx, col0=col0):
            tap_refs[out_idx][:, :, col0:col0 + tn] = y.reshape(nb, tt, tn)

    for jval, head0 in vt_cols:
        @pl.when(j == jval)
        def _(head0=head0):
            for hh in range(tn // HEAD_DIM):
                yt = y[:, hh * HEAD_DIM:(hh + 1) * HEAD_DIM].T.astype(BF16)
                yt = jnp.concatenate([yt, jnp.ones((VT_ROWS - HEAD_DIM, yt.shape[1]), BF16)], axis=0)
                for s in range(tt // vt_tile):
                    vt_ref[0, head0 + hh, s] = yt[:, s * vt_tile:(s + 1) * vt_tile]


def _seg_taps(tn, width, segments, last_rows_only):
    per = width // tn
    assert per * tn == width
    return [(o, sidx * per + s, s * tn, last_rows_only) for o, sidx in enumerate(segments) for s in range(per)]


def _vt_cols(tn, width, segments):
    per = width // tn
    hpb = tn // HEAD_DIM
    return [(sidx * per + s, (o * per + s) * hpb) for o, sidx in enumerate(segments) for s in range(per)]


def _proj(x, mods, slots, g, w_bf, col_scale, tn, tap_shapes, taps, wf=None, bf=None, vt_cols=(), vt_tile=None):
    b, t, d = x.shape
    n = w_bf.shape[1]
    nb, tt, tps, nrows = _row_blocking(x)
    has_f = wf is not None
    assert not vt_cols or nb == 1
    in_specs = [
        pl.BlockSpec((nb, tt, d), _x_index(nb, tps)),
        _mod_spec(nb, tps, d, slots[0]),
        _mod_spec(nb, tps, d, slots[1]),
        pl.BlockSpec((1, d), lambda i, j: (0, 0)),
        pl.BlockSpec((d, tn), lambda i, j: (0, j)),
        pl.BlockSpec((1, tn), lambda i, j: (0, j)),
    ]
    args = [x, mods, mods, g.reshape(1, d), w_bf, jnp.asarray(col_scale, F32).reshape(1, n)]
    if has_f:
        in_specs += [pl.BlockSpec(wf.shape, lambda i, j: (0, 0)), pl.BlockSpec(bf.shape, lambda i, j: (0, 0))]
        args += [wf, bf]
    xi = _x_index(nb, tps)
    out_specs = [pl.BlockSpec((nb, tt, tn), lambda i, j: xi(i, j)[:2] + (j,))]
    out_shape = [jax.ShapeDtypeStruct((b, t, n), BF16)]
    for shp, follows_rows in tap_shapes:
        if follows_rows:
            out_specs.append(pl.BlockSpec((nb, tt, shp[2]), xi))
        elif nb == 1:
            out_specs.append(pl.BlockSpec((1, shp[1], shp[2]), lambda i, j: (i // tps, 0, 0)))
        else:
            out_specs.append(pl.BlockSpec(shp, lambda i, j: (0, 0, 0)))
        out_shape.append(jax.ShapeDtypeStruct(shp, F32))
    if vt_cols:
        n_vh = len(vt_cols) * (tn // HEAD_DIM)
        out_specs.append(pl.BlockSpec((1, n_vh, tt // vt_tile, VT_ROWS, vt_tile), lambda i, j: (i // tps, 0, i % tps, 0, 0)))
        out_shape.append(jax.ShapeDtypeStruct((b, n_vh, t // vt_tile, VT_ROWS, vt_tile), BF16))
    if has_f:
        nf = bf.shape[0]
        out_specs.append(pl.BlockSpec((nf, nb * tt), lambda i, j: (0, i)))
        out_shape.append(jax.ShapeDtypeStruct((nf, b * t), F32))
    return pl.pallas_call(
        functools.partial(_proj_kernel, nb=nb, tt=tt, tn=tn, tps=tps, taps=tuple(taps), has_f=has_f,
                          vt_cols=tuple(vt_cols), vt_tile=vt_tile),
        grid=(nrows, n // tn),
        in_specs=in_specs,
        out_specs=out_specs,
        out_shape=out_shape,
        scratch_shapes=[pltpu.VMEM((nb * tt, d), BF16)],
        compiler_params=_params("arbitrary", "arbitrary"),
        name="mixer_in_proj",
    )(*args)


def _outproj_kernel(*refs, nb, tt, n_in):
    o_refs = refs[:n_in]
    w_ref, x_ref, gate_ref, out_ref = refs[n_in:]
    d = x_ref.shape[-1]
    acc = None
    row0 = 0
    for o_ref in o_refs:
        wd = o_ref.shape[-1]
        part = jnp.dot(o_ref[...].reshape(nb * tt, wd), w_ref[row0:row0 + wd, :], preferred_element_type=F32)
        acc = part if acc is None else acc + part
        row0 += wd
    out_ref[...] = x_ref[...] + gate_ref[...] * acc.reshape(nb, tt, d)


def _outproj(o_list, w_bf, x, mods, gate_slot):
    b, t, d = x.shape
    nb, tt, tps, nrows = _row_blocking(x)
    xi = _x_index(nb, tps)
    in_specs = [pl.BlockSpec((nb, tt, o.shape[-1]), xi) for o in o_list]
    in_specs += [
        pl.BlockSpec(w_bf.shape, lambda i, j: (0, 0)),
        pl.BlockSpec((nb, tt, d), xi),
        _mod_spec(nb, tps, d, gate_slot),
    ]
    return pl.pallas_call(
        functools.partial(_outproj_kernel, nb=nb, tt=tt, n_in=len(o_list)),
        grid=(nrows, 1),
        in_specs=in_specs,
        out_specs=pl.BlockSpec((nb, tt, d), xi),
        out_shape=jax.ShapeDtypeStruct((b, t, d), F32),
        compiler_params=_params("parallel", "arbitrary"),
        name="mixer_out_proj",
    )(*o_list, w_bf, x, mods)


def _cumsum_kernel(x_ref, c_ref, o_ref):
    x = x_ref[0]
    n = x.shape[1]
    lane = lax.broadcasted_iota(jnp.int32, x.shape, 1)
    s = 1
    while s < n:
        x = x + jnp.where(lane >= s, pltpu.roll(x, s, axis=1), 0.0)
        s *= 2
    o_ref[0] = x + c_ref[0][:, :1]


def _cumsum(x, carry):
    g, h, n = x.shape
    return pl.pallas_call(
        _cumsum_kernel,
        grid=(g,),
        in_specs=[pl.BlockSpec((1, h, n), lambda i: (i, 0, 0)), pl.BlockSpec((1, h, LANES), lambda i: (i, 0, 0))],
        out_specs=pl.BlockSpec((1, h, n), lambda i: (i, 0, 0)),
        out_shape=jax.ShapeDtypeStruct((g, h, n), F32),
        compiler_params=_params("arbitrary"),
        name="logf_cumsum",
    )(x, carry)


def _softmax_init(m_scr, acc_scr):
    m_scr[...] = jnp.full(m_scr.shape, NEG_INF, F32)
    acc_scr[...] = jnp.zeros(acc_scr.shape, F32)


def _softmax_update_t(s, s_max, vt, m_prev, acc_prev):
    m_new = jnp.maximum(m_prev, s_max)
    alpha = jnp.exp2(m_prev - m_new)
    p = jnp.exp2(s - m_new)
    acc_new = alpha * acc_prev + jnp.dot(vt, p.astype(BF16), preferred_element_type=F32)
    return m_new, acc_new


def _normalised_t(acc):
    return (acc[:HEAD_DIM] / acc[HEAD_DIM:HEAD_DIM + 1]).T


def _tile_stage(slot, vt_of, m_scr, acc_scr, s_scr, smax_scr, n_groups, next_scores=None):
    g = m_scr.shape[1] // n_groups
    sls = [slice(i * g, (i + 1) * g) for i in range(n_groups)]
    state = [(m_scr[:, sl], acc_scr[:, sl]) for sl in sls]
    new = []
    for i, sl in enumerate(sls):
        if next_scores is not None:
            s = next_scores(i)
            s_scr[1 - slot, :, sl] = s
            smax_scr[1 - slot, :, sl] = jnp.max(s, axis=0, keepdims=True)
        new.append(_softmax_update_t(s_scr[slot, :, sl], smax_scr[slot, :, sl], vt_of(i), *state[i]))
    for sl, (m_new, acc_new) in zip(sls, new):
        m_scr[:, sl] = m_new
        acc_scr[:, sl] = acc_new


def _first_tile(scores, s_scr, smax_scr, n_groups):
    g = s_scr.shape[2] // n_groups
    for i in range(n_groups):
        sl = slice(i * g, (i + 1) * g)
        s = scores(i)
        s_scr[0, :, sl] = s
        smax_scr[0, :, sl] = jnp.max(s, axis=0, keepdims=True)


def _pipelined_tiles(last, n_special, produce_first, stage):
    def kind(dist):
        return dist if dist < n_special else "far"

    for k in range(n_special):
        @pl.when(last == k)
        def _(k=k):
            produce_first(kind(k))

    @pl.when(last >= n_special)
    def _():
        produce_first("far")

    n_pairs = jnp.maximum(last - n_special, 0) // 2

    def pair(i, carry):
        stage(2 * i, 0, "far")
        stage(2 * i + 1, 1, "far")
        return carry

    lax.fori_loop(0, n_pairs, pair, 0)
    t0 = 2 * n_pairs
    remaining = last - t0 + 1
    for r in range(1, n_special + 3):
        @pl.when(remaining == r)
        def _(r=r):
            for u in range(r):
                stage(t0 + u, u % 2, kind(r - 2 - u) if u + 1 < r else None)


def _toeplitz_values(tbl_ref, idx, n_vals, nh, h, ref_row):
    ref_val = tbl_ref[ref_row * nh + h]

    def body(v, acc):
        return jnp.where(idx == v, tbl_ref[v * nh + h] - ref_val, acc)

    return lax.fori_loop(0, n_vals, body, jnp.zeros(idx.shape, F32))


def _stack_diff_queries(q):
    lane = lax.broadcasted_iota(jnp.int32, q.shape, 1)
    zero = jnp.zeros_like(q)
    qs = jnp.concatenate([jnp.where(lane < DA_HALF, q, zero), jnp.where(lane >= DA_HALF, q, zero)], axis=0)
    return qs


def _diff_finalize(o, lam, g, out_scale):
    tq = o.shape[0] // 2
    od = o[:tq] - lam * o[tq:]
    return od * lax.rsqrt(jnp.mean(od * od, axis=-1, keepdims=True) + EPS) * g * out_scale


def _diff_p_kernel(lam_ref, tbl_ref, q_ref, k_ref, vt_ref, idx_ref, base_ref, g_ref, o_ref,
                   m_scr, acc_scr, bias_scr, s_scr, smax_scr, *, tq, nh, hps, far_bucket, out_scale):
    hg = pl.program_id(0)
    bi = pl.program_id(1)
    qi = pl.program_id(2)
    nblk = tq // LANES

    @pl.when(jnp.logical_and(bi == 0, qi == 0))
    def _():
        for e in range(hps):
            t = _toeplitz_values(tbl_ref, idx_ref[...], N_T5_BUCKETS, nh, hg * hps + e, far_bucket) * LOG2E
            bias_scr[e] = base_ref[...]
            for blk in range(nblk):
                sl = slice(blk * LANES, (blk + 1) * LANES)
                bias_scr[e, 0, sl, sl] += t[0]
                if blk >= 1:
                    bias_scr[e, 0, (blk - 1) * LANES:blk * LANES, sl] += t[1]
            bias_scr[e, 1, (nblk - 1) * LANES:, :LANES] += t[1]

    qs = [_stack_diff_queries(q_ref[0, :, e * HEAD_DIM:(e + 1) * HEAD_DIM]) for e in range(hps)]
    _softmax_init(m_scr, acc_scr)
    gph = 2 * tq // ATTN_GROUP
    ngrp = hps * gph

    def scores(j, kind):
        start = pl.multiple_of(j * tq, tq)

        def group(i):
            e, gi = divmod(i, gph)
            k = k_ref[0, pl.ds(start, tq), e * HEAD_DIM:(e + 1) * HEAD_DIM]
            s = lax.dot_general(k, qs[e][gi * ATTN_GROUP:(gi + 1) * ATTN_GROUP], _NT, preferred_element_type=F32)
            if kind != "far":
                q0 = (gi * ATTN_GROUP) % tq
                s = s + bias_scr[e, kind, :, q0:q0 + ATTN_GROUP]
            return s
        return group

    def stage(t, slot, next_kind):
        nxt = None if next_kind is None else scores(t + 1, next_kind)
        _tile_stage(slot, lambda i: vt_ref[0, i // gph, t], m_scr, acc_scr, s_scr, smax_scr, ngrp, nxt)

    _pipelined_tiles(qi, 2, lambda kind: _first_tile(scores(0, kind), s_scr, smax_scr, ngrp), stage)
    outs = []
    for e in range(hps):
        o = _normalised_t(acc_scr[:, e * 2 * tq:(e + 1) * 2 * tq])
        outs.append(_diff_finalize(o, lam_ref[0], g_ref[...], out_scale).astype(BF16))
    o_ref[0] = jnp.concatenate(outs, axis=1)


def _diff_prompt(lam, tbl, qkv, vt, g, n_heads, out_scale):
    b, t, _ = qkv.shape
    tq = vt.shape[-1]
    far_bucket = N_T5_BUCKETS // 2 - 1
    pos = np.arange(LANES)
    idx = np.stack([_t5_bucket_np(pos[:, None] - pos[None, :]),
                    _t5_bucket_np(pos[:, None] - pos[None, :] - LANES)]).astype(np.int32)
    kq = np.arange(tq)
    base = np.zeros((2, tq, tq), np.float32)
    base[0] = np.where((kq[:, None] // CHUNK) <= (kq[None, :] // CHUNK), 0.0, NEG_INF)
    hps = math.gcd(n_heads, DIFF_HEADS)
    ngrp = n_heads // hps
    assert ngrp * hps == n_heads
    wd = hps * HEAD_DIM
    lanes = hps * 2 * tq
    return pl.pallas_call(
        functools.partial(_diff_p_kernel, tq=tq, nh=n_heads, hps=hps, far_bucket=far_bucket, out_scale=out_scale),
        grid=(ngrp, b, t // tq),
        in_specs=[
            pl.BlockSpec(memory_space=pltpu.SMEM),
            pl.BlockSpec(memory_space=pltpu.SMEM),
            pl.BlockSpec((1, tq, wd), lambda h, bi, qi: (bi, qi, h)),
            pl.BlockSpec((1, t, wd), lambda h, bi, qi: (bi, 0, ngrp + h)),
            pl.BlockSpec((1, hps, t // tq, VT_ROWS, tq), lambda h, bi, qi: (bi, h, 0, 0, 0)),
            pl.BlockSpec(idx.shape, lambda h, bi, qi: (0, 0, 0)),
            pl.BlockSpec(base.shape, lambda h, bi, qi: (0, 0, 0)),
            pl.BlockSpec((1, HEAD_DIM), lambda h, bi, qi: (0, 0)),
        ],
        out_specs=pl.BlockSpec((1, tq, wd), lambda h, bi, qi: (bi, qi, h)),
        out_shape=jax.ShapeDtypeStruct((b, t, n_heads * HEAD_DIM), BF16),
        scratch_shapes=[pltpu.VMEM((1, lanes), F32), pltpu.VMEM((VT_ROWS, lanes), F32),
                        pltpu.VMEM((hps, 2, tq, tq), F32), pltpu.VMEM((2, tq, lanes), F32),
                        pltpu.VMEM((2, 1, lanes), F32)],
        compiler_params=_params("arbitrary", "arbitrary", "arbitrary"),
        name="diff_attn_prompt",
    )(lam, tbl, qkv, qkv, vt, jnp.asarray(idx), jnp.asarray(base), g)


def _fox_p_kernel(fref_ref, q_ref, k_ref, vt_ref, f_ref, o_ref, m_scr, acc_scr, fcol_scr, s_scr, smax_scr,
                  *, tq, hps):
    bi = pl.program_id(0)
    hg = pl.program_id(1)
    qi = pl.program_id(2)
    nblk = tq // LANES

    @pl.when(qi == 0)
    def _():
        for e in range(hps):
            for c in range(f_ref.shape[2] * nblk):
                row = f_ref[0, e, c // nblk][:, (c % nblk) * LANES:(c % nblk + 1) * LANES]
                fcol_scr[e, c * LANES:(c + 1) * LANES, :] = jnp.broadcast_to(row, (LANES, LANES)).T

    n_heads = pl.num_programs(1) * hps
    frefs = [fref_ref[(bi * n_heads + hg * hps + e) * pl.num_programs(2) + qi] for e in range(hps)]
    _softmax_init(m_scr, acc_scr)

    gph = tq // ATTN_GROUP
    ngrp = hps * gph

    def scores(j, kind):
        start = pl.multiple_of(j * tq, tq)

        def group(i):
            e, gi = divmod(i, gph)
            hs = slice(e * HEAD_DIM, (e + 1) * HEAD_DIM)
            decay = (frefs[e] - fcol_scr[e, pl.ds(start, tq), :]) * LOG2E
            decay = jnp.concatenate([decay] * (ATTN_GROUP // LANES), axis=1)
            s = lax.dot_general(k_ref[0, pl.ds(start, tq), hs], q_ref[0, gi * ATTN_GROUP:(gi + 1) * ATTN_GROUP, hs],
                                _NT, preferred_element_type=F32)
            s = s + decay
            if kind != "far":
                key = lax.broadcasted_iota(jnp.int32, s.shape, 0)
                qry = lax.broadcasted_iota(jnp.int32, s.shape, 1) + gi * ATTN_GROUP
                s = jnp.where(key <= qry, s, NEG_INF)
            return s
        return group

    def stage(t, slot, next_kind):
        nxt = None if next_kind is None else scores(t + 1, next_kind)
        _tile_stage(slot, lambda i: vt_ref[0, i // gph, t], m_scr, acc_scr, s_scr, smax_scr, ngrp, nxt)

    _pipelined_tiles(qi, 1, lambda kind: _first_tile(scores(0, kind), s_scr, smax_scr, ngrp), stage)
    o_ref[0] = jnp.concatenate(
        [_normalised_t(acc_scr[:, e * tq:(e + 1) * tq]).astype(BF16) for e in range(hps)], axis=1)


def _fox_prompt(fref, qkv, vt, fcum, n_heads, col0, vt_head0):
    b, t, _ = qkv.shape
    tq = vt.shape[-1]
    nq = t // tq
    hps = math.gcd(math.gcd(n_heads, FOX_HEADS), math.gcd(col0, vt_head0))
    ngrp = n_heads // hps
    assert ngrp * hps == n_heads and col0 % hps == 0 and vt_head0 % hps == 0
    wd = hps * HEAD_DIM
    lanes = hps * tq
    return pl.pallas_call(
        functools.partial(_fox_p_kernel, tq=tq, hps=hps),
        grid=(b, ngrp, nq),
        in_specs=[
            pl.BlockSpec(memory_space=pltpu.SMEM),
            pl.BlockSpec((1, tq, wd), lambda bi, h, qi: (bi, qi, col0 // hps + h)),
            pl.BlockSpec((1, t, wd), lambda bi, h, qi: (bi, 0, (col0 + n_heads) // hps + h)),
            pl.BlockSpec((1, hps, nq, VT_ROWS, tq), lambda bi, h, qi: (bi, vt_head0 // hps + h, 0, 0, 0)),
            pl.BlockSpec((1, hps, nq, 1, tq), lambda bi, h, qi: (bi, h, 0, 0, 0)),
        ],
        out_specs=pl.BlockSpec((1, tq, wd), lambda bi, h, qi: (bi, qi, h)),
        out_shape=jax.ShapeDtypeStruct((b, t, n_heads * HEAD_DIM), BF16),
        scratch_shapes=[pltpu.VMEM((1, lanes), F32), pltpu.VMEM((VT_ROWS, lanes), F32),
                        pltpu.VMEM((hps, t, LANES), F32), pltpu.VMEM((2, tq, lanes), F32),
                        pltpu.VMEM((2, 1, lanes), F32)],
        compiler_params=_params("arbitrary", "arbitrary", "arbitrary"),
        name="fox_attn_prompt",
    )(fref, qkv, qkv, vt, fcum)


def _joint_softmax_pv(s_c, s_n, vc, vn):
    m = jnp.maximum(jnp.max(s_c, axis=1, keepdims=True), jnp.max(s_n, axis=1, keepdims=True))
    p_c = jnp.exp2(s_c - m)
    p_n = jnp.exp2(s_n - m)
    l = jnp.sum(p_c, axis=1, keepdims=True) + jnp.sum(p_n, axis=1, keepdims=True)
    acc = jnp.dot(p_c.astype(BF16), vc, preferred_element_type=F32)
    acc = acc + jnp.dot(p_n.astype(BF16), vn, preferred_element_type=F32)
    return acc / l


def _diff_s_kernel(lam_ref, q_ref, kn_ref, vn_ref, kc_ref, vc_ref, bnear_ref, bnew_ref, g_ref, o_ref, *, near, out_scale):
    qs = _stack_diff_queries(q_ref[0])
    kc = kc_ref[0, 0].astype(BF16)
    vc = vc_ref[0, 0].astype(BF16)
    p = kc.shape[0]
    s_c = lax.dot_general(qs, kc, _NT, preferred_element_type=F32)
    bnear = bnear_ref[0]
    s_c = jnp.concatenate([s_c[:, :p - near], s_c[:, p - near:] + jnp.concatenate([bnear, bnear], axis=0)], axis=1)
    bnew = bnew_ref[0]
    s_n = lax.dot_general(qs, kn_ref[0], _NT, preferred_element_type=F32) + jnp.concatenate([bnew, bnew], axis=0)
    o = _joint_softmax_pv(s_c, s_n, vc, vn_ref[0])
    o_ref[0] = _diff_finalize(o, lam_ref[0], g_ref[...], out_scale).astype(BF16)


def _diff_sample(lam, qkv, kc, vc, bnear, bnew, g, n_heads, out_scale):
    b, t, _ = qkv.shape
    p = kc.shape[2]
    near = bnear.shape[-1]
    hd = HEAD_DIM
    return pl.pallas_call(
        functools.partial(_diff_s_kernel, near=near, out_scale=out_scale),
        grid=(b, n_heads),
        in_specs=[
            pl.BlockSpec(memory_space=pltpu.SMEM),
            pl.BlockSpec((1, t, hd), lambda bi, h: (bi, 0, h)),
            pl.BlockSpec((1, t, hd), lambda bi, h: (bi, 0, n_heads + h)),
            pl.BlockSpec((1, t, hd), lambda bi, h: (bi, 0, 2 * n_heads + h)),
            pl.BlockSpec((1, 1, p, hd), lambda bi, h: (bi, h, 0, 0)),
            pl.BlockSpec((1, 1, p, hd), lambda bi, h: (bi, h, 0, 0)),
            pl.BlockSpec((1, t, near), lambda bi, h: (h, 0, 0)),
            pl.BlockSpec((1, t, t), lambda bi, h: (h, 0, 0)),
            pl.BlockSpec((1, hd), lambda bi, h: (0, 0)),
        ],
        out_specs=pl.BlockSpec((1, t, hd), lambda bi, h: (bi, 0, h)),
        out_shape=jax.ShapeDtypeStruct((b, t, n_heads * hd), BF16),
        compiler_params=_params("parallel", "parallel"),
        name="diff_attn_sample",
    )(lam, qkv, qkv, qkv, kc, vc, bnear, bnew, g)


def _fox_s_kernel(fref_ref, q_ref, kn_ref, vn_ref, kc_ref, vc_ref, fc_ref, fn_ref, o_ref):
    bi = pl.program_id(0)
    h = pl.program_id(1)
    q = q_ref[0]
    t = q.shape[0]
    fref = fref_ref[bi * pl.num_programs(1) + h]
    s_c = lax.dot_general(q, kc_ref[0, 0].astype(BF16), _NT, preferred_element_type=F32) + (fref - fc_ref[0, 0]) * LOG2E
    s_n = lax.dot_general(q, kn_ref[0], _NT, preferred_element_type=F32) + (fref - fn_ref[0, 0][:, :t]) * LOG2E
    row = lax.broadcasted_iota(jnp.int32, s_n.shape, 0)
    col = lax.broadcasted_iota(jnp.int32, s_n.shape, 1)
    s_n = jnp.where(col <= row, s_n, NEG_INF)
    o_ref[0] = _joint_softmax_pv(s_c, s_n, vc_ref[0, 0].astype(BF16), vn_ref[0]).astype(BF16)


def _fox_sample(fref, qkv, kc, vc, fc, fn, n_heads, col0):
    b, t, _ = qkv.shape
    p = kc.shape[2]
    hd = HEAD_DIM
    return pl.pallas_call(
        _fox_s_kernel,
        grid=(b, n_heads),
        in_specs=[
            pl.BlockSpec(memory_space=pltpu.SMEM),
            pl.BlockSpec((1, t, hd), lambda bi, h: (bi, 0, col0 + h)),
            pl.BlockSpec((1, t, hd), lambda bi, h: (bi, 0, col0 + n_heads + h)),
            pl.BlockSpec((1, t, hd), lambda bi, h: (bi, 0, col0 + 2 * n_heads + h)),
            pl.BlockSpec((1, 1, p, hd), lambda bi, h: (bi, h, 0, 0)),
            pl.BlockSpec((1, 1, p, hd), lambda bi, h: (bi, h, 0, 0)),
            pl.BlockSpec((1, 1, 1, p), lambda bi, h: (bi, h, 0, 0)),
            pl.BlockSpec((1, 1, 1, fn.shape[-1]), lambda bi, h: (bi, h, 0, 0)),
        ],
        out_specs=pl.BlockSpec((1, t, hd), lambda bi, h: (bi, 0, h)),
        out_shape=jax.ShapeDtypeStruct((b, t, n_heads * hd), BF16),
        compiler_params=_params("parallel", "parallel"),
        name="fox_attn_sample",
    )(fref, qkv, qkv, qkv, kc, vc, fc, fn)


def _band_p_kernel(tbl_ref, q_ref, k_ref, vt_ref, idx_ref, base_ref, o_ref, bm_scr, *, tq, nkb, nh, hps):
    hg = pl.program_id(0)
    bi = pl.program_id(1)
    i = pl.program_id(2)
    nblk = tq // LANES
    qoff = (nkb - 1) * nblk

    @pl.when(jnp.logical_and(bi == 0, i == 0))
    def _():
        for e in range(hps):
            t = _toeplitz_values(tbl_ref, idx_ref[...], 2 * REL_CLIP + 1, nh, hg * hps + e, 0)
            bm_scr[e] = base_ref[...]
            for bq in range(nblk):
                kd = qoff + bq
                bm_scr[e, kd * LANES:(kd + 1) * LANES, bq * LANES:(bq + 1) * LANES] += t[0] * LOG2E
                bm_scr[e, (kd - 1) * LANES:kd * LANES, bq * LANES:(bq + 1) * LANES] += t[1] * LOG2E

    def head_scores(e):
        hs = slice(e * HEAD_DIM, (e + 1) * HEAD_DIM)
        q = q_ref[0, :, hs]
        ss = []
        for blk in range(nkb):
            jb = i - (nkb - 1) + blk
            jc = jnp.maximum(jb, 0)
            k = k_ref[0, pl.ds(pl.multiple_of(jc * tq, tq), tq), hs]
            s = lax.dot_general(k, q, _NT, preferred_element_type=F32) + bm_scr[e, blk * tq:(blk + 1) * tq, :]
            if blk < nkb - 1:
                s = s + jnp.where(jb < 0, NEG_INF, 0.0)
            ss.append(s)
        return ss, functools.reduce(jnp.maximum, [jnp.max(s, axis=0, keepdims=True) for s in ss])

    outs = []
    pending = [head_scores(e) for e in range(min(BAND_LOOKAHEAD, hps))]
    for e in range(hps):
        ss, m = pending.pop(0)
        if e + BAND_LOOKAHEAD < hps:
            pending.append(head_scores(e + BAND_LOOKAHEAD))
        acc = None
        for blk, s in enumerate(ss):
            jc = jnp.maximum(i - (nkb - 1) + blk, 0)
            a = jnp.dot(vt_ref[0, e, jc], jnp.exp2(s - m).astype(BF16), preferred_element_type=F32)
            acc = a if acc is None else acc + a
        outs.append(_normalised_t(acc).astype(BF16))
    o_ref[0] = jnp.concatenate(outs, axis=1)


def _band_prompt(tbl, qkv, vt, n_heads):
    b, t, _ = qkv.shape
    tq = vt.shape[-1]
    nkb = C_BAND // tq + 1
    assert (nkb - 1) * tq == C_BAND and tq % LANES == 0 and LANES >= REL_CLIP
    pos = np.arange(LANES)
    rel = pos[:, None] - pos[None, :]
    idx = (np.stack([np.clip(rel, -REL_CLIP, REL_CLIP), np.clip(rel - LANES, -REL_CLIP, REL_CLIP)]) + REL_CLIP).astype(np.int32)
    k_pos = np.arange(nkb * tq)[:, None]
    q_pos = C_BAND + np.arange(tq)[None, :]
    kc, qc = k_pos // CHUNK, q_pos // CHUNK
    base = np.where((kc <= qc) & (qc - kc <= C_PREV_CHUNKS), 0.0, NEG_INF).astype(np.float32)
    hps = math.gcd(n_heads, BAND_HEADS)
    ngrp = n_heads // hps
    assert ngrp * hps == n_heads
    wd = hps * HEAD_DIM
    return pl.pallas_call(
        functools.partial(_band_p_kernel, tq=tq, nkb=nkb, nh=n_heads, hps=hps),
        grid=(ngrp, b, t // tq),
        in_specs=[
            pl.BlockSpec(memory_space=pltpu.SMEM),
            pl.BlockSpec((1, tq, wd), lambda h, bi, i: (bi, i, h)),
            pl.BlockSpec((1, t, wd), lambda h, bi, i: (bi, 0, ngrp + h)),
            pl.BlockSpec((1, hps, t // tq, VT_ROWS, tq), lambda h, bi, i: (bi, h, 0, 0, 0)),
            pl.BlockSpec(idx.shape, lambda h, bi, i: (0, 0, 0)),
            pl.BlockSpec(base.shape, lambda h, bi, i: (0, 0)),
        ],
        out_specs=pl.BlockSpec((1, tq, wd), lambda h, bi, i: (bi, i, h)),
        out_shape=jax.ShapeDtypeStruct((b, t, n_heads * HEAD_DIM), BF16),
        scratch_shapes=[pltpu.VMEM((hps, nkb * tq, tq), F32)],
        compiler_params=_params("arbitrary", "arbitrary", "arbitrary"),
        name="band_attn_prompt",
    )(tbl, qkv, qkv, vt, jnp.asarray(idx), jnp.asarray(base))


def _band_s_kernel(q_ref, kn_ref, vn_ref, kc_ref, vc_ref, bmc_ref, bmn_ref, o_ref, *, n_heads):
    outs = []
    for h in range(n_heads):
        hs = slice(h * HEAD_DIM, (h + 1) * HEAD_DIM)
        q = q_ref[0, :, hs]
        kc = kc_ref[0, :, h, :].astype(BF16)
        vc = vc_ref[0, :, h, :].astype(BF16)
        s_c = lax.dot_general(q, kc, _NT, preferred_element_type=F32) + bmc_ref[h]
        s_n = lax.dot_general(q, kn_ref[0, :, hs], _NT, preferred_element_type=F32) + bmn_ref[h]
        outs.append(_joint_softmax_pv(s_c, s_n, vc, vn_ref[0, :, hs]).astype(BF16))
    o_ref[0] = jnp.concatenate(outs, axis=1)


def _band_sample(qkv, kc, vc, bmc, bmn, n_heads):
    b, t, _ = qkv.shape
    lc = kc.shape[1]
    cw = n_heads * HEAD_DIM
    return pl.pallas_call(
        functools.partial(_band_s_kernel, n_heads=n_heads),
        grid=(b,),
        in_specs=[
            pl.BlockSpec((1, t, cw), lambda bi: (bi, 0, 0)),
            pl.BlockSpec((1, t, cw), lambda bi: (bi, 0, 1)),
            pl.BlockSpec((1, t, cw), lambda bi: (bi, 0, 2)),
            pl.BlockSpec((1, lc, n_heads, HEAD_DIM), lambda bi: (bi, 0, 0, 0)),
            pl.BlockSpec((1, lc, n_heads, HEAD_DIM), lambda bi: (bi, 0, 0, 0)),
            pl.BlockSpec(bmc.shape, lambda bi: (0, 0, 0)),
            pl.BlockSpec(bmn.shape, lambda bi: (0, 0, 0)),
        ],
        out_specs=pl.BlockSpec((1, t, cw), lambda bi: (bi, 0, 0)),
        out_shape=jax.ShapeDtypeStruct((b, t, cw), BF16),
        compiler_params=_params("parallel"),
        name="band_attn_sample",
    )(qkv, qkv, qkv, kc, vc, bmc, bmn)


def _t5_bucket_np(rel):
    nb = N_T5_BUCKETS // 2
    max_exact = nb // 2
    n = np.abs(rel)
    nf = np.maximum(n, 1).astype(np.float64)
    large = max_exact + (np.log(nf / max_exact) / math.log(T5_MAX_DIST / max_exact) * (nb - max_exact)).astype(np.int64)
    large = np.minimum(large, nb - 1)
    return np.where(rel > 0, nb, 0) + np.where(n < max_exact, n, large)


def _toeplitz(table, index_of_rel, q_pos, k_pos):
    r, c = len(q_pos), len(k_pos)
    assert (np.diff(q_pos) == 1).all() and (np.diff(k_pos) == 1).all()
    rels = (k_pos[0] - q_pos[0]) - (r - 1) + np.arange(r + c - 1)
    v = jnp.pad(table.astype(F32)[index_of_rel(rels)].T, ((0, 0), (0, 1)))
    x = jnp.tile(v, (1, r))[:, :r * (r + c - 1)].reshape(v.shape[0], r, r + c - 1)
    return x[:, :, r - 1:r - 1 + c]


def _t5_bias_tiles(t5_table, q_pos, k_pos, far_bucket):
    mask = (k_pos[None, :] // CHUNK) <= (q_pos[:, None] // CHUNK)
    bias = _toeplitz(t5_table, _t5_bucket_np, q_pos, k_pos)
    bias = (bias - t5_table.astype(F32)[far_bucket][:, None, None]) * LOG2E
    return jnp.where(mask[None], bias, NEG_INF)


def _band_bias_tiles(rel_table, q_pos, k_pos):
    qc = q_pos[:, None] // CHUNK
    kc = k_pos[None, :] // CHUNK
    mask = (kc <= qc) & (qc - kc <= C_PREV_CHUNKS) & (k_pos[None, :] >= 0)
    bias = _toeplitz(rel_table, lambda rel: np.clip(rel, -REL_CLIP, REL_CLIP) + REL_CLIP, q_pos, k_pos) * LOG2E
    return jnp.where(mask[None], bias, NEG_INF)


def kernel(x_prompt, x_sample, cache_a_k, cache_a_v, cache_b_k, cache_b_v, cache_b_logf, cache_c_k, cache_c_v, c_prompt, c_sample, w_ada, b_ada, norm_g, w_ffn_in, w_ffn_out, w_in_ab, b_forget, w_out_ab, lambda_q1, lambda_k1, lambda_q2, lambda_k2, subln_g, t5_table, w_in_c, w_out_c, c_rel_bias, final_g):
    depth = w_ada.shape[0]
    bsz, seq, d = x_prompt.shape
    dbsz, dseq, _ = x_sample.shape
    past = cache_b_logf.shape[2]
    h_a = cache_a_k.shape[3]
    h_b = cache_b_k.shape[3]
    h_c = cache_c_k.shape[3]
    wa, wb, cw = h_a * HEAD_DIM, h_b * HEAD_DIM, h_c * HEAD_DIM
    assert bsz == dbsz and wa == wb and 3 * wa + 3 * wb == 3 * cw
    assert LANES >= T5_MAX_DIST and SAMPLE_NEAR >= T5_MAX_DIST and past >= SAMPLE_NEAR
    far_bucket = (N_T5_BUCKETS // 2) - 1

    mods = _ada_mods(jnp.concatenate([c_prompt, c_sample], axis=0), w_ada, b_ada)
    mods = mods.reshape(depth, 2, bsz, 9, d).transpose(0, 3, 1, 2, 4).reshape(depth * 9 * 2, bsz, 1, d)

    def slot(l, k, grp):
        return (l * 9 + k) * 2 + grp

    xs = [x_prompt, x_sample]
    ab_states = [[], []]
    c_states = [[], []]
    for l in range(depth):
        i = l // 2
        last = l == depth - 1
        w1 = _prep_ffn_weights(w_ffn_in[l, 0], w_ffn_out[l, 0])
        w2 = _prep_ffn_weights(w_ffn_in[l, 1], w_ffn_out[l, 1])
        for grp in range(2):
            xs[grp] = _ffn(xs[grp], mods, [slot(l, k, grp) for k in range(3)], norm_g[l, 0], *w1)

        if l % 2 == 0:
            lam_init = 0.8 - 0.6 * math.exp(-0.3 * l)
            lam = (jnp.exp(jnp.sum(lambda_q1[i].astype(F32) * lambda_k1[i].astype(F32)))
                   - jnp.exp(jnp.sum(lambda_q2[i].astype(F32) * lambda_k2[i].astype(F32))) + lam_init).reshape(1)
            n_main = 3 * wa + 3 * wb
            w_main = w_in_ab[i][:, :n_main].astype(BF16)
            wf = jnp.pad(w_in_ab[i][:, n_main:].T.astype(BF16), ((0, 16 - h_b), (0, 0)))
            bfo = b_forget[i].astype(F32).reshape(h_b, 1)
            w_out = w_out_ab[i].astype(BF16)
            g_sub = subln_g[i].astype(F32).reshape(1, HEAD_DIM)
            tn = min(PROJ_COL_TILE, wa)
            taps = _seg_taps(tn, wa, (1, 2, 4, 5), False)
            cs = np.ones((n_main,), np.float32)
            cs[:wa] = DA_HALF ** -0.5 * LOG2E
            cs[3 * wa:3 * wa + wb] = HEAD_DIM ** -0.5 * LOG2E
            for grp in range(2):
                x = xs[grp]
                b, t, _ = x.shape
                tap_shapes = [((b, t, wa), True)] * 4
                if grp == 0:
                    qkv, ka, va, kb, vb, vt, logf_t = _proj(
                        x, mods, [slot(l, 3, grp), slot(l, 4, grp)], norm_g[l, 1], w_main, cs, tn, tap_shapes, taps,
                        wf, bfo, vt_cols=_vt_cols(tn, wa, (2, 5)), vt_tile=ROW_TILE)
                else:
                    qkv, ka, va, kb, vb, logf_t = _proj(
                        x, mods, [slot(l, 3, grp), slot(l, 4, grp)], norm_g[l, 1], w_main, cs, tn, tap_shapes, taps,
                        wf, bfo)
                logf = logf_t.reshape(h_b, b, t).transpose(1, 2, 0)
                ab_states[grp].append((ka.reshape(b, t, h_a, HEAD_DIM), va.reshape(b, t, h_a, HEAD_DIM),
                                       kb.reshape(b, t, h_b, HEAD_DIM), vb.reshape(b, t, h_b, HEAD_DIM), logf))
                logf_bh = logf_t.reshape(h_b, b, t).transpose(1, 0, 2)
                if grp == 0:
                    o_a = _diff_prompt(lam, t5_table.astype(F32).reshape(-1), qkv, vt, g_sub, h_a, 1.0 - lam_init)
                    fcum = _cumsum(logf_bh, jnp.zeros((b, h_b, LANES), F32))
                    nq = t // ROW_TILE
                    fref = fcum[:, :, ::ROW_TILE].reshape(-1)
                    o_b = _fox_prompt(fref, qkv, vt, fcum.reshape(b, h_b, nq, 1, ROW_TILE), h_b, 3 * h_a, h_a)
                else:
                    q_pos = past + np.arange(t)
                    bnear = _t5_bias_tiles(t5_table, q_pos, past - SAMPLE_NEAR + np.arange(SAMPLE_NEAR), far_bucket)
                    bnew = _t5_bias_tiles(t5_table, q_pos, q_pos, far_bucket)
                    kc = cache_a_k[i].transpose(0, 2, 1, 3)
                    vc = cache_a_v[i].transpose(0, 2, 1, 3)
                    o_a = _diff_sample(lam, qkv, kc, vc, bnear, bnew, g_sub, h_a, 1.0 - lam_init)
                    fc = _cumsum(cache_b_logf[i].astype(F32).transpose(0, 2, 1), jnp.zeros((b, h_b, LANES), F32))
                    carry = fc[:, :, past - 1:past]
                    fn = _cumsum(jnp.pad(logf_bh, ((0, 0), (0, 0), (0, LANES - t))),
                                 jnp.broadcast_to(carry, (b, h_b, LANES)))
                    o_b = _fox_sample(carry.reshape(-1), qkv, cache_b_k[i].transpose(0, 2, 1, 3),
                                      cache_b_v[i].transpose(0, 2, 1, 3), fc.reshape(b, h_b, 1, past),
                                      fn.reshape(b, h_b, 1, LANES), h_b, 3 * h_a)
                xs[grp] = _outproj([o_a, o_b], w_out, x, mods, slot(l, 5, grp))
        else:
            w_in = w_in_c[i].astype(BF16)
            w_out = w_out_c[i].astype(BF16)
            tn = min(PROJ_COL_TILE, cw)
            taps = _seg_taps(tn, cw, (1, 2), True)
            cs = np.ones((3 * cw,), np.float32)
            cs[:cw] = HEAD_DIM ** -0.5 * LOG2E
            for grp in range(2):
                x = xs[grp]
                b, t, _ = x.shape
                keep = min(C_BAND, t)
                assert keep == min(t, ROW_TILE)
                tap_shapes = [((b, keep, cw), False)] * 2
                if grp == 0:
                    qkv, k_new, v_new, vt = _proj(
                        x, mods, [slot(l, 3, grp), slot(l, 4, grp)], norm_g[l, 1], w_in, cs, tn, tap_shapes, taps,
                        vt_cols=_vt_cols(tn, cw, (2,)), vt_tile=BAND_TILE)
                else:
                    qkv, k_new, v_new = _proj(
                        x, mods, [slot(l, 3, grp), slot(l, 4, grp)], norm_g[l, 1], w_in, cs, tn, tap_shapes, taps)
                c_states[grp].append((k_new.reshape(b, keep, h_c, HEAD_DIM), v_new.reshape(b, keep, h_c, HEAD_DIM)))
                if grp == 0:
                    o_c = _band_prompt(c_rel_bias[i].astype(F32).reshape(-1), qkv, vt, h_c)
                else:
                    lc = cache_c_k.shape[2]
                    q_pos = past + np.arange(t)
                    bmc = _band_bias_tiles(c_rel_bias[i], q_pos, past - lc + np.arange(lc))
                    bmn = _band_bias_tiles(c_rel_bias[i], q_pos, q_pos)
                    o_c = _band_sample(qkv, cache_c_k[i], cache_c_v[i], bmc, bmn, h_c)
                xs[grp] = _outproj([o_c], w_out, x, mods, slot(l, 5, grp))

        for grp in range(2):
            xs[grp] = _ffn(xs[grp], mods, [slot(l, k, grp) for k in (6, 7, 8)], norm_g[l, 2], *w2,
                           final_g=final_g if last else None)

    outs = [xs[0], xs[1]]
    for grp in range(2):
        st = ab_states[grp]
        outs += [jnp.stack([s[k] for s in st]) for k in range(5)]
        st = c_states[grp]
        outs += [jnp.stack([s[k] for s in st]) for k in range(2)]
    return tuple(outs)
```

```python
import functools
import math

import numpy as np
import jax
import jax.numpy as jnp
from jax import lax
from jax.experimental import pallas as pl
from jax.experimental.pallas import tpu as pltpu

F32 = jnp.float32
BF16 = jnp.bfloat16

CHUNK = 64
HEAD_DIM = 128
DA_HALF = HEAD_DIM // 2
N_T5_BUCKETS = 32
T5_MAX_DIST = 128
C_PREV_CHUNKS = 8
C_BAND = C_PREV_CHUNKS * CHUNK
REL_CLIP = 128
EPS = 1e-6
NEG_INF = -1e30
LOG2E = math.log2(math.e)
VT_ROWS = HEAD_DIM + 16

VMEM_LIMIT_BYTES = 56 * 1024 * 1024
VMEM_TALL_LIMIT_BYTES = 60 * 1024 * 1024
LANES = 128

ROW_TILE = 512
FF_TILE = 512
FFN_OUT_TILE = 512
FFN_ROW_SPLIT = 2
FFN_TALL_ROW_TILE = 1024
PROJ_ROW_SPLIT = 2
PROJ_COL_TILE = 1024
ADA_COL_TILE = 1024
BAND_TILE = 256
BAND_HEADS = 8
BAND_LOOKAHEAD = 3
DIFF_HEADS = 4
FOX_HEADS = 4
ATTN_GROUP = 512
SAMPLE_NEAR = 256

_NT = (((1,), (1,)), ((), ()))


def _params(*sem, flags=None):
    return pltpu.CompilerParams(dimension_semantics=sem, vmem_limit_bytes=VMEM_LIMIT_BYTES, flags=flags)


def _modulated_norm(x, g, shift, scale):
    y = x * lax.rsqrt(jnp.mean(x * x, axis=-1, keepdims=True) + EPS) * g
    return y * (1.0 + scale) + shift


def _normed_rows(x_ref, shift_ref, scale_ref, g_ref, h_scr, n_split, r):
    nb, tt, d = x_ref.shape
    rows = nb * tt // n_split
    if nb == 1:
        x, sh, sc = x_ref[:, r * rows:(r + 1) * rows, :], shift_ref[...], scale_ref[...]
    else:
        seqs = slice(r * (nb // n_split), (r + 1) * (nb // n_split))
        x, sh, sc = x_ref[seqs], shift_ref[seqs], scale_ref[seqs]
    hr = _modulated_norm(x, g_ref[...], sh, sc).reshape(rows, d).astype(BF16)
    h_scr[r * rows:(r + 1) * rows, :] = hr
    return hr


def _ada_kernel(c_ref, w_ref, b_ref, o_ref):
    c = c_ref[...]
    a = (c * jax.nn.sigmoid(c)).astype(BF16)
    o_ref[0] = jnp.dot(a, w_ref[0].astype(BF16), preferred_element_type=F32) + b_ref[0]


def _ada_mods(c_all, w_ada, b_ada):
    depth, d, n = w_ada.shape
    r = c_all.shape[0]
    tn = math.gcd(n, ADA_COL_TILE)
    return pl.pallas_call(
        _ada_kernel,
        grid=(depth, n // tn),
        in_specs=[
            pl.BlockSpec((r, d), lambda l, j: (0, 0)),
            pl.BlockSpec((1, d, tn), lambda l, j: (l, 0, j)),
            pl.BlockSpec((1, 1, tn), lambda l, j: (l, 0, j)),
        ],
        out_specs=pl.BlockSpec((1, r, tn), lambda l, j: (l, 0, j)),
        out_shape=jax.ShapeDtypeStruct((depth, r, n), F32),
        compiler_params=_params("arbitrary", "arbitrary"),
        name="ada_mods",
    )(c_all, w_ada, b_ada.reshape(depth, 1, n))


def _row_blocking(x, row_tile=ROW_TILE):
    b, t, _ = x.shape
    if t >= row_tile:
        assert t % row_tile == 0
        return 1, row_tile, t // row_tile, b * (t // row_tile)
    assert (b * t) % 8 == 0
    return b, t, 1, 1


def _x_index(nb, tps):
    if nb == 1:
        return lambda i, j: (i // tps, i % tps, 0)
    return lambda i, j: (0, 0, 0)


def _mod_spec(nb, tps, d, slot):
    if nb == 1:
        return pl.BlockSpec((None, 1, 1, d), lambda i, j: (slot, i // tps, 0, 0))
    return pl.BlockSpec((None, nb, 1, d), lambda i, j: (slot, 0, 0, 0))


def _ffn_kernel(*refs, nb, tt, n_ff, last_cols, n_out, final):
    if final:
        (x_ref, shift_ref, scale_ref, gate_ref, g_ref, wg_ref, wu_ref, wo_ref, fg_ref,
         o_ref, h_scr, a_scr) = refs
    else:
        (x_ref, shift_ref, scale_ref, gate_ref, g_ref, wg_ref, wu_ref, wo_ref,
         o_ref, h_scr, a_scr) = refs
    j = pl.program_id(1)
    d = x_ref.shape[-1]
    tm = nb * tt
    tf = wg_ref.shape[1]
    tn = wo_ref.shape[1]

    rows = tm // FFN_ROW_SPLIT
    normed_rows = functools.partial(_normed_rows, x_ref, shift_ref, scale_ref, g_ref, h_scr, FFN_ROW_SPLIT)

    def hidden_chunk(rows_of, cols=tf):
        def products(r):
            hr = rows_of(r)
            return (jnp.dot(hr, wg_ref[:, :cols], preferred_element_type=F32),
                    jnp.dot(hr, wu_ref[:, :cols], preferred_element_type=F32))

        parts = []
        cur = products(0)
        for r in range(FFN_ROW_SPLIT):
            nxt = products(r + 1) if r + 1 < FFN_ROW_SPLIT else None
            gg, uu = cur
            parts.append((gg * jax.nn.sigmoid(gg) * uu).astype(BF16))
            cur = nxt
        a_scr[j, :, :cols] = jnp.concatenate(parts, axis=0)

    kept_rows = lambda r: h_scr[r * rows:(r + 1) * rows, :]
    n_full = n_ff if last_cols == tf else n_ff - 1
    assert n_full >= 1

    @pl.when(j == 0)
    def _():
        hidden_chunk(normed_rows)

    @pl.when(jnp.logical_and(j > 0, j < n_full))
    def _():
        hidden_chunk(kept_rows)

    if n_full < n_ff:
        @pl.when(j == n_ff - 1)
        def _():
            hidden_chunk(kept_rows, last_cols)

    @pl.when(j >= n_ff)
    def _():
        a = jnp.concatenate([a_scr[f] for f in range(n_full)] + [a_scr[f, :, :last_cols] for f in range(n_full, n_ff)],
                            axis=1)
        acc = jnp.dot(a, wo_ref[...], preferred_element_type=F32).reshape(nb, tt, tn)
        for n in range(n_out):
            @pl.when(j == n_ff + n)
            def _(n=n):
                cs = slice(n * tn, (n + 1) * tn)
                y = x_ref[:, :, cs] + 0.5 * gate_ref[:, :, cs] * acc
                if final:
                    o_ref[:, :, cs] = y
                else:
                    o_ref[...] = y

    if final:
        @pl.when(j == n_ff + n_out - 1)
        def _():
            y = o_ref[...]
            o_ref[...] = y * lax.rsqrt(jnp.mean(y * y, axis=-1, keepdims=True) + EPS) * fg_ref[...]


def _ffn(x, mods, slots, g, wg_p, wu_p, w_out_p, final_g=None):
    b, t, d = x.shape
    final = final_g is not None
    tall = not final and t >= FFN_TALL_ROW_TILE
    nb, tt, tps, nrows = _row_blocking(x, FFN_TALL_ROW_TILE if tall else ROW_TILE)
    ff = w_out_p.shape[0]
    n_ff = -(-ff // FF_TILE)
    last_cols = ff - (n_ff - 1) * FF_TILE
    tn = math.gcd(d, FFN_OUT_TILE)
    n_out = d // tn
    xi = _x_index(nb, tps)
    in_specs = [
        pl.BlockSpec((nb, tt, d), xi),
        _mod_spec(nb, tps, d, slots[0]),
        _mod_spec(nb, tps, d, slots[1]),
        _mod_spec(nb, tps, d, slots[2]),
        pl.BlockSpec((1, d), lambda i, j: (0, 0)),
        pl.BlockSpec((d, FF_TILE), lambda i, j: (0, jnp.minimum(j, n_ff - 1))),
        pl.BlockSpec((d, FF_TILE), lambda i, j: (0, jnp.minimum(j, n_ff - 1))),
        pl.BlockSpec((ff, tn), lambda i, j: (0, jnp.maximum(j - n_ff, 0))),
    ]
    args = [x, mods, mods, mods, g.reshape(1, d), wg_p, wu_p, w_out_p]
    if final:
        in_specs.append(pl.BlockSpec((1, d), lambda i, j: (0, 0)))
        args.append(final_g.reshape(1, d))
    return pl.pallas_call(
        functools.partial(_ffn_kernel, nb=nb, tt=tt, n_ff=n_ff, last_cols=last_cols, n_out=n_out, final=final),
        grid=(nrows, n_ff + n_out),
        in_specs=in_specs,
        out_specs=(pl.BlockSpec((nb, tt, d), xi) if final else
                   pl.BlockSpec((nb, tt, tn), lambda i, j: xi(i, j)[:2] + (jnp.maximum(j - n_ff, 0),))),
        out_shape=jax.ShapeDtypeStruct((b, t, d), F32),
        scratch_shapes=[pltpu.VMEM((nb * tt, d), BF16), pltpu.VMEM((n_ff, nb * tt, FF_TILE), BF16)],
        compiler_params=pltpu.CompilerParams(
            dimension_semantics=("parallel", "arbitrary"),
            vmem_limit_bytes=VMEM_TALL_LIMIT_BYTES if tall else VMEM_LIMIT_BYTES),
        name="ffn",
    )(*args)


def _rmsnorm_kernel(x_ref, g_ref, o_ref):
    x = x_ref[...]
    o_ref[...] = x * lax.rsqrt(jnp.mean(x * x, axis=-1, keepdims=True) + EPS) * g_ref[...]


def _final_norm(x, g):
    b, t, d = x.shape
    nb, tt, tps, nrows = _row_blocking(x)
    xi = _x_index(nb, tps)
    return pl.pallas_call(
        _rmsnorm_kernel,
        grid=(nrows, 1),
        in_specs=[pl.BlockSpec((nb, tt, d), xi), pl.BlockSpec((1, d), lambda i, j: (0, 0))],
        out_specs=pl.BlockSpec((nb, tt, d), xi),
        out_shape=jax.ShapeDtypeStruct((b, t, d), F32),
        compiler_params=_params("parallel", "arbitrary"),
        name="final_norm",
    )(x, g.reshape(1, d))


def _prep_ffn_weights(w_in, w_out):
    ff = w_out.shape[0]
    return w_in[:, :ff].astype(BF16), w_in[:, ff:].astype(BF16), w_out.astype(BF16)


def _log_sigmoid(x):
    return jnp.minimum(x, 0.0) - jnp.log1p(jnp.exp(-jnp.abs(x)))


def _proj_kernel(*refs, nb, tt, tn, tps, taps, has_f, vt_cols, vt_tile):
    x_ref, shift_ref, scale_ref, g_ref, w_ref, cs_ref = refs[:6]
    pos = 6
    if has_f:
        wf_ref, bf_ref = refs[6:8]
        pos = 8
    obf_ref = refs[pos]
    tap_refs = refs[pos + 1:pos + 1 + len({tp[0] for tp in taps})]
    pos = pos + 1 + len(tap_refs)
    if vt_cols:
        vt_ref = refs[pos]
        pos += 1
    if has_f:
        logf_ref = refs[pos]
        pos += 1
    h_scr = refs[pos]
    i = pl.program_id(0)
    j = pl.program_id(1)
    d = x_ref.shape[-1]

    def store_attention_copy(y):
        obf_ref[...] = (y * cs_ref[...]).astype(BF16).reshape(nb, tt, tn)

    assert all(tp[1] > 0 for tp in taps) and all(vc[0] > 0 for vc in vt_cols)

    @pl.when(j == 0)
    def _():
        hs = []
        ys = []
        for r in range(PROJ_ROW_SPLIT):
            hs.append(_normed_rows(x_ref, shift_ref, scale_ref, g_ref, h_scr, PROJ_ROW_SPLIT, r))
            ys.append(jnp.dot(hs[-1], w_ref[...], preferred_element_type=F32))
        store_attention_copy(jnp.concatenate(ys, axis=0))
        if has_f:
            fr = lax.dot_general(wf_ref[...], jnp.concatenate(hs, axis=0), _NT, preferred_element_type=F32)
            logf_ref[...] = _log_sigmoid(fr[:logf_ref.shape[0]] + bf_ref[...])

    @pl.when(j > 0)
    def _():
        y = jnp.dot(h_scr[...], w_ref[...], preferred_element_type=F32)
        store_attention_copy(y)
        _proj_taps(y, i, j, tap_refs, vt_ref if vt_cols else None, nb=nb, tt=tt, tn=tn, tps=tps, taps=taps,
                   vt_cols=vt_cols, vt_tile=vt_tile)


def _proj_taps(y, i, j, tap_refs, vt_ref, *, nb, tt, tn, tps, taps, vt_cols, vt_tile):
    for out_idx, jval, col0, last_rows_only in taps:
        cond = j == jval
        if last_rows_only and tps > 1:
            cond = jnp.logical_and(cond, i % tps == tps - 1)

        @pl.when(cond)
        def _(out_idx=out_idx, col0=col0):
            tap_refs[out_idx][:, :, col0:col0 + tn] = y.reshape(nb, tt, tn)

    for jval, head0 in vt_cols:
        @pl.when(j == jval)
        def _(head0=head0):
            for hh in range(tn // HEAD_DIM):
                yt = y[:, hh * HEAD_DIM:(hh + 1) * HEAD_DIM].T.astype(BF16)
                yt = jnp.concatenate([yt, jnp.ones((VT_ROWS - HEAD_DIM, yt.shape[1]), BF16)], axis=0)
                for s in range(tt // vt_tile):
                    vt_ref[0, head0 + hh, s] = yt[:, s * vt_tile:(s + 1) * vt_tile]


def _seg_taps(tn, width, segments, last_rows_only):
    per = width // tn
    assert per * tn == width
    return [(o, sidx * per + s, s * tn, last_rows_only) for o, sidx in enumerate(segments) for s in range(per)]


def _vt_cols(tn, width, segments):
    per = width // tn
    hpb = tn // HEAD_DIM
    return [(sidx * per + s, (o * per + s) * hpb) for o, sidx in enumerate(segments) for s in range(per)]


def _proj(x, mods, slots, g, w_bf, col_scale, tn, tap_shapes, taps, wf=None, bf=None, vt_cols=(), vt_tile=None):
    b, t, d = x.shape
    n = w_bf.shape[1]
    nb, tt, tps, nrows = _row_blocking(x)
    has_f = wf is not None
    assert not vt_cols or nb == 1
    in_specs = [
        pl.BlockSpec((nb, tt, d), _x_index(nb, tps)),
        _mod_spec(nb, tps, d, slots[0]),
        _mod_spec(nb, tps, d, slots[1]),
        pl.BlockSpec((1, d), lambda i, j: (0, 0)),
        pl.BlockSpec((d, tn), lambda i, j: (0, j)),
        pl.BlockSpec((1, tn), lambda i, j: (0, j)),
    ]
    args = [x, mods, mods, g.reshape(1, d), w_bf, jnp.asarray(col_scale, F32).reshape(1, n)]
    if has_f:
        in_specs += [pl.BlockSpec(wf.shape, lambda i, j: (0, 0)), pl.BlockSpec(bf.shape, lambda i, j: (0, 0))]
        args += [wf, bf]
    xi = _x_index(nb, tps)
    out_specs = [pl.BlockSpec((nb, tt, tn), lambda i, j: xi(i, j)[:2] + (j,))]
    out_shape = [jax.ShapeDtypeStruct((b, t, n), BF16)]
    for shp, follows_rows in tap_shapes:
        if follows_rows:
            out_specs.append(pl.BlockSpec((nb, tt, shp[2]), xi))
        elif nb == 1:
            out_specs.append(pl.BlockSpec((1, shp[1], shp[2]), lambda i, j: (i // tps, 0, 0)))
        else:
            out_specs.append(pl.BlockSpec(shp, lambda i, j: (0, 0, 0)))
        out_shape.append(jax.ShapeDtypeStruct(shp, F32))
    if vt_cols:
        n_vh = len(vt_cols) * (tn // HEAD_DIM)
        out_specs.append(pl.BlockSpec((1, n_vh, tt // vt_tile, VT_ROWS, vt_tile), lambda i, j: (i // tps, 0, i % tps, 0, 0)))
        out_shape.append(jax.ShapeDtypeStruct((b, n_vh, t // vt_tile, VT_ROWS, vt_tile), BF16))
    if has_f:
        nf = bf.shape[0]
        out_specs.append(pl.BlockSpec((nf, nb * tt), lambda i, j: (0, i)))
        out_shape.append(jax.ShapeDtypeStruct((nf, b * t), F32))
    return pl.pallas_call(
        functools.partial(_proj_kernel, nb=nb, tt=tt, tn=tn, tps=tps, taps=tuple(taps), has_f=has_f,
                          vt_cols=tuple(vt_cols), vt_tile=vt_tile),
        grid=(nrows, n // tn),
        in_specs=in_specs,
        out_specs=out_specs,
        out_shape=out_shape,
        scratch_shapes=[pltpu.VMEM((nb * tt, d), BF16)],
        compiler_params=_params("arbitrary", "arbitrary"),
        name="mixer_in_proj",
    )(*args)


def _outproj_kernel(*refs, nb, tt, n_in):
    o_refs = refs[:n_in]
    w_ref, x_ref, gate_ref, out_ref = refs[n_in:]
    d = x_ref.shape[-1]
    acc = None
    row0 = 0
    for o_ref in o_refs:
        wd = o_ref.shape[-1]
        part = jnp.dot(o_ref[...].reshape(nb * tt, wd), w_ref[row0:row0 + wd, :], preferred_element_type=F32)
        acc = part if acc is None else acc + part
        row0 += wd
    out_ref[...] = x_ref[...] + gate_ref[...] * acc.reshape(nb, tt, d)


def _outproj(o_list, w_bf, x, mods, gate_slot):
    b, t, d = x.shape
    nb, tt, tps, nrows = _row_blocking(x)
    xi = _x_index(nb, tps)
    in_specs = [pl.BlockSpec((nb, tt, o.shape[-1]), xi) for o in o_list]
    in_specs += [
        pl.BlockSpec(w_bf.shape, lambda i, j: (0, 0)),
        pl.BlockSpec((nb, tt, d), xi),
        _mod_spec(nb, tps, d, gate_slot),
    ]
    return pl.pallas_call(
        functools.partial(_outproj_kernel, nb=nb, tt=tt, n_in=len(o_list)),
        grid=(nrows, 1),
        in_specs=in_specs,
        out_specs=pl.BlockSpec((nb, tt, d), xi),
        out_shape=jax.ShapeDtypeStruct((b, t, d), F32),
        compiler_params=_params("parallel", "arbitrary"),
        name="mixer_out_proj",
    )(*o_list, w_bf, x, mods)


def _cumsum_kernel(x_ref, c_ref, o_ref):
    x = x_ref[0]
    n = x.shape[1]
    lane = lax.broadcasted_iota(jnp.int32, x.shape, 1)
    s = 1
    while s < n:
        x = x + jnp.where(lane >= s, pltpu.roll(x, s, axis=1), 0.0)
        s *= 2
    o_ref[0] = x + c_ref[0][:, :1]


def _cumsum(x, carry):
    g, h, n = x.shape
    return pl.pallas_call(
        _cumsum_kernel,
        grid=(g,),
        in_specs=[pl.BlockSpec((1, h, n), lambda i: (i, 0, 0)), pl.BlockSpec((1, h, LANES), lambda i: (i, 0, 0))],
        out_specs=pl.BlockSpec((1, h, n), lambda i: (i, 0, 0)),
        out_shape=jax.ShapeDtypeStruct((g, h, n), F32),
        compiler_params=_params("arbitrary"),
        name="logf_cumsum",
    )(x, carry)


def _softmax_init(m_scr, acc_scr):
    m_scr[...] = jnp.full(m_scr.shape, NEG_INF, F32)
    acc_scr[...] = jnp.zeros(acc_scr.shape, F32)


def _softmax_update_t(s, s_max, vt, m_prev, acc_prev):
    m_new = jnp.maximum(m_prev, s_max)
    alpha = jnp.exp2(m_prev - m_new)
    p = jnp.exp2(s - m_new)
    acc_new = alpha * acc_prev + jnp.dot(vt, p.astype(BF16), preferred_element_type=F32)
    return m_new, acc_new


def _normalised_t(acc):
    return (acc[:HEAD_DIM] / acc[HEAD_DIM:HEAD_DIM + 1]).T


def _tile_stage(slot, vt_of, m_scr, acc_scr, s_scr, smax_scr, n_groups, next_scores=None):
    g = m_scr.shape[1] // n_groups
    sls = [slice(i * g, (i + 1) * g) for i in range(n_groups)]
    state = [(m_scr[:, sl], acc_scr[:, sl]) for sl in sls]
    new = []
    for i, sl in enumerate(sls):
        if next_scores is not None:
            s = next_scores(i)
            s_scr[1 - slot, :, sl] = s
            smax_scr[1 - slot, :, sl] = jnp.max(s, axis=0, keepdims=True)
        new.append(_softmax_update_t(s_scr[slot, :, sl], smax_scr[slot, :, sl], vt_of(i), *state[i]))
    for sl, (m_new, acc_new) in zip(sls, new):
        m_scr[:, sl] = m_new
        acc_scr[:, sl] = acc_new


def _first_tile(scores, s_scr, smax_scr, n_groups):
    g = s_scr.shape[2] // n_groups
    for i in range(n_groups):
        sl = slice(i * g, (i + 1) * g)
        s = scores(i)
        s_scr[0, :, sl] = s
        smax_scr[0, :, sl] = jnp.max(s, axis=0, keepdims=True)


def _pipelined_tiles(last, n_special, produce_first, stage):
    def kind(dist):
        return dist if dist < n_special else "far"

    for k in range(n_special):
        @pl.when(last == k)
        def _(k=k):
            produce_first(kind(k))

    @pl.when(last >= n_special)
    def _():
        produce_first("far")

    n_pairs = jnp.maximum(last - n_special, 0) // 2

    def pair(i, carry):
        stage(2 * i, 0, "far")
        stage(2 * i + 1, 1, "far")
        return carry

    lax.fori_loop(0, n_pairs, pair, 0)
    t0 = 2 * n_pairs
    remaining = last - t0 + 1
    for r in range(1, n_special + 3):
        @pl.when(remaining == r)
        def _(r=r):
            for u in range(r):
                stage(t0 + u, u % 2, kind(r - 2 - u) if u + 1 < r else None)


def _toeplitz_values(tbl_ref, idx, n_vals, nh, h, ref_row):
    ref_val = tbl_ref[ref_row * nh + h]

    def body(v, acc):
        return jnp.where(idx == v, tbl_ref[v * nh + h] - ref_val, acc)

    return lax.fori_loop(0, n_vals, body, jnp.zeros(idx.shape, F32))


def _stack_diff_queries(q):
    lane = lax.broadcasted_iota(jnp.int32, q.shape, 1)
    zero = jnp.zeros_like(q)
    qs = jnp.concatenate([jnp.where(lane < DA_HALF, q, zero), jnp.where(lane >= DA_HALF, q, zero)], axis=0)
    return qs


def _diff_finalize(o, lam, g, out_scale):
    tq = o.shape[0] // 2
    od = o[:tq] - lam * o[tq:]
    return od * lax.rsqrt(jnp.mean(od * od, axis=-1, keepdims=True) + EPS) * g * out_scale


def _diff_p_kernel(lam_ref, tbl_ref, q_ref, k_ref, vt_ref, idx_ref, base_ref, g_ref, o_ref,
                   m_scr, acc_scr, bias_scr, s_scr, smax_scr, *, tq, nh, hps, far_bucket, out_scale):
    hg = pl.program_id(0)
    bi = pl.program_id(1)
    qi = pl.program_id(2)
    nblk = tq // LANES

    @pl.when(jnp.logical_and(bi == 0, qi == 0))
    def _():
        for e in range(hps):
            t = _toeplitz_values(tbl_ref, idx_ref[...], N_T5_BUCKETS, nh, hg * hps + e, far_bucket) * LOG2E
            bias_scr[e] = base_ref[...]
            for blk in range(nblk):
                sl = slice(blk * LANES, (blk + 1) * LANES)
                bias_scr[e, 0, sl, sl] += t[0]
                if blk >= 1:
                    bias_scr[e, 0, (blk - 1) * LANES:blk * LANES, sl] += t[1]
            bias_scr[e, 1, (nblk - 1) * LANES:, :LANES] += t[1]

    qs = [_stack_diff_queries(q_ref[0, :, e * HEAD_DIM:(e + 1) * HEAD_DIM]) for e in range(hps)]
    _softmax_init(m_scr, acc_scr)
    gph = 2 * tq // ATTN_GROUP
    ngrp = hps * gph

    def scores(j, kind):
        start = pl.multiple_of(j * tq, tq)

        def group(i):
            e, gi = divmod(i, gph)
            k = k_ref[0, pl.ds(start, tq), e * HEAD_DIM:(e + 1) * HEAD_DIM]
            s = lax.dot_general(k, qs[e][gi * ATTN_GROUP:(gi + 1) * ATTN_GROUP], _NT, preferred_element_type=F32)
            if kind != "far":
                q0 = (gi * ATTN_GROUP) % tq
                s = s + bias_scr[e, kind, :, q0:q0 + ATTN_GROUP]
            return s
        return group

    def stage(t, slot, next_kind):
        nxt = None if next_kind is None else scores(t + 1, next_kind)
        _tile_stage(slot, lambda i: vt_ref[0, i // gph, t], m_scr, acc_scr, s_scr, smax_scr, ngrp, nxt)

    _pipelined_tiles(qi, 2, lambda kind: _first_tile(scores(0, kind), s_scr, smax_scr, ngrp), stage)
    outs = []
    for e in range(hps):
        o = _normalised_t(acc_scr[:, e * 2 * tq:(e + 1) * 2 * tq])
        outs.append(_diff_finalize(o, lam_ref[0], g_ref[...], out_scale).astype(BF16))
    o_ref[0] = jnp.concatenate(outs, axis=1)


def _diff_prompt(lam, tbl, qkv, vt, g, n_heads, out_scale):
    b, t, _ = qkv.shape
    tq = vt.shape[-1]
    far_bucket = N_T5_BUCKETS // 2 - 1
    pos = np.arange(LANES)
    idx = np.stack([_t5_bucket_np(pos[:, None] - pos[None, :]),
                    _t5_bucket_np(pos[:, None] - pos[None, :] - LANES)]).astype(np.int32)
    kq = np.arange(tq)
    base = np.zeros((2, tq, tq), np.float32)
    base[0] = np.where((kq[:, None] // CHUNK) <= (kq[None, :] // CHUNK), 0.0, NEG_INF)
    hps = math.gcd(n_heads, DIFF_HEADS)
    ngrp = n_heads // hps
    assert ngrp * hps == n_heads
    wd = hps * HEAD_DIM
    lanes = hps * 2 * tq
    return pl.pallas_call(
        functools.partial(_diff_p_kernel, tq=tq, nh=n_heads, hps=hps, far_bucket=far_bucket, out_scale=out_scale),
        grid=(ngrp, b, t // tq),
        in_specs=[
            pl.BlockSpec(memory_space=pltpu.SMEM),
            pl.BlockSpec(memory_space=pltpu.SMEM),
            pl.BlockSpec((1, tq, wd), lambda h, bi, qi: (bi, qi, h)),
            pl.BlockSpec((1, t, wd), lambda h, bi, qi: (bi, 0, ngrp + h)),
            pl.BlockSpec((1, hps, t // tq, VT_ROWS, tq), lambda h, bi, qi: (bi, h, 0, 0, 0)),
            pl.BlockSpec(idx.shape, lambda h, bi, qi: (0, 0, 0)),
            pl.BlockSpec(base.shape, lambda h, bi, qi: (0, 0, 0)),
            pl.BlockSpec((1, HEAD_DIM), lambda h, bi, qi: (0, 0)),
        ],
        out_specs=pl.BlockSpec((1, tq, wd), lambda h, bi, qi: (bi, qi, h)),
        out_shape=jax.ShapeDtypeStruct((b, t, n_heads * HEAD_DIM), BF16),
        scratch_shapes=[pltpu.VMEM((1, lanes), F32), pltpu.VMEM((VT_ROWS, lanes), F32),
                        pltpu.VMEM((hps, 2, tq, tq), F32), pltpu.VMEM((2, tq, lanes), F32),
                        pltpu.VMEM((2, 1, lanes), F32)],
        compiler_params=_params("arbitrary", "arbitrary", "arbitrary"),
        name="diff_attn_prompt",
    )(lam, tbl, qkv, qkv, vt, jnp.asarray(idx), jnp.asarray(base), g)


def _fox_p_kernel(fref_ref, q_ref, k_ref, vt_ref, f_ref, o_ref, m_scr, acc_scr, fcol_scr, s_scr, smax_scr,
                  *, tq, hps):
    bi = pl.program_id(0)
    hg = pl.program_id(1)
    qi = pl.program_id(2)
    nblk = tq // LANES

    @pl.when(qi == 0)
    def _():
        for e in range(hps):
            for c in range(f_ref.shape[2] * nblk):
                row = f_ref[0, e, c // nblk][:, (c % nblk) * LANES:(c % nblk + 1) * LANES]
                fcol_scr[e, c * LANES:(c + 1) * LANES, :] = jnp.broadcast_to(row, (LANES, LANES)).T

    n_heads = pl.num_programs(1) * hps
    frefs = [fref_ref[(bi * n_heads + hg * hps + e) * pl.num_programs(2) + qi] for e in range(hps)]
    _softmax_init(m_scr, acc_scr)

    gph = tq // ATTN_GROUP
    ngrp = hps * gph

    def scores(j, kind):
        start = pl.multiple_of(j * tq, tq)

        def group(i):
            e, gi = divmod(i, gph)
            hs = slice(e * HEAD_DIM, (e + 1) * HEAD_DIM)
            decay = (frefs[e] - fcol_scr[e, pl.ds(start, tq), :]) * LOG2E
            decay = jnp.concatenate([decay] * (ATTN_GROUP // LANES), axis=1)
            s = lax.dot_general(k_ref[0, pl.ds(start, tq), hs], q_ref[0, gi * ATTN_GROUP:(gi + 1) * ATTN_GROUP, hs],
                                _NT, preferred_element_type=F32)
            s = s + decay
            if kind != "far":
                key = lax.broadcasted_iota(jnp.int32, s.shape, 0)
                qry = lax.broadcasted_iota(jnp.int32, s.shape, 1) + gi * ATTN_GROUP
                s = jnp.where(key <= qry, s, NEG_INF)
            return s
        return group

    def stage(t, slot, next_kind):
        nxt = None if next_kind is None else scores(t + 1, next_kind)
        _tile_stage(slot, lambda i: vt_ref[0, i // gph, t], m_scr, acc_scr, s_scr, smax_scr, ngrp, nxt)

    _pipelined_tiles(qi, 1, lambda kind: _first_tile(scores(0, kind), s_scr, smax_scr, ngrp), stage)
    o_ref[0] = jnp.concatenate(
        [_normalised_t(acc_scr[:, e * tq:(e + 1) * tq]).astype(BF16) for e in range(hps)], axis=1)


def _fox_prompt(fref, qkv, vt, fcum, n_heads, col0, vt_head0):
    b, t, _ = qkv.shape
    tq = vt.shape[-1]
    nq = t // tq
    hps = math.gcd(math.gcd(n_heads, FOX_HEADS), math.gcd(col0, vt_head0))
    ngrp = n_heads // hps
    assert ngrp * hps == n_heads and col0 % hps == 0 and vt_head0 % hps == 0
    wd = hps * HEAD_DIM
    lanes = hps * tq
    return pl.pallas_call(
        functools.partial(_fox_p_kernel, tq=tq, hps=hps),
        grid=(b, ngrp, nq),
        in_specs=[
            pl.BlockSpec(memory_space=pltpu.SMEM),
            pl.BlockSpec((1, tq, wd), lambda bi, h, qi: (bi, qi, col0 // hps + h)),
            pl.BlockSpec((1, t, wd), lambda bi, h, qi: (bi, 0, (col0 + n_heads) // hps + h)),
            pl.BlockSpec((1, hps, nq, VT_ROWS, tq), lambda bi, h, qi: (bi, vt_head0 // hps + h, 0, 0, 0)),
            pl.BlockSpec((1, hps, nq, 1, tq), lambda bi, h, qi: (bi, h, 0, 0, 0)),
        ],
        out_specs=pl.BlockSpec((1, tq, wd), lambda bi, h, qi: (bi, qi, h)),
        out_shape=jax.ShapeDtypeStruct((b, t, n_heads * HEAD_DIM), BF16),
        scratch_shapes=[pltpu.VMEM((1, lanes), F32), pltpu.VMEM((VT_ROWS, lanes), F32),
                        pltpu.VMEM((hps, t, LANES), F32), pltpu.VMEM((2, tq, lanes), F32),
                        pltpu.VMEM((2, 1, lanes), F32)],
        compiler_params=_params("arbitrary", "arbitrary", "arbitrary"),
        name="fox_attn_prompt",
    )(fref, qkv, qkv, vt, fcum)


def _joint_softmax_pv(s_c, s_n, vc, vn):
    m = jnp.maximum(jnp.max(s_c, axis=1, keepdims=True), jnp.max(s_n, axis=1, keepdims=True))
    p_c = jnp.exp2(s_c - m)
    p_n = jnp.exp2(s_n - m)
    l = jnp.sum(p_c, axis=1, keepdims=True) + jnp.sum(p_n, axis=1, keepdims=True)
    acc = jnp.dot(p_c.astype(BF16), vc, preferred_element_type=F32)
    acc = acc + jnp.dot(p_n.astype(BF16), vn, preferred_element_type=F32)
    return acc / l


def _diff_s_kernel(lam_ref, q_ref, kn_ref, vn_ref, kc_ref, vc_ref, bnear_ref, bnew_ref, g_ref, o_ref, *, near, out_scale):
    qs = _stack_diff_queries(q_ref[0])
    kc = kc_ref[0, 0].astype(BF16)
    vc = vc_ref[0, 0].astype(BF16)
    p = kc.shape[0]
    s_c = lax.dot_general(qs, kc, _NT, preferred_element_type=F32)
    bnear = bnear_ref[0]
    s_c = jnp.concatenate([s_c[:, :p - near], s_c[:, p - near:] + jnp.concatenate([bnear, bnear], axis=0)], axis=1)
    bnew = bnew_ref[0]
    s_n = lax.dot_general(qs, kn_ref[0], _NT, preferred_element_type=F32) + jnp.concatenate([bnew, bnew], axis=0)
    o = _joint_softmax_pv(s_c, s_n, vc, vn_ref[0])
    o_ref[0] = _diff_finalize(o, lam_ref[0], g_ref[...], out_scale).astype(BF16)


def _diff_sample(lam, qkv, kc, vc, bnear, bnew, g, n_heads, out_scale):
    b, t, _ = qkv.shape
    p = kc.shape[2]
    near = bnear.shape[-1]
    hd = HEAD_DIM
    return pl.pallas_call(
        functools.partial(_diff_s_kernel, near=near, out_scale=out_scale),
        grid=(b, n_heads),
        in_specs=[
            pl.BlockSpec(memory_space=pltpu.SMEM),
            pl.BlockSpec((1, t, hd), lambda bi, h: (bi, 0, h)),
            pl.BlockSpec((1, t, hd), lambda bi, h: (bi, 0, n_heads + h)),
            pl.BlockSpec((1, t, hd), lambda bi, h: (bi, 0, 2 * n_heads + h)),
            pl.BlockSpec((1, 1, p, hd), lambda bi, h: (bi, h, 0, 0)),
            pl.BlockSpec((1, 1, p, hd), lambda bi, h: (bi, h, 0, 0)),
            pl.BlockSpec((1, t, near), lambda bi, h: (h, 0, 0)),
            pl.BlockSpec((1, t, t), lambda bi, h: (h, 0, 0)),
            pl.BlockSpec((1, hd), lambda bi, h: (0, 0)),
        ],
        out_specs=pl.BlockSpec((1, t, hd), lambda bi, h: (bi, 0, h)),
        out_shape=jax.ShapeDtypeStruct((b, t, n_heads * hd), BF16),
        compiler_params=_params("parallel", "parallel"),
        name="diff_attn_sample",
    )(lam, qkv, qkv, qkv, kc, vc, bnear, bnew, g)


def _fox_s_kernel(fref_ref, q_ref, kn_ref, vn_ref, kc_ref, vc_ref, fc_ref, fn_ref, o_ref):
    bi = pl.program_id(0)
    h = pl.program_id(1)
    q = q_ref[0]
    t = q.shape[0]
    fref = fref_ref[bi * pl.num_programs(1) + h]
    s_c = lax.dot_general(q, kc_ref[0, 0].astype(BF16), _NT, preferred_element_type=F32) + (fref - fc_ref[0, 0]) * LOG2E
    s_n = lax.dot_general(q, kn_ref[0], _NT, preferred_element_type=F32) + (fref - fn_ref[0, 0][:, :t]) * LOG2E
    row = lax.broadcasted_iota(jnp.int32, s_n.shape, 0)
    col = lax.broadcasted_iota(jnp.int32, s_n.shape, 1)
    s_n = jnp.where(col <= row, s_n, NEG_INF)
    o_ref[0] = _joint_softmax_pv(s_c, s_n, vc_ref[0, 0].astype(BF16), vn_ref[0]).astype(BF16)


def _fox_sample(fref, qkv, kc, vc, fc, fn, n_heads, col0):
    b, t, _ = qkv.shape
    p = kc.shape[2]
    hd = HEAD_DIM
    return pl.pallas_call(
        _fox_s_kernel,
        grid=(b, n_heads),
        in_specs=[
            pl.BlockSpec(memory_space=pltpu.SMEM),
            pl.BlockSpec((1, t, hd), lambda bi, h: (bi, 0, col0 + h)),
            pl.BlockSpec((1, t, hd), lambda bi, h: (bi, 0, col0 + n_heads + h)),
            pl.BlockSpec((1, t, hd), lambda bi, h: (bi, 0, col0 + 2 * n_heads + h)),
            pl.BlockSpec((1, 1, p, hd), lambda bi, h: (bi, h, 0, 0)),
            pl.BlockSpec((1, 1, p, hd), lambda bi, h: (bi, h, 0, 0)),
            pl.BlockSpec((1, 1, 1, p), lambda bi, h: (bi, h, 0, 0)),
            pl.BlockSpec((1, 1, 1, fn.shape[-1]), lambda bi, h: (bi, h, 0, 0)),
        ],
        out_specs=pl.BlockSpec((1, t, hd), lambda bi, h: (bi, 0, h)),
        out_shape=jax.ShapeDtypeStruct((b, t, n_heads * hd), BF16),
        compiler_params=_params("parallel", "parallel"),
        name="fox_attn_sample",
    )(fref, qkv, qkv, qkv, kc, vc, fc, fn)


def _band_p_kernel(tbl_ref, q_ref, k_ref, vt_ref, idx_ref, base_ref, o_ref, bm_scr, *, tq, nkb, nh, hps):
    hg = pl.program_id(0)
    bi = pl.program_id(1)
    i = pl.program_id(2)
    nblk = tq // LANES
    qoff = (nkb - 1) * nblk

    @pl.when(jnp.logical_and(bi == 0, i == 0))
    def _():
        for e in range(hps):
            t = _toeplitz_values(tbl_ref, idx_ref[...], 2 * REL_CLIP + 1, nh, hg * hps + e, 0)
            bm_scr[e] = base_ref[...]
            for bq in range(nblk):
                kd = qoff + bq
                bm_scr[e, kd * LANES:(kd + 1) * LANES, bq * LANES:(bq + 1) * LANES] += t[0] * LOG2E
                bm_scr[e, (kd - 1) * LANES:kd * LANES, bq * LANES:(bq + 1) * LANES] += t[1] * LOG2E

    def head_scores(e):
        hs = slice(e * HEAD_DIM, (e + 1) * HEAD_DIM)
        q = q_ref[0, :, hs]
        ss = []
        for blk in range(nkb):
            jb = i - (nkb - 1) + blk
            jc = jnp.maximum(jb, 0)
            k = k_ref[0, pl.ds(pl.multiple_of(jc * tq, tq), tq), hs]
            s = lax.dot_general(k, q, _NT, preferred_element_type=F32) + bm_scr[e, blk * tq:(blk + 1) * tq, :]
            if blk < nkb - 1:
                s = s + jnp.where(jb < 0, NEG_INF, 0.0)
            ss.append(s)
        return ss, functools.reduce(jnp.maximum, [jnp.max(s, axis=0, keepdims=True) for s in ss])

    outs = []
    pending = [head_scores(e) for e in range(min(BAND_LOOKAHEAD, hps))]
    for e in range(hps):
        ss, m = pending.pop(0)
        if e + BAND_LOOKAHEAD < hps:
            pending.append(head_scores(e + BAND_LOOKAHEAD))
        acc = None
        for blk, s in enumerate(ss):
            jc = jnp.maximum(i - (nkb - 1) + blk, 0)
            a = jnp.dot(vt_ref[0, e, jc], jnp.exp2(s - m).astype(BF16), preferred_element_type=F32)
            acc = a if acc is None else acc + a
        outs.append(_normalised_t(acc).astype(BF16))
    o_ref[0] = jnp.concatenate(outs, axis=1)


def _band_prompt(tbl, qkv, vt, n_heads):
    b, t, _ = qkv.shape
    tq = vt.shape[-1]
    nkb = C_BAND // tq + 1
    assert (nkb - 1) * tq == C_BAND and tq % LANES == 0 and LANES >= REL_CLIP
    pos = np.arange(LANES)
    rel = pos[:, None] - pos[None, :]
    idx = (np.stack([np.clip(rel, -REL_CLIP, REL_CLIP), np.clip(rel - LANES, -REL_CLIP, REL_CLIP)]) + REL_CLIP).astype(np.int32)
    k_pos = np.arange(nkb * tq)[:, None]
    q_pos = C_BAND + np.arange(tq)[None, :]
    kc, qc = k_pos // CHUNK, q_pos // CHUNK
    base = np.where((kc <= qc) & (qc - kc <= C_PREV_CHUNKS), 0.0, NEG_INF).astype(np.float32)
    hps = math.gcd(n_heads, BAND_HEADS)
    ngrp = n_heads // hps
    assert ngrp * hps == n_heads
    wd = hps * HEAD_DIM
    return pl.pallas_call(
        functools.partial(_band_p_kernel, tq=tq, nkb=nkb, nh=n_heads, hps=hps),
        grid=(ngrp, b, t // tq),
        in_specs=[
            pl.BlockSpec(memory_space=pltpu.SMEM),
            pl.BlockSpec((1, tq, wd), lambda h, bi, i: (bi, i, h)),
            pl.BlockSpec((1, t, wd), lambda h, bi, i: (bi, 0, ngrp + h)),
            pl.BlockSpec((1, hps, t // tq, VT_ROWS, tq), lambda h, bi, i: (bi, h, 0, 0, 0)),
            pl.BlockSpec(idx.shape, lambda h, bi, i: (0, 0, 0)),
            pl.BlockSpec(base.shape, lambda h, bi, i: (0, 0)),
        ],
        out_specs=pl.BlockSpec((1, tq, wd), lambda h, bi, i: (bi, i, h)),
        out_shape=jax.ShapeDtypeStruct((b, t, n_heads * HEAD_DIM), BF16),
        scratch_shapes=[pltpu.VMEM((hps, nkb * tq, tq), F32)],
        compiler_params=_params("arbitrary", "arbitrary", "arbitrary"),
        name="band_attn_prompt",
    )(tbl, qkv, qkv, vt, jnp.asarray(idx), jnp.asarray(base))


def _band_s_kernel(q_ref, kn_ref, vn_ref, kc_ref, vc_ref, bmc_ref, bmn_ref, o_ref, *, n_heads):
    outs = []
    for h in range(n_heads):
        hs = slice(h * HEAD_DIM, (h + 1) * HEAD_DIM)
        q = q_ref[0, :, hs]
        kc = kc_ref[0, :, h, :].astype(BF16)
        vc = vc_ref[0, :, h, :].astype(BF16)
        s_c = lax.dot_general(q, kc, _NT, preferred_element_type=F32) + bmc_ref[h]
        s_n = lax.dot_general(q, kn_ref[0, :, hs], _NT, preferred_element_type=F32) + bmn_ref[h]
        outs.append(_joint_softmax_pv(s_c, s_n, vc, vn_ref[0, :, hs]).astype(BF16))
    o_ref[0] = jnp.concatenate(outs, axis=1)


def _band_sample(qkv, kc, vc, bmc, bmn, n_heads):
    b, t, _ = qkv.shape
    lc = kc.shape[1]
    cw = n_heads * HEAD_DIM
    return pl.pallas_call(
        functools.partial(_band_s_kernel, n_heads=n_heads),
        grid=(b,),
        in_specs=[
            pl.BlockSpec((1, t, cw), lambda bi: (bi, 0, 0)),
            pl.BlockSpec((1, t, cw), lambda bi: (bi, 0, 1)),
            pl.BlockSpec((1, t, cw), lambda bi: (bi, 0, 2)),
            pl.BlockSpec((1, lc, n_heads, HEAD_DIM), lambda bi: (bi, 0, 0, 0)),
            pl.BlockSpec((1, lc, n_heads, HEAD_DIM), lambda bi: (bi, 0, 0, 0)),
            pl.BlockSpec(bmc.shape, lambda bi: (0, 0, 0)),
            pl.BlockSpec(bmn.shape, lambda bi: (0, 0, 0)),
        ],
        out_specs=pl.BlockSpec((1, t, cw), lambda bi: (bi, 0, 0)),
        out_shape=jax.ShapeDtypeStruct((b, t, cw), BF16),
        compiler_params=_params("parallel"),
        name="band_attn_sample",
    )(qkv, qkv, qkv, kc, vc, bmc, bmn)


def _t5_bucket_np(rel):
    nb = N_T5_BUCKETS // 2
    max_exact = nb // 2
    n = np.abs(rel)
    nf = np.maximum(n, 1).astype(np.float64)
    large = max_exact + (np.log(nf / max_exact) / math.log(T5_MAX_DIST / max_exact) * (nb - max_exact)).astype(np.int64)
    large = np.minimum(large, nb - 1)
    return np.where(rel > 0, nb, 0) + np.where(n < max_exact, n, large)


def _toeplitz(table, index_of_rel, q_pos, k_pos):
    r, c = len(q_pos), len(k_pos)
    assert (np.diff(q_pos) == 1).all() and (np.diff(k_pos) == 1).all()
    rels = (k_pos[0] - q_pos[0]) - (r - 1) + np.arange(r + c - 1)
    v = jnp.pad(table.astype(F32)[index_of_rel(rels)].T, ((0, 0), (0, 1)))
    x = jnp.tile(v, (1, r))[:, :r * (r + c - 1)].reshape(v.shape[0], r, r + c - 1)
    return x[:, :, r - 1:r - 1 + c]


def _t5_bias_tiles(t5_table, q_pos, k_pos, far_bucket):
    mask = (k_pos[None, :] // CHUNK) <= (q_pos[:, None] // CHUNK)
    bias = _toeplitz(t5_table, _t5_bucket_np, q_pos, k_pos)
    bias = (bias - t5_table.astype(F32)[far_bucket][:, None, None]) * LOG2E
    return jnp.where(mask[None], bias, NEG_INF)


def _band_bias_tiles(rel_table, q_pos, k_pos):
    qc = q_pos[:, None] // CHUNK
    kc = k_pos[None, :] // CHUNK
    mask = (kc <= qc) & (qc - kc <= C_PREV_CHUNKS) & (k_pos[None, :] >= 0)
    bias = _toeplitz(rel_table, lambda rel: np.clip(rel, -REL_CLIP, REL_CLIP) + REL_CLIP, q_pos, k_pos) * LOG2E
    return jnp.where(mask[None], bias, NEG_INF)


def kernel(x_prompt, x_sample, cache_a_k, cache_a_v, cache_b_k, cache_b_v, cache_b_logf, cache_c_k, cache_c_v, c_prompt, c_sample, w_ada, b_ada, norm_g, w_ffn_in, w_ffn_out, w_in_ab, b_forget, w_out_ab, lambda_q1, lambda_k1, lambda_q2, lambda_k2, subln_g, t5_table, w_in_c, w_out_c, c_rel_bias, final_g):
    depth = w_ada.shape[0]
    bsz, seq, d = x_prompt.shape
    dbsz, dseq, _ = x_sample.shape
    past = cache_b_logf.shape[2]
    h_a = cache_a_k.shape[3]
    h_b = cache_b_k.shape[3]
    h_c = cache_c_k.shape[3]
    wa, wb, cw = h_a * HEAD_DIM, h_b * HEAD_DIM, h_c * HEAD_DIM
    assert bsz == dbsz and wa == wb and 3 * wa + 3 * wb == 3 * cw
    assert LANES >= T5_MAX_DIST and SAMPLE_NEAR >= T5_MAX_DIST and past >= SAMPLE_NEAR
    far_bucket = (N_T5_BUCKETS // 2) - 1

    mods = _ada_mods(jnp.concatenate([c_prompt, c_sample], axis=0), w_ada, b_ada)
    mods = mods.reshape(depth, 2, bsz, 9, d).transpose(0, 3, 1, 2, 4).reshape(depth * 9 * 2, bsz, 1, d)

    def slot(l, k, grp):
        return (l * 9 + k) * 2 + grp

    xs = [x_prompt, x_sample]
    ab_states = [[], []]
    c_states = [[], []]
    for l in range(depth):
        i = l // 2
        last = l == depth - 1
        w1 = _prep_ffn_weights(w_ffn_in[l, 0], w_ffn_out[l, 0])
        w2 = _prep_ffn_weights(w_ffn_in[l, 1], w_ffn_out[l, 1])
        for grp in range(2):
            xs[grp] = _ffn(xs[grp], mods, [slot(l, k, grp) for k in range(3)], norm_g[l, 0], *w1)

        if l % 2 == 0:
            lam_init = 0.8 - 0.6 * math.exp(-0.3 * l)
            lam = (jnp.exp(jnp.sum(lambda_q1[i].astype(F32) * lambda_k1[i].astype(F32)))
                   - jnp.exp(jnp.sum(lambda_q2[i].astype(F32) * lambda_k2[i].astype(F32))) + lam_init).reshape(1)
            n_main = 3 * wa + 3 * wb
            w_main = w_in_ab[i][:, :n_main].astype(BF16)
            wf = jnp.pad(w_in_ab[i][:, n_main:].T.astype(BF16), ((0, 16 - h_b), (0, 0)))
            bfo = b_forget[i].astype(F32).reshape(h_b, 1)
            w_out = w_out_ab[i].astype(BF16)
            g_sub = subln_g[i].astype(F32).reshape(1, HEAD_DIM)
            tn = min(PROJ_COL_TILE, wa)
            taps = _seg_taps(tn, wa, (1, 2, 4, 5), False)
            cs = np.ones((n_main,), np.float32)
            cs[:wa] = DA_HALF ** -0.5 * LOG2E
            cs[3 * wa:3 * wa + wb] = HEAD_DIM ** -0.5 * LOG2E
            for grp in range(2):
                x = xs[grp]
                b, t, _ = x.shape
                tap_shapes = [((b, t, wa), True)] * 4
                if grp == 0:
                    qkv, ka, va, kb, vb, vt, logf_t = _proj(
                        x, mods, [slot(l, 3, grp), slot(l, 4, grp)], norm_g[l, 1], w_main, cs, tn, tap_shapes, taps,
                        wf, bfo, vt_cols=_vt_cols(tn, wa, (2, 5)), vt_tile=ROW_TILE)
                else:
                    qkv, ka, va, kb, vb, logf_t = _proj(
                        x, mods, [slot(l, 3, grp), slot(l, 4, grp)], norm_g[l, 1], w_main, cs, tn, tap_shapes, taps,
                        wf, bfo)
                logf = logf_t.reshape(h_b, b, t).transpose(1, 2, 0)
                ab_states[grp].append((ka.reshape(b, t, h_a, HEAD_DIM), va.reshape(b, t, h_a, HEAD_DIM),
                                       kb.reshape(b, t, h_b, HEAD_DIM), vb.reshape(b, t, h_b, HEAD_DIM), logf))
                logf_bh = logf_t.reshape(h_b, b, t).transpose(1, 0, 2)
                if grp == 0:
                    o_a = _diff_prompt(lam, t5_table.astype(F32).reshape(-1), qkv, vt, g_sub, h_a, 1.0 - lam_init)
                    fcum = _cumsum(logf_bh, jnp.zeros((b, h_b, LANES), F32))
                    nq = t // ROW_TILE
                    fref = fcum[:, :, ::ROW_TILE].reshape(-1)
                    o_b = _fox_prompt(fref, qkv, vt, fcum.reshape(b, h_b, nq, 1, ROW_TILE), h_b, 3 * h_a, h_a)
                else:
                    q_pos = past + np.arange(t)
                    bnear = _t5_bias_tiles(t5_table, q_pos, past - SAMPLE_NEAR + np.arange(SAMPLE_NEAR), far_bucket)
                    bnew = _t5_bias_tiles(t5_table, q_pos, q_pos, far_bucket)
                    kc = cache_a_k[i].transpose(0, 2, 1, 3)
                    vc = cache_a_v[i].transpose(0, 2, 1, 3)
                    o_a = _diff_sample(lam, qkv, kc, vc, bnear, bnew, g_sub, h_a, 1.0 - lam_init)
                    fc = _cumsum(cache_b_logf[i].astype(F32).transpose(0, 2, 1), jnp.zeros((b, h_b, LANES), F32))
                    carry = fc[:, :, past - 1:past]
                    fn = _cumsum(jnp.pad(logf_bh, ((0, 0), (0, 0), (0, LANES - t))),
                                 jnp.broadcast_to(carry, (b, h_b, LANES)))
                    o_b = _fox_sample(carry.reshape(-1), qkv, cache_b_k[i].transpose(0, 2, 1, 3),
                                      cache_b_v[i].transpose(0, 2, 1, 3), fc.reshape(b, h_b, 1, past),
                                      fn.reshape(b, h_b, 1, LANES), h_b, 3 * h_a)
                xs[grp] = _outproj([o_a, o_b], w_out, x, mods, slot(l, 5, grp))
        else:
            w_in = w_in_c[i].astype(BF16)
            w_out = w_out_c[i].astype(BF16)
            tn = min(PROJ_COL_TILE, cw)
            taps = _seg_taps(tn, cw, (1, 2), True)
            cs = np.ones((3 * cw,), np.float32)
            cs[:cw] = HEAD_DIM ** -0.5 * LOG2E
            for grp in range(2):
                x = xs[grp]
                b, t, _ = x.shape
                keep = min(C_BAND, t)
                assert keep == min(t, ROW_TILE)
                tap_shapes = [((b, keep, cw), False)] * 2
                if grp == 0:
                    qkv, k_new, v_new, vt = _proj(
                        x, mods, [slot(l, 3, grp), slot(l, 4, grp)], norm_g[l, 1], w_in, cs, tn, tap_shapes, taps,
                        vt_cols=_vt_cols(tn, cw, (2,)), vt_tile=BAND_TILE)
                else:
                    qkv, k_new, v_new = _proj(
                        x, mods, [slot(l, 3, grp), slot(l, 4, grp)], norm_g[l, 1], w_in, cs, tn, tap_shapes, taps)
                c_states[grp].append((k_new.reshape(b, keep, h_c, HEAD_DIM), v_new.reshape(b, keep, h_c, HEAD_DIM)))
                if grp == 0:
                    o_c = _band_prompt(c_rel_bias[i].astype(F32).reshape(-1), qkv, vt, h_c)
                else:
                    lc = cache_c_k.shape[2]
                    q_pos = past + np.arange(t)
                    bmc = _band_bias_tiles(c_rel_bias[i], q_pos, past - lc + np.arange(lc))
                    bmn = _band_bias_tiles(c_rel_bias[i], q_pos, q_pos)
                    o_c = _band_sample(qkv, cache_c_k[i], cache_c_v[i], bmc, bmn, h_c)
                xs[grp] = _outproj([o_c], w_out, x, mods, slot(l, 5, grp))

        for grp in range(2):
            xs[grp] = _ffn(xs[grp], mods, [slot(l, k, grp) for k in (6, 7, 8)], norm_g[l, 2], *w2)
            if last:
                xs[grp] = _final_norm(xs[grp], final_g)

    outs = [xs[0], xs[1]]
    for grp in range(2):
        st = ab_states[grp]
        outs += [jnp.stack([s[k] for s in st]) for k in range(5)]
        st = c_states[grp]
        outs += [jnp.stack([s[k] for s in st]) for k in range(2)]
    return tuple(outs)
```
